```python
import jax, jax.numpy as jnp
from jax import lax
import numpy as np

D_MODEL = 1024
BATCH = 8
SEQ = 8192
DEPTH = 4

POOL_WINDOWS = (2, 4, 8, 16)
POOL_GROUPS = len(POOL_WINDOWS)
POOL_GROUP_DIM = D_MODEL // 8
POOL_DIM = POOL_GROUPS * POOL_GROUP_DIM
POOL_WMAX = max(POOL_WINDOWS)
HEAD_DIM = 64
N_Q_HEADS = D_MODEL // 128
N_KV_HEADS = 2
GQA_GROUP = N_Q_HEADS // N_KV_HEADS
ATTN_DIM = N_Q_HEADS * HEAD_DIM
KV_DIM = N_KV_HEADS * HEAD_DIM
WINDOW = 128
BLOCK = 128
ROPE_THETA = 500000.0
ROT_DIM = HEAD_DIM // 4
N_BRANCHES = 2
IN_DIM = POOL_DIM + ATTN_DIM + 2 * KV_DIM + N_BRANCHES * D_MODEL
D_FF = 2816
EPS = 1e-6

kernel_name = "hybrid_pool_swa_macaron"


def rmsnorm(x, g):
    xf = x.astype(jnp.float32)
    y = xf * lax.rsqrt(jnp.mean(xf * xf, axis=-1, keepdims=True) + EPS)
    return (y * g.astype(jnp.float32)).astype(x.dtype)


def swiglu(h, w_gu, w_down):
    g, u = jnp.split(h @ w_gu, 2, axis=-1)
    return (jax.nn.silu(g) * u) @ w_down


def pool_mixer(u, w_grp, scale):
    B, S, _ = u.shape
    uf = u.astype(jnp.float32)
    c = jnp.cumsum(uf, axis=1)
    c_pad = jnp.pad(c, ((0, 0), (POOL_WMAX, 0), (0, 0)))
    t = jnp.arange(S)
    outs = []
    for g, w in enumerate(POOL_WINDOWS):
        lo, hi = g * POOL_GROUP_DIM, (g + 1) * POOL_GROUP_DIM
        win_sum = c[:, :, lo:hi] - c_pad[:, POOL_WMAX - w:POOL_WMAX - w + S, lo:hi]
        count = jnp.minimum(t + 1, w).astype(jnp.float32)[None, :, None]
        outs.append(win_sum / count - uf[:, :, lo:hi])
    d = jnp.stack(outs, axis=2).astype(u.dtype)
    y = jnp.einsum('bsgc,gcd->bsgd', d, w_grp).reshape(B, S, POOL_DIM)
    return y * scale


def partial_rope(x, cos, sin):
    half = ROT_DIM // 2
    x1 = x[..., :half].astype(jnp.float32)
    x2 = x[..., half:ROT_DIM].astype(jnp.float32)
    c = cos[None, :, None, :]
    s = sin[None, :, None, :]
    rot = jnp.concatenate([x1 * c - x2 * s, x2 * c + x1 * s], axis=-1).astype(x.dtype)
    return jnp.concatenate([rot, x[..., ROT_DIM:]], axis=-1)


def swa_sink_attention(q, k, v, sinks):
    B, S = q.shape[0], q.shape[1]
    nb = S // BLOCK
    qb = q.reshape(B, nb, BLOCK, N_KV_HEADS, GQA_GROUP, HEAD_DIM)

    def with_prev(t):
        tb = t.reshape(B, nb, BLOCK, N_KV_HEADS, HEAD_DIM)
        prev = jnp.pad(tb[:, :-1], ((0, 0), (1, 0), (0, 0), (0, 0), (0, 0)))
        return jnp.concatenate([prev, tb], axis=2)

    kb, vb = with_prev(k), with_prev(v)
    s = jnp.einsum('bnqhgd,bnkhd->bnhgqk', qb, kb).astype(jnp.float32) * (HEAD_DIM ** -0.5)
    qi = jnp.arange(BLOCK)[:, None]
    ki = jnp.arange(2 * BLOCK)[None, :]
    diff = qi + BLOCK - ki
    band = (diff >= 0) & (diff < WINDOW)
    valid = (jnp.arange(nb)[:, None, None] > 0) | (ki[None] >= BLOCK)
    mask = band[None] & valid
    s = jnp.where(mask[None, :, None, None], s, -jnp.inf)
    sink = sinks.astype(jnp.float32).reshape(1, 1, N_KV_HEADS, GQA_GROUP, 1, 1)
    m = jnp.maximum(jnp.max(s, axis=-1, keepdims=True), sink)
    p = jnp.exp(s - m)
    denom = jnp.sum(p, axis=-1, keepdims=True) + jnp.exp(sink - m)
    p = (p / denom).astype(v.dtype)
    o = jnp.einsum('bnhgqk,bnkhd->bnqhgd', p, vb)
    return o.reshape(B, S, ATTN_DIM)


def _fwd_setup_inputs(seed: int = 0) -> dict:
    key = jax.random.key(seed)
    ks = jax.random.split(key, 17)
    f32 = jnp.float32

    def w(k, shape, fan_in):
        return jax.random.normal(k, shape, f32) * (fan_in ** -0.5)

    def gain(k, shape):
        return 1.0 + 0.05 * jax.random.normal(k, shape, f32)

    L = DEPTH
    return {
        "x": jax.random.normal(ks[0], (BATCH, SEQ, D_MODEL), f32),
        "ln_ffn1": gain(ks[1], (L, D_MODEL)),
        "w_ffn1_gu": w(ks[2], (L, D_MODEL, 2 * D_FF), D_MODEL),
        "w_ffn1_down": w(ks[3], (L, D_FF, D_MODEL), D_FF),
        "ln_mix": gain(ks[4], (L, D_MODEL)),
        "w_in": w(ks[5], (L, D_MODEL, IN_DIM), D_MODEL),
        "pool_w": w(ks[6], (L, POOL_GROUPS, POOL_GROUP_DIM, POOL_GROUP_DIM), POOL_GROUP_DIM),
        "pool_scale": 1.0 + 0.1 * jax.random.normal(ks[7], (L, POOL_DIM), f32),
        "w_pool_branch": w(ks[8], (L, POOL_DIM, D_MODEL), POOL_DIM),
        "q_norm": gain(ks[9], (L, HEAD_DIM)),
        "k_norm": gain(ks[10], (L, HEAD_DIM)),
        "sinks": 0.5 * jax.random.normal(ks[11], (L, N_Q_HEADS), f32),
        "w_attn_branch": w(ks[12], (L, ATTN_DIM, D_MODEL), ATTN_DIM),
        "w_out": w(ks[13], (L, D_MODEL, D_MODEL), D_MODEL),
        "ln_ffn2": gain(ks[14], (L, D_MODEL)),
        "w_ffn2_gu": w(ks[15], (L, D_MODEL, 2 * D_FF), D_MODEL),
        "w_ffn2_down": w(ks[16], (L, D_FF, D_MODEL), D_FF),
    }


def _fwd_reference(x, ln_ffn1, w_ffn1_gu, w_ffn1_down, ln_mix, w_in, pool_w, pool_scale,
              w_pool_branch, q_norm, k_norm, sinks, w_attn_branch, w_out,
              ln_ffn2, w_ffn2_gu, w_ffn2_down):
    B, S, _ = x.shape
    pos = jnp.arange(S, dtype=jnp.float32)
    inv_freq = ROPE_THETA ** (-jnp.arange(0, ROT_DIM, 2, dtype=jnp.float32) / ROT_DIM)
    ang = pos[:, None] * inv_freq[None, :]
    cos, sin = jnp.cos(ang), jnp.sin(ang)
    splits = [POOL_DIM, POOL_DIM + ATTN_DIM, POOL_DIM + ATTN_DIM + KV_DIM,
              POOL_DIM + ATTN_DIM + 2 * KV_DIM]

    for l in range(DEPTH):
        x = x + 0.5 * swiglu(rmsnorm(x, ln_ffn1[l]), w_ffn1_gu[l], w_ffn1_down[l])

        h = rmsnorm(x, ln_mix[l])
        z = h @ w_in[l]
        u_pool, q, k, v, gate_logits = jnp.split(z, splits, axis=-1)

        a = pool_mixer(u_pool, pool_w[l], pool_scale[l]) @ w_pool_branch[l]

        q = rmsnorm(q.reshape(B, S, N_Q_HEADS, HEAD_DIM), q_norm[l])
        k = rmsnorm(k.reshape(B, S, N_KV_HEADS, HEAD_DIM), k_norm[l])
        v = v.reshape(B, S, N_KV_HEADS, HEAD_DIM)
        q = partial_rope(q, cos, sin)
        k = partial_rope(k, cos, sin)
        b = swa_sink_attention(q, k, v, sinks[l]) @ w_attn_branch[l]

        g_pool, g_attn = jnp.split(jax.nn.sigmoid(gate_logits), N_BRANCHES, axis=-1)
        x = x + (g_pool * a + g_attn * b) @ w_out[l]

        x = x + 0.5 * swiglu(rmsnorm(x, ln_ffn2[l]), w_ffn2_gu[l], w_ffn2_down[l])
    return x


import jax as _jax
import jax.numpy as _jnp

TWIN_FORMAT = 'train_step'
FWD_PARAMS = ['x', 'ln_ffn1', 'w_ffn1_gu', 'w_ffn1_down', 'ln_mix', 'w_in', 'pool_w', 'pool_scale', 'w_pool_branch', 'q_norm', 'k_norm', 'sinks', 'w_attn_branch', 'w_out', 'ln_ffn2', 'w_ffn2_gu', 'w_ffn2_down']
TWIN_WEIGHTS = ['ln_ffn1', 'w_ffn1_gu', 'w_ffn1_down', 'ln_mix', 'w_in', 'pool_w', 'pool_scale', 'w_pool_branch', 'q_norm', 'k_norm', 'sinks', 'w_attn_branch', 'w_out', 'ln_ffn2', 'w_ffn2_gu', 'w_ffn2_down']
TWIN_DIFF_INPUT = 'x'
TWIN_INPUTS = ['x', 'ln_ffn1', 'w_ffn1_gu', 'w_ffn1_down', 'ln_mix', 'w_in', 'pool_w', 'pool_scale', 'w_pool_branch', 'q_norm', 'k_norm', 'sinks', 'w_attn_branch', 'w_out', 'ln_ffn2', 'w_ffn2_gu', 'w_ffn2_down', 'loss_target', 'm_ln_ffn1', 'm_w_ffn1_gu', 'm_w_ffn1_down', 'm_ln_mix', 'm_w_in', 'm_pool_w', 'm_pool_scale', 'm_w_pool_branch', 'm_q_norm', 'm_k_norm', 'm_sinks', 'm_w_attn_branch', 'm_w_out', 'm_ln_ffn2', 'm_w_ffn2_gu', 'm_w_ffn2_down', 'v_ln_ffn1', 'v_w_ffn1_gu', 'v_w_ffn1_down', 'v_ln_mix', 'v_w_in', 'v_pool_w', 'v_pool_scale', 'v_w_pool_branch', 'v_q_norm', 'v_k_norm', 'v_sinks', 'v_w_attn_branch', 'v_w_out', 'v_ln_ffn2', 'v_w_ffn2_gu', 'v_w_ffn2_down']
TWIN_OUTPUTS = ['loss', 'grad_x', 'grad_ln_ffn1', 'grad_w_ffn1_gu', 'grad_w_ffn1_down', 'grad_ln_mix', 'grad_w_in', 'grad_pool_w', 'grad_pool_scale', 'grad_w_pool_branch', 'grad_q_norm', 'grad_k_norm', 'grad_sinks', 'grad_w_attn_branch', 'grad_w_out', 'grad_ln_ffn2', 'grad_w_ffn2_gu', 'grad_w_ffn2_down', 'delta_ln_ffn1', 'delta_w_ffn1_gu', 'delta_w_ffn1_down', 'delta_ln_mix', 'delta_w_in', 'delta_pool_w', 'delta_pool_scale', 'delta_w_pool_branch', 'delta_q_norm', 'delta_k_norm', 'delta_sinks', 'delta_w_attn_branch', 'delta_w_out', 'delta_ln_ffn2', 'delta_w_ffn2_gu', 'delta_w_ffn2_down', 'new_m_ln_ffn1', 'new_m_w_ffn1_gu', 'new_m_w_ffn1_down', 'new_m_ln_mix', 'new_m_w_in', 'new_m_pool_w', 'new_m_pool_scale', 'new_m_w_pool_branch', 'new_m_q_norm', 'new_m_k_norm', 'new_m_sinks', 'new_m_w_attn_branch', 'new_m_w_out', 'new_m_ln_ffn2', 'new_m_w_ffn2_gu', 'new_m_w_ffn2_down', 'new_v_ln_ffn1', 'new_v_w_ffn1_gu', 'new_v_w_ffn1_down', 'new_v_ln_mix', 'new_v_w_in', 'new_v_pool_w', 'new_v_pool_scale', 'new_v_w_pool_branch', 'new_v_q_norm', 'new_v_k_norm', 'new_v_sinks', 'new_v_w_attn_branch', 'new_v_w_out', 'new_v_ln_ffn2', 'new_v_w_ffn2_gu', 'new_v_w_ffn2_down']
TWIN_LEAF_KINDS = {'loss': 'loss', 'grad_x': 'grad_x', 'grad_ln_ffn1': 'grad_w', 'grad_w_ffn1_gu': 'grad_w', 'grad_w_ffn1_down': 'grad_w', 'grad_ln_mix': 'grad_w', 'grad_w_in': 'grad_w', 'grad_pool_w': 'grad_w', 'grad_pool_scale': 'grad_w', 'grad_w_pool_branch': 'grad_w', 'grad_q_norm': 'grad_w', 'grad_k_norm': 'grad_w', 'grad_sinks': 'grad_w', 'grad_w_attn_branch': 'grad_w', 'grad_w_out': 'grad_w', 'grad_ln_ffn2': 'grad_w', 'grad_w_ffn2_gu': 'grad_w', 'grad_w_ffn2_down': 'grad_w', 'delta_ln_ffn1': 'delta_w', 'delta_w_ffn1_gu': 'delta_w', 'delta_w_ffn1_down': 'delta_w', 'delta_ln_mix': 'delta_w', 'delta_w_in': 'delta_w', 'delta_pool_w': 'delta_w', 'delta_pool_scale': 'delta_w', 'delta_w_pool_branch': 'delta_w', 'delta_q_norm': 'delta_w', 'delta_k_norm': 'delta_w', 'delta_sinks': 'delta_w', 'delta_w_attn_branch': 'delta_w', 'delta_w_out': 'delta_w', 'delta_ln_ffn2': 'delta_w', 'delta_w_ffn2_gu': 'delta_w', 'delta_w_ffn2_down': 'delta_w', 'new_m_ln_ffn1': 'new_m', 'new_m_w_ffn1_gu': 'new_m', 'new_m_w_ffn1_down': 'new_m', 'new_m_ln_mix': 'new_m', 'new_m_w_in': 'new_m', 'new_m_pool_w': 'new_m', 'new_m_pool_scale': 'new_m', 'new_m_w_pool_branch': 'new_m', 'new_m_q_norm': 'new_m', 'new_m_k_norm': 'new_m', 'new_m_sinks': 'new_m', 'new_m_w_attn_branch': 'new_m', 'new_m_w_out': 'new_m', 'new_m_ln_ffn2': 'new_m', 'new_m_w_ffn2_gu': 'new_m', 'new_m_w_ffn2_down': 'new_m', 'new_v_ln_ffn1': 'new_v', 'new_v_w_ffn1_gu': 'new_v', 'new_v_w_ffn1_down': 'new_v', 'new_v_ln_mix': 'new_v', 'new_v_w_in': 'new_v', 'new_v_pool_w': 'new_v', 'new_v_pool_scale': 'new_v', 'new_v_w_pool_branch': 'new_v', 'new_v_q_norm': 'new_v', 'new_v_k_norm': 'new_v', 'new_v_sinks': 'new_v', 'new_v_w_attn_branch': 'new_v', 'new_v_w_out': 'new_v', 'new_v_ln_ffn2': 'new_v', 'new_v_w_ffn2_gu': 'new_v', 'new_v_w_ffn2_down': 'new_v'}


def _forward(args):
    return _fwd_reference(*[args[k] for k in FWD_PARAMS])


def _output_shape():
    out = _jax.eval_shape(lambda: _forward(_fwd_setup_inputs(0)))
    return out.shape, out.dtype

N_MICROBATCH = 1
ADAM_LR = 0.001
ADAM_B1 = 0.9
ADAM_B2 = 0.999
ADAM_EPS = 1e-08
ADAM_WD = 0.01
ADAM_STEP = 10
PER_EXAMPLE_BATCH_AXIS = {'x': 0, 'loss_target': 0}
SHARED_INPUTS = []
_WEIGHT_DTYPES = {'ln_ffn1': _jnp.float32, 'w_ffn1_gu': _jnp.float32, 'w_ffn1_down': _jnp.float32, 'ln_mix': _jnp.float32, 'w_in': _jnp.float32, 'pool_w': _jnp.float32, 'pool_scale': _jnp.float32, 'w_pool_branch': _jnp.float32, 'q_norm': _jnp.float32, 'k_norm': _jnp.float32, 'sinks': _jnp.float32, 'w_attn_branch': _jnp.float32, 'w_out': _jnp.float32, 'ln_ffn2': _jnp.float32, 'w_ffn2_gu': _jnp.float32, 'w_ffn2_down': _jnp.float32}
MOMENT_SCALE = {'ln_ffn1': 1.222439e+01, 'w_ffn1_gu': 1.616763e-01, 'w_ffn1_down': 2.914100e-01, 'ln_mix': 1.598583e+01, 'w_in': 6.160212e-01, 'pool_w': 2.795011e+00, 'pool_scale': 3.037046e+01, 'w_pool_branch': 1.351569e+00, 'q_norm': 2.964372e+00, 'k_norm': 2.959648e+00, 'sinks': 7.132657e-01, 'w_attn_branch': 9.549475e-02, 'w_out': 1.156785e+00, 'ln_ffn2': 1.210724e+01, 'w_ffn2_gu': 1.481289e-01, 'w_ffn2_down': 2.739295e-01}


def _to_microbatches(a, axis):
    t = _jnp.moveaxis(a, axis, 0)
    t = t.reshape((N_MICROBATCH, t.shape[0] // N_MICROBATCH) + t.shape[1:])
    return _jnp.moveaxis(t, 1, axis + 1)


def setup_inputs(seed: int = 0) -> dict:
    inp = _fwd_setup_inputs(seed)
    key = _jax.random.fold_in(_jax.random.key(seed), 7919)
    shape, _ = _output_shape()
    out = dict(inp)
    out["loss_target"] = _jax.random.normal(_jax.random.fold_in(key, 0), shape, _jnp.float32)
    for i, name in enumerate(TWIN_WEIGHTS):
        w = inp[name].astype(_jnp.float32)
        if MOMENT_SCALE is None:
            s = _jnp.sqrt(_jnp.mean(_jnp.square(w)) + 1e-30)
        else:
            s = MOMENT_SCALE[name]
        km, kv = _jax.random.split(_jax.random.fold_in(key, i + 1))
        out[name] = w
        out["m_" + name] = s * _jax.random.normal(km, w.shape, _jnp.float32)
        out["v_" + name] = (s * s) * _jax.random.uniform(kv, w.shape, _jnp.float32, 0.5, 1.5)
    if N_MICROBATCH > 1:
        for name, axis in PER_EXAMPLE_BATCH_AXIS.items():
            out[name] = _to_microbatches(out[name], axis)
    return {'x': out['x'], 'ln_ffn1': out['ln_ffn1'], 'w_ffn1_gu': out['w_ffn1_gu'], 'w_ffn1_down': out['w_ffn1_down'], 'ln_mix': out['ln_mix'], 'w_in': out['w_in'], 'pool_w': out['pool_w'], 'pool_scale': out['pool_scale'], 'w_pool_branch': out['w_pool_branch'], 'q_norm': out['q_norm'], 'k_norm': out['k_norm'], 'sinks': out['sinks'], 'w_attn_branch': out['w_attn_branch'], 'w_out': out['w_out'], 'ln_ffn2': out['ln_ffn2'], 'w_ffn2_gu': out['w_ffn2_gu'], 'w_ffn2_down': out['w_ffn2_down'], 'loss_target': out['loss_target'], 'm_ln_ffn1': out['m_ln_ffn1'], 'm_w_ffn1_gu': out['m_w_ffn1_gu'], 'm_w_ffn1_down': out['m_w_ffn1_down'], 'm_ln_mix': out['m_ln_mix'], 'm_w_in': out['m_w_in'], 'm_pool_w': out['m_pool_w'], 'm_pool_scale': out['m_pool_scale'], 'm_w_pool_branch': out['m_w_pool_branch'], 'm_q_norm': out['m_q_norm'], 'm_k_norm': out['m_k_norm'], 'm_sinks': out['m_sinks'], 'm_w_attn_branch': out['m_w_attn_branch'], 'm_w_out': out['m_w_out'], 'm_ln_ffn2': out['m_ln_ffn2'], 'm_w_ffn2_gu': out['m_w_ffn2_gu'], 'm_w_ffn2_down': out['m_w_ffn2_down'], 'v_ln_ffn1': out['v_ln_ffn1'], 'v_w_ffn1_gu': out['v_w_ffn1_gu'], 'v_w_ffn1_down': out['v_w_ffn1_down'], 'v_ln_mix': out['v_ln_mix'], 'v_w_in': out['v_w_in'], 'v_pool_w': out['v_pool_w'], 'v_pool_scale': out['v_pool_scale'], 'v_w_pool_branch': out['v_w_pool_branch'], 'v_q_norm': out['v_q_norm'], 'v_k_norm': out['v_k_norm'], 'v_sinks': out['v_sinks'], 'v_w_attn_branch': out['v_w_attn_branch'], 'v_w_out': out['v_w_out'], 'v_ln_ffn2': out['v_ln_ffn2'], 'v_w_ffn2_gu': out['v_w_ffn2_gu'], 'v_w_ffn2_down': out['v_w_ffn2_down']}


def _loss(weights, diff, rest, loss_target):
    with _jax.named_scope("forward"):
        args = {**rest, TWIN_DIFF_INPUT: diff, **{k: w.astype(_WEIGHT_DTYPES[k]) for k, w in weights.items()}}
        y = _forward(args)
    with _jax.named_scope("loss_head"):
        err = _jnp.square(y.astype(_jnp.float32) - loss_target)
        return 0.5 * _jnp.sum(_jnp.mean(err, axis=-1)) if err.ndim else 0.5 * err


def _adamw(w, g, m, v):
    m = ADAM_B1 * m + (1.0 - ADAM_B1) * g
    v = ADAM_B2 * v + (1.0 - ADAM_B2) * _jnp.square(g)
    m_hat = m / (1.0 - ADAM_B1 ** ADAM_STEP)
    v_hat = v / (1.0 - ADAM_B2 ** ADAM_STEP)
    delta = -ADAM_LR * (m_hat / (_jnp.sqrt(v_hat) + ADAM_EPS) + ADAM_WD * w)
    return delta, m, v


def reference(x, ln_ffn1, w_ffn1_gu, w_ffn1_down, ln_mix, w_in, pool_w, pool_scale, w_pool_branch, q_norm, k_norm, sinks, w_attn_branch, w_out, ln_ffn2, w_ffn2_gu, w_ffn2_down, loss_target, m_ln_ffn1, m_w_ffn1_gu, m_w_ffn1_down, m_ln_mix, m_w_in, m_pool_w, m_pool_scale, m_w_pool_branch, m_q_norm, m_k_norm, m_sinks, m_w_attn_branch, m_w_out, m_ln_ffn2, m_w_ffn2_gu, m_w_ffn2_down, v_ln_ffn1, v_w_ffn1_gu, v_w_ffn1_down, v_ln_mix, v_w_in, v_pool_w, v_pool_scale, v_w_pool_branch, v_q_norm, v_k_norm, v_sinks, v_w_attn_branch, v_w_out, v_ln_ffn2, v_w_ffn2_gu, v_w_ffn2_down):
    given = dict(x=x, ln_ffn1=ln_ffn1, w_ffn1_gu=w_ffn1_gu, w_ffn1_down=w_ffn1_down, ln_mix=ln_mix, w_in=w_in, pool_w=pool_w, pool_scale=pool_scale, w_pool_branch=w_pool_branch, q_norm=q_norm, k_norm=k_norm, sinks=sinks, w_attn_branch=w_attn_branch, w_out=w_out, ln_ffn2=ln_ffn2, w_ffn2_gu=w_ffn2_gu, w_ffn2_down=w_ffn2_down, loss_target=loss_target, m_ln_ffn1=m_ln_ffn1, m_w_ffn1_gu=m_w_ffn1_gu, m_w_ffn1_down=m_w_ffn1_down, m_ln_mix=m_ln_mix, m_w_in=m_w_in, m_pool_w=m_pool_w, m_pool_scale=m_pool_scale, m_w_pool_branch=m_w_pool_branch, m_q_norm=m_q_norm, m_k_norm=m_k_norm, m_sinks=m_sinks, m_w_attn_branch=m_w_attn_branch, m_w_out=m_w_out, m_ln_ffn2=m_ln_ffn2, m_w_ffn2_gu=m_w_ffn2_gu, m_w_ffn2_down=m_w_ffn2_down, v_ln_ffn1=v_ln_ffn1, v_w_ffn1_gu=v_w_ffn1_gu, v_w_ffn1_down=v_w_ffn1_down, v_ln_mix=v_ln_mix, v_w_in=v_w_in, v_pool_w=v_pool_w, v_pool_scale=v_pool_scale, v_w_pool_branch=v_w_pool_branch, v_q_norm=v_q_norm, v_k_norm=v_k_norm, v_sinks=v_sinks, v_w_attn_branch=v_w_attn_branch, v_w_out=v_w_out, v_ln_ffn2=v_ln_ffn2, v_w_ffn2_gu=v_w_ffn2_gu, v_w_ffn2_down=v_w_ffn2_down)
    weights = {n: given[n] for n in TWIN_WEIGHTS}
    shared = {n: given[n] for n in SHARED_INPUTS}
    per_example = {n: given[n] for n in ['x']}
    grad_fn = _jax.value_and_grad(_loss, argnums=(0, 1))

    def one_microbatch(ex, loss_target):
        ex = dict(ex)
        diff = ex.pop(TWIN_DIFF_INPUT)
        return grad_fn(weights, diff, {**shared, **ex}, loss_target)

    if N_MICROBATCH == 1:
        loss, (grad_w, grad_x) = one_microbatch(per_example, given["loss_target"])
    else:
        def body(carry, xs):
            loss_sum, grad_sum = carry
            l_k, (gw_k, gx_k) = one_microbatch(xs[0], xs[1])
            with _jax.named_scope("update"):
                return (loss_sum + l_k, _jax.tree.map(_jnp.add, grad_sum, gw_k)), gx_k

        init = (_jnp.zeros((), _jnp.float32), _jax.tree.map(_jnp.zeros_like, weights))
        (loss, grad_w), grad_x = _jax.lax.scan(body, init, (per_example, given["loss_target"]))
    with _jax.named_scope("update"):
        delta_w, new_m, new_v = {}, {}, {}
        for n in TWIN_WEIGHTS:
            delta_w[n], new_m[n], new_v[n] = _adamw(weights[n], grad_w[n], given["m_" + n], given["v_" + n])
    return (loss, grad_x, *[grad_w[n] for n in TWIN_WEIGHTS], *[delta_w[n] for n in TWIN_WEIGHTS],
            *[new_m[n] for n in TWIN_WEIGHTS], *[new_v[n] for n in TWIN_WEIGHTS])
```

```python
import functools

import jax
import jax.numpy as jnp
from jax import lax
from jax.experimental import pallas as pl
from jax.experimental.pallas import tpu as pltpu

F32 = jnp.float32
BF16 = jnp.bfloat16

N_DEV = 8
MESH_AXES = ("x", "y", "c")
EPS = 1e-6

HEAD_DIM = 64
N_Q_HEADS = 8
N_KV_HEADS = 2
GQA_GROUP = N_Q_HEADS // N_KV_HEADS
ATTN_BLOCK = 128
ATTN_SCALE = HEAD_DIM ** -0.5
ROPE_THETA = 500000.0
ROT_DIM = 16
POOL_WINDOWS = (2, 4, 8, 16)
POOL_HALO = 16
GROUP_DIM = 128
POOL_DIM = 512
ATTN_DIM = 512
KV_DIM = 128
QKV_END = POOL_DIM + ATTN_DIM + 2 * KV_DIM

ADAM_LR = 0.001
ADAM_B1 = 0.9
ADAM_B2 = 0.999
ADAM_EPS = 1e-08
ADAM_WD = 0.01
ADAM_STEP = 10

ROW_TILE = 512
FFN_ROW_TILE = 256
VMEM_LIMIT_BYTES = 56 << 20
ADAMW_BLOCK_ELEMS = 96 * 1024
NEG_BIG = -1e30

_NT = (((1,), (1,)), ((), ()))
_TN = (((0,), (0,)), ((), ()))


def _cp(*sem):
    return pltpu.CompilerParams(dimension_semantics=sem, vmem_limit_bytes=VMEM_LIMIT_BYTES)


def _resident(block, index):
    return pl.BlockSpec(block, lambda *_: index, pipeline_mode=pl.Buffered(1))


def _row_tile(m, cap=None):
    return min(ROW_TILE if cap is None else min(ROW_TILE, cap), m)


def _sds(shape, dtype):
    return jax.ShapeDtypeStruct(shape, dtype)


def _mesh_pos():
    return lax.axis_index("x"), lax.axis_index("y"), lax.axis_index("c")


def _all_gather_many(shards, name):
    n = len(shards)

    def body(*refs):
        ins, outs = refs[:n], refs[n:2 * n]
        send_sems, recv_sems, local_sems = refs[2 * n:]
        x, y, c = _mesh_pos()
        me, sibling = (x, y, c), (x, y, 1 - c)
        chips = [(1 - x, y), (x, 1 - y), (1 - x, 1 - y)]

        def slot(a, pos):
            return outs[a].at[4 * pos[0] + 2 * pos[1] + pos[2]]

        def copy(a, k, block, to, src=None):
            return pltpu.make_async_remote_copy(
                src_ref=slot(a, block) if src is None else src, dst_ref=slot(a, block),
                send_sem=send_sems.at[a, k], recv_sem=recv_sems.at[a, k],
                device_id=to, device_id_type=pl.DeviceIdType.MESH)

        mine = [pltpu.make_async_copy(ins[a], slot(a, me), local_sems.at[a]) for a in range(n)]
        for cp in mine:
            cp.start()
        first = []
        for a in range(n):
            first.append(copy(a, 0, me, sibling, src=ins[a]))
            for j, chip in enumerate(chips):
                first.append(copy(a, 1 + j, me, (*chip, c), src=ins[a]))
        for cp in first:
            cp.start()
        passed = []
        for j, chip in enumerate(chips):
            for a in range(n):
                copy(a, 1 + j, (*chip, c), me).wait_recv()
                fwd = copy(a, 4 + j, (*chip, c), sibling)
                fwd.start()
                passed.append(fwd)
        for a in range(n):
            copy(a, 0, sibling, me).wait_recv()
        for j, chip in enumerate(chips):
            for a in range(n):
                copy(a, 4 + j, (*chip, 1 - c), me).wait_recv()
        for cp in first + passed:
            cp.wait_send()
        for cp in mine:
            cp.wait()

    any_spec = pl.BlockSpec(memory_space=pl.ANY)
    return pl.pallas_call(
        body, name=name,
        out_shape=[_sds((N_DEV,) + s.shape, s.dtype) for s in shards],
        in_specs=[any_spec] * n, out_specs=[any_spec] * n,
        scratch_shapes=[pltpu.SemaphoreType.DMA((n, 7)), pltpu.SemaphoreType.DMA((n, 7)),
                        pltpu.SemaphoreType.DMA((n,))],
    )(*shards)


def _all_to_all_many(bufs, name):
    n = len(bufs)

    def body(*refs):
        ins, outs = refs[:n], refs[n:2 * n]
        send_sems, recv_sems, local_sems = refs[2 * n:]
        x, y, c = _mesh_pos()
        me = 4 * x + 2 * y + c
        mine = [pltpu.make_async_copy(ins[a].at[me], outs[a].at[me], local_sems.at[a]) for a in range(n)]
        for cp in mine:
            cp.start()
        copies = []
        for k in range(1, N_DEV):
            px = 1 - x if k & 4 else x
            py = 1 - y if k & 2 else y
            pc = 1 - c if k & 1 else c
            peer = 4 * px + 2 * py + pc
            for a in range(n):
                cp = pltpu.make_async_remote_copy(
                    src_ref=ins[a].at[peer], dst_ref=outs[a].at[me],
                    send_sem=send_sems.at[a, k - 1], recv_sem=recv_sems.at[a, k - 1],
                    device_id=(px, py, pc), device_id_type=pl.DeviceIdType.MESH)
                cp.start()
                copies.append(cp)
        for cp in copies:
            cp.wait()
        for cp in mine:
            cp.wait()

    any_spec = pl.BlockSpec(memory_space=pl.ANY)
    return pl.pallas_call(
        body, name=name,
        out_shape=[_sds(b.shape, b.dtype) for b in bufs],
        in_specs=[any_spec] * n, out_specs=[any_spec] * n,
        scratch_shapes=[pltpu.SemaphoreType.DMA((n, 7)), pltpu.SemaphoreType.DMA((n, 7)),
                        pltpu.SemaphoreType.DMA((n,))],
    )(*bufs)


def _rms_fwd(xv, gain):
    r = lax.rsqrt(jnp.mean(xv * xv, axis=-1, keepdims=True) + EPS)
    return xv * r * gain


def _rms_bwd(dh, xv, gain):
    r = lax.rsqrt(jnp.mean(xv * xv, axis=-1, keepdims=True) + EPS)
    xn = xv * r
    dxn = dh * gain
    dx = r * (dxn - xn * jnp.mean(dxn * xn, axis=-1, keepdims=True))
    return dx, dh * xn


def _silu_parts(g):
    s = jax.nn.sigmoid(g)
    return g * s, s * (1.0 + g * (1.0 - s))


def _segment_mean(v, width):
    r = lax.broadcasted_iota(jnp.int32, (width, width), 0) >> 6
    c = lax.broadcasted_iota(jnp.int32, (width, width), 1) >> 6
    bd = (r == c).astype(F32)
    return jnp.dot(v, bd, precision=lax.Precision.HIGHEST, preferred_element_type=F32) * (1.0 / HEAD_DIM)


def _rope_partner(v):
    width = v.shape[1]
    half = ROT_DIM // 2
    lane = lax.broadcasted_iota(jnp.int32, v.shape, 1) & (HEAD_DIM - 1)
    up = jnp.where(lane < ROT_DIM, pltpu.roll(v, half, 1), 0.0)
    return jnp.where(lane < half, pltpu.roll(v, width - half, 1), up)


def _tile_lanes(t, width):
    return t if width == t.shape[1] else jnp.tile(t, (1, width // t.shape[1]))


def _ffn_up(x, ln, wgu, name):
    m, d = x.shape
    c = wgu.shape[-1]
    tm = _row_tile(m, FFN_ROW_TILE)

    def body(x_ref, ln_ref, w_ref, h_ref, gu_ref, a_ref):
        h = _rms_fwd(x_ref[...], ln_ref[...]).astype(BF16)
        h_ref[...] = h
        for j in range(4):
            g = jnp.dot(h, w_ref[0, j], preferred_element_type=F32)
            u = jnp.dot(h, w_ref[1, j], preferred_element_type=F32)
            gu_ref[0, j] = g
            gu_ref[1, j] = u
            a_ref[j] = (g * jax.nn.sigmoid(g) * u).astype(BF16)

    return pl.pallas_call(
        body, name=name, grid=(m // tm,),
        in_specs=[pl.BlockSpec((tm, d), lambda i: (i, 0)), _resident((1, d), (0, 0)),
                  _resident((2, 4, d, c), (0, 0, 0, 0))],
        out_specs=[pl.BlockSpec((tm, d), lambda i: (i, 0)),
                   pl.BlockSpec((2, 4, tm, c), lambda i: (0, 0, i, 0)),
                   pl.BlockSpec((4, tm, c), lambda i: (0, i, 0))],
        out_shape=[_sds((m, d), BF16), _sds((2, 4, m, c), F32), _sds((4, m, c), BF16)],
        compiler_params=_cp("parallel"),
    )(x, ln, wgu)


def _ffn_down(x, act, wd, name):
    m, d = x.shape
    c = act.shape[-1]
    tm = _row_tile(m)

    def body(x_ref, a_ref, w_ref, o_ref):
        acc = jnp.dot(a_ref[0], w_ref[0], preferred_element_type=F32)
        for j in range(1, 4):
            acc += jnp.dot(a_ref[j], w_ref[j], preferred_element_type=F32)
        o_ref[...] = x_ref[...] + 0.5 * acc

    return pl.pallas_call(
        body, name=name, grid=(m // tm,),
        in_specs=[pl.BlockSpec((tm, d), lambda i: (i, 0)), pl.BlockSpec((4, tm, c), lambda i: (0, i, 0)),
                  _resident((4, c, d), (0, 0, 0))],
        out_specs=pl.BlockSpec((tm, d), lambda i: (i, 0)),
        out_shape=_sds((m, d), F32),
        compiler_params=_cp("parallel"),
    )(x, act, wd)


def _ffn_bwd_act(dy, gu, wd, name):
    m, d = dy.shape
    c = gu.shape[-1]
    tm = _row_tile(m, FFN_ROW_TILE)

    def body(dy_ref, gu_ref, w_ref, dyh_ref, dgu_ref):
        dyh = (0.5 * dy_ref[...]).astype(BF16)
        dyh_ref[...] = dyh
        for j in range(4):
            da = lax.dot_general(dyh, w_ref[j], _NT, preferred_element_type=F32)
            g = gu_ref[0, j]
            u = gu_ref[1, j]
            silu, dsilu = _silu_parts(g)
            dgu_ref[0, j] = (da * u * dsilu).astype(BF16)
            dgu_ref[1, j] = (da * silu).astype(BF16)

    return pl.pallas_call(
        body, name=name, grid=(m // tm,),
        in_specs=[pl.BlockSpec((tm, d), lambda i: (i, 0)), pl.BlockSpec((2, 4, tm, c), lambda i: (0, 0, i, 0)),
                  _resident((4, c, d), (0, 0, 0))],
        out_specs=[pl.BlockSpec((tm, d), lambda i: (i, 0)), pl.BlockSpec((2, 4, tm, c), lambda i: (0, 0, i, 0))],
        out_shape=[_sds((m, d), BF16), _sds((2, 4, m, c), BF16)],
        compiler_params=_cp("parallel"),
    )(dy, gu, wd)


def _ffn_bwd_x(dy, x, ln, dgu, wgu, name):
    m, d = dy.shape
    c = dgu.shape[-1]
    tm = _row_tile(m)

    def body(dy_ref, x_ref, ln_ref, dgu_ref, w_ref, dx_ref, dln_ref):
        @pl.when(pl.program_id(0) == 0)
        def _():
            dln_ref[...] = jnp.zeros_like(dln_ref)

        dh = None
        for half in range(2):
            for j in range(4):
                t = lax.dot_general(dgu_ref[half, j], w_ref[half, j], _NT, preferred_element_type=F32)
                dh = t if dh is None else dh + t
        dx, dgain = _rms_bwd(dh, x_ref[...], ln_ref[...])
        dx_ref[...] = dy_ref[...] + dx
        dln_ref[...] += jnp.sum(dgain, axis=0, keepdims=True)

    return pl.pallas_call(
        body, name=name, grid=(m // tm,),
        in_specs=[pl.BlockSpec((tm, d), lambda i: (i, 0)), pl.BlockSpec((tm, d), lambda i: (i, 0)),
                  _resident((1, d), (0, 0)), pl.BlockSpec((2, 4, tm, c), lambda i: (0, 0, i, 0)),
                  _resident((2, 4, d, c), (0, 0, 0, 0))],
        out_specs=[pl.BlockSpec((tm, d), lambda i: (i, 0)), pl.BlockSpec((1, d), lambda i: (0, 0))],
        out_shape=[_sds((m, d), F32), _sds((1, d), F32)],
        compiler_params=_cp("arbitrary"),
    )(dy, x, ln, dgu, wgu)


def _matmul_tn(a, b, name, out_dtype=BF16):
    ja, m, k = a.shape
    jb, _, n = b.shape
    nj = max(ja, jb)
    tm = _row_tile(m)
    nm = m // tm

    def body(a_ref, b_ref, o_ref, acc):
        step = pl.program_id(1)

        @pl.when(step == 0)
        def _():
            acc[...] = jnp.zeros_like(acc)

        acc[...] += lax.dot_general(a_ref[...], b_ref[...], _TN, preferred_element_type=F32)

        @pl.when(step == nm - 1)
        def _():
            o_ref[...] = acc[...].astype(o_ref.dtype)

    return pl.pallas_call(
        body, name=name, grid=(nj, nm),
        in_specs=[pl.BlockSpec((None, tm, k), (lambda j, s: (j, s, 0)) if ja > 1 else (lambda j, s: (0, s, 0))),
                  pl.BlockSpec((None, tm, n), (lambda j, s: (j, s, 0)) if jb > 1 else (lambda j, s: (0, s, 0)))],
        out_specs=pl.BlockSpec((None, k, n), lambda j, s: (j, 0, 0)),
        out_shape=_sds((nj, k, n), out_dtype),
        scratch_shapes=[pltpu.VMEM((k, n), F32)],
        compiler_params=_cp("parallel", "arbitrary"),
    )(a, b)


def _mix_in(x, ln, w_in, name):
    m, d = x.shape
    n_in = w_in.shape[1]
    tm = _row_tile(m)

    def body(x_ref, ln_ref, w_ref, h_ref, zq_ref, zg_ref):
        h = _rms_fwd(x_ref[...], ln_ref[...]).astype(BF16)
        h_ref[...] = h
        zq_ref[...] = jnp.dot(h, w_ref[:, :QKV_END], preferred_element_type=F32)
        zg_ref[...] = jnp.dot(h, w_ref[:, QKV_END:], preferred_element_type=F32)

    return pl.pallas_call(
        body, name=name, grid=(m // tm,),
        in_specs=[pl.BlockSpec((tm, d), lambda i: (i, 0)), _resident((1, d), (0, 0)), _resident((d, n_in), (0, 0))],
        out_specs=[pl.BlockSpec((tm, d), lambda i: (i, 0)), pl.BlockSpec((tm, QKV_END), lambda i: (i, 0)),
                   pl.BlockSpec((tm, n_in - QKV_END), lambda i: (i, 0))],
        out_shape=[_sds((m, d), BF16), _sds((m, QKV_END), F32), _sds((m, n_in - QKV_END), F32)],
        compiler_params=_cp("parallel"),
    )(x, ln, w_in)


def _pool_fwd(zq, pool_w, scale, name):
    m = zq.shape[0]
    tm = _row_tile(m)
    halo_blocks = tm // POOL_HALO

    def body(zc_ref, zh_ref, pw_ref, sc_ref, d_ref, p_ref):
        i = pl.program_id(0)
        halo = jnp.where(i > 0, zh_ref[...], 0.0)
        ext = jnp.concatenate([halo, zc_ref[...]], axis=0)
        t = i * tm + lax.broadcasted_iota(jnp.int32, (tm, 1), 0)
        for g, w in enumerate(POOL_WINDOWS):
            lanes = slice(g * GROUP_DIM, (g + 1) * GROUP_DIM)
            e = ext[:, lanes]
            s, k = e, 1
            while k < w:
                s = s + pltpu.roll(s, k, 0)
                k *= 2
            cnt = jnp.minimum(t + 1, w).astype(F32)
            dg = (s[POOL_HALO:, :] / cnt - e[POOL_HALO:, :]).astype(BF16)
            y = jnp.dot(dg, pw_ref[g].astype(BF16), preferred_element_type=F32)
            d_ref[:, lanes] = dg
            p_ref[:, lanes] = (y * sc_ref[:, lanes]).astype(BF16)

    return pl.pallas_call(
        body, name=name, grid=(m // tm,),
        in_specs=[pl.BlockSpec((tm, POOL_DIM), lambda i: (i, 0)),
                  pl.BlockSpec((POOL_HALO, POOL_DIM), lambda i: (jnp.maximum(i * halo_blocks - 1, 0), 0)),
                  _resident((4, GROUP_DIM, GROUP_DIM), (0, 0, 0)), _resident((1, POOL_DIM), (0, 0))],
        out_specs=[pl.BlockSpec((tm, POOL_DIM), lambda i: (i, 0)), pl.BlockSpec((tm, POOL_DIM), lambda i: (i, 0))],
        out_shape=[_sds((m, POOL_DIM), BF16), _sds((m, POOL_DIM), BF16)],
        compiler_params=_cp("parallel"),
    )(zq, zq, pool_w, scale)


def _pool_bwd(dp, d, pool_w, scale, name):
    m = dp.shape[0]
    tm = _row_tile(m)
    nb = m // tm
    halo_blocks = tm // POOL_HALO
    rows = tm + POOL_HALO

    def body(dpc_ref, dph_ref, d_ref, pw_ref, sc_ref, du_ref, dpw_ref, dsc_ref):
        i = pl.program_id(0)

        @pl.when(i == 0)
        def _():
            dpw_ref[...] = jnp.zeros_like(dpw_ref)
            dsc_ref[...] = jnp.zeros_like(dsc_ref)

        halo = jnp.where(i < nb - 1, dph_ref[...], 0.0)
        dpc = dpc_ref[...]
        ext = jnp.concatenate([dpc, halo], axis=0)
        t = i * tm + lax.broadcasted_iota(jnp.int32, (rows, 1), 0)
        for g, w in enumerate(POOL_WINDOWS):
            lanes = slice(g * GROUP_DIM, (g + 1) * GROUP_DIM)
            pwb = pw_ref[g].astype(BF16)
            dyb = (ext[:, lanes] * sc_ref[:, lanes]).astype(BF16)
            dd = lax.dot_general(dyb, pwb, _NT, preferred_element_type=F32)
            cnt = jnp.minimum(t + 1, w).astype(F32)
            s, k = dd / cnt, 1
            while k < w:
                s = s + pltpu.roll(s, rows - k, 0)
                k *= 2
            du_ref[:, lanes] = (s[:tm, :] - dd[:tm, :]).astype(BF16)
            dcur = d_ref[:, lanes]
            y = jnp.dot(dcur, pwb, preferred_element_type=F32)
            dsc_ref[:, lanes] += jnp.sum(dpc[:, lanes] * y, axis=0, keepdims=True)
            dpw_ref[g] += lax.dot_general(dcur, dyb[:tm, :], _TN, preferred_element_type=F32)

    return pl.pallas_call(
        body, name=name, grid=(nb,),
        in_specs=[pl.BlockSpec((tm, POOL_DIM), lambda i: (i, 0)),
                  pl.BlockSpec((POOL_HALO, POOL_DIM), lambda i: (jnp.minimum((i + 1) * halo_blocks, nb * halo_blocks - 1), 0)),
                  pl.BlockSpec((tm, POOL_DIM), lambda i: (i, 0)),
                  _resident((4, GROUP_DIM, GROUP_DIM), (0, 0, 0)), _resident((1, POOL_DIM), (0, 0))],
        out_specs=[pl.BlockSpec((tm, POOL_DIM), lambda i: (i, 0)),
                   pl.BlockSpec((4, GROUP_DIM, GROUP_DIM), lambda i: (0, 0, 0)),
                   pl.BlockSpec((1, POOL_DIM), lambda i: (0, 0))],
        out_shape=[_sds((m, POOL_DIM), BF16), _sds((4, GROUP_DIM, GROUP_DIM), F32), _sds((1, POOL_DIM), F32)],
        compiler_params=_cp("arbitrary"),
    )(dp, dp, d, pool_w, scale)


def _qk_norm_rope(xv, gain, cos, sin):
    width = xv.shape[1]
    r = lax.rsqrt(_segment_mean(xv * xv, width) + EPS)
    y = xv * r * gain
    return y * _tile_lanes(cos, width) + _rope_partner(y) * _tile_lanes(sin, width)


def _qk_prep(zq, qgain, kgain, cos, sin, name):
    m = zq.shape[0]
    tm = _row_tile(m)

    def body(q_ref, kv_ref, qg_ref, kg_ref, cos_ref, sin_ref, qr_ref, kr_ref, v_ref):
        cos_v, sin_v = cos_ref[...], sin_ref[...]
        qr_ref[...] = _qk_norm_rope(q_ref[...], qg_ref[...], cos_v, sin_v).astype(BF16)
        kv = kv_ref[...]
        kr_ref[...] = _qk_norm_rope(kv[:, :KV_DIM], kg_ref[...], cos_v, sin_v).astype(BF16)
        v_ref[...] = kv[:, KV_DIM:].astype(BF16)

    return pl.pallas_call(
        body, name=name, grid=(m // tm,),
        in_specs=[pl.BlockSpec((tm, ATTN_DIM), lambda i: (i, 1)), pl.BlockSpec((tm, 2 * KV_DIM), lambda i: (i, 4)),
                  _resident((1, ATTN_DIM), (0, 0)), _resident((1, KV_DIM), (0, 0)),
                  pl.BlockSpec((tm, 128), lambda i: (i, 0)), pl.BlockSpec((tm, 128), lambda i: (i, 0))],
        out_specs=[pl.BlockSpec((tm, ATTN_DIM), lambda i: (i, 0)), pl.BlockSpec((tm, KV_DIM), lambda i: (i, 0)),
                   pl.BlockSpec((tm, KV_DIM), lambda i: (i, 0))],
        out_shape=[_sds((m, ATTN_DIM), BF16), _sds((m, KV_DIM), BF16), _sds((m, KV_DIM), BF16)],
        compiler_params=_cp("parallel"),
    )(zq, zq, qgain, kgain, cos, sin)


def _band_mask(n):
    qi = lax.broadcasted_iota(jnp.int32, (ATTN_BLOCK, 2 * ATTN_BLOCK), 0)
    ki = lax.broadcasted_iota(jnp.int32, (ATTN_BLOCK, 2 * ATTN_BLOCK), 1)
    diff = qi + ATTN_BLOCK - ki
    return (diff >= 0) & (diff < ATTN_BLOCK) & ((ki >= ATTN_BLOCK) | (n > 0))


def _head_probs(qh, kh, mask, sink):
    s = lax.dot_general(qh, kh, _NT, preferred_element_type=F32) * ATTN_SCALE
    s = jnp.where(mask, s, NEG_BIG)
    mx = jnp.maximum(jnp.max(s, axis=-1, keepdims=True), sink)
    p = jnp.exp(s - mx)
    es = jnp.exp(sink - mx)
    den = jnp.sum(p, axis=-1, keepdims=True) + es
    return p / den, es / den


def _attn_fwd(qr, kr, vb, sinks, name):
    m = qr.shape[0]
    nb = m // ATTN_BLOCK

    def body(q_ref, kp_ref, kc_ref, vp_ref, vc_ref, sk_ref, o_ref):
        mask = _band_mask(pl.program_id(0))
        qv = q_ref[...]
        kk = jnp.concatenate([kp_ref[...], kc_ref[...]], axis=0)
        vv = jnp.concatenate([vp_ref[...], vc_ref[...]], axis=0)
        outs = []
        for h in range(N_Q_HEADS):
            kv_lanes = slice((h // GQA_GROUP) * HEAD_DIM, (h // GQA_GROUP + 1) * HEAD_DIM)
            p, _ = _head_probs(qv[:, h * HEAD_DIM:(h + 1) * HEAD_DIM], kk[:, kv_lanes], mask, sk_ref[h])
            outs.append(jnp.dot(p.astype(BF16), vv[:, kv_lanes], preferred_element_type=F32))
        o_ref[...] = jnp.concatenate(outs, axis=1).astype(BF16)

    prev = lambda n: (jnp.maximum(n - 1, 0), 0)
    cur = lambda n: (n, 0)
    return pl.pallas_call(
        body, name=name, grid=(nb,),
        in_specs=[pl.BlockSpec((ATTN_BLOCK, ATTN_DIM), cur),
                  pl.BlockSpec((ATTN_BLOCK, KV_DIM), prev), pl.BlockSpec((ATTN_BLOCK, KV_DIM), cur),
                  pl.BlockSpec((ATTN_BLOCK, KV_DIM), prev), pl.BlockSpec((ATTN_BLOCK, KV_DIM), cur),
                  pl.BlockSpec(memory_space=pltpu.SMEM)],
        out_specs=pl.BlockSpec((ATTN_BLOCK, ATTN_DIM), cur),
        out_shape=_sds((m, ATTN_DIM), BF16),
        compiler_params=_cp("parallel"),
    )(qr, kr, kr, vb, vb, sinks)


def _attn_bwd(do, qr, kr, vb, sinks, name):
    m = qr.shape[0]
    nb = m // ATTN_BLOCK

    def body(do_ref, q_ref, kp_ref, kc_ref, vp_ref, vc_ref, sk_ref, dq_ref, dk_ref, dv_ref, ds_ref, carry_k, carry_v):
        n = pl.program_id(0)

        @pl.when(n == 0)
        def _():
            carry_k[...] = jnp.zeros_like(carry_k)
            carry_v[...] = jnp.zeros_like(carry_v)
            ds_ref[...] = jnp.zeros_like(ds_ref)

        @pl.when(n < nb)
        def _():
            mask = _band_mask(n)
            qv = q_ref[...]
            dov = do_ref[...]
            kk = jnp.concatenate([kp_ref[...], kc_ref[...]], axis=0)
            vv = jnp.concatenate([vp_ref[...], vc_ref[...]], axis=0)
            lane = lax.broadcasted_iota(jnp.int32, (1, 128), 1)
            dsink = jnp.zeros((1, 128), F32)
            dqs, dks, dvs = [], [], []
            for kvh in range(N_KV_HEADS):
                kv_lanes = slice(kvh * HEAD_DIM, (kvh + 1) * HEAD_DIM)
                kh, vh = kk[:, kv_lanes], vv[:, kv_lanes]
                dkh = jnp.zeros((2 * ATTN_BLOCK, HEAD_DIM), F32)
                dvh = jnp.zeros((2 * ATTN_BLOCK, HEAD_DIM), F32)
                for g in range(GQA_GROUP):
                    h = kvh * GQA_GROUP + g
                    q_lanes = slice(h * HEAD_DIM, (h + 1) * HEAD_DIM)
                    qh = qv[:, q_lanes]
                    doh = dov[:, q_lanes].astype(BF16)
                    p, ps = _head_probs(qh, kh, mask, sk_ref[h])
                    dpr = lax.dot_general(doh, vh, _NT, preferred_element_type=F32)
                    delta = jnp.sum(p * dpr, axis=-1, keepdims=True)
                    dsb = (p * (dpr - delta) * ATTN_SCALE).astype(BF16)
                    dsink = dsink + jnp.where(lane == h, -jnp.sum(ps * delta), 0.0)
                    dqs.append(jnp.dot(dsb, kh, preferred_element_type=F32))
                    dkh = dkh + lax.dot_general(dsb, qh, _TN, preferred_element_type=F32)
                    dvh = dvh + lax.dot_general(p.astype(BF16), doh, _TN, preferred_element_type=F32)
                dks.append(dkh)
                dvs.append(dvh)
            dq_ref[...] = jnp.concatenate(dqs, axis=1)
            dkk = jnp.concatenate(dks, axis=1)
            dvv = jnp.concatenate(dvs, axis=1)
            dk_ref[...] = carry_k[...] + dkk[:ATTN_BLOCK, :]
            dv_ref[...] = carry_v[...] + dvv[:ATTN_BLOCK, :]
            carry_k[...] = dkk[ATTN_BLOCK:, :]
            carry_v[...] = dvv[ATTN_BLOCK:, :]
            ds_ref[...] += dsink

        @pl.when(n == nb)
        def _():
            dk_ref[...] = carry_k[...]
            dv_ref[...] = carry_v[...]

    cur = lambda n: (jnp.minimum(n, nb - 1), 0)
    prev = lambda n: (jnp.clip(n - 1, 0, nb - 1), 0)
    return pl.pallas_call(
        body, name=name, grid=(nb + 1,),
        in_specs=[pl.BlockSpec((ATTN_BLOCK, ATTN_DIM), cur), pl.BlockSpec((ATTN_BLOCK, ATTN_DIM), cur),
                  pl.BlockSpec((ATTN_BLOCK, KV_DIM), prev), pl.BlockSpec((ATTN_BLOCK, KV_DIM), cur),
                  pl.BlockSpec((ATTN_BLOCK, KV_DIM), prev), pl.BlockSpec((ATTN_BLOCK, KV_DIM), cur),
                  pl.BlockSpec(memory_space=pltpu.SMEM)],
        out_specs=[pl.BlockSpec((ATTN_BLOCK, ATTN_DIM), cur), pl.BlockSpec((ATTN_BLOCK, KV_DIM), prev),
                   pl.BlockSpec((ATTN_BLOCK, KV_DIM), prev), pl.BlockSpec((1, 128), lambda n: (0, 0))],
        out_shape=[_sds((m, ATTN_DIM), F32), _sds((m, KV_DIM), F32), _sds((m, KV_DIM), F32), _sds((1, 128), F32)],
        scratch_shapes=[pltpu.VMEM((ATTN_BLOCK, KV_DIM), F32), pltpu.VMEM((ATTN_BLOCK, KV_DIM), F32)],
        compiler_params=_cp("arbitrary"),
    )(do, qr, kr, kr, vb, vb, sinks)


def _qk_norm_rope_bwd(dout, xv, gain, cos, sin):
    width = xv.shape[1]
    r = lax.rsqrt(_segment_mean(xv * xv, width) + EPS)
    xn = xv * r
    dy = dout * _tile_lanes(cos, width) + _rope_partner(dout * _tile_lanes(sin, width))
    dxn = dy * gain
    dx = r * (dxn - xn * _segment_mean(dxn * xn, width))
    return dx, jnp.sum(dy * xn, axis=0, keepdims=True)


def _fold_heads(v):
    out = v[:, :HEAD_DIM]
    for h in range(1, v.shape[1] // HEAD_DIM):
        out = out + v[:, h * HEAD_DIM:(h + 1) * HEAD_DIM]
    return out


def _qk_bwd(dq, dk, dv, zq, qgain, kgain, cos, sin, name):
    m = zq.shape[0]
    tm = _row_tile(m)

    def body(dq_ref, dk_ref, dv_ref, q_ref, kv_ref, qg_ref, kg_ref, cos_ref, sin_ref, dz_ref, dqg_ref, dkg_ref):
        @pl.when(pl.program_id(0) == 0)
        def _():
            dqg_ref[...] = jnp.zeros_like(dqg_ref)
            dkg_ref[...] = jnp.zeros_like(dkg_ref)

        cos_v, sin_v = cos_ref[...], sin_ref[...]
        dxq, dgq = _qk_norm_rope_bwd(dq_ref[...], q_ref[...], qg_ref[...], cos_v, sin_v)
        dxk, dgk = _qk_norm_rope_bwd(dk_ref[...], kv_ref[:, :KV_DIM], kg_ref[...], cos_v, sin_v)
        dz_ref[:, :ATTN_DIM] = dxq.astype(BF16)
        dz_ref[:, ATTN_DIM:ATTN_DIM + KV_DIM] = dxk.astype(BF16)
        dz_ref[:, ATTN_DIM + KV_DIM:] = dv_ref[...].astype(BF16)
        dqg_ref[...] += _fold_heads(dgq)
        dkg_ref[...] += _fold_heads(dgk)

    row = lambda i: (i, 0)
    return pl.pallas_call(
        body, name=name, grid=(m // tm,),
        in_specs=[pl.BlockSpec((tm, ATTN_DIM), row), pl.BlockSpec((tm, KV_DIM), row), pl.BlockSpec((tm, KV_DIM), row),
                  pl.BlockSpec((tm, ATTN_DIM), lambda i: (i, 1)), pl.BlockSpec((tm, 2 * KV_DIM), lambda i: (i, 4)),
                  _resident((1, ATTN_DIM), (0, 0)), _resident((1, KV_DIM), (0, 0)),
                  pl.BlockSpec((tm, 128), row), pl.BlockSpec((tm, 128), row)],
        out_specs=[pl.BlockSpec((tm, ATTN_DIM + 2 * KV_DIM), row), pl.BlockSpec((1, HEAD_DIM), lambda i: (0, 0)),
                   pl.BlockSpec((1, HEAD_DIM), lambda i: (0, 0))],
        out_shape=[_sds((m, ATTN_DIM + 2 * KV_DIM), BF16), _sds((1, HEAD_DIM), F32), _sds((1, HEAD_DIM), F32)],
        compiler_params=_cp("arbitrary"),
    )(dq, dk, dv, zq, zq, qgain, kgain, cos, sin)


def _merge_fwd(x, p, o, zg, w_pb, w_ab, w_out, name):
    m, d = x.shape
    tm = _row_tile(m)

    def body(x_ref, p_ref, o_ref, zg_ref, wpb_ref, wab_ref, wo_ref, xo_ref, mix_ref):
        a = jnp.dot(p_ref[...], wpb_ref[...], preferred_element_type=F32)
        b = jnp.dot(o_ref[...], wab_ref[...], preferred_element_type=F32)
        mix = (jax.nn.sigmoid(zg_ref[:, :d]) * a + jax.nn.sigmoid(zg_ref[:, d:]) * b).astype(BF16)
        mix_ref[...] = mix
        xo_ref[...] = x_ref[...] + jnp.dot(mix, wo_ref[...], preferred_element_type=F32)

    row = lambda i: (i, 0)
    return pl.pallas_call(
        body, name=name, grid=(m // tm,),
        in_specs=[pl.BlockSpec((tm, d), row), pl.BlockSpec((tm, POOL_DIM), row), pl.BlockSpec((tm, ATTN_DIM), row),
                  pl.BlockSpec((tm, 2 * d), row), _resident((POOL_DIM, d), (0, 0)), _resident((ATTN_DIM, d), (0, 0)),
                  _resident((d, d), (0, 0))],
        out_specs=[pl.BlockSpec((tm, d), row)] * 2,
        out_shape=[_sds((m, d), F32), _sds((m, d), BF16)],
        compiler_params=_cp("parallel"),
    )(x, p, o, zg, w_pb, w_ab, w_out)


def _merge_bwd(dy, p, o, zg, w_out, w_pb, w_ab, name):
    m, d = dy.shape
    tm = _row_tile(m)

    def body(dy_ref, p_ref, o_ref, zg_ref, wo_ref, wpb_ref, wab_ref, dyb_ref, da_ref, db_ref, dp_ref, do_ref, dzg_ref):
        dyb = dy_ref[...].astype(BF16)
        dyb_ref[...] = dyb
        dmix = lax.dot_general(dyb, wo_ref[...], _NT, preferred_element_type=F32)
        gp = jax.nn.sigmoid(zg_ref[:, :d])
        ga = jax.nn.sigmoid(zg_ref[:, d:])
        da = (dmix * gp).astype(BF16)
        db = (dmix * ga).astype(BF16)
        da_ref[...] = da
        db_ref[...] = db
        a = jnp.dot(p_ref[...], wpb_ref[...], preferred_element_type=F32)
        b = jnp.dot(o_ref[...], wab_ref[...], preferred_element_type=F32)
        dzg_ref[:, :d] = (dmix * a * gp * (1.0 - gp)).astype(BF16)
        dzg_ref[:, d:] = (dmix * b * ga * (1.0 - ga)).astype(BF16)
        dp_ref[...] = lax.dot_general(da, wpb_ref[...], _NT, preferred_element_type=F32)
        do_ref[...] = lax.dot_general(db, wab_ref[...], _NT, preferred_element_type=F32)

    row = lambda i: (i, 0)
    return pl.pallas_call(
        body, name=name, grid=(m // tm,),
        in_specs=[pl.BlockSpec((tm, d), row), pl.BlockSpec((tm, POOL_DIM), row), pl.BlockSpec((tm, ATTN_DIM), row),
                  pl.BlockSpec((tm, 2 * d), row), _resident((d, d), (0, 0)), _resident((POOL_DIM, d), (0, 0)),
                  _resident((ATTN_DIM, d), (0, 0))],
        out_specs=[pl.BlockSpec((tm, d), row), pl.BlockSpec((tm, d), row), pl.BlockSpec((tm, d), row),
                   pl.BlockSpec((tm, POOL_DIM), row), pl.BlockSpec((tm, ATTN_DIM), row), pl.BlockSpec((tm, 2 * d), row)],
        out_shape=[_sds((m, d), BF16), _sds((m, d), BF16), _sds((m, d), BF16), _sds((m, POOL_DIM), F32),
                   _sds((m, ATTN_DIM), F32), _sds((m, 2 * d), BF16)],
        compiler_params=_cp("parallel"),
    )(dy, p, o, zg, w_out, w_pb, w_ab)


def _mix_bwd_x(dy, x, ln, dzp, dzqkv, dzg, w_in, name):
    m, d = dy.shape
    n_in = w_in.shape[1]
    tm = _row_tile(m)

    def body(dy_ref, x_ref, ln_ref, dzp_ref, dzq_ref, dzg_ref, w_ref, dx_ref, dln_ref):
        @pl.when(pl.program_id(0) == 0)
        def _():
            dln_ref[...] = jnp.zeros_like(dln_ref)

        dh = lax.dot_general(dzp_ref[...], w_ref[:, :POOL_DIM], _NT, preferred_element_type=F32)
        dh += lax.dot_general(dzq_ref[...], w_ref[:, POOL_DIM:QKV_END], _NT, preferred_element_type=F32)
        dh += lax.dot_general(dzg_ref[...], w_ref[:, QKV_END:], _NT, preferred_element_type=F32)
        dx, dgain = _rms_bwd(dh, x_ref[...], ln_ref[...])
        dx_ref[...] = dy_ref[...] + dx
        dln_ref[...] += jnp.sum(dgain, axis=0, keepdims=True)

    row = lambda i: (i, 0)
    return pl.pallas_call(
        body, name=name, grid=(m // tm,),
        in_specs=[pl.BlockSpec((tm, d), row), pl.BlockSpec((tm, d), row), _resident((1, d), (0, 0)),
                  pl.BlockSpec((tm, POOL_DIM), row), pl.BlockSpec((tm, QKV_END - POOL_DIM), row),
                  pl.BlockSpec((tm, n_in - QKV_END), row), _resident((d, n_in), (0, 0))],
        out_specs=[pl.BlockSpec((tm, d), row), pl.BlockSpec((1, d), lambda i: (0, 0))],
        out_shape=[_sds((m, d), F32), _sds((1, d), F32)],
        compiler_params=_cp("arbitrary"),
    )(dy, x, ln, dzp, dzqkv, dzg, w_in)


def _loss_head(y, target, name):
    m, d = y.shape
    tm = _row_tile(m)

    def body(y_ref, t_ref, loss_ref, dy_ref):
        @pl.when(pl.program_id(0) == 0)
        def _():
            loss_ref[...] = jnp.zeros_like(loss_ref)

        diff = y_ref[...] - t_ref[...]
        dy_ref[...] = diff * (1.0 / d)
        loss_ref[...] += 0.5 * jnp.sum(jnp.mean(diff * diff, axis=-1, keepdims=True), axis=0, keepdims=True)

    row = lambda i: (i, 0)
    return pl.pallas_call(
        body, name=name, grid=(m // tm,),
        in_specs=[pl.BlockSpec((tm, d), row), pl.BlockSpec((tm, d), row)],
        out_specs=[pl.BlockSpec((1, 1), lambda i: (0, 0)), pl.BlockSpec((tm, d), row)],
        out_shape=[_sds((1, 1), F32), _sds((m, d), F32)],
        compiler_params=_cp("arbitrary"),
    )(y, target)


def _adamw_math(g, w, m, v):
    m2 = ADAM_B1 * m + (1.0 - ADAM_B1) * g
    v2 = ADAM_B2 * v + (1.0 - ADAM_B2) * (g * g)
    m_hat = m2 / (1.0 - ADAM_B1 ** ADAM_STEP)
    v_hat = v2 / (1.0 - ADAM_B2 ** ADAM_STEP)
    delta = -ADAM_LR * (m_hat / (jnp.sqrt(v_hat) + ADAM_EPS) + ADAM_WD * w)
    return delta, m2, v2


def _sum_parts(parts_ref):
    g = parts_ref[0].astype(F32)
    for s in range(1, N_DEV):
        g = g + parts_ref[s].astype(F32)
    return g


def _adamw_sharded(parts, w, m, v, name):
    n_layers, rows, cols = w.shape
    tr = next(t for t in (512, 256, 128, 64, 32, 16, 8) if rows % t == 0 and t * cols <= ADAMW_BLOCK_ELEMS)
    nr = rows // tr

    def body(*refs):
        part_refs = refs[:n_layers]
        w_ref, m_ref, v_ref, g_out, d_out, m_out, v_out = refs[n_layers:]
        layer = pl.program_id(0)
        for l in range(n_layers):
            @pl.when(layer == l)
            def _(l=l):
                g = _sum_parts(part_refs[l])
                delta, m2, v2 = _adamw_math(g, w_ref[...], m_ref[...], v_ref[...])
                g_out[...] = g
                d_out[...] = delta
                m_out[...] = m2
                v_out[...] = v2

    def part_map(l):
        return lambda layer, r: (0, jnp.where(layer == l, r, jnp.where(layer < l, 0, nr - 1)), 0)

    wspec = pl.BlockSpec((None, tr, cols), lambda layer, r: (layer, r, 0))
    return pl.pallas_call(
        body, name=name, grid=(n_layers, nr),
        in_specs=[pl.BlockSpec((N_DEV, tr, cols), part_map(l)) for l in range(n_layers)] + [wspec] * 3,
        out_specs=[wspec] * 4,
        out_shape=[_sds(w.shape, F32)] * 4,
        compiler_params=_cp("arbitrary", "arbitrary"),
    )(*parts, w, m, v)


def _adamw_packed(parts, w, m, v, name):
    def body(p_ref, w_ref, m_ref, v_ref, g_out, d_out, m_out, v_out):
        g = _sum_parts(p_ref)
        delta, m2, v2 = _adamw_math(g, w_ref[...], m_ref[...], v_ref[...])
        g_out[...] = g
        d_out[...] = delta
        m_out[...] = m2
        v_out[...] = v2

    return pl.pallas_call(
        body, name=name, out_shape=[_sds(w.shape, F32)] * 4,
        compiler_params=pltpu.CompilerParams(vmem_limit_bytes=VMEM_LIMIT_BYTES),
    )(parts, w, m, v)


_SMALL = ("ln_ffn1", "ln_mix", "pool_w", "pool_scale", "q_norm", "k_norm", "sinks", "ln_ffn2")


def _pack_small(arrs):
    rows = []
    for a in arrs:
        flat = a.reshape(-1)
        pad = (-flat.shape[0]) % 1024
        rows.append(jnp.pad(flat, (0, pad)).reshape(-1, 128))
    return jnp.concatenate(rows, axis=0)


def _unpack_small(packed, like):
    out, r0 = [], 0
    for a in like:
        size = a.size
        nrows = (size + 1023) // 1024 * 8
        out.append(packed[r0:r0 + nrows].reshape(-1)[:size].reshape(a.shape))
        r0 += nrows
    return out


def _rope_tables(m):
    pos = jnp.arange(m, dtype=F32)
    inv_freq = ROPE_THETA ** (-jnp.arange(0, ROT_DIM, 2, dtype=F32) / ROT_DIM)
    ang = pos[:, None] * inv_freq[None, :]
    cos8, sin8 = jnp.cos(ang), jnp.sin(ang)
    rest = HEAD_DIM - ROT_DIM
    cos64 = jnp.concatenate([cos8, cos8, jnp.ones((m, rest), F32)], axis=1)
    sin64 = jnp.concatenate([-sin8, sin8, jnp.zeros((m, rest), F32)], axis=1)
    return jnp.tile(cos64, (1, 2)), jnp.tile(sin64, (1, 2))


def _to_shard_major_cols(w):
    k = w.shape[0]
    return w.reshape(k, N_DEV, -1).transpose(1, 0, 2)


def _from_shard_major_cols(w):
    return w.transpose(1, 0, 2).reshape(w.shape[1], -1)


def kernel(x, ln_ffn1, w_ffn1_gu, w_ffn1_down, ln_mix, w_in, pool_w, pool_scale, w_pool_branch, q_norm, k_norm, sinks, w_attn_branch, w_out, ln_ffn2, w_ffn2_gu, w_ffn2_down, loss_target, m_ln_ffn1, m_w_ffn1_gu, m_w_ffn1_down, m_ln_mix, m_w_in, m_pool_w, m_pool_scale, m_w_pool_branch, m_q_norm, m_k_norm, m_sinks, m_w_attn_branch, m_w_out, m_ln_ffn2, m_w_ffn2_gu, m_w_ffn2_down, v_ln_ffn1, v_w_ffn1_gu, v_w_ffn1_down, v_ln_mix, v_w_in, v_pool_w, v_pool_scale, v_w_pool_branch, v_q_norm, v_k_norm, v_sinks, v_w_attn_branch, v_w_out, v_ln_ffn2, v_w_ffn2_gu, v_w_ffn2_down):
    weights = dict(ln_ffn1=ln_ffn1, w_ffn1_gu=w_ffn1_gu, w_ffn1_down=w_ffn1_down, ln_mix=ln_mix, w_in=w_in, pool_w=pool_w,
                   pool_scale=pool_scale, w_pool_branch=w_pool_branch, q_norm=q_norm, k_norm=k_norm, sinks=sinks,
                   w_attn_branch=w_attn_branch, w_out=w_out, ln_ffn2=ln_ffn2, w_ffn2_gu=w_ffn2_gu, w_ffn2_down=w_ffn2_down)
    mom_m = dict(ln_ffn1=m_ln_ffn1, w_ffn1_gu=m_w_ffn1_gu, w_ffn1_down=m_w_ffn1_down, ln_mix=m_ln_mix, w_in=m_w_in,
                 pool_w=m_pool_w, pool_scale=m_pool_scale, w_pool_branch=m_w_pool_branch, q_norm=m_q_norm, k_norm=m_k_norm,
                 sinks=m_sinks, w_attn_branch=m_w_attn_branch, w_out=m_w_out, ln_ffn2=m_ln_ffn2, w_ffn2_gu=m_w_ffn2_gu,
                 w_ffn2_down=m_w_ffn2_down)
    mom_v = dict(ln_ffn1=v_ln_ffn1, w_ffn1_gu=v_w_ffn1_gu, w_ffn1_down=v_w_ffn1_down, ln_mix=v_ln_mix, w_in=v_w_in,
                 pool_w=v_pool_w, pool_scale=v_pool_scale, w_pool_branch=v_w_pool_branch, q_norm=v_q_norm, k_norm=v_k_norm,
                 sinks=v_sinks, w_attn_branch=v_w_attn_branch, w_out=v_w_out, ln_ffn2=v_ln_ffn2, w_ffn2_gu=v_w_ffn2_gu,
                 w_ffn2_down=v_w_ffn2_down)
    order = ("ln_ffn1", "w_ffn1_gu", "w_ffn1_down", "ln_mix", "w_in", "pool_w", "pool_scale", "w_pool_branch", "q_norm",
             "k_norm", "sinks", "w_attn_branch", "w_out", "ln_ffn2", "w_ffn2_gu", "w_ffn2_down")
    big = ("w_ffn1_gu", "w_ffn1_down", "w_in", "w_pool_branch", "w_attn_branch", "w_out", "w_ffn2_gu", "w_ffn2_down")

    n_layers = ln_ffn1.shape[0]
    seq, d = x.shape[-2], x.shape[-1]
    xs = x.reshape(seq, d)
    target = loss_target.reshape(seq, d)
    cos, sin = _rope_tables(seq)

    gathered = []
    for l in range(n_layers):
        full = _all_gather_many([weights[k][l].astype(BF16) for k in big], name=f"gather_weights_l{l}")
        g = dict(zip(big, full))
        c_ff = g["w_ffn1_gu"].shape[-1]
        lw = dict(
            gu1=g["w_ffn1_gu"].reshape(2, 4, d, c_ff), down1=g["w_ffn1_down"].reshape(4, c_ff, d),
            gu2=g["w_ffn2_gu"].reshape(2, 4, d, c_ff), down2=g["w_ffn2_down"].reshape(4, c_ff, d),
            w_in=_from_shard_major_cols(g["w_in"]), w_pb=_from_shard_major_cols(g["w_pool_branch"]),
            w_ab=_from_shard_major_cols(g["w_attn_branch"]), w_out=g["w_out"].reshape(d, d),
            ln1=ln_ffn1[l][None], ln_mix=ln_mix[l][None], ln2=ln_ffn2[l][None], pool_w=pool_w[l],
            pool_scale=pool_scale[l][None], sinks=sinks[l],
            qgain=jnp.tile(q_norm[l], N_Q_HEADS)[None], kgain=jnp.tile(k_norm[l], N_KV_HEADS)[None])
        gathered.append(lw)

    saved = []
    cur = xs
    for l in range(n_layers):
        lw = gathered[l]
        s = dict(x0=cur)
        s["h1"], s["gu1"], act1 = _ffn_up(cur, lw["ln1"], lw["gu1"], name=f"ffn1_up_l{l}")
        s["act1"] = act1
        x1 = _ffn_down(cur, act1, lw["down1"], name=f"ffn1_down_l{l}")
        s["x1"] = x1
        s["h2"], zq, zg = _mix_in(x1, lw["ln_mix"], lw["w_in"], name=f"mix_in_l{l}")
        s["zq"], s["zg"] = zq, zg
        s["d"], s["p"] = _pool_fwd(zq, lw["pool_w"], lw["pool_scale"], name=f"pool_fwd_l{l}")
        s["qr"], s["kr"], s["vb"] = _qk_prep(zq, lw["qgain"], lw["kgain"], cos, sin, name=f"qk_prep_l{l}")
        s["o"] = _attn_fwd(s["qr"], s["kr"], s["vb"], lw["sinks"], name=f"attn_fwd_l{l}")
        x2, s["mix"] = _merge_fwd(x1, s["p"], s["o"], zg, lw["w_pb"], lw["w_ab"], lw["w_out"],
                                                  name=f"merge_fwd_l{l}")
        s["x2"] = x2
        s["h3"], s["gu2"], act2 = _ffn_up(x2, lw["ln2"], lw["gu2"], name=f"ffn2_up_l{l}")
        s["act2"] = act2
        cur = _ffn_down(x2, act2, lw["down2"], name=f"ffn2_down_l{l}")
        saved.append(s)

    loss_local, dy = _loss_head(cur, target, name="loss_head")
    loss = lax.psum(loss_local[0, 0], MESH_AXES)

    small_grads = {k: [None] * n_layers for k in _SMALL}
    received = {k: [None] * n_layers for k in big}
    for l in reversed(range(n_layers)):
        lw, s = gathered[l], saved[l]
        c_ff = lw["gu1"].shape[-1]

        def ffn_bwd(dy, xin, h, gu, act, ln, wgu, wdown, tag):
            dyh, dgu = _ffn_bwd_act(dy, gu, wdown, name=f"{tag}_bwd_act_l{l}")
            dw_down = _matmul_tn(act, dyh[None], name=f"{tag}_dw_down_l{l}")
            dw_gu = _matmul_tn(h[None], dgu.reshape(N_DEV, seq, c_ff), name=f"{tag}_dw_gu_l{l}")
            dx, dln = _ffn_bwd_x(dy, xin, ln, dgu, wgu, name=f"{tag}_bwd_x_l{l}")
            return dx, dln, dw_gu, dw_down.reshape(N_DEV, c_ff // 2, d)

        dx2, dln2, dw_gu2, dw_down2 = ffn_bwd(dy, s["x2"], s["h3"], s["gu2"], s["act2"], lw["ln2"], lw["gu2"], lw["down2"], "ffn2")

        dyb, da, db, dp, do, dzg = _merge_bwd(dx2, s["p"], s["o"], s["zg"], lw["w_out"], lw["w_pb"], lw["w_ab"],
                                              name=f"merge_bwd_l{l}")
        dw_out = _matmul_tn(s["mix"][None], dyb[None], name=f"dw_out_l{l}")[0]
        dw_pb = _matmul_tn(s["p"][None], da[None], name=f"dw_pb_l{l}")[0]
        dw_ab = _matmul_tn(s["o"][None], db[None], name=f"dw_ab_l{l}")[0]
        dzp, dpw, dsc = _pool_bwd(dp, s["d"], lw["pool_w"], lw["pool_scale"], name=f"pool_bwd_l{l}")
        dq, dk, dv, dsinks = _attn_bwd(do, s["qr"], s["kr"], s["vb"], lw["sinks"], name=f"attn_bwd_l{l}")
        dzqkv, dqg, dkg = _qk_bwd(dq, dk, dv, s["zq"], lw["qgain"], lw["kgain"], cos, sin, name=f"qk_bwd_l{l}")
        dw_in = jnp.concatenate([_matmul_tn(s["h2"][None], dzp[None], name=f"dw_in_pool_l{l}")[0],
                                 _matmul_tn(s["h2"][None], dzqkv[None], name=f"dw_in_qkv_l{l}")[0],
                                 _matmul_tn(s["h2"][None], dzg[None], name=f"dw_in_gate_l{l}")[0]], axis=1)
        dx1, dlnm = _mix_bwd_x(dx2, s["x1"], lw["ln_mix"], dzp, dzqkv, dzg, lw["w_in"], name=f"mix_bwd_x_l{l}")

        dy, dln1, dw_gu1, dw_down1 = ffn_bwd(dx1, s["x0"], s["h1"], s["gu1"], s["act1"], lw["ln1"], lw["gu1"], lw["down1"], "ffn1")

        partial = dict(w_ffn1_gu=dw_gu1, w_ffn1_down=dw_down1, w_in=_to_shard_major_cols(dw_in),
                       w_pool_branch=_to_shard_major_cols(dw_pb), w_attn_branch=_to_shard_major_cols(dw_ab),
                       w_out=dw_out.reshape(N_DEV, d // N_DEV, d), w_ffn2_gu=dw_gu2, w_ffn2_down=dw_down2)
        got = _all_to_all_many([partial[k] for k in big], name=f"exchange_grads_l{l}")
        for k, r in zip(big, got):
            received[k][l] = r
        small_grads["ln_ffn1"][l] = dln1[0]
        small_grads["ln_mix"][l] = dlnm[0]
        small_grads["ln_ffn2"][l] = dln2[0]
        small_grads["pool_w"][l] = dpw
        small_grads["pool_scale"][l] = dsc[0]
        small_grads["q_norm"][l] = dqg[0]
        small_grads["k_norm"][l] = dkg[0]
        small_grads["sinks"][l] = dsinks[0, :N_Q_HEADS]

    grad_x = dy.reshape(x.shape)

    grads, deltas, new_m, new_v = {}, {}, {}, {}
    for k in big:
        w = weights[k]
        shape2 = (n_layers, -1, w.shape[-1])
        parts = [r.reshape(N_DEV, -1, w.shape[-1]) for r in received[k]]
        outs = _adamw_sharded(parts, w.reshape(shape2), mom_m[k].reshape(shape2), mom_v[k].reshape(shape2), name=f"adamw_{k}")
        grads[k], deltas[k], new_m[k], new_v[k] = (o.reshape(w.shape) for o in outs)

    small_w = [weights[k] for k in _SMALL]
    packed_g = _pack_small([jnp.stack(small_grads[k]).reshape(weights[k].shape) for k in _SMALL])
    (parts_small,) = _all_gather_many([packed_g], name="gather_small_grads")
    outs = _adamw_packed(parts_small, _pack_small(small_w), _pack_small([mom_m[k] for k in _SMALL]),
                         _pack_small([mom_v[k] for k in _SMALL]), name="adamw_small")
    for res, o in zip((grads, deltas, new_m, new_v), outs):
        for k, a in zip(_SMALL, _unpack_small(o, small_w)):
            res[k] = a

    return (loss, grad_x, *[grads[k] for k in order], *[deltas[k] for k in order],
            *[new_m[k] for k in order], *[new_v[k] for k in order])
```

```python
import functools

import jax
import jax.numpy as jnp
from jax import lax
from jax.experimental import pallas as pl
from jax.experimental.pallas import tpu as pltpu

F32 = jnp.float32
BF16 = jnp.bfloat16

N_DEV = 8
MESH_AXES = ("x", "y", "c")
EPS = 1e-6

HEAD_DIM = 64
N_Q_HEADS = 8
N_KV_HEADS = 2
GQA_GROUP = N_Q_HEADS // N_KV_HEADS
ATTN_BLOCK = 128
ATTN_SCALE = HEAD_DIM ** -0.5
ROPE_THETA = 500000.0
ROT_DIM = 16
POOL_WINDOWS = (2, 4, 8, 16)
POOL_HALO = 16
GROUP_DIM = 128
POOL_DIM = 512
ATTN_DIM = 512
KV_DIM = 128
QKV_END = POOL_DIM + ATTN_DIM + 2 * KV_DIM

ADAM_LR = 0.001
ADAM_B1 = 0.9
ADAM_B2 = 0.999
ADAM_EPS = 1e-08
ADAM_WD = 0.01
ADAM_STEP = 10

ROW_TILE = 512
FFN_ROW_TILE = 256
VMEM_LIMIT_BYTES = 56 << 20
ADAMW_BLOCK_ELEMS = 96 * 1024
NEG_BIG = -1e30

_NT = (((1,), (1,)), ((), ()))
_TN = (((0,), (0,)), ((), ()))


def _cp(*sem):
    return pltpu.CompilerParams(dimension_semantics=sem, vmem_limit_bytes=VMEM_LIMIT_BYTES)


def _resident(block, index):
    return pl.BlockSpec(block, lambda *_: index, pipeline_mode=pl.Buffered(1))


def _row_tile(m, cap=None):
    return min(ROW_TILE if cap is None else min(ROW_TILE, cap), m)


def _sds(shape, dtype):
    return jax.ShapeDtypeStruct(shape, dtype)


def _mesh_pos():
    return lax.axis_index("x"), lax.axis_index("y"), lax.axis_index("c")


def _all_gather_many(shards, name):
    n = len(shards)

    def body(*refs):
        ins, outs = refs[:n], refs[n:2 * n]
        send_sems, recv_sems, local_sems = refs[2 * n:]
        x, y, c = _mesh_pos()
        me, sibling = (x, y, c), (x, y, 1 - c)
        chips = [(1 - x, y), (x, 1 - y), (1 - x, 1 - y)]

        def slot(a, pos):
            return outs[a].at[4 * pos[0] + 2 * pos[1] + pos[2]]

        def copy(a, k, block, to, src=None):
            return pltpu.make_async_remote_copy(
                src_ref=slot(a, block) if src is None else src, dst_ref=slot(a, block),
                send_sem=send_sems.at[a, k], recv_sem=recv_sems.at[a, k],
                device_id=to, device_id_type=pl.DeviceIdType.MESH)

        mine = [pltpu.make_async_copy(ins[a], slot(a, me), local_sems.at[a]) for a in range(n)]
        for cp in mine:
            cp.start()
        first = []
        for a in range(n):
            first.append(copy(a, 0, me, sibling, src=ins[a]))
            for j, chip in enumerate(chips):
                first.append(copy(a, 1 + j, me, (*chip, c), src=ins[a]))
        for cp in first:
            cp.start()
        passed = []
        for j, chip in enumerate(chips):
            for a in range(n):
                copy(a, 1 + j, (*chip, c), me).wait_recv()
                fwd = copy(a, 4 + j, (*chip, c), sibling)
                fwd.start()
                passed.append(fwd)
        for a in range(n):
            copy(a, 0, sibling, me).wait_recv()
        for j, chip in enumerate(chips):
            for a in range(n):
                copy(a, 4 + j, (*chip, 1 - c), me).wait_recv()
        for cp in first + passed:
            cp.wait_send()
        for cp in mine:
            cp.wait()

    any_spec = pl.BlockSpec(memory_space=pl.ANY)
    return pl.pallas_call(
        body, name=name,
        out_shape=[_sds((N_DEV,) + s.shape, s.dtype) for s in shards],
        in_specs=[any_spec] * n, out_specs=[any_spec] * n,
        scratch_shapes=[pltpu.SemaphoreType.DMA((n, 7)), pltpu.SemaphoreType.DMA((n, 7)),
                        pltpu.SemaphoreType.DMA((n,))],
    )(*shards)


def _direct_copies(src, land, send_sem, recv_sem, local_sem, scatter):
    x, y, c = _mesh_pos()
    me = 4 * x + 2 * y + c
    local = pltpu.make_async_copy(src.at[me] if scatter else src, land.at[me], local_sem)
    remote = []
    for k in range(1, N_DEV):
        px = 1 - x if k & 4 else x
        py = 1 - y if k & 2 else y
        pc = 1 - c if k & 1 else c
        remote.append(pltpu.make_async_remote_copy(
            src_ref=src.at[4 * px + 2 * py + pc] if scatter else src, dst_ref=land.at[me],
            send_sem=send_sem, recv_sem=recv_sem, device_id=(px, py, pc), device_id_type=pl.DeviceIdType.MESH))
    seven = land.at[pl.ds(0, N_DEV - 1)]
    drain = pltpu.make_async_remote_copy(src_ref=seven, dst_ref=seven, send_sem=send_sem, recv_sem=recv_sem,
                                         device_id=(x, y, c), device_id_type=pl.DeviceIdType.MESH)
    return local, remote, drain


_HBM_SPEC = pl.BlockSpec(memory_space=pltpu.HBM)
_SEM_SPEC = pl.BlockSpec(memory_space=pltpu.SEMAPHORE)
_DATAFLOW = pltpu.SideEffectType.DATAFLOW_SIDE_EFFECTING
_SEMS_PER_ARRAY = 3


def _exchange_start(srcs, scatter, name):
    n = len(srcs)
    n_sems = _SEMS_PER_ARRAY * n
    land_shapes = [s.shape if scatter else (N_DEV,) + s.shape for s in srcs]

    def body(*refs):
        ins, lands, sems = refs[:n], refs[n:2 * n], refs[2 * n:2 * n + n_sems]
        for a in range(n):
            local, remote, _ = _direct_copies(ins[a], lands[a], *sems[3 * a:3 * a + 3], scatter)
            local.start()
            for cp in remote:
                cp.start()
        refs[-1][...] = jnp.zeros_like(refs[-1])

    outs = pl.pallas_call(
        body, name=name,
        out_shape=(*[pltpu.SemaphoreType.DMA(())] * n_sems,
                   *[pltpu.HBM(s.shape, s.dtype) for s in srcs],
                   *[pltpu.HBM(shape, s.dtype) for shape, s in zip(land_shapes, srcs)],
                   _sds((8, 128), F32)),
        in_specs=[_HBM_SPEC] * (2 * n),
        out_specs=(*[_SEM_SPEC] * n_sems, *[_HBM_SPEC] * (2 * n), pl.BlockSpec(memory_space=pltpu.VMEM)),
        input_output_aliases={i: n_sems + i for i in range(2 * n)},
        compiler_params=pltpu.CompilerParams(has_side_effects=_DATAFLOW),
    )(*[pltpu.with_memory_space_constraint(s, pltpu.HBM) for s in srcs],
      *[pltpu.with_memory_space_constraint(lax.empty(shape, s.dtype), pltpu.HBM) for shape, s in zip(land_shapes, srcs)])
    return (outs[:n_sems], outs[n_sems:n_sems + n], outs[n_sems + n:n_sems + 2 * n], scatter), outs[-1]


def _exchange_wait(state, after, name):
    sems, srcs, lands, scatter = state
    n = len(srcs)
    n_sems = len(sems)

    def body(*refs):
        ins, zones, ss = refs[:n], refs[n:2 * n], refs[2 * n:2 * n + n_sems]
        for a in range(n):
            local, _, drain = _direct_copies(ins[a], zones[a], *ss[3 * a:3 * a + 3], scatter)
            drain.wait_send()
            drain.wait_recv()
            local.wait()

    outs = pl.pallas_call(
        body, name=name,
        out_shape=(*[pltpu.HBM(s.shape, s.dtype) for s in srcs], *[pltpu.HBM(z.shape, z.dtype) for z in lands]),
        in_specs=[_HBM_SPEC] * (2 * n) + [_SEM_SPEC] * n_sems + [pl.BlockSpec(memory_space=pl.ANY)],
        out_specs=[_HBM_SPEC] * (2 * n),
        input_output_aliases={i: i for i in range(2 * n)},
        compiler_params=pltpu.CompilerParams(has_side_effects=_DATAFLOW),
    )(*srcs, *lands, *sems, after)
    return outs[n:]


def _rms_fwd(xv, gain):
    r = lax.rsqrt(jnp.mean(xv * xv, axis=-1, keepdims=True) + EPS)
    return xv * r * gain


def _rms_bwd(dh, xv, gain):
    r = lax.rsqrt(jnp.mean(xv * xv, axis=-1, keepdims=True) + EPS)
    xn = xv * r
    dxn = dh * gain
    dx = r * (dxn - xn * jnp.mean(dxn * xn, axis=-1, keepdims=True))
    return dx, dh * xn


def _silu_parts(g):
    s = jax.nn.sigmoid(g)
    return g * s, s * (1.0 + g * (1.0 - s))


def _segment_mean(v, width):
    r = lax.broadcasted_iota(jnp.int32, (width, width), 0) >> 6
    c = lax.broadcasted_iota(jnp.int32, (width, width), 1) >> 6
    bd = (r == c).astype(F32)
    return jnp.dot(v, bd, precision=lax.Precision.HIGHEST, preferred_element_type=F32) * (1.0 / HEAD_DIM)


def _rope_partner(v):
    width = v.shape[1]
    half = ROT_DIM // 2
    lane = lax.broadcasted_iota(jnp.int32, v.shape, 1) & (HEAD_DIM - 1)
    up = jnp.where(lane < ROT_DIM, pltpu.roll(v, half, 1), 0.0)
    return jnp.where(lane < half, pltpu.roll(v, width - half, 1), up)


def _tile_lanes(t, width):
    return t if width == t.shape[1] else jnp.tile(t, (1, width // t.shape[1]))


def _ffn_up(x, ln, wgu, name, after=None):
    m, d = x.shape
    c = wgu.shape[-1]
    tm = _row_tile(m, FFN_ROW_TILE)
    deps = [] if after is None else [after]

    def body(*refs):
        x_ref, ln_ref, w_ref = refs[:3]
        h_ref, gu_ref, a_ref = refs[-3:]
        h = _rms_fwd(x_ref[...], ln_ref[...]).astype(BF16)
        h_ref[...] = h
        for j in range(4):
            g = jnp.dot(h, w_ref[0, j], preferred_element_type=F32)
            u = jnp.dot(h, w_ref[1, j], preferred_element_type=F32)
            gu_ref[0, j] = g
            gu_ref[1, j] = u
            a_ref[j] = (g * jax.nn.sigmoid(g) * u).astype(BF16)

    return pl.pallas_call(
        body, name=name, grid=(m // tm,),
        in_specs=[pl.BlockSpec((tm, d), lambda i: (i, 0)), _resident((1, d), (0, 0)),
                  _resident((2, 4, d, c), (0, 0, 0, 0))] + [pl.BlockSpec(memory_space=pl.ANY)] * len(deps),
        out_specs=[pl.BlockSpec((tm, d), lambda i: (i, 0)),
                   pl.BlockSpec((2, 4, tm, c), lambda i: (0, 0, i, 0)),
                   pl.BlockSpec((4, tm, c), lambda i: (0, i, 0))],
        out_shape=[_sds((m, d), BF16), _sds((2, 4, m, c), F32), _sds((4, m, c), BF16)],
        compiler_params=_cp("parallel"),
    )(x, ln, wgu, *deps)


def _ffn_down(x, act, wd, name):
    m, d = x.shape
    c = act.shape[-1]
    tm = _row_tile(m)

    def body(x_ref, a_ref, w_ref, o_ref):
        acc = jnp.dot(a_ref[0], w_ref[0], preferred_element_type=F32)
        for j in range(1, 4):
            acc += jnp.dot(a_ref[j], w_ref[j], preferred_element_type=F32)
        o_ref[...] = x_ref[...] + 0.5 * acc

    return pl.pallas_call(
        body, name=name, grid=(m // tm,),
        in_specs=[pl.BlockSpec((tm, d), lambda i: (i, 0)), pl.BlockSpec((4, tm, c), lambda i: (0, i, 0)),
                  _resident((4, c, d), (0, 0, 0))],
        out_specs=pl.BlockSpec((tm, d), lambda i: (i, 0)),
        out_shape=_sds((m, d), F32),
        compiler_params=_cp("parallel"),
    )(x, act, wd)


def _ffn_bwd_act(dy, gu, wd, name):
    m, d = dy.shape
    c = gu.shape[-1]
    tm = _row_tile(m, FFN_ROW_TILE)

    def body(dy_ref, gu_ref, w_ref, dyh_ref, dgu_ref):
        dyh = (0.5 * dy_ref[...]).astype(BF16)
        dyh_ref[...] = dyh
        for j in range(4):
            da = lax.dot_general(dyh, w_ref[j], _NT, preferred_element_type=F32)
            g = gu_ref[0, j]
            u = gu_ref[1, j]
            silu, dsilu = _silu_parts(g)
            dgu_ref[0, j] = (da * u * dsilu).astype(BF16)
            dgu_ref[1, j] = (da * silu).astype(BF16)

    return pl.pallas_call(
        body, name=name, grid=(m // tm,),
        in_specs=[pl.BlockSpec((tm, d), lambda i: (i, 0)), pl.BlockSpec((2, 4, tm, c), lambda i: (0, 0, i, 0)),
                  _resident((4, c, d), (0, 0, 0))],
        out_specs=[pl.BlockSpec((tm, d), lambda i: (i, 0)), pl.BlockSpec((2, 4, tm, c), lambda i: (0, 0, i, 0))],
        out_shape=[_sds((m, d), BF16), _sds((2, 4, m, c), BF16)],
        compiler_params=_cp("parallel"),
    )(dy, gu, wd)


def _ffn_bwd_x(dy, x, ln, dgu, wgu, name):
    m, d = dy.shape
    c = dgu.shape[-1]
    tm = _row_tile(m)

    def body(dy_ref, x_ref, ln_ref, dgu_ref, w_ref, dx_ref, dln_ref):
        @pl.when(pl.program_id(0) == 0)
        def _():
            dln_ref[...] = jnp.zeros_like(dln_ref)

        dh = None
        for half in range(2):
            for j in range(4):
                t = lax.dot_general(dgu_ref[half, j], w_ref[half, j], _NT, preferred_element_type=F32)
                dh = t if dh is None else dh + t
        dx, dgain = _rms_bwd(dh, x_ref[...], ln_ref[...])
        dx_ref[...] = dy_ref[...] + dx
        dln_ref[...] += jnp.sum(dgain, axis=0, keepdims=True)

    return pl.pallas_call(
        body, name=name, grid=(m // tm,),
        in_specs=[pl.BlockSpec((tm, d), lambda i: (i, 0)), pl.BlockSpec((tm, d), lambda i: (i, 0)),
                  _resident((1, d), (0, 0)), pl.BlockSpec((2, 4, tm, c), lambda i: (0, 0, i, 0)),
                  _resident((2, 4, d, c), (0, 0, 0, 0))],
        out_specs=[pl.BlockSpec((tm, d), lambda i: (i, 0)), pl.BlockSpec((1, d), lambda i: (0, 0))],
        out_shape=[_sds((m, d), F32), _sds((1, d), F32)],
        compiler_params=_cp("arbitrary"),
    )(dy, x, ln, dgu, wgu)


def _matmul_tn(a, b, name, out_dtype=BF16):
    ja, m, k = a.shape
    jb, _, n = b.shape
    nj = max(ja, jb)
    tm = _row_tile(m)
    nm = m // tm

    def body(a_ref, b_ref, o_ref, acc):
        step = pl.program_id(1)

        @pl.when(step == 0)
        def _():
            acc[...] = jnp.zeros_like(acc)

        acc[...] += lax.dot_general(a_ref[...], b_ref[...], _TN, preferred_element_type=F32)

        @pl.when(step == nm - 1)
        def _():
            o_ref[...] = acc[...].astype(o_ref.dtype)

    return pl.pallas_call(
        body, name=name, grid=(nj, nm),
        in_specs=[pl.BlockSpec((None, tm, k), (lambda j, s: (j, s, 0)) if ja > 1 else (lambda j, s: (0, s, 0))),
                  pl.BlockSpec((None, tm, n), (lambda j, s: (j, s, 0)) if jb > 1 else (lambda j, s: (0, s, 0)))],
        out_specs=pl.BlockSpec((None, k, n), lambda j, s: (j, 0, 0)),
        out_shape=_sds((nj, k, n), out_dtype),
        scratch_shapes=[pltpu.VMEM((k, n), F32)],
        compiler_params=_cp("parallel", "arbitrary"),
    )(a, b)


def _mix_in(x, ln, w_in, name):
    m, d = x.shape
    n_in = w_in.shape[1]
    tm = _row_tile(m)

    def body(x_ref, ln_ref, w_ref, h_ref, zq_ref, zg_ref):
        h = _rms_fwd(x_ref[...], ln_ref[...]).astype(BF16)
        h_ref[...] = h
        zq_ref[...] = jnp.dot(h, w_ref[:, :QKV_END], preferred_element_type=F32)
        zg_ref[...] = jnp.dot(h, w_ref[:, QKV_END:], preferred_element_type=F32)

    return pl.pallas_call(
        body, name=name, grid=(m // tm,),
        in_specs=[pl.BlockSpec((tm, d), lambda i: (i, 0)), _resident((1, d), (0, 0)), _resident((d, n_in), (0, 0))],
        out_specs=[pl.BlockSpec((tm, d), lambda i: (i, 0)), pl.BlockSpec((tm, QKV_END), lambda i: (i, 0)),
                   pl.BlockSpec((tm, n_in - QKV_END), lambda i: (i, 0))],
        out_shape=[_sds((m, d), BF16), _sds((m, QKV_END), F32), _sds((m, n_in - QKV_END), F32)],
        compiler_params=_cp("parallel"),
    )(x, ln, w_in)


def _pool_fwd(zq, pool_w, scale, name):
    m = zq.shape[0]
    tm = _row_tile(m)
    halo_blocks = tm // POOL_HALO

    def body(zc_ref, zh_ref, pw_ref, sc_ref, d_ref, p_ref):
        i = pl.program_id(0)
        halo = jnp.where(i > 0, zh_ref[...], 0.0)
        ext = jnp.concatenate([halo, zc_ref[...]], axis=0)
        t = i * tm + lax.broadcasted_iota(jnp.int32, (tm, 1), 0)
        for g, w in enumerate(POOL_WINDOWS):
            lanes = slice(g * GROUP_DIM, (g + 1) * GROUP_DIM)
            e = ext[:, lanes]
            s, k = e, 1
            while k < w:
                s = s + pltpu.roll(s, k, 0)
                k *= 2
            cnt = jnp.minimum(t + 1, w).astype(F32)
            dg = (s[POOL_HALO:, :] / cnt - e[POOL_HALO:, :]).astype(BF16)
            y = jnp.dot(dg, pw_ref[g].astype(BF16), preferred_element_type=F32)
            d_ref[:, lanes] = dg
            p_ref[:, lanes] = (y * sc_ref[:, lanes]).astype(BF16)

    return pl.pallas_call(
        body, name=name, grid=(m // tm,),
        in_specs=[pl.BlockSpec((tm, POOL_DIM), lambda i: (i, 0)),
                  pl.BlockSpec((POOL_HALO, POOL_DIM), lambda i: (jnp.maximum(i * halo_blocks - 1, 0), 0)),
                  _resident((4, GROUP_DIM, GROUP_DIM), (0, 0, 0)), _resident((1, POOL_DIM), (0, 0))],
        out_specs=[pl.BlockSpec((tm, POOL_DIM), lambda i: (i, 0)), pl.BlockSpec((tm, POOL_DIM), lambda i: (i, 0))],
        out_shape=[_sds((m, POOL_DIM), BF16), _sds((m, POOL_DIM), BF16)],
        compiler_params=_cp("parallel"),
    )(zq, zq, pool_w, scale)


def _pool_bwd(dp, d, pool_w, scale, name):
    m = dp.shape[0]
    tm = _row_tile(m)
    nb = m // tm
    halo_blocks = tm // POOL_HALO
    rows = tm + POOL_HALO

    def body(dpc_ref, dph_ref, d_ref, pw_ref, sc_ref, du_ref, dpw_ref, dsc_ref):
        i = pl.program_id(0)

        @pl.when(i == 0)
        def _():
            dpw_ref[...] = jnp.zeros_like(dpw_ref)
            dsc_ref[...] = jnp.zeros_like(dsc_ref)

        halo = jnp.where(i < nb - 1, dph_ref[...], 0.0)
        dpc = dpc_ref[...]
        ext = jnp.concatenate([dpc, halo], axis=0)
        t = i * tm + lax.broadcasted_iota(jnp.int32, (rows, 1), 0)
        for g, w in enumerate(POOL_WINDOWS):
            lanes = slice(g * GROUP_DIM, (g + 1) * GROUP_DIM)
            pwb = pw_ref[g].astype(BF16)
            dyb = (ext[:, lanes] * sc_ref[:, lanes]).astype(BF16)
            dd = lax.dot_general(dyb, pwb, _NT, preferred_element_type=F32)
            cnt = jnp.minimum(t + 1, w).astype(F32)
            s, k = dd / cnt, 1
            while k < w:
                s = s + pltpu.roll(s, rows - k, 0)
                k *= 2
            du_ref[:, lanes] = (s[:tm, :] - dd[:tm, :]).astype(BF16)
            dcur = d_ref[:, lanes]
            y = jnp.dot(dcur, pwb, preferred_element_type=F32)
            dsc_ref[:, lanes] += jnp.sum(dpc[:, lanes] * y, axis=0, keepdims=True)
            dpw_ref[g] += lax.dot_general(dcur, dyb[:tm, :], _TN, preferred_element_type=F32)

    return pl.pallas_call(
        body, name=name, grid=(nb,),
        in_specs=[pl.BlockSpec((tm, POOL_DIM), lambda i: (i, 0)),
                  pl.BlockSpec((POOL_HALO, POOL_DIM), lambda i: (jnp.minimum((i + 1) * halo_blocks, nb * halo_blocks - 1), 0)),
                  pl.BlockSpec((tm, POOL_DIM), lambda i: (i, 0)),
                  _resident((4, GROUP_DIM, GROUP_DIM), (0, 0, 0)), _resident((1, POOL_DIM), (0, 0))],
        out_specs=[pl.BlockSpec((tm, POOL_DIM), lambda i: (i, 0)),
                   pl.BlockSpec((4, GROUP_DIM, GROUP_DIM), lambda i: (0, 0, 0)),
                   pl.BlockSpec((1, POOL_DIM), lambda i: (0, 0))],
        out_shape=[_sds((m, POOL_DIM), BF16), _sds((4, GROUP_DIM, GROUP_DIM), F32), _sds((1, POOL_DIM), F32)],
        compiler_params=_cp("arbitrary"),
    )(dp, dp, d, pool_w, scale)


def _qk_norm_rope(xv, gain, cos, sin):
    width = xv.shape[1]
    r = lax.rsqrt(_segment_mean(xv * xv, width) + EPS)
    y = xv * r * gain
    return y * _tile_lanes(cos, width) + _rope_partner(y) * _tile_lanes(sin, width)


def _qk_prep(zq, qgain, kgain, cos, sin, name):
    m = zq.shape[0]
    tm = _row_tile(m)

    def body(q_ref, kv_ref, qg_ref, kg_ref, cos_ref, sin_ref, qr_ref, kr_ref, v_ref):
        cos_v, sin_v = cos_ref[...], sin_ref[...]
        qr_ref[...] = _qk_norm_rope(q_ref[...], qg_ref[...], cos_v, sin_v).astype(BF16)
        kv = kv_ref[...]
        kr_ref[...] = _qk_norm_rope(kv[:, :KV_DIM], kg_ref[...], cos_v, sin_v).astype(BF16)
        v_ref[...] = kv[:, KV_DIM:].astype(BF16)

    return pl.pallas_call(
        body, name=name, grid=(m // tm,),
        in_specs=[pl.BlockSpec((tm, ATTN_DIM), lambda i: (i, 1)), pl.BlockSpec((tm, 2 * KV_DIM), lambda i: (i, 4)),
                  _resident((1, ATTN_DIM), (0, 0)), _resident((1, KV_DIM), (0, 0)),
                  pl.BlockSpec((tm, 128), lambda i: (i, 0)), pl.BlockSpec((tm, 128), lambda i: (i, 0))],
        out_specs=[pl.BlockSpec((tm, ATTN_DIM), lambda i: (i, 0)), pl.BlockSpec((tm, KV_DIM), lambda i: (i, 0)),
                   pl.BlockSpec((tm, KV_DIM), lambda i: (i, 0))],
        out_shape=[_sds((m, ATTN_DIM), BF16), _sds((m, KV_DIM), BF16), _sds((m, KV_DIM), BF16)],
        compiler_params=_cp("parallel"),
    )(zq, zq, qgain, kgain, cos, sin)


def _band_mask(n):
    qi = lax.broadcasted_iota(jnp.int32, (ATTN_BLOCK, 2 * ATTN_BLOCK), 0)
    ki = lax.broadcasted_iota(jnp.int32, (ATTN_BLOCK, 2 * ATTN_BLOCK), 1)
    diff = qi + ATTN_BLOCK - ki
    return (diff >= 0) & (diff < ATTN_BLOCK) & ((ki >= ATTN_BLOCK) | (n > 0))


def _head_probs(qh, kh, mask, sink):
    s = lax.dot_general(qh, kh, _NT, preferred_element_type=F32) * ATTN_SCALE
    s = jnp.where(mask, s, NEG_BIG)
    mx = jnp.maximum(jnp.max(s, axis=-1, keepdims=True), sink)
    p = jnp.exp(s - mx)
    es = jnp.exp(sink - mx)
    den = jnp.sum(p, axis=-1, keepdims=True) + es
    return p / den, es / den


def _attn_fwd(qr, kr, vb, sinks, name):
    m = qr.shape[0]
    nb = m // ATTN_BLOCK

    def body(q_ref, kp_ref, kc_ref, vp_ref, vc_ref, sk_ref, o_ref):
        mask = _band_mask(pl.program_id(0))
        qv = q_ref[...]
        kk = jnp.concatenate([kp_ref[...], kc_ref[...]], axis=0)
        vv = jnp.concatenate([vp_ref[...], vc_ref[...]], axis=0)
        outs = []
        for h in range(N_Q_HEADS):
            kv_lanes = slice((h // GQA_GROUP) * HEAD_DIM, (h // GQA_GROUP + 1) * HEAD_DIM)
            p, _ = _head_probs(qv[:, h * HEAD_DIM:(h + 1) * HEAD_DIM], kk[:, kv_lanes], mask, sk_ref[h])
            outs.append(jnp.dot(p.astype(BF16), vv[:, kv_lanes], preferred_element_type=F32))
        o_ref[...] = jnp.concatenate(outs, axis=1).astype(BF16)

    prev = lambda n: (jnp.maximum(n - 1, 0), 0)
    cur = lambda n: (n, 0)
    return pl.pallas_call(
        body, name=name, grid=(nb,),
        in_specs=[pl.BlockSpec((ATTN_BLOCK, ATTN_DIM), cur),
                  pl.BlockSpec((ATTN_BLOCK, KV_DIM), prev), pl.BlockSpec((ATTN_BLOCK, KV_DIM), cur),
                  pl.BlockSpec((ATTN_BLOCK, KV_DIM), prev), pl.BlockSpec((ATTN_BLOCK, KV_DIM), cur),
                  pl.BlockSpec(memory_space=pltpu.SMEM)],
        out_specs=pl.BlockSpec((ATTN_BLOCK, ATTN_DIM), cur),
        out_shape=_sds((m, ATTN_DIM), BF16),
        compiler_params=_cp("parallel"),
    )(qr, kr, kr, vb, vb, sinks)


def _attn_bwd(do, qr, kr, vb, sinks, name):
    m = qr.shape[0]
    nb = m // ATTN_BLOCK

    def body(do_ref, q_ref, kp_ref, kc_ref, vp_ref, vc_ref, sk_ref, dq_ref, dk_ref, dv_ref, ds_ref, carry_k, carry_v):
        n = pl.program_id(0)

        @pl.when(n == 0)
        def _():
            carry_k[...] = jnp.zeros_like(carry_k)
            carry_v[...] = jnp.zeros_like(carry_v)
            ds_ref[...] = jnp.zeros_like(ds_ref)

        @pl.when(n < nb)
        def _():
            mask = _band_mask(n)
            qv = q_ref[...]
            dov = do_ref[...]
            kk = jnp.concatenate([kp_ref[...], kc_ref[...]], axis=0)
            vv = jnp.concatenate([vp_ref[...], vc_ref[...]], axis=0)
            lane = lax.broadcasted_iota(jnp.int32, (1, 128), 1)
            dsink = jnp.zeros((1, 128), F32)
            dqs, dks, dvs = [], [], []
            for kvh in range(N_KV_HEADS):
                kv_lanes = slice(kvh * HEAD_DIM, (kvh + 1) * HEAD_DIM)
                kh, vh = kk[:, kv_lanes], vv[:, kv_lanes]
                dkh = jnp.zeros((2 * ATTN_BLOCK, HEAD_DIM), F32)
                dvh = jnp.zeros((2 * ATTN_BLOCK, HEAD_DIM), F32)
                for g in range(GQA_GROUP):
                    h = kvh * GQA_GROUP + g
                    q_lanes = slice(h * HEAD_DIM, (h + 1) * HEAD_DIM)
                    qh = qv[:, q_lanes]
                    doh = dov[:, q_lanes].astype(BF16)
                    p, ps = _head_probs(qh, kh, mask, sk_ref[h])
                    dpr = lax.dot_general(doh, vh, _NT, preferred_element_type=F32)
                    delta = jnp.sum(p * dpr, axis=-1, keepdims=True)
                    dsb = (p * (dpr - delta) * ATTN_SCALE).astype(BF16)
                    dsink = dsink + jnp.where(lane == h, -jnp.sum(ps * delta), 0.0)
                    dqs.append(jnp.dot(dsb, kh, preferred_element_type=F32))
                    dkh = dkh + lax.dot_general(dsb, qh, _TN, preferred_element_type=F32)
                    dvh = dvh + lax.dot_general(p.astype(BF16), doh, _TN, preferred_element_type=F32)
                dks.append(dkh)
                dvs.append(dvh)
            dq_ref[...] = jnp.concatenate(dqs, axis=1)
            dkk = jnp.concatenate(dks, axis=1)
            dvv = jnp.concatenate(dvs, axis=1)
            dk_ref[...] = carry_k[...] + dkk[:ATTN_BLOCK, :]
            dv_ref[...] = carry_v[...] + dvv[:ATTN_BLOCK, :]
            carry_k[...] = dkk[ATTN_BLOCK:, :]
            carry_v[...] = dvv[ATTN_BLOCK:, :]
            ds_ref[...] += dsink

        @pl.when(n == nb)
        def _():
            dk_ref[...] = carry_k[...]
            dv_ref[...] = carry_v[...]

    cur = lambda n: (jnp.minimum(n, nb - 1), 0)
    prev = lambda n: (jnp.clip(n - 1, 0, nb - 1), 0)
    return pl.pallas_call(
        body, name=name, grid=(nb + 1,),
        in_specs=[pl.BlockSpec((ATTN_BLOCK, ATTN_DIM), cur), pl.BlockSpec((ATTN_BLOCK, ATTN_DIM), cur),
                  pl.BlockSpec((ATTN_BLOCK, KV_DIM), prev), pl.BlockSpec((ATTN_BLOCK, KV_DIM), cur),
                  pl.BlockSpec((ATTN_BLOCK, KV_DIM), prev), pl.BlockSpec((ATTN_BLOCK, KV_DIM), cur),
                  pl.BlockSpec(memory_space=pltpu.SMEM)],
        out_specs=[pl.BlockSpec((ATTN_BLOCK, ATTN_DIM), cur), pl.BlockSpec((ATTN_BLOCK, KV_DIM), prev),
                   pl.BlockSpec((ATTN_BLOCK, KV_DIM), prev), pl.BlockSpec((1, 128), lambda n: (0, 0))],
        out_shape=[_sds((m, ATTN_DIM), F32), _sds((m, KV_DIM), F32), _sds((m, KV_DIM), F32), _sds((1, 128), F32)],
        scratch_shapes=[pltpu.VMEM((ATTN_BLOCK, KV_DIM), F32), pltpu.VMEM((ATTN_BLOCK, KV_DIM), F32)],
        compiler_params=_cp("arbitrary"),
    )(do, qr, kr, kr, vb, vb, sinks)


def _qk_norm_rope_bwd(dout, xv, gain, cos, sin):
    width = xv.shape[1]
    r = lax.rsqrt(_segment_mean(xv * xv, width) + EPS)
    xn = xv * r
    dy = dout * _tile_lanes(cos, width) + _rope_partner(dout * _tile_lanes(sin, width))
    dxn = dy * gain
    dx = r * (dxn - xn * _segment_mean(dxn * xn, width))
    return dx, jnp.sum(dy * xn, axis=0, keepdims=True)


def _fold_heads(v):
    out = v[:, :HEAD_DIM]
    for h in range(1, v.shape[1] // HEAD_DIM):
        out = out + v[:, h * HEAD_DIM:(h + 1) * HEAD_DIM]
    return out


def _qk_bwd(dq, dk, dv, zq, qgain, kgain, cos, sin, name):
    m = zq.shape[0]
    tm = _row_tile(m)

    def body(dq_ref, dk_ref, dv_ref, q_ref, kv_ref, qg_ref, kg_ref, cos_ref, sin_ref, dz_ref, dqg_ref, dkg_ref):
        @pl.when(pl.program_id(0) == 0)
        def _():
            dqg_ref[...] = jnp.zeros_like(dqg_ref)
            dkg_ref[...] = jnp.zeros_like(dkg_ref)

        cos_v, sin_v = cos_ref[...], sin_ref[...]
        dxq, dgq = _qk_norm_rope_bwd(dq_ref[...], q_ref[...], qg_ref[...], cos_v, sin_v)
        dxk, dgk = _qk_norm_rope_bwd(dk_ref[...], kv_ref[:, :KV_DIM], kg_ref[...], cos_v, sin_v)
        dz_ref[:, :ATTN_DIM] = dxq.astype(BF16)
        dz_ref[:, ATTN_DIM:ATTN_DIM + KV_DIM] = dxk.astype(BF16)
        dz_ref[:, ATTN_DIM + KV_DIM:] = dv_ref[...].astype(BF16)
        dqg_ref[...] += _fold_heads(dgq)
        dkg_ref[...] += _fold_heads(dgk)

    row = lambda i: (i, 0)
    return pl.pallas_call(
        body, name=name, grid=(m // tm,),
        in_specs=[pl.BlockSpec((tm, ATTN_DIM), row), pl.BlockSpec((tm, KV_DIM), row), pl.BlockSpec((tm, KV_DIM), row),
                  pl.BlockSpec((tm, ATTN_DIM), lambda i: (i, 1)), pl.BlockSpec((tm, 2 * KV_DIM), lambda i: (i, 4)),
                  _resident((1, ATTN_DIM), (0, 0)), _resident((1, KV_DIM), (0, 0)),
                  pl.BlockSpec((tm, 128), row), pl.BlockSpec((tm, 128), row)],
        out_specs=[pl.BlockSpec((tm, ATTN_DIM + 2 * KV_DIM), row), pl.BlockSpec((1, HEAD_DIM), lambda i: (0, 0)),
                   pl.BlockSpec((1, HEAD_DIM), lambda i: (0, 0))],
        out_shape=[_sds((m, ATTN_DIM + 2 * KV_DIM), BF16), _sds((1, HEAD_DIM), F32), _sds((1, HEAD_DIM), F32)],
        compiler_params=_cp("arbitrary"),
    )(dq, dk, dv, zq, zq, qgain, kgain, cos, sin)


def _merge_fwd(x, p, o, zg, w_pb, w_ab, w_out, name):
    m, d = x.shape
    tm = _row_tile(m)

    def body(x_ref, p_ref, o_ref, zg_ref, wpb_ref, wab_ref, wo_ref, xo_ref, mix_ref):
        a = jnp.dot(p_ref[...], wpb_ref[...], preferred_element_type=F32)
        b = jnp.dot(o_ref[...], wab_ref[...], preferred_element_type=F32)
        mix = (jax.nn.sigmoid(zg_ref[:, :d]) * a + jax.nn.sigmoid(zg_ref[:, d:]) * b).astype(BF16)
        mix_ref[...] = mix
        xo_ref[...] = x_ref[...] + jnp.dot(mix, wo_ref[...], preferred_element_type=F32)

    row = lambda i: (i, 0)
    return pl.pallas_call(
        body, name=name, grid=(m // tm,),
        in_specs=[pl.BlockSpec((tm, d), row), pl.BlockSpec((tm, POOL_DIM), row), pl.BlockSpec((tm, ATTN_DIM), row),
                  pl.BlockSpec((tm, 2 * d), row), _resident((POOL_DIM, d), (0, 0)), _resident((ATTN_DIM, d), (0, 0)),
                  _resident((d, d), (0, 0))],
        out_specs=[pl.BlockSpec((tm, d), row)] * 2,
        out_shape=[_sds((m, d), F32), _sds((m, d), BF16)],
        compiler_params=_cp("parallel"),
    )(x, p, o, zg, w_pb, w_ab, w_out)


def _merge_bwd(dy, p, o, zg, w_out, w_pb, w_ab, name):
    m, d = dy.shape
    tm = _row_tile(m)

    def body(dy_ref, p_ref, o_ref, zg_ref, wo_ref, wpb_ref, wab_ref, dyb_ref, da_ref, db_ref, dp_ref, do_ref, dzg_ref):
        dyb = dy_ref[...].astype(BF16)
        dyb_ref[...] = dyb
        dmix = lax.dot_general(dyb, wo_ref[...], _NT, preferred_element_type=F32)
        gp = jax.nn.sigmoid(zg_ref[:, :d])
        ga = jax.nn.sigmoid(zg_ref[:, d:])
        da = (dmix * gp).astype(BF16)
        db = (dmix * ga).astype(BF16)
        da_ref[...] = da
        db_ref[...] = db
        a = jnp.dot(p_ref[...], wpb_ref[...], preferred_element_type=F32)
        b = jnp.dot(o_ref[...], wab_ref[...], preferred_element_type=F32)
        dzg_ref[:, :d] = (dmix * a * gp * (1.0 - gp)).astype(BF16)
        dzg_ref[:, d:] = (dmix * b * ga * (1.0 - ga)).astype(BF16)
        dp_ref[...] = lax.dot_general(da, wpb_ref[...], _NT, preferred_element_type=F32)
        do_ref[...] = lax.dot_general(db, wab_ref[...], _NT, preferred_element_type=F32)

    row = lambda i: (i, 0)
    return pl.pallas_call(
        body, name=name, grid=(m // tm,),
        in_specs=[pl.BlockSpec((tm, d), row), pl.BlockSpec((tm, POOL_DIM), row), pl.BlockSpec((tm, ATTN_DIM), row),
                  pl.BlockSpec((tm, 2 * d), row), _resident((d, d), (0, 0)), _resident((POOL_DIM, d), (0, 0)),
                  _resident((ATTN_DIM, d), (0, 0))],
        out_specs=[pl.BlockSpec((tm, d), row), pl.BlockSpec((tm, d), row), pl.BlockSpec((tm, d), row),
                   pl.BlockSpec((tm, POOL_DIM), row), pl.BlockSpec((tm, ATTN_DIM), row), pl.BlockSpec((tm, 2 * d), row)],
        out_shape=[_sds((m, d), BF16), _sds((m, d), BF16), _sds((m, d), BF16), _sds((m, POOL_DIM), F32),
                   _sds((m, ATTN_DIM), F32), _sds((m, 2 * d), BF16)],
        compiler_params=_cp("parallel"),
    )(dy, p, o, zg, w_out, w_pb, w_ab)


def _mix_bwd_x(dy, x, ln, dzp, dzqkv, dzg, w_in, name):
    m, d = dy.shape
    n_in = w_in.shape[1]
    tm = _row_tile(m)

    def body(dy_ref, x_ref, ln_ref, dzp_ref, dzq_ref, dzg_ref, w_ref, dx_ref, dln_ref):
        @pl.when(pl.program_id(0) == 0)
        def _():
            dln_ref[...] = jnp.zeros_like(dln_ref)

        dh = lax.dot_general(dzp_ref[...], w_ref[:, :POOL_DIM], _NT, preferred_element_type=F32)
        dh += lax.dot_general(dzq_ref[...], w_ref[:, POOL_DIM:QKV_END], _NT, preferred_element_type=F32)
        dh += lax.dot_general(dzg_ref[...], w_ref[:, QKV_END:], _NT, preferred_element_type=F32)
        dx, dgain = _rms_bwd(dh, x_ref[...], ln_ref[...])
        dx_ref[...] = dy_ref[...] + dx
        dln_ref[...] += jnp.sum(dgain, axis=0, keepdims=True)

    row = lambda i: (i, 0)
    return pl.pallas_call(
        body, name=name, grid=(m // tm,),
        in_specs=[pl.BlockSpec((tm, d), row), pl.BlockSpec((tm, d), row), _resident((1, d), (0, 0)),
                  pl.BlockSpec((tm, POOL_DIM), row), pl.BlockSpec((tm, QKV_END - POOL_DIM), row),
                  pl.BlockSpec((tm, n_in - QKV_END), row), _resident((d, n_in), (0, 0))],
        out_specs=[pl.BlockSpec((tm, d), row), pl.BlockSpec((1, d), lambda i: (0, 0))],
        out_shape=[_sds((m, d), F32), _sds((1, d), F32)],
        compiler_params=_cp("arbitrary"),
    )(dy, x, ln, dzp, dzqkv, dzg, w_in)


def _loss_head(y, target, name):
    m, d = y.shape
    tm = _row_tile(m)

    def body(y_ref, t_ref, loss_ref, dy_ref):
        @pl.when(pl.program_id(0) == 0)
        def _():
            loss_ref[...] = jnp.zeros_like(loss_ref)

        diff = y_ref[...] - t_ref[...]
        dy_ref[...] = diff * (1.0 / d)
        loss_ref[...] += 0.5 * jnp.sum(jnp.mean(diff * diff, axis=-1, keepdims=True), axis=0, keepdims=True)

    row = lambda i: (i, 0)
    return pl.pallas_call(
        body, name=name, grid=(m // tm,),
        in_specs=[pl.BlockSpec((tm, d), row), pl.BlockSpec((tm, d), row)],
        out_specs=[pl.BlockSpec((1, 1), lambda i: (0, 0)), pl.BlockSpec((tm, d), row)],
        out_shape=[_sds((1, 1), F32), _sds((m, d), F32)],
        compiler_params=_cp("arbitrary"),
    )(y, target)


def _adamw_math(g, w, m, v):
    m2 = ADAM_B1 * m + (1.0 - ADAM_B1) * g
    v2 = ADAM_B2 * v + (1.0 - ADAM_B2) * (g * g)
    m_hat = m2 / (1.0 - ADAM_B1 ** ADAM_STEP)
    v_hat = v2 / (1.0 - ADAM_B2 ** ADAM_STEP)
    delta = -ADAM_LR * (m_hat / (jnp.sqrt(v_hat) + ADAM_EPS) + ADAM_WD * w)
    return delta, m2, v2


def _sum_parts(parts_ref):
    g = parts_ref[0].astype(F32)
    for s in range(1, N_DEV):
        g = g + parts_ref[s].astype(F32)
    return g


def _adamw_sharded(parts, w, m, v, name):
    n_layers, rows, cols = w.shape
    tr = next(t for t in (512, 256, 128, 64, 32, 16, 8) if rows % t == 0 and t * cols <= ADAMW_BLOCK_ELEMS)
    nr = rows // tr

    def body(*refs):
        part_refs = refs[:n_layers]
        w_ref, m_ref, v_ref, g_out, d_out, m_out, v_out = refs[n_layers:]
        layer = pl.program_id(0)
        for l in range(n_layers):
            @pl.when(layer == l)
            def _(l=l):
                g = _sum_parts(part_refs[l])
                delta, m2, v2 = _adamw_math(g, w_ref[...], m_ref[...], v_ref[...])
                g_out[...] = g
                d_out[...] = delta
                m_out[...] = m2
                v_out[...] = v2

    def part_map(l):
        return lambda layer, r: (0, jnp.where(layer == l, r, jnp.where(layer < l, 0, nr - 1)), 0)

    wspec = pl.BlockSpec((None, tr, cols), lambda layer, r: (layer, r, 0))
    return pl.pallas_call(
        body, name=name, grid=(n_layers, nr),
        in_specs=[pl.BlockSpec((N_DEV, tr, cols), part_map(l)) for l in range(n_layers)] + [wspec] * 3,
        out_specs=[wspec] * 4,
        out_shape=[_sds(w.shape, F32)] * 4,
        compiler_params=_cp("arbitrary", "arbitrary"),
    )(*parts, w, m, v)


def _adamw_packed(parts, w, m, v, name):
    def body(p_ref, w_ref, m_ref, v_ref, g_out, d_out, m_out, v_out):
        g = _sum_parts(p_ref)
        delta, m2, v2 = _adamw_math(g, w_ref[...], m_ref[...], v_ref[...])
        g_out[...] = g
        d_out[...] = delta
        m_out[...] = m2
        v_out[...] = v2

    return pl.pallas_call(
        body, name=name, out_shape=[_sds(w.shape, F32)] * 4,
        compiler_params=pltpu.CompilerParams(vmem_limit_bytes=VMEM_LIMIT_BYTES),
    )(parts, w, m, v)


_SMALL = ("ln_ffn1", "ln_mix", "pool_w", "pool_scale", "q_norm", "k_norm", "sinks", "ln_ffn2")


def _pack_small(arrs):
    rows = []
    for a in arrs:
        flat = a.reshape(-1)
        pad = (-flat.shape[0]) % 1024
        rows.append(jnp.pad(flat, (0, pad)).reshape(-1, 128))
    return jnp.concatenate(rows, axis=0)


def _unpack_small(packed, like):
    out, r0 = [], 0
    for a in like:
        size = a.size
        nrows = (size + 1023) // 1024 * 8
        out.append(packed[r0:r0 + nrows].reshape(-1)[:size].reshape(a.shape))
        r0 += nrows
    return out


def _rope_tables(m):
    pos = jnp.arange(m, dtype=F32)
    inv_freq = ROPE_THETA ** (-jnp.arange(0, ROT_DIM, 2, dtype=F32) / ROT_DIM)
    ang = pos[:, None] * inv_freq[None, :]
    cos8, sin8 = jnp.cos(ang), jnp.sin(ang)
    rest = HEAD_DIM - ROT_DIM
    cos64 = jnp.concatenate([cos8, cos8, jnp.ones((m, rest), F32)], axis=1)
    sin64 = jnp.concatenate([-sin8, sin8, jnp.zeros((m, rest), F32)], axis=1)
    return jnp.tile(cos64, (1, 2)), jnp.tile(sin64, (1, 2))


def _to_shard_major_cols(w):
    k = w.shape[0]
    return w.reshape(k, N_DEV, -1).transpose(1, 0, 2)


def _from_shard_major_cols(w):
    return w.transpose(1, 0, 2).reshape(w.shape[1], -1)


def kernel(x, ln_ffn1, w_ffn1_gu, w_ffn1_down, ln_mix, w_in, pool_w, pool_scale, w_pool_branch, q_norm, k_norm, sinks, w_attn_branch, w_out, ln_ffn2, w_ffn2_gu, w_ffn2_down, loss_target, m_ln_ffn1, m_w_ffn1_gu, m_w_ffn1_down, m_ln_mix, m_w_in, m_pool_w, m_pool_scale, m_w_pool_branch, m_q_norm, m_k_norm, m_sinks, m_w_attn_branch, m_w_out, m_ln_ffn2, m_w_ffn2_gu, m_w_ffn2_down, v_ln_ffn1, v_w_ffn1_gu, v_w_ffn1_down, v_ln_mix, v_w_in, v_pool_w, v_pool_scale, v_w_pool_branch, v_q_norm, v_k_norm, v_sinks, v_w_attn_branch, v_w_out, v_ln_ffn2, v_w_ffn2_gu, v_w_ffn2_down):
    weights = dict(ln_ffn1=ln_ffn1, w_ffn1_gu=w_ffn1_gu, w_ffn1_down=w_ffn1_down, ln_mix=ln_mix, w_in=w_in, pool_w=pool_w,
                   pool_scale=pool_scale, w_pool_branch=w_pool_branch, q_norm=q_norm, k_norm=k_norm, sinks=sinks,
                   w_attn_branch=w_attn_branch, w_out=w_out, ln_ffn2=ln_ffn2, w_ffn2_gu=w_ffn2_gu, w_ffn2_down=w_ffn2_down)
    mom_m = dict(ln_ffn1=m_ln_ffn1, w_ffn1_gu=m_w_ffn1_gu, w_ffn1_down=m_w_ffn1_down, ln_mix=m_ln_mix, w_in=m_w_in,
                 pool_w=m_pool_w, pool_scale=m_pool_scale, w_pool_branch=m_w_pool_branch, q_norm=m_q_norm, k_norm=m_k_norm,
                 sinks=m_sinks, w_attn_branch=m_w_attn_branch, w_out=m_w_out, ln_ffn2=m_ln_ffn2, w_ffn2_gu=m_w_ffn2_gu,
                 w_ffn2_down=m_w_ffn2_down)
    mom_v = dict(ln_ffn1=v_ln_ffn1, w_ffn1_gu=v_w_ffn1_gu, w_ffn1_down=v_w_ffn1_down, ln_mix=v_ln_mix, w_in=v_w_in,
                 pool_w=v_pool_w, pool_scale=v_pool_scale, w_pool_branch=v_w_pool_branch, q_norm=v_q_norm, k_norm=v_k_norm,
                 sinks=v_sinks, w_attn_branch=v_w_attn_branch, w_out=v_w_out, ln_ffn2=v_ln_ffn2, w_ffn2_gu=v_w_ffn2_gu,
                 w_ffn2_down=v_w_ffn2_down)
    order = ("ln_ffn1", "w_ffn1_gu", "w_ffn1_down", "ln_mix", "w_in", "pool_w", "pool_scale", "w_pool_branch", "q_norm",
             "k_norm", "sinks", "w_attn_branch", "w_out", "ln_ffn2", "w_ffn2_gu", "w_ffn2_down")
    big = ("w_ffn1_gu", "w_ffn1_down", "w_in", "w_pool_branch", "w_attn_branch", "w_out", "w_ffn2_gu", "w_ffn2_down")

    n_layers = ln_ffn1.shape[0]
    seq, d = x.shape[-2], x.shape[-1]
    xs = x.reshape(seq, d)
    target = loss_target.reshape(seq, d)
    cos, sin = _rope_tables(seq)

    def layer_shards(l):
        return [weights[k][l].astype(BF16) for k in big]

    def layer_weights(l, full):
        g = dict(zip(big, full))
        c_ff = g["w_ffn1_gu"].shape[-1]
        return dict(
            gu1=g["w_ffn1_gu"].reshape(2, 4, d, c_ff), down1=g["w_ffn1_down"].reshape(4, c_ff, d),
            gu2=g["w_ffn2_gu"].reshape(2, 4, d, c_ff), down2=g["w_ffn2_down"].reshape(4, c_ff, d),
            w_in=_from_shard_major_cols(g["w_in"]), w_pb=_from_shard_major_cols(g["w_pool_branch"]),
            w_ab=_from_shard_major_cols(g["w_attn_branch"]), w_out=g["w_out"].reshape(d, d),
            ln1=ln_ffn1[l][None], ln_mix=ln_mix[l][None], ln2=ln_ffn2[l][None], pool_w=pool_w[l],
            pool_scale=pool_scale[l][None], sinks=sinks[l],
            qgain=jnp.tile(q_norm[l], N_Q_HEADS)[None], kgain=jnp.tile(k_norm[l], N_KV_HEADS)[None])

    gathered = [layer_weights(0, _all_gather_many(layer_shards(0), name="gather_weights_l0"))]
    saved = []
    cur = xs
    for l in range(n_layers):
        lw = gathered[l]
        s = dict(x0=cur)
        in_flight, token = None, None
        if l + 1 < n_layers:
            in_flight, token = _exchange_start(layer_shards(l + 1), scatter=False, name=f"gather_start_l{l + 1}")
        s["h1"], s["gu1"], act1 = _ffn_up(cur, lw["ln1"], lw["gu1"], name=f"ffn1_up_l{l}", after=token)
        s["act1"] = act1
        x1 = _ffn_down(cur, act1, lw["down1"], name=f"ffn1_down_l{l}")
        s["x1"] = x1
        s["h2"], zq, zg = _mix_in(x1, lw["ln_mix"], lw["w_in"], name=f"mix_in_l{l}")
        s["zq"], s["zg"] = zq, zg
        s["d"], s["p"] = _pool_fwd(zq, lw["pool_w"], lw["pool_scale"], name=f"pool_fwd_l{l}")
        s["qr"], s["kr"], s["vb"] = _qk_prep(zq, lw["qgain"], lw["kgain"], cos, sin, name=f"qk_prep_l{l}")
        s["o"] = _attn_fwd(s["qr"], s["kr"], s["vb"], lw["sinks"], name=f"attn_fwd_l{l}")
        x2, s["mix"] = _merge_fwd(x1, s["p"], s["o"], zg, lw["w_pb"], lw["w_ab"], lw["w_out"],
                                                  name=f"merge_fwd_l{l}")
        s["x2"] = x2
        s["h3"], s["gu2"], act2 = _ffn_up(x2, lw["ln2"], lw["gu2"], name=f"ffn2_up_l{l}")
        s["act2"] = act2
        cur = _ffn_down(x2, act2, lw["down2"], name=f"ffn2_down_l{l}")
        saved.append(s)
        if in_flight is not None:
            gathered.append(layer_weights(l + 1, _exchange_wait(in_flight, cur, name=f"gather_wait_l{l + 1}")))

    loss_local, dy = _loss_head(cur, target, name="loss_head")
    loss = lax.psum(loss_local[0, 0], MESH_AXES)

    small_grads = {k: [None] * n_layers for k in _SMALL}
    received = {k: [None] * n_layers for k in big}
    grads_in_flight = [None] * n_layers
    for l in reversed(range(n_layers)):
        lw, s = gathered[l], saved[l]
        c_ff = lw["gu1"].shape[-1]

        def ffn_bwd(dy, xin, h, gu, act, ln, wgu, wdown, tag):
            dyh, dgu = _ffn_bwd_act(dy, gu, wdown, name=f"{tag}_bwd_act_l{l}")
            dw_down = _matmul_tn(act, dyh[None], name=f"{tag}_dw_down_l{l}")
            dw_gu = _matmul_tn(h[None], dgu.reshape(N_DEV, seq, c_ff), name=f"{tag}_dw_gu_l{l}")
            dx, dln = _ffn_bwd_x(dy, xin, ln, dgu, wgu, name=f"{tag}_bwd_x_l{l}")
            return dx, dln, dw_gu, dw_down.reshape(N_DEV, c_ff // 2, d)

        dx2, dln2, dw_gu2, dw_down2 = ffn_bwd(dy, s["x2"], s["h3"], s["gu2"], s["act2"], lw["ln2"], lw["gu2"], lw["down2"], "ffn2")

        dyb, da, db, dp, do, dzg = _merge_bwd(dx2, s["p"], s["o"], s["zg"], lw["w_out"], lw["w_pb"], lw["w_ab"],
                                              name=f"merge_bwd_l{l}")
        dw_out = _matmul_tn(s["mix"][None], dyb[None], name=f"dw_out_l{l}")[0]
        dw_pb = _matmul_tn(s["p"][None], da[None], name=f"dw_pb_l{l}")[0]
        dw_ab = _matmul_tn(s["o"][None], db[None], name=f"dw_ab_l{l}")[0]
        dzp, dpw, dsc = _pool_bwd(dp, s["d"], lw["pool_w"], lw["pool_scale"], name=f"pool_bwd_l{l}")
        dq, dk, dv, dsinks = _attn_bwd(do, s["qr"], s["kr"], s["vb"], lw["sinks"], name=f"attn_bwd_l{l}")
        dzqkv, dqg, dkg = _qk_bwd(dq, dk, dv, s["zq"], lw["qgain"], lw["kgain"], cos, sin, name=f"qk_bwd_l{l}")
        dw_in = jnp.concatenate([_matmul_tn(s["h2"][None], dzp[None], name=f"dw_in_pool_l{l}")[0],
                                 _matmul_tn(s["h2"][None], dzqkv[None], name=f"dw_in_qkv_l{l}")[0],
                                 _matmul_tn(s["h2"][None], dzg[None], name=f"dw_in_gate_l{l}")[0]], axis=1)
        dx1, dlnm = _mix_bwd_x(dx2, s["x1"], lw["ln_mix"], dzp, dzqkv, dzg, lw["w_in"], name=f"mix_bwd_x_l{l}")

        dy, dln1, dw_gu1, dw_down1 = ffn_bwd(dx1, s["x0"], s["h1"], s["gu1"], s["act1"], lw["ln1"], lw["gu1"], lw["down1"], "ffn1")

        partial = dict(w_ffn1_gu=dw_gu1, w_ffn1_down=dw_down1, w_in=_to_shard_major_cols(dw_in),
                       w_pool_branch=_to_shard_major_cols(dw_pb), w_attn_branch=_to_shard_major_cols(dw_ab),
                       w_out=dw_out.reshape(N_DEV, d // N_DEV, d), w_ffn2_gu=dw_gu2, w_ffn2_down=dw_down2)
        grads_in_flight[l], _ = _exchange_start([partial[k] for k in big], scatter=True, name=f"grads_start_l{l}")
        small_grads["ln_ffn1"][l] = dln1[0]
        small_grads["ln_mix"][l] = dlnm[0]
        small_grads["ln_ffn2"][l] = dln2[0]
        small_grads["pool_w"][l] = dpw
        small_grads["pool_scale"][l] = dsc[0]
        small_grads["q_norm"][l] = dqg[0]
        small_grads["k_norm"][l] = dkg[0]
        small_grads["sinks"][l] = dsinks[0, :N_Q_HEADS]

    grad_x = dy.reshape(x.shape)
    for l in reversed(range(n_layers)):
        got = _exchange_wait(grads_in_flight[l], dy, name=f"grads_wait_l{l}")
        for k, r in zip(big, got):
            received[k][l] = r

    grads, deltas, new_m, new_v = {}, {}, {}, {}
    for k in big:
        w = weights[k]
        shape2 = (n_layers, -1, w.shape[-1])
        parts = [r.reshape(N_DEV, -1, w.shape[-1]) for r in received[k]]
        outs = _adamw_sharded(parts, w.reshape(shape2), mom_m[k].reshape(shape2), mom_v[k].reshape(shape2), name=f"adamw_{k}")
        grads[k], deltas[k], new_m[k], new_v[k] = (o.reshape(w.shape) for o in outs)

    small_w = [weights[k] for k in _SMALL]
    packed_g = _pack_small([jnp.stack(small_grads[k]).reshape(weights[k].shape) for k in _SMALL])
    (parts_small,) = _all_gather_many([packed_g], name="gather_small_grads")
    outs = _adamw_packed(parts_small, _pack_small(small_w), _pack_small([mom_m[k] for k in _SMALL]),
                         _pack_small([mom_v[k] for k in _SMALL]), name="adamw_small")
    for res, o in zip((grads, deltas, new_m, new_v), outs):
        for k, a in zip(_SMALL, _unpack_small(o, small_w)):
            res[k] = a

    return (loss, grad_x, *[grads[k] for k in order], *[deltas[k] for k in order],
            *[new_m[k] for k in order], *[new_v[k] for k in order])
```

```python
import functools

import jax
import jax.numpy as jnp
from jax import lax
from jax.experimental import pallas as pl
from jax.experimental.pallas import tpu as pltpu

F32 = jnp.float32
BF16 = jnp.bfloat16

N_DEV = 8
MESH_AXES = ("x", "y", "c")
EPS = 1e-6

HEAD_DIM = 64
N_Q_HEADS = 8
N_KV_HEADS = 2
GQA_GROUP = N_Q_HEADS // N_KV_HEADS
ATTN_BLOCK = 128
ATTN_SCALE = HEAD_DIM ** -0.5
ROPE_THETA = 500000.0
ROT_DIM = 16
POOL_WINDOWS = (2, 4, 8, 16)
POOL_HALO = 16
GROUP_DIM = 128
POOL_DIM = 512
ATTN_DIM = 512
KV_DIM = 128
QKV_END = POOL_DIM + ATTN_DIM + 2 * KV_DIM

ADAM_LR = 0.001
ADAM_B1 = 0.9
ADAM_B2 = 0.999
ADAM_EPS = 1e-08
ADAM_WD = 0.01
ADAM_STEP = 10

ROW_TILE = 512
TN_ROW_TILE = 2048
VMEM_LIMIT_BYTES = 56 << 20
ADAMW_BLOCK_ELEMS = 96 * 1024
NEG_BIG = -1e30

_NT = (((1,), (1,)), ((), ()))
_TN = (((0,), (0,)), ((), ()))


def _cp(*sem):
    return pltpu.CompilerParams(dimension_semantics=sem, vmem_limit_bytes=VMEM_LIMIT_BYTES)


def _resident(block, index):
    return pl.BlockSpec(block, lambda *_: index, pipeline_mode=pl.Buffered(1))


def _row_tile(m):
    return min(ROW_TILE, m)


def _sds(shape, dtype):
    return jax.ShapeDtypeStruct(shape, dtype)


def _mesh_pos():
    return lax.axis_index("x"), lax.axis_index("y"), lax.axis_index("c")


def _all_gather_many(shards, name):
    n = len(shards)

    def body(*refs):
        ins, outs = refs[:n], refs[n:2 * n]
        send_sems, recv_sems, local_sems = refs[2 * n:]
        x, y, c = _mesh_pos()
        me, sibling = (x, y, c), (x, y, 1 - c)
        chips = [(1 - x, y), (x, 1 - y), (1 - x, 1 - y)]

        def slot(a, pos):
            return outs[a].at[4 * pos[0] + 2 * pos[1] + pos[2]]

        def copy(a, k, block, to, src=None):
            return pltpu.make_async_remote_copy(
                src_ref=slot(a, block) if src is None else src, dst_ref=slot(a, block),
                send_sem=send_sems.at[a, k], recv_sem=recv_sems.at[a, k],
                device_id=to, device_id_type=pl.DeviceIdType.MESH)

        mine = [pltpu.make_async_copy(ins[a], slot(a, me), local_sems.at[a]) for a in range(n)]
        for cp in mine:
            cp.start()
        first = []
        for a in range(n):
            first.append(copy(a, 0, me, sibling, src=ins[a]))
            for j, chip in enumerate(chips):
                first.append(copy(a, 1 + j, me, (*chip, c), src=ins[a]))
        for cp in first:
            cp.start()
        passed = []
        for j, chip in enumerate(chips):
            for a in range(n):
                copy(a, 1 + j, (*chip, c), me).wait_recv()
                fwd = copy(a, 4 + j, (*chip, c), sibling)
                fwd.start()
                passed.append(fwd)
        for a in range(n):
            copy(a, 0, sibling, me).wait_recv()
        for j, chip in enumerate(chips):
            for a in range(n):
                copy(a, 4 + j, (*chip, 1 - c), me).wait_recv()
        for cp in first + passed:
            cp.wait_send()
        for cp in mine:
            cp.wait()

    any_spec = pl.BlockSpec(memory_space=pl.ANY)
    return pl.pallas_call(
        body, name=name,
        out_shape=[_sds((N_DEV,) + s.shape, s.dtype) for s in shards],
        in_specs=[any_spec] * n, out_specs=[any_spec] * n,
        scratch_shapes=[pltpu.SemaphoreType.DMA((n, 7)), pltpu.SemaphoreType.DMA((n, 7)),
                        pltpu.SemaphoreType.DMA((n,))],
    )(*shards)


def _direct_copies(src, land, send_sem, recv_sem, local_sem, scatter):
    x, y, c = _mesh_pos()
    me = 4 * x + 2 * y + c
    local = pltpu.make_async_copy(src.at[me] if scatter else src, land.at[me], local_sem)
    remote = []
    for k in range(1, N_DEV):
        px = 1 - x if k & 4 else x
        py = 1 - y if k & 2 else y
        pc = 1 - c if k & 1 else c
        remote.append(pltpu.make_async_remote_copy(
            src_ref=src.at[4 * px + 2 * py + pc] if scatter else src, dst_ref=land.at[me],
            send_sem=send_sem, recv_sem=recv_sem, device_id=(px, py, pc), device_id_type=pl.DeviceIdType.MESH))
    seven = land.at[pl.ds(0, N_DEV - 1)]
    drain = pltpu.make_async_remote_copy(src_ref=seven, dst_ref=seven, send_sem=send_sem, recv_sem=recv_sem,
                                         device_id=(x, y, c), device_id_type=pl.DeviceIdType.MESH)
    return local, remote, drain


_HBM_SPEC = pl.BlockSpec(memory_space=pltpu.HBM)
_SEM_SPEC = pl.BlockSpec(memory_space=pltpu.SEMAPHORE)
_DATAFLOW = pltpu.SideEffectType.DATAFLOW_SIDE_EFFECTING
_SEMS_PER_ARRAY = 3


def _exchange_start(srcs, scatter, name):
    n = len(srcs)
    n_sems = _SEMS_PER_ARRAY * n
    land_shapes = [s.shape if scatter else (N_DEV,) + s.shape for s in srcs]

    def body(*refs):
        ins, lands, sems = refs[:n], refs[n:2 * n], refs[2 * n:2 * n + n_sems]
        for a in range(n):
            local, remote, _ = _direct_copies(ins[a], lands[a], *sems[3 * a:3 * a + 3], scatter)
            local.start()
            for cp in remote:
                cp.start()
        refs[-1][...] = jnp.zeros_like(refs[-1])

    outs = pl.pallas_call(
        body, name=name,
        out_shape=(*[pltpu.SemaphoreType.DMA(())] * n_sems,
                   *[pltpu.HBM(s.shape, s.dtype) for s in srcs],
                   *[pltpu.HBM(shape, s.dtype) for shape, s in zip(land_shapes, srcs)],
                   _sds((8, 128), F32)),
        in_specs=[_HBM_SPEC] * (2 * n),
        out_specs=(*[_SEM_SPEC] * n_sems, *[_HBM_SPEC] * (2 * n), pl.BlockSpec(memory_space=pltpu.VMEM)),
        input_output_aliases={i: n_sems + i for i in range(2 * n)},
        compiler_params=pltpu.CompilerParams(has_side_effects=_DATAFLOW),
    )(*[pltpu.with_memory_space_constraint(s, pltpu.HBM) for s in srcs],
      *[pltpu.with_memory_space_constraint(lax.empty(shape, s.dtype), pltpu.HBM) for shape, s in zip(land_shapes, srcs)])
    return (outs[:n_sems], outs[n_sems:n_sems + n], outs[n_sems + n:n_sems + 2 * n], scatter), outs[-1]


def _exchange_wait(state, after, name):
    sems, srcs, lands, scatter = state
    n = len(srcs)
    n_sems = len(sems)

    def body(*refs):
        ins, zones, ss = refs[:n], refs[n:2 * n], refs[2 * n:2 * n + n_sems]
        for a in range(n):
            local, _, drain = _direct_copies(ins[a], zones[a], *ss[3 * a:3 * a + 3], scatter)
            drain.wait_send()
            drain.wait_recv()
            local.wait()

    outs = pl.pallas_call(
        body, name=name,
        out_shape=(*[pltpu.HBM(s.shape, s.dtype) for s in srcs], *[pltpu.HBM(z.shape, z.dtype) for z in lands]),
        in_specs=[_HBM_SPEC] * (2 * n) + [_SEM_SPEC] * n_sems + [pl.BlockSpec(memory_space=pl.ANY)],
        out_specs=[_HBM_SPEC] * (2 * n),
        input_output_aliases={i: i for i in range(2 * n)},
        compiler_params=pltpu.CompilerParams(has_side_effects=_DATAFLOW),
    )(*srcs, *lands, *sems, after)
    return outs[n:]


def _rms_fwd(xv, gain):
    r = lax.rsqrt(jnp.mean(xv * xv, axis=-1, keepdims=True) + EPS)
    return xv * r * gain


def _rms_bwd(dh, xv, gain):
    r = lax.rsqrt(jnp.mean(xv * xv, axis=-1, keepdims=True) + EPS)
    xn = xv * r
    dxn = dh * gain
    dx = r * (dxn - xn * jnp.mean(dxn * xn, axis=-1, keepdims=True))
    return dx, dh * xn


def _silu_parts(g):
    s = jax.nn.sigmoid(g)
    return g * s, s * (1.0 + g * (1.0 - s))


def _segment_mean(v, width):
    r = lax.broadcasted_iota(jnp.int32, (width, width), 0) >> 6
    c = lax.broadcasted_iota(jnp.int32, (width, width), 1) >> 6
    bd = (r == c).astype(F32)
    return jnp.dot(v, bd, precision=lax.Precision.HIGHEST, preferred_element_type=F32) * (1.0 / HEAD_DIM)


def _rope_partner(v):
    width = v.shape[1]
    half = ROT_DIM // 2
    lane = lax.broadcasted_iota(jnp.int32, v.shape, 1) & (HEAD_DIM - 1)
    up = jnp.where(lane < ROT_DIM, pltpu.roll(v, half, 1), 0.0)
    return jnp.where(lane < half, pltpu.roll(v, width - half, 1), up)


def _tile_lanes(t, width):
    return t if width == t.shape[1] else jnp.tile(t, (1, width // t.shape[1]))


def _ffn_up(x, ln, wgu, name, after=None):
    m, d = x.shape
    c = wgu.shape[-1]
    tm = _row_tile(m)
    deps = [] if after is None else [after]

    def body(*refs):
        x_ref, ln_ref, w_ref = refs[:3]
        h_ref, gu_ref, a_ref = refs[-3:]
        h = _rms_fwd(x_ref[...], ln_ref[...]).astype(BF16)
        h_ref[...] = h
        for j in range(4):
            g = jnp.dot(h, w_ref[0, j], preferred_element_type=F32)
            u = jnp.dot(h, w_ref[1, j], preferred_element_type=F32)
            gu_ref[0, j] = g.astype(BF16)
            gu_ref[1, j] = u.astype(BF16)
            a_ref[j] = (g * jax.nn.sigmoid(g) * u).astype(BF16)

    return pl.pallas_call(
        body, name=name, grid=(m // tm,),
        in_specs=[pl.BlockSpec((tm, d), lambda i: (i, 0)), _resident((1, d), (0, 0)),
                  _resident((2, 4, d, c), (0, 0, 0, 0))] + [pl.BlockSpec(memory_space=pl.ANY)] * len(deps),
        out_specs=[pl.BlockSpec((tm, d), lambda i: (i, 0)),
                   pl.BlockSpec((2, 4, tm, c), lambda i: (0, 0, i, 0)),
                   pl.BlockSpec((4, tm, c), lambda i: (0, i, 0))],
        out_shape=[_sds((m, d), BF16), _sds((2, 4, m, c), BF16), _sds((4, m, c), BF16)],
        compiler_params=_cp("parallel"),
    )(x, ln, wgu, *deps)


def _ffn_down(x, act, wd, name):
    m, d = x.shape
    c = act.shape[-1]
    tm = _row_tile(m)

    def body(x_ref, a_ref, w_ref, o_ref):
        acc = jnp.dot(a_ref[0], w_ref[0], preferred_element_type=F32)
        for j in range(1, 4):
            acc += jnp.dot(a_ref[j], w_ref[j], preferred_element_type=F32)
        o_ref[...] = x_ref[...] + 0.5 * acc

    return pl.pallas_call(
        body, name=name, grid=(m // tm,),
        in_specs=[pl.BlockSpec((tm, d), lambda i: (i, 0)), pl.BlockSpec((4, tm, c), lambda i: (0, i, 0)),
                  _resident((4, c, d), (0, 0, 0))],
        out_specs=pl.BlockSpec((tm, d), lambda i: (i, 0)),
        out_shape=_sds((m, d), F32),
        compiler_params=_cp("parallel"),
    )(x, act, wd)


def _ffn_bwd_act(dy, gu, wd, name, after=None):
    m, d = dy.shape
    c = gu.shape[-1]
    tm = _row_tile(m)
    deps = [] if after is None else [after]

    def body(*refs):
        dy_ref, gu_ref, w_ref = refs[:3]
        dyh_ref, dgu_ref = refs[-2:]
        dyh = (0.5 * dy_ref[...]).astype(BF16)
        dyh_ref[...] = dyh
        for j in range(4):
            da = lax.dot_general(dyh, w_ref[j], _NT, preferred_element_type=F32)
            g = gu_ref[0, j].astype(F32)
            u = gu_ref[1, j].astype(F32)
            silu, dsilu = _silu_parts(g)
            dgu_ref[0, j] = (da * u * dsilu).astype(BF16)
            dgu_ref[1, j] = (da * silu).astype(BF16)

    return pl.pallas_call(
        body, name=name, grid=(m // tm,),
        in_specs=[pl.BlockSpec((tm, d), lambda i: (i, 0)), pl.BlockSpec((2, 4, tm, c), lambda i: (0, 0, i, 0)),
                  _resident((4, c, d), (0, 0, 0))] + [pl.BlockSpec(memory_space=pl.ANY)] * len(deps),
        out_specs=[pl.BlockSpec((tm, d), lambda i: (i, 0)), pl.BlockSpec((2, 4, tm, c), lambda i: (0, 0, i, 0))],
        out_shape=[_sds((m, d), BF16), _sds((2, 4, m, c), BF16)],
        compiler_params=_cp("parallel"),
    )(dy, gu, wd, *deps)


def _ffn_bwd_x(dy, x, ln, dgu, wgu, name):
    m, d = dy.shape
    c = dgu.shape[-1]
    tm = _row_tile(m)

    def body(dy_ref, x_ref, ln_ref, dgu_ref, w_ref, dx_ref, dln_ref):
        @pl.when(pl.program_id(0) == 0)
        def _():
            dln_ref[...] = jnp.zeros_like(dln_ref)

        dh = None
        for half in range(2):
            for j in range(4):
                t = lax.dot_general(dgu_ref[half, j], w_ref[half, j], _NT, preferred_element_type=F32)
                dh = t if dh is None else dh + t
        dx, dgain = _rms_bwd(dh, x_ref[...], ln_ref[...])
        dx_ref[...] = dy_ref[...] + dx
        dln_ref[...] += jnp.sum(dgain, axis=0, keepdims=True)

    return pl.pallas_call(
        body, name=name, grid=(m // tm,),
        in_specs=[pl.BlockSpec((tm, d), lambda i: (i, 0)), pl.BlockSpec((tm, d), lambda i: (i, 0)),
                  _resident((1, d), (0, 0)), pl.BlockSpec((2, 4, tm, c), lambda i: (0, 0, i, 0)),
                  _resident((2, 4, d, c), (0, 0, 0, 0))],
        out_specs=[pl.BlockSpec((tm, d), lambda i: (i, 0)), pl.BlockSpec((1, d), lambda i: (0, 0))],
        out_shape=[_sds((m, d), F32), _sds((1, d), F32)],
        compiler_params=_cp("arbitrary"),
    )(dy, x, ln, dgu, wgu)


def _matmul_tn(a, b, name, out_dtype=BF16):
    ja, m, k = a.shape
    jb, _, n = b.shape
    nj = max(ja, jb)
    tm = min(TN_ROW_TILE, m)
    nm = m // tm

    def body(a_ref, b_ref, o_ref, acc):
        step = pl.program_id(1)

        @pl.when(step == 0)
        def _():
            acc[...] = jnp.zeros_like(acc)

        acc[...] += lax.dot_general(a_ref[...], b_ref[...], _TN, preferred_element_type=F32)

        @pl.when(step == nm - 1)
        def _():
            o_ref[...] = acc[...].astype(o_ref.dtype)

    return pl.pallas_call(
        body, name=name, grid=(nj, nm),
        in_specs=[pl.BlockSpec((None, tm, k), (lambda j, s: (j, s, 0)) if ja > 1 else (lambda j, s: (0, s, 0))),
                  pl.BlockSpec((None, tm, n), (lambda j, s: (j, s, 0)) if jb > 1 else (lambda j, s: (0, s, 0)))],
        out_specs=pl.BlockSpec((None, k, n), lambda j, s: (j, 0, 0)),
        out_shape=_sds((nj, k, n), out_dtype),
        scratch_shapes=[pltpu.VMEM((k, n), F32)],
        compiler_params=_cp("parallel", "arbitrary"),
    )(a, b)


def _mix_in(x, ln, w_in, name):
    m, d = x.shape
    n_in = w_in.shape[1]
    tm = _row_tile(m)

    def body(x_ref, ln_ref, w_ref, h_ref, zq_ref, zg_ref):
        h = _rms_fwd(x_ref[...], ln_ref[...]).astype(BF16)
        h_ref[...] = h
        zq_ref[...] = jnp.dot(h, w_ref[:, :QKV_END], preferred_element_type=F32)
        zg_ref[...] = jnp.dot(h, w_ref[:, QKV_END:], preferred_element_type=F32)

    return pl.pallas_call(
        body, name=name, grid=(m // tm,),
        in_specs=[pl.BlockSpec((tm, d), lambda i: (i, 0)), _resident((1, d), (0, 0)), _resident((d, n_in), (0, 0))],
        out_specs=[pl.BlockSpec((tm, d), lambda i: (i, 0)), pl.BlockSpec((tm, QKV_END), lambda i: (i, 0)),
                   pl.BlockSpec((tm, n_in - QKV_END), lambda i: (i, 0))],
        out_shape=[_sds((m, d), BF16), _sds((m, QKV_END), F32), _sds((m, n_in - QKV_END), F32)],
        compiler_params=_cp("parallel"),
    )(x, ln, w_in)


def _pool_fwd(zq, pool_w, scale, name):
    m = zq.shape[0]
    tm = _row_tile(m)
    halo_blocks = tm // POOL_HALO

    def body(zc_ref, zh_ref, pw_ref, sc_ref, d_ref, p_ref):
        i = pl.program_id(0)
        halo = jnp.where(i > 0, zh_ref[...], 0.0)
        ext = jnp.concatenate([halo, zc_ref[...]], axis=0)
        t = i * tm + lax.broadcasted_iota(jnp.int32, (tm, 1), 0)
        for g, w in enumerate(POOL_WINDOWS):
            lanes = slice(g * GROUP_DIM, (g + 1) * GROUP_DIM)
            e = ext[:, lanes]
            s, k = e, 1
            while k < w:
                s = s + pltpu.roll(s, k, 0)
                k *= 2
            cnt = jnp.minimum(t + 1, w).astype(F32)
            dg = (s[POOL_HALO:, :] / cnt - e[POOL_HALO:, :]).astype(BF16)
            y = jnp.dot(dg, pw_ref[g].astype(BF16), preferred_element_type=F32)
            d_ref[:, lanes] = dg
            p_ref[:, lanes] = (y * sc_ref[:, lanes]).astype(BF16)

    return pl.pallas_call(
        body, name=name, grid=(m // tm,),
        in_specs=[pl.BlockSpec((tm, POOL_DIM), lambda i: (i, 0)),
                  pl.BlockSpec((POOL_HALO, POOL_DIM), lambda i: (jnp.maximum(i * halo_blocks - 1, 0), 0)),
                  _resident((4, GROUP_DIM, GROUP_DIM), (0, 0, 0)), _resident((1, POOL_DIM), (0, 0))],
        out_specs=[pl.BlockSpec((tm, POOL_DIM), lambda i: (i, 0)), pl.BlockSpec((tm, POOL_DIM), lambda i: (i, 0))],
        out_shape=[_sds((m, POOL_DIM), BF16), _sds((m, POOL_DIM), BF16)],
        compiler_params=_cp("parallel"),
    )(zq, zq, pool_w, scale)


def _pool_bwd(dp, d, pool_w, scale, name):
    m = dp.shape[0]
    tm = _row_tile(m)
    nb = m // tm
    halo_blocks = tm // POOL_HALO
    rows = tm + POOL_HALO

    def body(dpc_ref, dph_ref, d_ref, pw_ref, sc_ref, du_ref, dpw_ref, dsc_ref):
        i = pl.program_id(0)

        @pl.when(i == 0)
        def _():
            dpw_ref[...] = jnp.zeros_like(dpw_ref)
            dsc_ref[...] = jnp.zeros_like(dsc_ref)

        halo = jnp.where(i < nb - 1, dph_ref[...], 0.0)
        dpc = dpc_ref[...]
        ext = jnp.concatenate([dpc, halo], axis=0)
        t = i * tm + lax.broadcasted_iota(jnp.int32, (rows, 1), 0)
        for g, w in enumerate(POOL_WINDOWS):
            lanes = slice(g * GROUP_DIM, (g + 1) * GROUP_DIM)
            pwb = pw_ref[g].astype(BF16)
            dyb = (ext[:, lanes] * sc_ref[:, lanes]).astype(BF16)
            dd = lax.dot_general(dyb, pwb, _NT, preferred_element_type=F32)
            cnt = jnp.minimum(t + 1, w).astype(F32)
            s, k = dd / cnt, 1
            while k < w:
                s = s + pltpu.roll(s, rows - k, 0)
                k *= 2
            du_ref[:, lanes] = (s[:tm, :] - dd[:tm, :]).astype(BF16)
            dcur = d_ref[:, lanes]
            y = jnp.dot(dcur, pwb, preferred_element_type=F32)
            dsc_ref[:, lanes] += jnp.sum(dpc[:, lanes] * y, axis=0, keepdims=True)
            dpw_ref[g] += lax.dot_general(dcur, dyb[:tm, :], _TN, preferred_element_type=F32)

    return pl.pallas_call(
        body, name=name, grid=(nb,),
        in_specs=[pl.BlockSpec((tm, POOL_DIM), lambda i: (i, 0)),
                  pl.BlockSpec((POOL_HALO, POOL_DIM), lambda i: (jnp.minimum((i + 1) * halo_blocks, nb * halo_blocks - 1), 0)),
                  pl.BlockSpec((tm, POOL_DIM), lambda i: (i, 0)),
                  _resident((4, GROUP_DIM, GROUP_DIM), (0, 0, 0)), _resident((1, POOL_DIM), (0, 0))],
        out_specs=[pl.BlockSpec((tm, POOL_DIM), lambda i: (i, 0)),
                   pl.BlockSpec((4, GROUP_DIM, GROUP_DIM), lambda i: (0, 0, 0)),
                   pl.BlockSpec((1, POOL_DIM), lambda i: (0, 0))],
        out_shape=[_sds((m, POOL_DIM), BF16), _sds((4, GROUP_DIM, GROUP_DIM), F32), _sds((1, POOL_DIM), F32)],
        compiler_params=_cp("arbitrary"),
    )(dp, dp, d, pool_w, scale)


def _qk_norm_rope(xv, gain, cos, sin):
    width = xv.shape[1]
    r = lax.rsqrt(_segment_mean(xv * xv, width) + EPS)
    y = xv * r * gain
    return y * _tile_lanes(cos, width) + _rope_partner(y) * _tile_lanes(sin, width)


def _qk_prep(zq, qgain, kgain, cos, sin, name):
    m = zq.shape[0]
    tm = _row_tile(m)

    def body(q_ref, kv_ref, qg_ref, kg_ref, cos_ref, sin_ref, qr_ref, kr_ref, v_ref):
        cos_v, sin_v = cos_ref[...], sin_ref[...]
        qr_ref[...] = _qk_norm_rope(q_ref[...], qg_ref[...], cos_v, sin_v).astype(BF16)
        kv = kv_ref[...]
        kr_ref[...] = _qk_norm_rope(kv[:, :KV_DIM], kg_ref[...], cos_v, sin_v).astype(BF16)
        v_ref[...] = kv[:, KV_DIM:].astype(BF16)

    return pl.pallas_call(
        body, name=name, grid=(m // tm,),
        in_specs=[pl.BlockSpec((tm, ATTN_DIM), lambda i: (i, 1)), pl.BlockSpec((tm, 2 * KV_DIM), lambda i: (i, 4)),
                  _resident((1, ATTN_DIM), (0, 0)), _resident((1, KV_DIM), (0, 0)),
                  pl.BlockSpec((tm, 128), lambda i: (i, 0)), pl.BlockSpec((tm, 128), lambda i: (i, 0))],
        out_specs=[pl.BlockSpec((tm, ATTN_DIM), lambda i: (i, 0)), pl.BlockSpec((tm, KV_DIM), lambda i: (i, 0)),
                   pl.BlockSpec((tm, KV_DIM), lambda i: (i, 0))],
        out_shape=[_sds((m, ATTN_DIM), BF16), _sds((m, KV_DIM), BF16), _sds((m, KV_DIM), BF16)],
        compiler_params=_cp("parallel"),
    )(zq, zq, qgain, kgain, cos, sin)


GROUP_ROWS = GQA_GROUP * ATTN_BLOCK


def _band_mask(n):
    qi = lax.broadcasted_iota(jnp.int32, (GROUP_ROWS, 2 * ATTN_BLOCK), 0) & (ATTN_BLOCK - 1)
    ki = lax.broadcasted_iota(jnp.int32, (GROUP_ROWS, 2 * ATTN_BLOCK), 1)
    diff = qi + ATTN_BLOCK - ki
    return (diff >= 0) & (diff < ATTN_BLOCK) & ((ki >= ATTN_BLOCK) | (n > 0))


def _stack_group(v, kvh):
    heads = range(kvh * GQA_GROUP, (kvh + 1) * GQA_GROUP)
    return jnp.concatenate([v[:, h * HEAD_DIM:(h + 1) * HEAD_DIM] for h in heads], axis=0)


def _unstack_groups(groups):
    return jnp.concatenate([grp[g * ATTN_BLOCK:(g + 1) * ATTN_BLOCK, :] for grp in groups for g in range(GQA_GROUP)],
                           axis=1)


def _group_sinks(sk_ref, kvh):
    row_head = lax.broadcasted_iota(jnp.int32, (GROUP_ROWS, 1), 0) >> 7
    col = jnp.full((GROUP_ROWS, 1), sk_ref[kvh * GQA_GROUP], F32)
    for g in range(1, GQA_GROUP):
        col = jnp.where(row_head == g, sk_ref[kvh * GQA_GROUP + g], col)
    return col


def _head_probs(qh, kh, mask, sink):
    s = lax.dot_general(qh, kh, _NT, preferred_element_type=F32) * ATTN_SCALE
    s = jnp.where(mask, s, NEG_BIG)
    mx = jnp.maximum(jnp.max(s, axis=-1, keepdims=True), sink)
    p = jnp.exp(s - mx)
    es = jnp.exp(sink - mx)
    den = jnp.sum(p, axis=-1, keepdims=True) + es
    return p / den, es / den


def _attn_fwd(qr, kr, vb, sinks, name):
    m = qr.shape[0]
    nb = m // ATTN_BLOCK

    def body(q_ref, kp_ref, kc_ref, vp_ref, vc_ref, sk_ref, o_ref):
        mask = _band_mask(pl.program_id(0))
        qv = q_ref[...]
        kk = jnp.concatenate([kp_ref[...], kc_ref[...]], axis=0)
        vv = jnp.concatenate([vp_ref[...], vc_ref[...]], axis=0)
        outs = []
        for kvh in range(N_KV_HEADS):
            kv_lanes = slice(kvh * HEAD_DIM, (kvh + 1) * HEAD_DIM)
            p, _ = _head_probs(_stack_group(qv, kvh), kk[:, kv_lanes], mask, _group_sinks(sk_ref, kvh))
            outs.append(jnp.dot(p.astype(BF16), vv[:, kv_lanes], preferred_element_type=F32))
        o_ref[...] = _unstack_groups(outs).astype(BF16)

    prev = lambda n: (jnp.maximum(n - 1, 0), 0)
    cur = lambda n: (n, 0)
    return pl.pallas_call(
        body, name=name, grid=(nb,),
        in_specs=[pl.BlockSpec((ATTN_BLOCK, ATTN_DIM), cur),
                  pl.BlockSpec((ATTN_BLOCK, KV_DIM), prev), pl.BlockSpec((ATTN_BLOCK, KV_DIM), cur),
                  pl.BlockSpec((ATTN_BLOCK, KV_DIM), prev), pl.BlockSpec((ATTN_BLOCK, KV_DIM), cur),
                  pl.BlockSpec(memory_space=pltpu.SMEM)],
        out_specs=pl.BlockSpec((ATTN_BLOCK, ATTN_DIM), cur),
        out_shape=_sds((m, ATTN_DIM), BF16),
        compiler_params=_cp("parallel"),
    )(qr, kr, kr, vb, vb, sinks)


def _attn_bwd(do, qr, kr, vb, sinks, name):
    m = qr.shape[0]
    nb = m // ATTN_BLOCK

    def body(do_ref, q_ref, kp_ref, kc_ref, vp_ref, vc_ref, sk_ref, dq_ref, dk_ref, dv_ref, ds_ref, carry_k, carry_v):
        n = pl.program_id(0)

        @pl.when(n == 0)
        def _():
            carry_k[...] = jnp.zeros_like(carry_k)
            carry_v[...] = jnp.zeros_like(carry_v)
            ds_ref[...] = jnp.zeros_like(ds_ref)

        @pl.when(n < nb)
        def _():
            mask = _band_mask(n)
            qv = q_ref[...]
            dov = do_ref[...]
            kk = jnp.concatenate([kp_ref[...], kc_ref[...]], axis=0)
            vv = jnp.concatenate([vp_ref[...], vc_ref[...]], axis=0)
            lane = lax.broadcasted_iota(jnp.int32, (1, 128), 1)
            dsink = jnp.zeros((1, 128), F32)
            dqs, dks, dvs = [], [], []
            for kvh in range(N_KV_HEADS):
                kv_lanes = slice(kvh * HEAD_DIM, (kvh + 1) * HEAD_DIM)
                kh, vh = kk[:, kv_lanes], vv[:, kv_lanes]
                qg = _stack_group(qv, kvh)
                dog = _stack_group(dov, kvh).astype(BF16)
                p, ps = _head_probs(qg, kh, mask, _group_sinks(sk_ref, kvh))
                dpr = lax.dot_general(dog, vh, _NT, preferred_element_type=F32)
                delta = jnp.sum(p * dpr, axis=-1, keepdims=True)
                dsb = (p * (dpr - delta) * ATTN_SCALE).astype(BF16)
                sink_term = ps * delta
                for g in range(GQA_GROUP):
                    rows = slice(g * ATTN_BLOCK, (g + 1) * ATTN_BLOCK)
                    dsink = dsink + jnp.where(lane == kvh * GQA_GROUP + g, -jnp.sum(sink_term[rows, :]), 0.0)
                dqs.append(jnp.dot(dsb, kh, preferred_element_type=F32))
                dks.append(lax.dot_general(dsb, qg, _TN, preferred_element_type=F32))
                dvs.append(lax.dot_general(p.astype(BF16), dog, _TN, preferred_element_type=F32))
            dq_ref[...] = _unstack_groups(dqs)
            dkk = jnp.concatenate(dks, axis=1)
            dvv = jnp.concatenate(dvs, axis=1)
            dk_ref[...] = carry_k[...] + dkk[:ATTN_BLOCK, :]
            dv_ref[...] = carry_v[...] + dvv[:ATTN_BLOCK, :]
            carry_k[...] = dkk[ATTN_BLOCK:, :]
            carry_v[...] = dvv[ATTN_BLOCK:, :]
            ds_ref[...] += dsink

        @pl.when(n == nb)
        def _():
            dk_ref[...] = carry_k[...]
            dv_ref[...] = carry_v[...]

    cur = lambda n: (jnp.minimum(n, nb - 1), 0)
    prev = lambda n: (jnp.clip(n - 1, 0, nb - 1), 0)
    return pl.pallas_call(
        body, name=name, grid=(nb + 1,),
        in_specs=[pl.BlockSpec((ATTN_BLOCK, ATTN_DIM), cur), pl.BlockSpec((ATTN_BLOCK, ATTN_DIM), cur),
                  pl.BlockSpec((ATTN_BLOCK, KV_DIM), prev), pl.BlockSpec((ATTN_BLOCK, KV_DIM), cur),
                  pl.BlockSpec((ATTN_BLOCK, KV_DIM), prev), pl.BlockSpec((ATTN_BLOCK, KV_DIM), cur),
                  pl.BlockSpec(memory_space=pltpu.SMEM)],
        out_specs=[pl.BlockSpec((ATTN_BLOCK, ATTN_DIM), cur), pl.BlockSpec((ATTN_BLOCK, KV_DIM), prev),
                   pl.BlockSpec((ATTN_BLOCK, KV_DIM), prev), pl.BlockSpec((1, 128), lambda n: (0, 0))],
        out_shape=[_sds((m, ATTN_DIM), F32), _sds((m, KV_DIM), F32), _sds((m, KV_DIM), F32), _sds((1, 128), F32)],
        scratch_shapes=[pltpu.VMEM((ATTN_BLOCK, KV_DIM), F32), pltpu.VMEM((ATTN_BLOCK, KV_DIM), F32)],
        compiler_params=_cp("arbitrary"),
    )(do, qr, kr, kr, vb, vb, sinks)


def _qk_norm_rope_bwd(dout, xv, gain, cos, sin):
    width = xv.shape[1]
    r = lax.rsqrt(_segment_mean(xv * xv, width) + EPS)
    xn = xv * r
    dy = dout * _tile_lanes(cos, width) + _rope_partner(dout * _tile_lanes(sin, width))
    dxn = dy * gain
    dx = r * (dxn - xn * _segment_mean(dxn * xn, width))
    return dx, jnp.sum(dy * xn, axis=0, keepdims=True)


def _fold_heads(v):
    out = v[:, :HEAD_DIM]
    for h in range(1, v.shape[1] // HEAD_DIM):
        out = out + v[:, h * HEAD_DIM:(h + 1) * HEAD_DIM]
    return out


def _qk_bwd(dq, dk, dv, zq, qgain, kgain, cos, sin, name):
    m = zq.shape[0]
    tm = _row_tile(m)

    def body(dq_ref, dk_ref, dv_ref, q_ref, kv_ref, qg_ref, kg_ref, cos_ref, sin_ref, dz_ref, dqg_ref, dkg_ref):
        @pl.when(pl.program_id(0) == 0)
        def _():
            dqg_ref[...] = jnp.zeros_like(dqg_ref)
            dkg_ref[...] = jnp.zeros_like(dkg_ref)

        cos_v, sin_v = cos_ref[...], sin_ref[...]
        dxq, dgq = _qk_norm_rope_bwd(dq_ref[...], q_ref[...], qg_ref[...], cos_v, sin_v)
        dxk, dgk = _qk_norm_rope_bwd(dk_ref[...], kv_ref[:, :KV_DIM], kg_ref[...], cos_v, sin_v)
        dz_ref[:, :ATTN_DIM] = dxq.astype(BF16)
        dz_ref[:, ATTN_DIM:ATTN_DIM + KV_DIM] = dxk.astype(BF16)
        dz_ref[:, ATTN_DIM + KV_DIM:] = dv_ref[...].astype(BF16)
        dqg_ref[...] += _fold_heads(dgq)
        dkg_ref[...] += _fold_heads(dgk)

    row = lambda i: (i, 0)
    return pl.pallas_call(
        body, name=name, grid=(m // tm,),
        in_specs=[pl.BlockSpec((tm, ATTN_DIM), row), pl.BlockSpec((tm, KV_DIM), row), pl.BlockSpec((tm, KV_DIM), row),
                  pl.BlockSpec((tm, ATTN_DIM), lambda i: (i, 1)), pl.BlockSpec((tm, 2 * KV_DIM), lambda i: (i, 4)),
                  _resident((1, ATTN_DIM), (0, 0)), _resident((1, KV_DIM), (0, 0)),
                  pl.BlockSpec((tm, 128), row), pl.BlockSpec((tm, 128), row)],
        out_specs=[pl.BlockSpec((tm, ATTN_DIM + 2 * KV_DIM), row), pl.BlockSpec((1, HEAD_DIM), lambda i: (0, 0)),
                   pl.BlockSpec((1, HEAD_DIM), lambda i: (0, 0))],
        out_shape=[_sds((m, ATTN_DIM + 2 * KV_DIM), BF16), _sds((1, HEAD_DIM), F32), _sds((1, HEAD_DIM), F32)],
        compiler_params=_cp("arbitrary"),
    )(dq, dk, dv, zq, zq, qgain, kgain, cos, sin)


def _merge_fwd(x, p, o, zg, w_pb, w_ab, w_out, name):
    m, d = x.shape
    tm = _row_tile(m)

    def body(x_ref, p_ref, o_ref, zg_ref, wpb_ref, wab_ref, wo_ref, xo_ref, mix_ref):
        a = jnp.dot(p_ref[...], wpb_ref[...], preferred_element_type=F32)
        b = jnp.dot(o_ref[...], wab_ref[...], preferred_element_type=F32)
        mix = (jax.nn.sigmoid(zg_ref[:, :d]) * a + jax.nn.sigmoid(zg_ref[:, d:]) * b).astype(BF16)
        mix_ref[...] = mix
        xo_ref[...] = x_ref[...] + jnp.dot(mix, wo_ref[...], preferred_element_type=F32)

    row = lambda i: (i, 0)
    return pl.pallas_call(
        body, name=name, grid=(m // tm,),
        in_specs=[pl.BlockSpec((tm, d), row), pl.BlockSpec((tm, POOL_DIM), row), pl.BlockSpec((tm, ATTN_DIM), row),
                  pl.BlockSpec((tm, 2 * d), row), _resident((POOL_DIM, d), (0, 0)), _resident((ATTN_DIM, d), (0, 0)),
                  _resident((d, d), (0, 0))],
        out_specs=[pl.BlockSpec((tm, d), row)] * 2,
        out_shape=[_sds((m, d), F32), _sds((m, d), BF16)],
        compiler_params=_cp("parallel"),
    )(x, p, o, zg, w_pb, w_ab, w_out)


def _merge_bwd(dy, p, o, zg, w_out, w_pb, w_ab, name):
    m, d = dy.shape
    tm = _row_tile(m)

    def body(dy_ref, p_ref, o_ref, zg_ref, wo_ref, wpb_ref, wab_ref, dyb_ref, da_ref, db_ref, dp_ref, do_ref, dzg_ref):
        dyb = dy_ref[...].astype(BF16)
        dyb_ref[...] = dyb
        dmix = lax.dot_general(dyb, wo_ref[...], _NT, preferred_element_type=F32)
        gp = jax.nn.sigmoid(zg_ref[:, :d])
        ga = jax.nn.sigmoid(zg_ref[:, d:])
        da = (dmix * gp).astype(BF16)
        db = (dmix * ga).astype(BF16)
        da_ref[...] = da
        db_ref[...] = db
        a = jnp.dot(p_ref[...], wpb_ref[...], preferred_element_type=F32)
        b = jnp.dot(o_ref[...], wab_ref[...], preferred_element_type=F32)
        dzg_ref[:, :d] = (dmix * a * gp * (1.0 - gp)).astype(BF16)
        dzg_ref[:, d:] = (dmix * b * ga * (1.0 - ga)).astype(BF16)
        dp_ref[...] = lax.dot_general(da, wpb_ref[...], _NT, preferred_element_type=F32)
        do_ref[...] = lax.dot_general(db, wab_ref[...], _NT, preferred_element_type=F32)

    row = lambda i: (i, 0)
    return pl.pallas_call(
        body, name=name, grid=(m // tm,),
        in_specs=[pl.BlockSpec((tm, d), row), pl.BlockSpec((tm, POOL_DIM), row), pl.BlockSpec((tm, ATTN_DIM), row),
                  pl.BlockSpec((tm, 2 * d), row), _resident((d, d), (0, 0)), _resident((POOL_DIM, d), (0, 0)),
                  _resident((ATTN_DIM, d), (0, 0))],
        out_specs=[pl.BlockSpec((tm, d), row), pl.BlockSpec((tm, d), row), pl.BlockSpec((tm, d), row),
                   pl.BlockSpec((tm, POOL_DIM), row), pl.BlockSpec((tm, ATTN_DIM), row), pl.BlockSpec((tm, 2 * d), row)],
        out_shape=[_sds((m, d), BF16), _sds((m, d), BF16), _sds((m, d), BF16), _sds((m, POOL_DIM), F32),
                   _sds((m, ATTN_DIM), F32), _sds((m, 2 * d), BF16)],
        compiler_params=_cp("parallel"),
    )(dy, p, o, zg, w_out, w_pb, w_ab)


def _mix_bwd_x(dy, x, ln, dzp, dzqkv, dzg, w_in, name):
    m, d = dy.shape
    n_in = w_in.shape[1]
    tm = _row_tile(m)

    def body(dy_ref, x_ref, ln_ref, dzp_ref, dzq_ref, dzg_ref, w_ref, dx_ref, dln_ref):
        @pl.when(pl.program_id(0) == 0)
        def _():
            dln_ref[...] = jnp.zeros_like(dln_ref)

        dh = lax.dot_general(dzp_ref[...], w_ref[:, :POOL_DIM], _NT, preferred_element_type=F32)
        dh += lax.dot_general(dzq_ref[...], w_ref[:, POOL_DIM:QKV_END], _NT, preferred_element_type=F32)
        dh += lax.dot_general(dzg_ref[...], w_ref[:, QKV_END:], _NT, preferred_element_type=F32)
        dx, dgain = _rms_bwd(dh, x_ref[...], ln_ref[...])
        dx_ref[...] = dy_ref[...] + dx
        dln_ref[...] += jnp.sum(dgain, axis=0, keepdims=True)

    row = lambda i: (i, 0)
    return pl.pallas_call(
        body, name=name, grid=(m // tm,),
        in_specs=[pl.BlockSpec((tm, d), row), pl.BlockSpec((tm, d), row), _resident((1, d), (0, 0)),
                  pl.BlockSpec((tm, POOL_DIM), row), pl.BlockSpec((tm, QKV_END - POOL_DIM), row),
                  pl.BlockSpec((tm, n_in - QKV_END), row), _resident((d, n_in), (0, 0))],
        out_specs=[pl.BlockSpec((tm, d), row), pl.BlockSpec((1, d), lambda i: (0, 0))],
        out_shape=[_sds((m, d), F32), _sds((1, d), F32)],
        compiler_params=_cp("arbitrary"),
    )(dy, x, ln, dzp, dzqkv, dzg, w_in)


def _loss_head(y, target, name):
    m, d = y.shape
    tm = _row_tile(m)

    def body(y_ref, t_ref, loss_ref, dy_ref):
        @pl.when(pl.program_id(0) == 0)
        def _():
            loss_ref[...] = jnp.zeros_like(loss_ref)

        diff = y_ref[...] - t_ref[...]
        dy_ref[...] = diff * (1.0 / d)
        loss_ref[...] += 0.5 * jnp.sum(jnp.mean(diff * diff, axis=-1, keepdims=True), axis=0, keepdims=True)

    row = lambda i: (i, 0)
    return pl.pallas_call(
        body, name=name, grid=(m // tm,),
        in_specs=[pl.BlockSpec((tm, d), row), pl.BlockSpec((tm, d), row)],
        out_specs=[pl.BlockSpec((1, 1), lambda i: (0, 0)), pl.BlockSpec((tm, d), row)],
        out_shape=[_sds((1, 1), F32), _sds((m, d), F32)],
        compiler_params=_cp("arbitrary"),
    )(y, target)


def _adamw_math(g, w, m, v):
    m2 = ADAM_B1 * m + (1.0 - ADAM_B1) * g
    v2 = ADAM_B2 * v + (1.0 - ADAM_B2) * (g * g)
    m_hat = m2 / (1.0 - ADAM_B1 ** ADAM_STEP)
    v_hat = v2 / (1.0 - ADAM_B2 ** ADAM_STEP)
    delta = -ADAM_LR * (m_hat / (jnp.sqrt(v_hat) + ADAM_EPS) + ADAM_WD * w)
    return delta, m2, v2


def _sum_parts(parts_ref):
    g = parts_ref[0].astype(F32)
    for s in range(1, N_DEV):
        g = g + parts_ref[s].astype(F32)
    return g


def _adamw_sharded(parts, w, m, v, name):
    n_layers, rows, cols = w.shape
    tr = next(t for t in (512, 256, 128, 64, 32, 16, 8) if rows % t == 0 and t * cols <= ADAMW_BLOCK_ELEMS)
    nr = rows // tr

    def body(*refs):
        part_refs = refs[:n_layers]
        w_ref, m_ref, v_ref, g_out, d_out, m_out, v_out = refs[n_layers:]
        layer = pl.program_id(0)
        for l in range(n_layers):
            @pl.when(layer == l)
            def _(l=l):
                g = _sum_parts(part_refs[l])
                delta, m2, v2 = _adamw_math(g, w_ref[...], m_ref[...], v_ref[...])
                g_out[...] = g
                d_out[...] = delta
                m_out[...] = m2
                v_out[...] = v2

    def part_map(l):
        return lambda layer, r: (0, jnp.where(layer == l, r, jnp.where(layer < l, 0, nr - 1)), 0)

    wspec = pl.BlockSpec((None, tr, cols), lambda layer, r: (layer, r, 0))
    return pl.pallas_call(
        body, name=name, grid=(n_layers, nr),
        in_specs=[pl.BlockSpec((N_DEV, tr, cols), part_map(l)) for l in range(n_layers)] + [wspec] * 3,
        out_specs=[wspec] * 4,
        out_shape=[_sds(w.shape, F32)] * 4,
        compiler_params=_cp("arbitrary", "arbitrary"),
    )(*parts, w, m, v)


def _adamw_packed(parts, w, m, v, name):
    def body(p_ref, w_ref, m_ref, v_ref, g_out, d_out, m_out, v_out):
        g = _sum_parts(p_ref)
        delta, m2, v2 = _adamw_math(g, w_ref[...], m_ref[...], v_ref[...])
        g_out[...] = g
        d_out[...] = delta
        m_out[...] = m2
        v_out[...] = v2

    return pl.pallas_call(
        body, name=name, out_shape=[_sds(w.shape, F32)] * 4,
        compiler_params=pltpu.CompilerParams(vmem_limit_bytes=VMEM_LIMIT_BYTES),
    )(parts, w, m, v)


_SMALL = ("ln_ffn1", "ln_mix", "pool_w", "pool_scale", "q_norm", "k_norm", "sinks", "ln_ffn2")


def _pack_small(arrs):
    rows = []
    for a in arrs:
        flat = a.reshape(-1)
        pad = (-flat.shape[0]) % 1024
        rows.append(jnp.pad(flat, (0, pad)).reshape(-1, 128))
    return jnp.concatenate(rows, axis=0)


def _unpack_small(packed, like):
    out, r0 = [], 0
    for a in like:
        size = a.size
        nrows = (size + 1023) // 1024 * 8
        out.append(packed[r0:r0 + nrows].reshape(-1)[:size].reshape(a.shape))
        r0 += nrows
    return out


def _rope_tables(m):
    pos = jnp.arange(m, dtype=F32)
    inv_freq = ROPE_THETA ** (-jnp.arange(0, ROT_DIM, 2, dtype=F32) / ROT_DIM)
    ang = pos[:, None] * inv_freq[None, :]
    cos8, sin8 = jnp.cos(ang), jnp.sin(ang)
    rest = HEAD_DIM - ROT_DIM
    cos64 = jnp.concatenate([cos8, cos8, jnp.ones((m, rest), F32)], axis=1)
    sin64 = jnp.concatenate([-sin8, sin8, jnp.zeros((m, rest), F32)], axis=1)
    return jnp.tile(cos64, (1, 2)), jnp.tile(sin64, (1, 2))


def _to_shard_major_cols(w):
    k = w.shape[0]
    return w.reshape(k, N_DEV, -1).transpose(1, 0, 2)


def _from_shard_major_cols(w):
    return w.transpose(1, 0, 2).reshape(w.shape[1], -1)


def kernel(x, ln_ffn1, w_ffn1_gu, w_ffn1_down, ln_mix, w_in, pool_w, pool_scale, w_pool_branch, q_norm, k_norm, sinks, w_attn_branch, w_out, ln_ffn2, w_ffn2_gu, w_ffn2_down, loss_target, m_ln_ffn1, m_w_ffn1_gu, m_w_ffn1_down, m_ln_mix, m_w_in, m_pool_w, m_pool_scale, m_w_pool_branch, m_q_norm, m_k_norm, m_sinks, m_w_attn_branch, m_w_out, m_ln_ffn2, m_w_ffn2_gu, m_w_ffn2_down, v_ln_ffn1, v_w_ffn1_gu, v_w_ffn1_down, v_ln_mix, v_w_in, v_pool_w, v_pool_scale, v_w_pool_branch, v_q_norm, v_k_norm, v_sinks, v_w_attn_branch, v_w_out, v_ln_ffn2, v_w_ffn2_gu, v_w_ffn2_down):
    weights = dict(ln_ffn1=ln_ffn1, w_ffn1_gu=w_ffn1_gu, w_ffn1_down=w_ffn1_down, ln_mix=ln_mix, w_in=w_in, pool_w=pool_w,
                   pool_scale=pool_scale, w_pool_branch=w_pool_branch, q_norm=q_norm, k_norm=k_norm, sinks=sinks,
                   w_attn_branch=w_attn_branch, w_out=w_out, ln_ffn2=ln_ffn2, w_ffn2_gu=w_ffn2_gu, w_ffn2_down=w_ffn2_down)
    mom_m = dict(ln_ffn1=m_ln_ffn1, w_ffn1_gu=m_w_ffn1_gu, w_ffn1_down=m_w_ffn1_down, ln_mix=m_ln_mix, w_in=m_w_in,
                 pool_w=m_pool_w, pool_scale=m_pool_scale, w_pool_branch=m_w_pool_branch, q_norm=m_q_norm, k_norm=m_k_norm,
                 sinks=m_sinks, w_attn_branch=m_w_attn_branch, w_out=m_w_out, ln_ffn2=m_ln_ffn2, w_ffn2_gu=m_w_ffn2_gu,
                 w_ffn2_down=m_w_ffn2_down)
    mom_v = dict(ln_ffn1=v_ln_ffn1, w_ffn1_gu=v_w_ffn1_gu, w_ffn1_down=v_w_ffn1_down, ln_mix=v_ln_mix, w_in=v_w_in,
                 pool_w=v_pool_w, pool_scale=v_pool_scale, w_pool_branch=v_w_pool_branch, q_norm=v_q_norm, k_norm=v_k_norm,
                 sinks=v_sinks, w_attn_branch=v_w_attn_branch, w_out=v_w_out, ln_ffn2=v_ln_ffn2, w_ffn2_gu=v_w_ffn2_gu,
                 w_ffn2_down=v_w_ffn2_down)
    order = ("ln_ffn1", "w_ffn1_gu", "w_ffn1_down", "ln_mix", "w_in", "pool_w", "pool_scale", "w_pool_branch", "q_norm",
             "k_norm", "sinks", "w_attn_branch", "w_out", "ln_ffn2", "w_ffn2_gu", "w_ffn2_down")
    big = ("w_ffn1_gu", "w_ffn1_down", "w_in", "w_pool_branch", "w_attn_branch", "w_out", "w_ffn2_gu", "w_ffn2_down")

    n_layers = ln_ffn1.shape[0]
    seq, d = x.shape[-2], x.shape[-1]
    xs = x.reshape(seq, d)
    target = loss_target.reshape(seq, d)
    cos, sin = _rope_tables(seq)

    def layer_shards(l):
        return [weights[k][l].astype(BF16) for k in big]

    def layer_weights(l, full):
        g = dict(zip(big, full))
        c_ff = g["w_ffn1_gu"].shape[-1]
        return dict(
            gu1=g["w_ffn1_gu"].reshape(2, 4, d, c_ff), down1=g["w_ffn1_down"].reshape(4, c_ff, d),
            gu2=g["w_ffn2_gu"].reshape(2, 4, d, c_ff), down2=g["w_ffn2_down"].reshape(4, c_ff, d),
            w_in=_from_shard_major_cols(g["w_in"]), w_pb=_from_shard_major_cols(g["w_pool_branch"]),
            w_ab=_from_shard_major_cols(g["w_attn_branch"]), w_out=g["w_out"].reshape(d, d),
            ln1=ln_ffn1[l][None], ln_mix=ln_mix[l][None], ln2=ln_ffn2[l][None], pool_w=pool_w[l],
            pool_scale=pool_scale[l][None], sinks=sinks[l],
            qgain=jnp.tile(q_norm[l], N_Q_HEADS)[None], kgain=jnp.tile(k_norm[l], N_KV_HEADS)[None])

    gathered = [layer_weights(0, _all_gather_many(layer_shards(0), name="gather_weights_l0"))]
    saved = []
    cur = xs
    for l in range(n_layers):
        lw = gathered[l]
        s = dict(x0=cur)
        in_flight, token = None, None
        if l + 1 < n_layers:
            in_flight, token = _exchange_start(layer_shards(l + 1), scatter=False, name=f"gather_start_l{l + 1}")
        s["h1"], s["gu1"], act1 = _ffn_up(cur, lw["ln1"], lw["gu1"], name=f"ffn1_up_l{l}", after=token)
        s["act1"] = act1
        x1 = _ffn_down(cur, act1, lw["down1"], name=f"ffn1_down_l{l}")
        s["x1"] = x1
        s["h2"], zq, zg = _mix_in(x1, lw["ln_mix"], lw["w_in"], name=f"mix_in_l{l}")
        s["zq"], s["zg"] = zq, zg
        s["d"], s["p"] = _pool_fwd(zq, lw["pool_w"], lw["pool_scale"], name=f"pool_fwd_l{l}")
        s["qr"], s["kr"], s["vb"] = _qk_prep(zq, lw["qgain"], lw["kgain"], cos, sin, name=f"qk_prep_l{l}")
        s["o"] = _attn_fwd(s["qr"], s["kr"], s["vb"], lw["sinks"], name=f"attn_fwd_l{l}")
        x2, s["mix"] = _merge_fwd(x1, s["p"], s["o"], zg, lw["w_pb"], lw["w_ab"], lw["w_out"],
                                                  name=f"merge_fwd_l{l}")
        s["x2"] = x2
        s["h3"], s["gu2"], act2 = _ffn_up(x2, lw["ln2"], lw["gu2"], name=f"ffn2_up_l{l}")
        s["act2"] = act2
        cur = _ffn_down(x2, act2, lw["down2"], name=f"ffn2_down_l{l}")
        saved.append(s)
        if in_flight is not None:
            gathered.append(layer_weights(l + 1, _exchange_wait(in_flight, cur, name=f"gather_wait_l{l + 1}")))

    loss_local, dy = _loss_head(cur, target, name="loss_head")
    loss = lax.psum(loss_local[0, 0], MESH_AXES)

    small_grads = {k: [None] * n_layers for k in _SMALL}
    received = {k: [None] * n_layers for k in big}
    big_late = ("w_ffn1_gu", "w_ffn1_down")
    big_early = tuple(k for k in big if k not in big_late)
    early_in_flight, late_in_flight = [None] * n_layers, [None] * n_layers
    token = None
    for l in reversed(range(n_layers)):
        lw, s = gathered[l], saved[l]
        c_ff = lw["gu1"].shape[-1]

        def ffn_bwd(dy, xin, h, gu, act, ln, wgu, wdown, tag, after):
            dyh, dgu = _ffn_bwd_act(dy, gu, wdown, name=f"{tag}_bwd_act_l{l}", after=after)
            dx, dln = _ffn_bwd_x(dy, xin, ln, dgu, wgu, name=f"{tag}_bwd_x_l{l}")
            dw_down = _matmul_tn(act, dyh[None], name=f"{tag}_dw_down_l{l}")
            dw_gu = _matmul_tn(h[None], dgu.reshape(N_DEV, seq, c_ff), name=f"{tag}_dw_gu_l{l}")
            return dx, dln, dw_gu, dw_down.reshape(N_DEV, c_ff // 2, d)

        dx2, dln2, dw_gu2, dw_down2 = ffn_bwd(dy, s["x2"], s["h3"], s["gu2"], s["act2"], lw["ln2"], lw["gu2"], lw["down2"], "ffn2", token)

        dyb, da, db, dp, do, dzg = _merge_bwd(dx2, s["p"], s["o"], s["zg"], lw["w_out"], lw["w_pb"], lw["w_ab"],
                                              name=f"merge_bwd_l{l}")
        dw_out = _matmul_tn(s["mix"][None], dyb[None], name=f"dw_out_l{l}")[0]
        dw_pb = _matmul_tn(s["p"][None], da[None], name=f"dw_pb_l{l}")[0]
        dw_ab = _matmul_tn(s["o"][None], db[None], name=f"dw_ab_l{l}")[0]
        dzp, dpw, dsc = _pool_bwd(dp, s["d"], lw["pool_w"], lw["pool_scale"], name=f"pool_bwd_l{l}")
        dq, dk, dv, dsinks = _attn_bwd(do, s["qr"], s["kr"], s["vb"], lw["sinks"], name=f"attn_bwd_l{l}")
        dzqkv, dqg, dkg = _qk_bwd(dq, dk, dv, s["zq"], lw["qgain"], lw["kgain"], cos, sin, name=f"qk_bwd_l{l}")
        dw_in = jnp.concatenate([_matmul_tn(s["h2"][None], dzp[None], name=f"dw_in_pool_l{l}")[0],
                                 _matmul_tn(s["h2"][None], dzqkv[None], name=f"dw_in_qkv_l{l}")[0],
                                 _matmul_tn(s["h2"][None], dzg[None], name=f"dw_in_gate_l{l}")[0]], axis=1)
        dx1, dlnm = _mix_bwd_x(dx2, s["x1"], lw["ln_mix"], dzp, dzqkv, dzg, lw["w_in"], name=f"mix_bwd_x_l{l}")

        partial = dict(w_in=_to_shard_major_cols(dw_in), w_pool_branch=_to_shard_major_cols(dw_pb),
                       w_attn_branch=_to_shard_major_cols(dw_ab), w_out=dw_out.reshape(N_DEV, d // N_DEV, d),
                       w_ffn2_gu=dw_gu2, w_ffn2_down=dw_down2)
        early_in_flight[l], token = _exchange_start([partial[k] for k in big_early], scatter=True,
                                                    name=f"grads_early_start_l{l}")

        dy, dln1, dw_gu1, dw_down1 = ffn_bwd(dx1, s["x0"], s["h1"], s["gu1"], s["act1"], lw["ln1"], lw["gu1"], lw["down1"], "ffn1", token)
        late_in_flight[l], token = _exchange_start([dw_gu1, dw_down1], scatter=True, name=f"grads_late_start_l{l}")
        small_grads["ln_ffn1"][l] = dln1[0]
        small_grads["ln_mix"][l] = dlnm[0]
        small_grads["ln_ffn2"][l] = dln2[0]
        small_grads["pool_w"][l] = dpw
        small_grads["pool_scale"][l] = dsc[0]
        small_grads["q_norm"][l] = dqg[0]
        small_grads["k_norm"][l] = dkg[0]
        small_grads["sinks"][l] = dsinks[0, :N_Q_HEADS]

    grad_x = dy.reshape(x.shape)
    after = token
    for l in reversed(range(n_layers)):
        got = _exchange_wait(early_in_flight[l], after, name=f"grads_early_wait_l{l}")
        after = got[0]
        for k, r in zip(big_early, got):
            received[k][l] = r

    grads, deltas, new_m, new_v = {}, {}, {}, {}

    def adamw(k):
        w = weights[k]
        shape2 = (n_layers, -1, w.shape[-1])
        parts = [r.reshape(N_DEV, -1, w.shape[-1]) for r in received[k]]
        outs = _adamw_sharded(parts, w.reshape(shape2), mom_m[k].reshape(shape2), mom_v[k].reshape(shape2), name=f"adamw_{k}")
        grads[k], deltas[k], new_m[k], new_v[k] = (o.reshape(w.shape) for o in outs)

    for k in big_early:
        adamw(k)

    small_w = [weights[k] for k in _SMALL]
    packed_g = _pack_small([jnp.stack(small_grads[k]).reshape(weights[k].shape) for k in _SMALL])
    (parts_small,) = _all_gather_many([packed_g], name="gather_small_grads")
    outs = _adamw_packed(parts_small, _pack_small(small_w), _pack_small([mom_m[k] for k in _SMALL]),
                         _pack_small([mom_v[k] for k in _SMALL]), name="adamw_small")
    for res, o in zip((grads, deltas, new_m, new_v), outs):
        for k, a in zip(_SMALL, _unpack_small(o, small_w)):
            res[k] = a

    after = outs[0]
    for l in reversed(range(n_layers)):
        got = _exchange_wait(late_in_flight[l], after, name=f"grads_late_wait_l{l}")
        after = got[0]
        for k, r in zip(big_late, got):
            received[k][l] = r
    for k in big_late:
        adamw(k)

    return (loss, grad_x, *[grads[k] for k in order], *[deltas[k] for k in order],
            *[new_m[k] for k in order], *[new_v[k] for k in order])
```

```python
import functools

import jax
import jax.numpy as jnp
from jax import lax
from jax.experimental import pallas as pl
from jax.experimental.pallas import tpu as pltpu

F32 = jnp.float32
BF16 = jnp.bfloat16

N_DEV = 8
MESH_AXES = ("x", "y", "c")
EPS = 1e-6

HEAD_DIM = 64
N_Q_HEADS = 8
N_KV_HEADS = 2
GQA_GROUP = N_Q_HEADS // N_KV_HEADS
ATTN_BLOCK = 128
ATTN_SCALE = HEAD_DIM ** -0.5
ROPE_THETA = 500000.0
ROT_DIM = 16
POOL_WINDOWS = (2, 4, 8, 16)
POOL_HALO = 16
GROUP_DIM = 128
POOL_DIM = 512
ATTN_DIM = 512
KV_DIM = 128
QKV_END = POOL_DIM + ATTN_DIM + 2 * KV_DIM

ADAM_LR = 0.001
ADAM_B1 = 0.9
ADAM_B2 = 0.999
ADAM_EPS = 1e-08
ADAM_WD = 0.01
ADAM_STEP = 10

ROW_TILE = 512
TN_ROW_TILE = 2048
VMEM_LIMIT_BYTES = 56 << 20
ADAMW_BLOCK_ELEMS = 96 * 1024
NEG_BIG = -1e30

_NT = (((1,), (1,)), ((), ()))
_TN = (((0,), (0,)), ((), ()))


def _cp(*sem):
    return pltpu.CompilerParams(dimension_semantics=sem, vmem_limit_bytes=VMEM_LIMIT_BYTES)


def _resident(block, index):
    return pl.BlockSpec(block, lambda *_: index, pipeline_mode=pl.Buffered(1))


def _row_tile(m):
    return min(ROW_TILE, m)


def _sds(shape, dtype):
    return jax.ShapeDtypeStruct(shape, dtype)


def _mesh_pos():
    return lax.axis_index("x"), lax.axis_index("y"), lax.axis_index("c")


def _all_gather_many(shards, name, after=None):
    n = len(shards)

    deps = [] if after is None else [after]

    def body(*refs):
        ins, outs = refs[:n], refs[n + len(deps):2 * n + len(deps)]
        send_sems, recv_sems, local_sems = refs[2 * n + len(deps):]
        x, y, c = _mesh_pos()
        me, sibling = (x, y, c), (x, y, 1 - c)
        chips = [(1 - x, y), (x, 1 - y), (1 - x, 1 - y)]

        def slot(a, pos):
            return outs[a].at[4 * pos[0] + 2 * pos[1] + pos[2]]

        def copy(a, k, block, to, src=None):
            return pltpu.make_async_remote_copy(
                src_ref=slot(a, block) if src is None else src, dst_ref=slot(a, block),
                send_sem=send_sems.at[a, k], recv_sem=recv_sems.at[a, k],
                device_id=to, device_id_type=pl.DeviceIdType.MESH)

        mine = [pltpu.make_async_copy(ins[a], slot(a, me), local_sems.at[a]) for a in range(n)]
        for cp in mine:
            cp.start()
        first = []
        for a in range(n):
            first.append(copy(a, 0, me, sibling, src=ins[a]))
            for j, chip in enumerate(chips):
                first.append(copy(a, 1 + j, me, (*chip, c), src=ins[a]))
        for cp in first:
            cp.start()
        passed = []
        for j, chip in enumerate(chips):
            for a in range(n):
                copy(a, 1 + j, (*chip, c), me).wait_recv()
                fwd = copy(a, 4 + j, (*chip, c), sibling)
                fwd.start()
                passed.append(fwd)
        for a in range(n):
            copy(a, 0, sibling, me).wait_recv()
        for j, chip in enumerate(chips):
            for a in range(n):
                copy(a, 4 + j, (*chip, 1 - c), me).wait_recv()
        for cp in first + passed:
            cp.wait_send()
        for cp in mine:
            cp.wait()

    any_spec = pl.BlockSpec(memory_space=pl.ANY)
    return pl.pallas_call(
        body, name=name,
        out_shape=[_sds((N_DEV,) + s.shape, s.dtype) for s in shards],
        in_specs=[any_spec] * (n + len(deps)), out_specs=[any_spec] * n,
        scratch_shapes=[pltpu.SemaphoreType.DMA((n, 7)), pltpu.SemaphoreType.DMA((n, 7)),
                        pltpu.SemaphoreType.DMA((n,))],
    )(*shards, *deps)


def _direct_copies(src, land, send_sem, recv_sem, local_sem, scatter):
    x, y, c = _mesh_pos()
    me = 4 * x + 2 * y + c
    local = pltpu.make_async_copy(src.at[me] if scatter else src, land.at[me], local_sem)
    remote = []
    for k in range(1, N_DEV):
        px = 1 - x if k & 4 else x
        py = 1 - y if k & 2 else y
        pc = 1 - c if k & 1 else c
        remote.append(pltpu.make_async_remote_copy(
            src_ref=src.at[4 * px + 2 * py + pc] if scatter else src, dst_ref=land.at[me],
            send_sem=send_sem, recv_sem=recv_sem, device_id=(px, py, pc), device_id_type=pl.DeviceIdType.MESH))
    seven = land.at[pl.ds(0, N_DEV - 1)]
    drain = pltpu.make_async_remote_copy(src_ref=seven, dst_ref=seven, send_sem=send_sem, recv_sem=recv_sem,
                                         device_id=(x, y, c), device_id_type=pl.DeviceIdType.MESH)
    return local, remote, drain


_HBM_SPEC = pl.BlockSpec(memory_space=pltpu.HBM)
_SEM_SPEC = pl.BlockSpec(memory_space=pltpu.SEMAPHORE)
_DATAFLOW = pltpu.SideEffectType.DATAFLOW_SIDE_EFFECTING
_SEMS_PER_ARRAY = 3


def _exchange_start(srcs, scatter, name):
    n = len(srcs)
    n_sems = _SEMS_PER_ARRAY * n
    land_shapes = [s.shape if scatter else (N_DEV,) + s.shape for s in srcs]

    def body(*refs):
        ins, lands, sems = refs[:n], refs[n:2 * n], refs[2 * n:2 * n + n_sems]
        for a in range(n):
            local, remote, _ = _direct_copies(ins[a], lands[a], *sems[3 * a:3 * a + 3], scatter)
            local.start()
            for cp in remote:
                cp.start()
        refs[-1][...] = jnp.zeros_like(refs[-1])

    outs = pl.pallas_call(
        body, name=name,
        out_shape=(*[pltpu.SemaphoreType.DMA(())] * n_sems,
                   *[pltpu.HBM(s.shape, s.dtype) for s in srcs],
                   *[pltpu.HBM(shape, s.dtype) for shape, s in zip(land_shapes, srcs)],
                   _sds((8, 128), F32)),
        in_specs=[_HBM_SPEC] * (2 * n),
        out_specs=(*[_SEM_SPEC] * n_sems, *[_HBM_SPEC] * (2 * n), pl.BlockSpec(memory_space=pltpu.VMEM)),
        input_output_aliases={i: n_sems + i for i in range(2 * n)},
        compiler_params=pltpu.CompilerParams(has_side_effects=_DATAFLOW),
    )(*[pltpu.with_memory_space_constraint(s, pltpu.HBM) for s in srcs],
      *[pltpu.with_memory_space_constraint(lax.empty(shape, s.dtype), pltpu.HBM) for shape, s in zip(land_shapes, srcs)])
    return (outs[:n_sems], outs[n_sems:n_sems + n], outs[n_sems + n:n_sems + 2 * n], scatter), outs[-1]


def _exchange_wait(state, after, name):
    sems, srcs, lands, scatter = state
    n = len(srcs)
    n_sems = len(sems)

    def body(*refs):
        ins, zones, ss = refs[:n], refs[n:2 * n], refs[2 * n:2 * n + n_sems]
        for a in range(n):
            local, _, drain = _direct_copies(ins[a], zones[a], *ss[3 * a:3 * a + 3], scatter)
            drain.wait_send()
            drain.wait_recv()
            local.wait()

    outs = pl.pallas_call(
        body, name=name,
        out_shape=(*[pltpu.HBM(s.shape, s.dtype) for s in srcs], *[pltpu.HBM(z.shape, z.dtype) for z in lands]),
        in_specs=[_HBM_SPEC] * (2 * n) + [_SEM_SPEC] * n_sems + [pl.BlockSpec(memory_space=pl.ANY)],
        out_specs=[_HBM_SPEC] * (2 * n),
        input_output_aliases={i: i for i in range(2 * n)},
        compiler_params=pltpu.CompilerParams(has_side_effects=_DATAFLOW),
    )(*srcs, *lands, *sems, after)
    return outs[n:]


def _rms_fwd(xv, gain):
    r = lax.rsqrt(jnp.mean(xv * xv, axis=-1, keepdims=True) + EPS)
    return xv * r * gain


def _rms_bwd(dh, xv, gain):
    r = lax.rsqrt(jnp.mean(xv * xv, axis=-1, keepdims=True) + EPS)
    xn = xv * r
    dxn = dh * gain
    dx = r * (dxn - xn * jnp.mean(dxn * xn, axis=-1, keepdims=True))
    return dx, dh * xn


def _silu_parts(g):
    s = jax.nn.sigmoid(g)
    return g * s, s * (1.0 + g * (1.0 - s))


def _segment_mean(v, width):
    r = lax.broadcasted_iota(jnp.int32, (width, width), 0) >> 6
    c = lax.broadcasted_iota(jnp.int32, (width, width), 1) >> 6
    bd = (r == c).astype(BF16)
    hi = v.astype(BF16)
    lo = (v - hi.astype(F32)).astype(BF16)
    total = jnp.dot(hi, bd, preferred_element_type=F32) + jnp.dot(lo, bd, preferred_element_type=F32)
    return total * (1.0 / HEAD_DIM)


def _rope_partner(v):
    width = v.shape[1]
    half = ROT_DIM // 2
    lane = lax.broadcasted_iota(jnp.int32, v.shape, 1) & (HEAD_DIM - 1)
    up = jnp.where(lane < ROT_DIM, pltpu.roll(v, half, 1), 0.0)
    return jnp.where(lane < half, pltpu.roll(v, width - half, 1), up)


def _tile_lanes(t, width):
    return t if width == t.shape[1] else jnp.tile(t, (1, width // t.shape[1]))


def _ffn_up(x, ln, wgu, name, after=None):
    m, d = x.shape
    c = wgu.shape[-2]
    tm = _row_tile(m)
    deps = [] if after is None else [after]

    def body(*refs):
        x_ref, ln_ref, w_ref = refs[:3]
        h_ref, gu_ref, a_ref = refs[-3:]
        h = _rms_fwd(x_ref[...], ln_ref[...]).astype(BF16)
        h_ref[...] = h
        for j in range(4):
            g = lax.dot_general(h, w_ref[0, j], _NT, preferred_element_type=F32)
            u = lax.dot_general(h, w_ref[1, j], _NT, preferred_element_type=F32)
            gu_ref[0, j] = g.astype(BF16)
            gu_ref[1, j] = u.astype(BF16)
            a_ref[j] = (g * jax.nn.sigmoid(g) * u).astype(BF16)

    return pl.pallas_call(
        body, name=name, grid=(m // tm,),
        in_specs=[pl.BlockSpec((tm, d), lambda i: (i, 0)), _resident((1, d), (0, 0)),
                  _resident((2, 4, c, d), (0, 0, 0, 0))] + [pl.BlockSpec(memory_space=pl.ANY)] * len(deps),
        out_specs=[pl.BlockSpec((tm, d), lambda i: (i, 0)),
                   pl.BlockSpec((2, 4, tm, c), lambda i: (0, 0, i, 0)),
                   pl.BlockSpec((4, tm, c), lambda i: (0, i, 0))],
        out_shape=[_sds((m, d), BF16), _sds((2, 4, m, c), BF16), _sds((4, m, c), BF16)],
        compiler_params=_cp("parallel"),
    )(x, ln, wgu, *deps)


def _ffn_down(x, act, wd, name):
    m, d = x.shape
    c = act.shape[-1]
    tm = _row_tile(m)

    def body(x_ref, a_ref, w_ref, o_ref):
        acc = jnp.dot(a_ref[0], w_ref[0], preferred_element_type=F32)
        for j in range(1, 4):
            acc += jnp.dot(a_ref[j], w_ref[j], preferred_element_type=F32)
        o_ref[...] = x_ref[...] + 0.5 * acc

    return pl.pallas_call(
        body, name=name, grid=(m // tm,),
        in_specs=[pl.BlockSpec((tm, d), lambda i: (i, 0)), pl.BlockSpec((4, tm, c), lambda i: (0, i, 0)),
                  _resident((4, c, d), (0, 0, 0))],
        out_specs=pl.BlockSpec((tm, d), lambda i: (i, 0)),
        out_shape=_sds((m, d), F32),
        compiler_params=_cp("parallel"),
    )(x, act, wd)


def _ffn_bwd_act(dy, gu, wd, name, after=None):
    m, d = dy.shape
    c = gu.shape[-1]
    tm = _row_tile(m)
    deps = [] if after is None else [after]

    def body(*refs):
        dy_ref, gu_ref, w_ref = refs[:3]
        dyh_ref, dgu_ref = refs[-2:]
        dyh = (0.5 * dy_ref[...]).astype(BF16)
        dyh_ref[...] = dyh
        for j in range(4):
            da = lax.dot_general(dyh, w_ref[j], _NT, preferred_element_type=F32)
            g = gu_ref[0, j].astype(F32)
            u = gu_ref[1, j].astype(F32)
            silu, dsilu = _silu_parts(g)
            dgu_ref[0, j] = (da * u * dsilu).astype(BF16)
            dgu_ref[1, j] = (da * silu).astype(BF16)

    return pl.pallas_call(
        body, name=name, grid=(m // tm,),
        in_specs=[pl.BlockSpec((tm, d), lambda i: (i, 0)), pl.BlockSpec((2, 4, tm, c), lambda i: (0, 0, i, 0)),
                  _resident((4, c, d), (0, 0, 0))] + [pl.BlockSpec(memory_space=pl.ANY)] * len(deps),
        out_specs=[pl.BlockSpec((tm, d), lambda i: (i, 0)), pl.BlockSpec((2, 4, tm, c), lambda i: (0, 0, i, 0))],
        out_shape=[_sds((m, d), BF16), _sds((2, 4, m, c), BF16)],
        compiler_params=_cp("parallel"),
    )(dy, gu, wd, *deps)


def _ffn_bwd_x(dy, x, ln, dgu, wgu, name):
    m, d = dy.shape
    c = dgu.shape[-1]
    tm = _row_tile(m)

    def body(dy_ref, x_ref, ln_ref, dgu_ref, w_ref, dx_ref, dln_ref):
        @pl.when(pl.program_id(0) == 0)
        def _():
            dln_ref[...] = jnp.zeros_like(dln_ref)

        dh = None
        for half in range(2):
            for j in range(4):
                t = jnp.dot(dgu_ref[half, j], w_ref[half, j], preferred_element_type=F32)
                dh = t if dh is None else dh + t
        dx, dgain = _rms_bwd(dh, x_ref[...], ln_ref[...])
        dx_ref[...] = dy_ref[...] + dx
        dln_ref[...] += jnp.sum(dgain, axis=0, keepdims=True)

    return pl.pallas_call(
        body, name=name, grid=(m // tm,),
        in_specs=[pl.BlockSpec((tm, d), lambda i: (i, 0)), pl.BlockSpec((tm, d), lambda i: (i, 0)),
                  _resident((1, d), (0, 0)), pl.BlockSpec((2, 4, tm, c), lambda i: (0, 0, i, 0)),
                  _resident((2, 4, c, d), (0, 0, 0, 0))],
        out_specs=[pl.BlockSpec((tm, d), lambda i: (i, 0)), pl.BlockSpec((1, d), lambda i: (0, 0))],
        out_shape=[_sds((m, d), F32), _sds((1, d), F32)],
        compiler_params=_cp("arbitrary"),
    )(dy, x, ln, dgu, wgu)


def _matmul_tn(a, b, name, out_dtype=BF16):
    ja, m, k = a.shape
    jb, _, n = b.shape
    nj = max(ja, jb)
    tm = min(TN_ROW_TILE, m)
    nm = m // tm

    def body(a_ref, b_ref, o_ref, acc):
        step = pl.program_id(1)

        @pl.when(step == 0)
        def _():
            acc[...] = jnp.zeros_like(acc)

        acc[...] += lax.dot_general(a_ref[...], b_ref[...], _TN, preferred_element_type=F32)

        @pl.when(step == nm - 1)
        def _():
            o_ref[...] = acc[...].astype(o_ref.dtype)

    return pl.pallas_call(
        body, name=name, grid=(nj, nm),
        in_specs=[pl.BlockSpec((None, tm, k), (lambda j, s: (j, s, 0)) if ja > 1 else (lambda j, s: (0, s, 0))),
                  pl.BlockSpec((None, tm, n), (lambda j, s: (j, s, 0)) if jb > 1 else (lambda j, s: (0, s, 0)))],
        out_specs=pl.BlockSpec((None, k, n), lambda j, s: (j, 0, 0)),
        out_shape=_sds((nj, k, n), out_dtype),
        scratch_shapes=[pltpu.VMEM((k, n), F32)],
        compiler_params=_cp("parallel", "arbitrary"),
    )(a, b)


def _mix_in(x, ln, w_in, name):
    m, d = x.shape
    n_in = w_in.shape[0]
    tm = _row_tile(m)

    def body(x_ref, ln_ref, w_ref, h_ref, zq_ref, zg_ref):
        h = _rms_fwd(x_ref[...], ln_ref[...]).astype(BF16)
        h_ref[...] = h
        zq_ref[...] = lax.dot_general(h, w_ref[:QKV_END, :], _NT, preferred_element_type=F32)
        zg_ref[...] = lax.dot_general(h, w_ref[QKV_END:, :], _NT, preferred_element_type=F32)

    return pl.pallas_call(
        body, name=name, grid=(m // tm,),
        in_specs=[pl.BlockSpec((tm, d), lambda i: (i, 0)), _resident((1, d), (0, 0)), _resident((n_in, d), (0, 0))],
        out_specs=[pl.BlockSpec((tm, d), lambda i: (i, 0)), pl.BlockSpec((tm, QKV_END), lambda i: (i, 0)),
                   pl.BlockSpec((tm, n_in - QKV_END), lambda i: (i, 0))],
        out_shape=[_sds((m, d), BF16), _sds((m, QKV_END), F32), _sds((m, n_in - QKV_END), F32)],
        compiler_params=_cp("parallel"),
    )(x, ln, w_in)


def _pool_fwd(zq, pool_w, scale, name):
    m = zq.shape[0]
    tm = _row_tile(m)
    halo_blocks = tm // POOL_HALO

    def body(zc_ref, zh_ref, pw_ref, sc_ref, d_ref, p_ref):
        i = pl.program_id(0)
        halo = jnp.where(i > 0, zh_ref[...], 0.0)
        ext = jnp.concatenate([halo, zc_ref[...]], axis=0)
        t = i * tm + lax.broadcasted_iota(jnp.int32, (tm, 1), 0)
        for g, w in enumerate(POOL_WINDOWS):
            lanes = slice(g * GROUP_DIM, (g + 1) * GROUP_DIM)
            e = ext[:, lanes]
            s, k = e, 1
            while k < w:
                s = s + pltpu.roll(s, k, 0)
                k *= 2
            cnt = jnp.minimum(t + 1, w).astype(F32)
            dg = (s[POOL_HALO:, :] / cnt - e[POOL_HALO:, :]).astype(BF16)
            y = jnp.dot(dg, pw_ref[g].astype(BF16), preferred_element_type=F32)
            d_ref[:, lanes] = dg
            p_ref[:, lanes] = (y * sc_ref[:, lanes]).astype(BF16)

    return pl.pallas_call(
        body, name=name, grid=(m // tm,),
        in_specs=[pl.BlockSpec((tm, POOL_DIM), lambda i: (i, 0)),
                  pl.BlockSpec((POOL_HALO, POOL_DIM), lambda i: (jnp.maximum(i * halo_blocks - 1, 0), 0)),
                  _resident((4, GROUP_DIM, GROUP_DIM), (0, 0, 0)), _resident((1, POOL_DIM), (0, 0))],
        out_specs=[pl.BlockSpec((tm, POOL_DIM), lambda i: (i, 0)), pl.BlockSpec((tm, POOL_DIM), lambda i: (i, 0))],
        out_shape=[_sds((m, POOL_DIM), BF16), _sds((m, POOL_DIM), BF16)],
        compiler_params=_cp("parallel"),
    )(zq, zq, pool_w, scale)


def _pool_bwd(dp, d, pool_w, scale, name):
    m = dp.shape[0]
    tm = _row_tile(m)
    nb = m // tm
    halo_blocks = tm // POOL_HALO
    rows = tm + POOL_HALO

    def body(dpc_ref, dph_ref, d_ref, pw_ref, sc_ref, du_ref, dpw_ref, dsc_ref):
        i = pl.program_id(0)

        @pl.when(i == 0)
        def _():
            dpw_ref[...] = jnp.zeros_like(dpw_ref)
            dsc_ref[...] = jnp.zeros_like(dsc_ref)

        halo = jnp.where(i < nb - 1, dph_ref[...], 0.0)
        dpc = dpc_ref[...]
        ext = jnp.concatenate([dpc, halo], axis=0)
        t = i * tm + lax.broadcasted_iota(jnp.int32, (rows, 1), 0)
        for g, w in enumerate(POOL_WINDOWS):
            lanes = slice(g * GROUP_DIM, (g + 1) * GROUP_DIM)
            pwb = pw_ref[g].astype(BF16)
            dyb = (ext[:, lanes] * sc_ref[:, lanes]).astype(BF16)
            dd = lax.dot_general(dyb, pwb, _NT, preferred_element_type=F32)
            cnt = jnp.minimum(t + 1, w).astype(F32)
            s, k = dd / cnt, 1
            while k < w:
                s = s + pltpu.roll(s, rows - k, 0)
                k *= 2
            du_ref[:, lanes] = (s[:tm, :] - dd[:tm, :]).astype(BF16)
            dcur = d_ref[:, lanes]
            y = jnp.dot(dcur, pwb, preferred_element_type=F32)
            dsc_ref[:, lanes] += jnp.sum(dpc[:, lanes] * y, axis=0, keepdims=True)
            dpw_ref[g] += lax.dot_general(dcur, dyb[:tm, :], _TN, preferred_element_type=F32)

    return pl.pallas_call(
        body, name=name, grid=(nb,),
        in_specs=[pl.BlockSpec((tm, POOL_DIM), lambda i: (i, 0)),
                  pl.BlockSpec((POOL_HALO, POOL_DIM), lambda i: (jnp.minimum((i + 1) * halo_blocks, nb * halo_blocks - 1), 0)),
                  pl.BlockSpec((tm, POOL_DIM), lambda i: (i, 0)),
                  _resident((4, GROUP_DIM, GROUP_DIM), (0, 0, 0)), _resident((1, POOL_DIM), (0, 0))],
        out_specs=[pl.BlockSpec((tm, POOL_DIM), lambda i: (i, 0)),
                   pl.BlockSpec((4, GROUP_DIM, GROUP_DIM), lambda i: (0, 0, 0)),
                   pl.BlockSpec((1, POOL_DIM), lambda i: (0, 0))],
        out_shape=[_sds((m, POOL_DIM), BF16), _sds((4, GROUP_DIM, GROUP_DIM), F32), _sds((1, POOL_DIM), F32)],
        compiler_params=_cp("arbitrary"),
    )(dp, dp, d, pool_w, scale)


def _qk_norm_rope(xv, gain, cos, sin):
    width = xv.shape[1]
    r = lax.rsqrt(_segment_mean(xv * xv, width) + EPS)
    y = xv * r * gain
    return y * _tile_lanes(cos, width) + _rope_partner(y) * _tile_lanes(sin, width)


def _qk_prep(zq, qgain, kgain, cos, sin, name):
    m = zq.shape[0]
    tm = _row_tile(m)

    def body(q_ref, kv_ref, qg_ref, kg_ref, cos_ref, sin_ref, qr_ref, kr_ref, v_ref):
        cos_v, sin_v = cos_ref[...], sin_ref[...]
        qr_ref[...] = _qk_norm_rope(q_ref[...], qg_ref[...], cos_v, sin_v).astype(BF16)
        kv = kv_ref[...]
        kr_ref[...] = _qk_norm_rope(kv[:, :KV_DIM], kg_ref[...], cos_v, sin_v).astype(BF16)
        v_ref[...] = kv[:, KV_DIM:].astype(BF16)

    return pl.pallas_call(
        body, name=name, grid=(m // tm,),
        in_specs=[pl.BlockSpec((tm, ATTN_DIM), lambda i: (i, 1)), pl.BlockSpec((tm, 2 * KV_DIM), lambda i: (i, 4)),
                  _resident((1, ATTN_DIM), (0, 0)), _resident((1, KV_DIM), (0, 0)),
                  pl.BlockSpec((tm, 128), lambda i: (i, 0)), pl.BlockSpec((tm, 128), lambda i: (i, 0))],
        out_specs=[pl.BlockSpec((tm, ATTN_DIM), lambda i: (i, 0)), pl.BlockSpec((tm, KV_DIM), lambda i: (i, 0)),
                   pl.BlockSpec((tm, KV_DIM), lambda i: (i, 0))],
        out_shape=[_sds((m, ATTN_DIM), BF16), _sds((m, KV_DIM), BF16), _sds((m, KV_DIM), BF16)],
        compiler_params=_cp("parallel"),
    )(zq, zq, qgain, kgain, cos, sin)


GROUP_ROWS = GQA_GROUP * ATTN_BLOCK


def _band_mask(n):
    qi = lax.broadcasted_iota(jnp.int32, (GROUP_ROWS, 2 * ATTN_BLOCK), 0) & (ATTN_BLOCK - 1)
    ki = lax.broadcasted_iota(jnp.int32, (GROUP_ROWS, 2 * ATTN_BLOCK), 1)
    diff = qi + ATTN_BLOCK - ki
    return (diff >= 0) & (diff < ATTN_BLOCK) & ((ki >= ATTN_BLOCK) | (n > 0))


def _stack_group(v, kvh):
    heads = range(kvh * GQA_GROUP, (kvh + 1) * GQA_GROUP)
    return jnp.concatenate([v[:, h * HEAD_DIM:(h + 1) * HEAD_DIM] for h in heads], axis=0)


def _unstack_groups(groups):
    return jnp.concatenate([grp[g * ATTN_BLOCK:(g + 1) * ATTN_BLOCK, :] for grp in groups for g in range(GQA_GROUP)],
                           axis=1)


def _group_sinks(sk_ref, kvh):
    row_head = lax.broadcasted_iota(jnp.int32, (GROUP_ROWS, 1), 0) >> 7
    col = jnp.full((GROUP_ROWS, 1), sk_ref[kvh * GQA_GROUP], F32)
    for g in range(1, GQA_GROUP):
        col = jnp.where(row_head == g, sk_ref[kvh * GQA_GROUP + g], col)
    return col


def _head_probs(qh, kh, mask, sink):
    s = lax.dot_general(qh, kh, _NT, preferred_element_type=F32) * ATTN_SCALE
    s = jnp.where(mask, s, NEG_BIG)
    mx = jnp.maximum(jnp.max(s, axis=-1, keepdims=True), sink)
    p = jnp.exp(s - mx)
    es = jnp.exp(sink - mx)
    den = jnp.sum(p, axis=-1, keepdims=True) + es
    return p / den, es / den


def _attn_fwd(qr, kr, vb, sinks, name):
    m = qr.shape[0]
    nb = m // ATTN_BLOCK

    def body(q_ref, kp_ref, kc_ref, vp_ref, vc_ref, sk_ref, o_ref):
        mask = _band_mask(pl.program_id(0))
        qv = q_ref[...]
        kk = jnp.concatenate([kp_ref[...], kc_ref[...]], axis=0)
        vv = jnp.concatenate([vp_ref[...], vc_ref[...]], axis=0)
        outs = []
        for kvh in range(N_KV_HEADS):
            kv_lanes = slice(kvh * HEAD_DIM, (kvh + 1) * HEAD_DIM)
            p, _ = _head_probs(_stack_group(qv, kvh), kk[:, kv_lanes], mask, _group_sinks(sk_ref, kvh))
            outs.append(jnp.dot(p.astype(BF16), vv[:, kv_lanes], preferred_element_type=F32))
        o_ref[...] = _unstack_groups(outs).astype(BF16)

    prev = lambda n: (jnp.maximum(n - 1, 0), 0)
    cur = lambda n: (n, 0)
    return pl.pallas_call(
        body, name=name, grid=(nb,),
        in_specs=[pl.BlockSpec((ATTN_BLOCK, ATTN_DIM), cur),
                  pl.BlockSpec((ATTN_BLOCK, KV_DIM), prev), pl.BlockSpec((ATTN_BLOCK, KV_DIM), cur),
                  pl.BlockSpec((ATTN_BLOCK, KV_DIM), prev), pl.BlockSpec((ATTN_BLOCK, KV_DIM), cur),
                  pl.BlockSpec(memory_space=pltpu.SMEM)],
        out_specs=pl.BlockSpec((ATTN_BLOCK, ATTN_DIM), cur),
        out_shape=_sds((m, ATTN_DIM), BF16),
        compiler_params=_cp("parallel"),
    )(qr, kr, kr, vb, vb, sinks)


def _attn_bwd(do, qr, kr, vb, sinks, name):
    m = qr.shape[0]
    nb = m // ATTN_BLOCK

    def body(do_ref, q_ref, kp_ref, kc_ref, vp_ref, vc_ref, sk_ref, dq_ref, dk_ref, dv_ref, ds_ref, carry_k, carry_v):
        n = pl.program_id(0)

        @pl.when(n == 0)
        def _():
            carry_k[...] = jnp.zeros_like(carry_k)
            carry_v[...] = jnp.zeros_like(carry_v)
            ds_ref[...] = jnp.zeros_like(ds_ref)

        @pl.when(n < nb)
        def _():
            mask = _band_mask(n)
            qv = q_ref[...]
            dov = do_ref[...]
            kk = jnp.concatenate([kp_ref[...], kc_ref[...]], axis=0)
            vv = jnp.concatenate([vp_ref[...], vc_ref[...]], axis=0)
            lane = lax.broadcasted_iota(jnp.int32, (1, 128), 1)
            dsink = jnp.zeros((1, 128), F32)
            dqs, dks, dvs = [], [], []
            for kvh in range(N_KV_HEADS):
                kv_lanes = slice(kvh * HEAD_DIM, (kvh + 1) * HEAD_DIM)
                kh, vh = kk[:, kv_lanes], vv[:, kv_lanes]
                qg = _stack_group(qv, kvh)
                dog = _stack_group(dov, kvh).astype(BF16)
                p, ps = _head_probs(qg, kh, mask, _group_sinks(sk_ref, kvh))
                dpr = lax.dot_general(dog, vh, _NT, preferred_element_type=F32)
                delta = jnp.sum(p * dpr, axis=-1, keepdims=True)
                dsb = (p * (dpr - delta) * ATTN_SCALE).astype(BF16)
                sink_term = ps * delta
                for g in range(GQA_GROUP):
                    rows = slice(g * ATTN_BLOCK, (g + 1) * ATTN_BLOCK)
                    dsink = dsink + jnp.where(lane == kvh * GQA_GROUP + g, -jnp.sum(sink_term[rows, :]), 0.0)
                dqs.append(jnp.dot(dsb, kh, preferred_element_type=F32))
                dks.append(lax.dot_general(dsb, qg, _TN, preferred_element_type=F32))
                dvs.append(lax.dot_general(p.astype(BF16), dog, _TN, preferred_element_type=F32))
            dq_ref[...] = _unstack_groups(dqs)
            dkk = jnp.concatenate(dks, axis=1)
            dvv = jnp.concatenate(dvs, axis=1)
            dk_ref[...] = carry_k[...] + dkk[:ATTN_BLOCK, :]
            dv_ref[...] = carry_v[...] + dvv[:ATTN_BLOCK, :]
            carry_k[...] = dkk[ATTN_BLOCK:, :]
            carry_v[...] = dvv[ATTN_BLOCK:, :]
            ds_ref[...] += dsink

        @pl.when(n == nb)
        def _():
            dk_ref[...] = carry_k[...]
            dv_ref[...] = carry_v[...]

    cur = lambda n: (jnp.minimum(n, nb - 1), 0)
    prev = lambda n: (jnp.clip(n - 1, 0, nb - 1), 0)
    return pl.pallas_call(
        body, name=name, grid=(nb + 1,),
        in_specs=[pl.BlockSpec((ATTN_BLOCK, ATTN_DIM), cur), pl.BlockSpec((ATTN_BLOCK, ATTN_DIM), cur),
                  pl.BlockSpec((ATTN_BLOCK, KV_DIM), prev), pl.BlockSpec((ATTN_BLOCK, KV_DIM), cur),
                  pl.BlockSpec((ATTN_BLOCK, KV_DIM), prev), pl.BlockSpec((ATTN_BLOCK, KV_DIM), cur),
                  pl.BlockSpec(memory_space=pltpu.SMEM)],
        out_specs=[pl.BlockSpec((ATTN_BLOCK, ATTN_DIM), cur), pl.BlockSpec((ATTN_BLOCK, KV_DIM), prev),
                   pl.BlockSpec((ATTN_BLOCK, KV_DIM), prev), pl.BlockSpec((1, 128), lambda n: (0, 0))],
        out_shape=[_sds((m, ATTN_DIM), F32), _sds((m, KV_DIM), F32), _sds((m, KV_DIM), F32), _sds((1, 128), F32)],
        scratch_shapes=[pltpu.VMEM((ATTN_BLOCK, KV_DIM), F32), pltpu.VMEM((ATTN_BLOCK, KV_DIM), F32)],
        compiler_params=_cp("arbitrary"),
    )(do, qr, kr, kr, vb, vb, sinks)


def _qk_norm_rope_bwd(dout, xv, gain, cos, sin):
    width = xv.shape[1]
    r = lax.rsqrt(_segment_mean(xv * xv, width) + EPS)
    xn = xv * r
    dy = dout * _tile_lanes(cos, width) + _rope_partner(dout * _tile_lanes(sin, width))
    dxn = dy * gain
    dx = r * (dxn - xn * _segment_mean(dxn * xn, width))
    return dx, jnp.sum(dy * xn, axis=0, keepdims=True)


def _fold_heads(v):
    out = v[:, :HEAD_DIM]
    for h in range(1, v.shape[1] // HEAD_DIM):
        out = out + v[:, h * HEAD_DIM:(h + 1) * HEAD_DIM]
    return out


def _qk_bwd(dq, dk, dv, zq, qgain, kgain, cos, sin, name):
    m = zq.shape[0]
    tm = _row_tile(m)

    def body(dq_ref, dk_ref, dv_ref, q_ref, kv_ref, qg_ref, kg_ref, cos_ref, sin_ref, dz_ref, dqg_ref, dkg_ref):
        @pl.when(pl.program_id(0) == 0)
        def _():
            dqg_ref[...] = jnp.zeros_like(dqg_ref)
            dkg_ref[...] = jnp.zeros_like(dkg_ref)

        cos_v, sin_v = cos_ref[...], sin_ref[...]
        dxq, dgq = _qk_norm_rope_bwd(dq_ref[...], q_ref[...], qg_ref[...], cos_v, sin_v)
        dxk, dgk = _qk_norm_rope_bwd(dk_ref[...], kv_ref[:, :KV_DIM], kg_ref[...], cos_v, sin_v)
        dz_ref[:, :ATTN_DIM] = dxq.astype(BF16)
        dz_ref[:, ATTN_DIM:ATTN_DIM + KV_DIM] = dxk.astype(BF16)
        dz_ref[:, ATTN_DIM + KV_DIM:] = dv_ref[...].astype(BF16)
        dqg_ref[...] += _fold_heads(dgq)
        dkg_ref[...] += _fold_heads(dgk)

    row = lambda i: (i, 0)
    return pl.pallas_call(
        body, name=name, grid=(m // tm,),
        in_specs=[pl.BlockSpec((tm, ATTN_DIM), row), pl.BlockSpec((tm, KV_DIM), row), pl.BlockSpec((tm, KV_DIM), row),
                  pl.BlockSpec((tm, ATTN_DIM), lambda i: (i, 1)), pl.BlockSpec((tm, 2 * KV_DIM), lambda i: (i, 4)),
                  _resident((1, ATTN_DIM), (0, 0)), _resident((1, KV_DIM), (0, 0)),
                  pl.BlockSpec((tm, 128), row), pl.BlockSpec((tm, 128), row)],
        out_specs=[pl.BlockSpec((tm, ATTN_DIM + 2 * KV_DIM), row), pl.BlockSpec((1, HEAD_DIM), lambda i: (0, 0)),
                   pl.BlockSpec((1, HEAD_DIM), lambda i: (0, 0))],
        out_shape=[_sds((m, ATTN_DIM + 2 * KV_DIM), BF16), _sds((1, HEAD_DIM), F32), _sds((1, HEAD_DIM), F32)],
        compiler_params=_cp("arbitrary"),
    )(dq, dk, dv, zq, zq, qgain, kgain, cos, sin)


def _merge_fwd(x, p, o, zg, w_pb, w_ab, w_out, name):
    m, d = x.shape
    tm = _row_tile(m)

    def body(x_ref, p_ref, o_ref, zg_ref, wpb_ref, wab_ref, wo_ref, xo_ref, mix_ref):
        a = jnp.dot(p_ref[...], wpb_ref[...], preferred_element_type=F32)
        b = jnp.dot(o_ref[...], wab_ref[...], preferred_element_type=F32)
        mix = (jax.nn.sigmoid(zg_ref[:, :d]) * a + jax.nn.sigmoid(zg_ref[:, d:]) * b).astype(BF16)
        mix_ref[...] = mix
        xo_ref[...] = x_ref[...] + jnp.dot(mix, wo_ref[...], preferred_element_type=F32)

    row = lambda i: (i, 0)
    return pl.pallas_call(
        body, name=name, grid=(m // tm,),
        in_specs=[pl.BlockSpec((tm, d), row), pl.BlockSpec((tm, POOL_DIM), row), pl.BlockSpec((tm, ATTN_DIM), row),
                  pl.BlockSpec((tm, 2 * d), row), _resident((POOL_DIM, d), (0, 0)), _resident((ATTN_DIM, d), (0, 0)),
                  _resident((d, d), (0, 0))],
        out_specs=[pl.BlockSpec((tm, d), row)] * 2,
        out_shape=[_sds((m, d), F32), _sds((m, d), BF16)],
        compiler_params=_cp("parallel"),
    )(x, p, o, zg, w_pb, w_ab, w_out)


def _merge_bwd(dy, p, o, zg, w_out, w_pb, w_ab, name):
    m, d = dy.shape
    tm = _row_tile(m)

    def body(dy_ref, p_ref, o_ref, zg_ref, wo_ref, wpb_ref, wab_ref, dyb_ref, da_ref, db_ref, dp_ref, do_ref, dzg_ref):
        dyb = dy_ref[...].astype(BF16)
        dyb_ref[...] = dyb
        dmix = lax.dot_general(dyb, wo_ref[...], _NT, preferred_element_type=F32)
        gp = jax.nn.sigmoid(zg_ref[:, :d])
        ga = jax.nn.sigmoid(zg_ref[:, d:])
        da = (dmix * gp).astype(BF16)
        db = (dmix * ga).astype(BF16)
        da_ref[...] = da
        db_ref[...] = db
        a = jnp.dot(p_ref[...], wpb_ref[...], preferred_element_type=F32)
        b = jnp.dot(o_ref[...], wab_ref[...], preferred_element_type=F32)
        dzg_ref[:, :d] = (dmix * a * gp * (1.0 - gp)).astype(BF16)
        dzg_ref[:, d:] = (dmix * b * ga * (1.0 - ga)).astype(BF16)
        dp_ref[...] = lax.dot_general(da, wpb_ref[...], _NT, preferred_element_type=F32)
        do_ref[...] = lax.dot_general(db, wab_ref[...], _NT, preferred_element_type=F32)

    row = lambda i: (i, 0)
    return pl.pallas_call(
        body, name=name, grid=(m // tm,),
        in_specs=[pl.BlockSpec((tm, d), row), pl.BlockSpec((tm, POOL_DIM), row), pl.BlockSpec((tm, ATTN_DIM), row),
                  pl.BlockSpec((tm, 2 * d), row), _resident((d, d), (0, 0)), _resident((POOL_DIM, d), (0, 0)),
                  _resident((ATTN_DIM, d), (0, 0))],
        out_specs=[pl.BlockSpec((tm, d), row), pl.BlockSpec((tm, d), row), pl.BlockSpec((tm, d), row),
                   pl.BlockSpec((tm, POOL_DIM), row), pl.BlockSpec((tm, ATTN_DIM), row), pl.BlockSpec((tm, 2 * d), row)],
        out_shape=[_sds((m, d), BF16), _sds((m, d), BF16), _sds((m, d), BF16), _sds((m, POOL_DIM), F32),
                   _sds((m, ATTN_DIM), F32), _sds((m, 2 * d), BF16)],
        compiler_params=_cp("parallel"),
    )(dy, p, o, zg, w_out, w_pb, w_ab)


def _mix_bwd_x(dy, x, ln, dzp, dzqkv, dzg, w_in, name):
    m, d = dy.shape
    n_in = w_in.shape[0]
    tm = _row_tile(m)

    def body(dy_ref, x_ref, ln_ref, dzp_ref, dzq_ref, dzg_ref, w_ref, dx_ref, dln_ref):
        @pl.when(pl.program_id(0) == 0)
        def _():
            dln_ref[...] = jnp.zeros_like(dln_ref)

        dh = jnp.dot(dzp_ref[...], w_ref[:POOL_DIM, :], preferred_element_type=F32)
        dh += jnp.dot(dzq_ref[...], w_ref[POOL_DIM:QKV_END, :], preferred_element_type=F32)
        dh += jnp.dot(dzg_ref[...], w_ref[QKV_END:, :], preferred_element_type=F32)
        dx, dgain = _rms_bwd(dh, x_ref[...], ln_ref[...])
        dx_ref[...] = dy_ref[...] + dx
        dln_ref[...] += jnp.sum(dgain, axis=0, keepdims=True)

    row = lambda i: (i, 0)
    return pl.pallas_call(
        body, name=name, grid=(m // tm,),
        in_specs=[pl.BlockSpec((tm, d), row), pl.BlockSpec((tm, d), row), _resident((1, d), (0, 0)),
                  pl.BlockSpec((tm, POOL_DIM), row), pl.BlockSpec((tm, QKV_END - POOL_DIM), row),
                  pl.BlockSpec((tm, n_in - QKV_END), row), _resident((n_in, d), (0, 0))],
        out_specs=[pl.BlockSpec((tm, d), row), pl.BlockSpec((1, d), lambda i: (0, 0))],
        out_shape=[_sds((m, d), F32), _sds((1, d), F32)],
        compiler_params=_cp("arbitrary"),
    )(dy, x, ln, dzp, dzqkv, dzg, w_in)


def _loss_head(y, target, name):
    m, d = y.shape
    tm = _row_tile(m)

    def body(y_ref, t_ref, loss_ref, dy_ref):
        @pl.when(pl.program_id(0) == 0)
        def _():
            loss_ref[...] = jnp.zeros_like(loss_ref)

        diff = y_ref[...] - t_ref[...]
        dy_ref[...] = diff * (1.0 / d)
        loss_ref[...] += 0.5 * jnp.sum(jnp.mean(diff * diff, axis=-1, keepdims=True), axis=0, keepdims=True)

    row = lambda i: (i, 0)
    return pl.pallas_call(
        body, name=name, grid=(m // tm,),
        in_specs=[pl.BlockSpec((tm, d), row), pl.BlockSpec((tm, d), row)],
        out_specs=[pl.BlockSpec((1, 1), lambda i: (0, 0)), pl.BlockSpec((tm, d), row)],
        out_shape=[_sds((1, 1), F32), _sds((m, d), F32)],
        compiler_params=_cp("arbitrary"),
    )(y, target)


def _adamw_math(g, w, m, v):
    m2 = ADAM_B1 * m + (1.0 - ADAM_B1) * g
    v2 = ADAM_B2 * v + (1.0 - ADAM_B2) * (g * g)
    m_hat = m2 / (1.0 - ADAM_B1 ** ADAM_STEP)
    v_hat = v2 / (1.0 - ADAM_B2 ** ADAM_STEP)
    delta = -ADAM_LR * (m_hat / (jnp.sqrt(v_hat) + ADAM_EPS) + ADAM_WD * w)
    return delta, m2, v2


def _sum_parts(parts_ref):
    g = parts_ref[0].astype(F32)
    for s in range(1, N_DEV):
        g = g + parts_ref[s].astype(F32)
    return g


def _adamw_sharded(parts, w, m, v, name):
    n_layers, rows, cols = w.shape
    tr = next(t for t in (512, 256, 128, 64, 32, 16, 8) if rows % t == 0 and t * cols <= ADAMW_BLOCK_ELEMS)
    nr = rows // tr

    def body(*refs):
        part_refs = refs[:n_layers]
        w_ref, m_ref, v_ref, g_out, d_out, m_out, v_out = refs[n_layers:]
        layer = pl.program_id(0)
        for l in range(n_layers):
            @pl.when(layer == l)
            def _(l=l):
                g = _sum_parts(part_refs[l])
                delta, m2, v2 = _adamw_math(g, w_ref[...], m_ref[...], v_ref[...])
                g_out[...] = g
                d_out[...] = delta
                m_out[...] = m2
                v_out[...] = v2

    def part_map(l):
        return lambda layer, r: (0, jnp.where(layer == l, r, jnp.where(layer < l, 0, nr - 1)), 0)

    wspec = pl.BlockSpec((None, tr, cols), lambda layer, r: (layer, r, 0))
    return pl.pallas_call(
        body, name=name, grid=(n_layers, nr),
        in_specs=[pl.BlockSpec((N_DEV, tr, cols), part_map(l)) for l in range(n_layers)] + [wspec] * 3,
        out_specs=[wspec] * 4,
        out_shape=[_sds(w.shape, F32)] * 4,
        compiler_params=_cp("arbitrary", "arbitrary"),
    )(*parts, w, m, v)


def _adamw_packed(parts, w, m, v, name):
    def body(p_ref, w_ref, m_ref, v_ref, g_out, d_out, m_out, v_out):
        g = _sum_parts(p_ref)
        delta, m2, v2 = _adamw_math(g, w_ref[...], m_ref[...], v_ref[...])
        g_out[...] = g
        d_out[...] = delta
        m_out[...] = m2
        v_out[...] = v2

    return pl.pallas_call(
        body, name=name, out_shape=[_sds(w.shape, F32)] * 4,
        compiler_params=pltpu.CompilerParams(vmem_limit_bytes=VMEM_LIMIT_BYTES),
    )(parts, w, m, v)


_SMALL = ("ln_ffn1", "ln_mix", "pool_w", "pool_scale", "q_norm", "k_norm", "sinks", "ln_ffn2")


def _pack_small(arrs):
    rows = []
    for a in arrs:
        flat = a.reshape(-1)
        pad = (-flat.shape[0]) % 1024
        rows.append(jnp.pad(flat, (0, pad)).reshape(-1, 128))
    return jnp.concatenate(rows, axis=0)


def _unpack_small(packed, like):
    out, r0 = [], 0
    for a in like:
        size = a.size
        nrows = (size + 1023) // 1024 * 8
        out.append(packed[r0:r0 + nrows].reshape(-1)[:size].reshape(a.shape))
        r0 += nrows
    return out


def _rope_tables(m):
    pos = jnp.arange(m, dtype=F32)
    inv_freq = ROPE_THETA ** (-jnp.arange(0, ROT_DIM, 2, dtype=F32) / ROT_DIM)
    ang = pos[:, None] * inv_freq[None, :]
    cos8, sin8 = jnp.cos(ang), jnp.sin(ang)
    rest = HEAD_DIM - ROT_DIM
    cos64 = jnp.concatenate([cos8, cos8, jnp.ones((m, rest), F32)], axis=1)
    sin64 = jnp.concatenate([-sin8, sin8, jnp.zeros((m, rest), F32)], axis=1)
    return jnp.tile(cos64, (1, 2)), jnp.tile(sin64, (1, 2))


def _to_shard_major_cols(w):
    k = w.shape[0]
    return w.reshape(k, N_DEV, -1).transpose(1, 0, 2)


def _from_shard_major_cols(w):
    return w.transpose(1, 0, 2).reshape(w.shape[1], -1)


def kernel(x, ln_ffn1, w_ffn1_gu, w_ffn1_down, ln_mix, w_in, pool_w, pool_scale, w_pool_branch, q_norm, k_norm, sinks, w_attn_branch, w_out, ln_ffn2, w_ffn2_gu, w_ffn2_down, loss_target, m_ln_ffn1, m_w_ffn1_gu, m_w_ffn1_down, m_ln_mix, m_w_in, m_pool_w, m_pool_scale, m_w_pool_branch, m_q_norm, m_k_norm, m_sinks, m_w_attn_branch, m_w_out, m_ln_ffn2, m_w_ffn2_gu, m_w_ffn2_down, v_ln_ffn1, v_w_ffn1_gu, v_w_ffn1_down, v_ln_mix, v_w_in, v_pool_w, v_pool_scale, v_w_pool_branch, v_q_norm, v_k_norm, v_sinks, v_w_attn_branch, v_w_out, v_ln_ffn2, v_w_ffn2_gu, v_w_ffn2_down):
    weights = dict(ln_ffn1=ln_ffn1, w_ffn1_gu=w_ffn1_gu, w_ffn1_down=w_ffn1_down, ln_mix=ln_mix, w_in=w_in, pool_w=pool_w,
                   pool_scale=pool_scale, w_pool_branch=w_pool_branch, q_norm=q_norm, k_norm=k_norm, sinks=sinks,
                   w_attn_branch=w_attn_branch, w_out=w_out, ln_ffn2=ln_ffn2, w_ffn2_gu=w_ffn2_gu, w_ffn2_down=w_ffn2_down)
    mom_m = dict(ln_ffn1=m_ln_ffn1, w_ffn1_gu=m_w_ffn1_gu, w_ffn1_down=m_w_ffn1_down, ln_mix=m_ln_mix, w_in=m_w_in,
                 pool_w=m_pool_w, pool_scale=m_pool_scale, w_pool_branch=m_w_pool_branch, q_norm=m_q_norm, k_norm=m_k_norm,
                 sinks=m_sinks, w_attn_branch=m_w_attn_branch, w_out=m_w_out, ln_ffn2=m_ln_ffn2, w_ffn2_gu=m_w_ffn2_gu,
                 w_ffn2_down=m_w_ffn2_down)
    mom_v = dict(ln_ffn1=v_ln_ffn1, w_ffn1_gu=v_w_ffn1_gu, w_ffn1_down=v_w_ffn1_down, ln_mix=v_ln_mix, w_in=v_w_in,
                 pool_w=v_pool_w, pool_scale=v_pool_scale, w_pool_branch=v_w_pool_branch, q_norm=v_q_norm, k_norm=v_k_norm,
                 sinks=v_sinks, w_attn_branch=v_w_attn_branch, w_out=v_w_out, ln_ffn2=v_ln_ffn2, w_ffn2_gu=v_w_ffn2_gu,
                 w_ffn2_down=v_w_ffn2_down)
    order = ("ln_ffn1", "w_ffn1_gu", "w_ffn1_down", "ln_mix", "w_in", "pool_w", "pool_scale", "w_pool_branch", "q_norm",
             "k_norm", "sinks", "w_attn_branch", "w_out", "ln_ffn2", "w_ffn2_gu", "w_ffn2_down")
    big = ("w_ffn1_gu", "w_ffn1_down", "w_in", "w_pool_branch", "w_attn_branch", "w_out", "w_ffn2_gu", "w_ffn2_down")

    transposed = ("w_ffn1_gu", "w_ffn2_gu", "w_in")
    for group in (weights, mom_m, mom_v):
        for k in transposed:
            group[k] = jnp.swapaxes(group[k], 1, 2)

    n_layers = ln_ffn1.shape[0]
    seq, d = x.shape[-2], x.shape[-1]
    xs = x.reshape(seq, d)
    target = loss_target.reshape(seq, d)
    cos, sin = _rope_tables(seq)

    def layer_shards(l):
        return [weights[k][l].astype(BF16) for k in big]

    def layer_weights(l, full):
        g = dict(zip(big, full))
        c_ff = g["w_ffn1_gu"].shape[-2]
        return dict(
            gu1=g["w_ffn1_gu"].reshape(2, 4, c_ff, d), down1=g["w_ffn1_down"].reshape(4, c_ff, d),
            gu2=g["w_ffn2_gu"].reshape(2, 4, c_ff, d), down2=g["w_ffn2_down"].reshape(4, c_ff, d),
            w_in=g["w_in"].reshape(-1, d), w_pb=_from_shard_major_cols(g["w_pool_branch"]),
            w_ab=_from_shard_major_cols(g["w_attn_branch"]), w_out=g["w_out"].reshape(d, d),
            ln1=ln_ffn1[l][None], ln_mix=ln_mix[l][None], ln2=ln_ffn2[l][None], pool_w=pool_w[l],
            pool_scale=pool_scale[l][None], sinks=sinks[l],
            qgain=jnp.tile(q_norm[l], N_Q_HEADS)[None], kgain=jnp.tile(k_norm[l], N_KV_HEADS)[None])

    gathered = [layer_weights(0, _all_gather_many(layer_shards(0), name="gather_weights_l0"))]
    saved = []
    cur = xs
    for l in range(n_layers):
        lw = gathered[l]
        s = dict(x0=cur)
        in_flight, token = None, None
        if l + 1 < n_layers:
            in_flight, token = _exchange_start(layer_shards(l + 1), scatter=False, name=f"gather_start_l{l + 1}")
        s["h1"], s["gu1"], act1 = _ffn_up(cur, lw["ln1"], lw["gu1"], name=f"ffn1_up_l{l}", after=token)
        s["act1"] = act1
        x1 = _ffn_down(cur, act1, lw["down1"], name=f"ffn1_down_l{l}")
        s["x1"] = x1
        s["h2"], zq, zg = _mix_in(x1, lw["ln_mix"], lw["w_in"], name=f"mix_in_l{l}")
        s["zq"], s["zg"] = zq, zg
        s["d"], s["p"] = _pool_fwd(zq, lw["pool_w"], lw["pool_scale"], name=f"pool_fwd_l{l}")
        s["qr"], s["kr"], s["vb"] = _qk_prep(zq, lw["qgain"], lw["kgain"], cos, sin, name=f"qk_prep_l{l}")
        s["o"] = _attn_fwd(s["qr"], s["kr"], s["vb"], lw["sinks"], name=f"attn_fwd_l{l}")
        x2, s["mix"] = _merge_fwd(x1, s["p"], s["o"], zg, lw["w_pb"], lw["w_ab"], lw["w_out"],
                                                  name=f"merge_fwd_l{l}")
        s["x2"] = x2
        s["h3"], s["gu2"], act2 = _ffn_up(x2, lw["ln2"], lw["gu2"], name=f"ffn2_up_l{l}")
        s["act2"] = act2
        cur = _ffn_down(x2, act2, lw["down2"], name=f"ffn2_down_l{l}")
        saved.append(s)
        if in_flight is not None:
            gathered.append(layer_weights(l + 1, _exchange_wait(in_flight, cur, name=f"gather_wait_l{l + 1}")))

    loss_local, dy = _loss_head(cur, target, name="loss_head")
    loss = lax.psum(loss_local[0, 0], MESH_AXES)

    small_grads = {k: [None] * n_layers for k in _SMALL}
    received = {k: [None] * n_layers for k in big}
    big_late = ("w_ffn1_gu", "w_ffn1_down")
    big_early = tuple(k for k in big if k not in big_late)
    early_in_flight, late_in_flight = [None] * n_layers, [None] * n_layers
    token = None
    for l in reversed(range(n_layers)):
        lw, s = gathered[l], saved[l]
        c_ff = lw["gu1"].shape[-2]

        def ffn_bwd(dy, xin, h, gu, act, ln, wgu, wdown, tag, after):
            dyh, dgu = _ffn_bwd_act(dy, gu, wdown, name=f"{tag}_bwd_act_l{l}", after=after)
            dx, dln = _ffn_bwd_x(dy, xin, ln, dgu, wgu, name=f"{tag}_bwd_x_l{l}")
            dw_down = _matmul_tn(act, dyh[None], name=f"{tag}_dw_down_l{l}")
            dw_gu = _matmul_tn(dgu.reshape(N_DEV, seq, c_ff), h[None], name=f"{tag}_dw_gu_l{l}")
            return dx, dln, dw_gu, dw_down.reshape(N_DEV, c_ff // 2, d)

        dx2, dln2, dw_gu2, dw_down2 = ffn_bwd(dy, s["x2"], s["h3"], s["gu2"], s["act2"], lw["ln2"], lw["gu2"], lw["down2"], "ffn2", token)

        dyb, da, db, dp, do, dzg = _merge_bwd(dx2, s["p"], s["o"], s["zg"], lw["w_out"], lw["w_pb"], lw["w_ab"],
                                              name=f"merge_bwd_l{l}")
        dw_out = _matmul_tn(s["mix"][None], dyb[None], name=f"dw_out_l{l}")[0]
        dw_pb = _matmul_tn(s["p"][None], da[None], name=f"dw_pb_l{l}")[0]
        dw_ab = _matmul_tn(s["o"][None], db[None], name=f"dw_ab_l{l}")[0]
        dzp, dpw, dsc = _pool_bwd(dp, s["d"], lw["pool_w"], lw["pool_scale"], name=f"pool_bwd_l{l}")
        dq, dk, dv, dsinks = _attn_bwd(do, s["qr"], s["kr"], s["vb"], lw["sinks"], name=f"attn_bwd_l{l}")
        dzqkv, dqg, dkg = _qk_bwd(dq, dk, dv, s["zq"], lw["qgain"], lw["kgain"], cos, sin, name=f"qk_bwd_l{l}")
        dw_in = jnp.concatenate([_matmul_tn(dzp[None], s["h2"][None], name=f"dw_in_pool_l{l}")[0],
                                 _matmul_tn(dzqkv[None], s["h2"][None], name=f"dw_in_qkv_l{l}")[0],
                                 _matmul_tn(dzg[None], s["h2"][None], name=f"dw_in_gate_l{l}")[0]], axis=0)
        dx1, dlnm = _mix_bwd_x(dx2, s["x1"], lw["ln_mix"], dzp, dzqkv, dzg, lw["w_in"], name=f"mix_bwd_x_l{l}")

        partial = dict(w_in=dw_in.reshape(N_DEV, -1, d), w_pool_branch=_to_shard_major_cols(dw_pb),
                       w_attn_branch=_to_shard_major_cols(dw_ab), w_out=dw_out.reshape(N_DEV, d // N_DEV, d),
                       w_ffn2_gu=dw_gu2, w_ffn2_down=dw_down2)
        early_in_flight[l], token = _exchange_start([partial[k] for k in big_early], scatter=True,
                                                    name=f"grads_early_start_l{l}")

        dy, dln1, dw_gu1, dw_down1 = ffn_bwd(dx1, s["x0"], s["h1"], s["gu1"], s["act1"], lw["ln1"], lw["gu1"], lw["down1"], "ffn1", token)
        late_in_flight[l], token = _exchange_start([dw_gu1, dw_down1], scatter=True, name=f"grads_late_start_l{l}")
        small_grads["ln_ffn1"][l] = dln1[0]
        small_grads["ln_mix"][l] = dlnm[0]
        small_grads["ln_ffn2"][l] = dln2[0]
        small_grads["pool_w"][l] = dpw
        small_grads["pool_scale"][l] = dsc[0]
        small_grads["q_norm"][l] = dqg[0]
        small_grads["k_norm"][l] = dkg[0]
        small_grads["sinks"][l] = dsinks[0, :N_Q_HEADS]

    grad_x = dy.reshape(x.shape)
    after = token
    for l in reversed(range(n_layers)):
        got = _exchange_wait(early_in_flight[l], after, name=f"grads_early_wait_l{l}")
        after = got[0]
        for k, r in zip(big_early, got):
            received[k][l] = r

    grads, deltas, new_m, new_v = {}, {}, {}, {}

    def adamw(k):
        w = weights[k]
        shape2 = (n_layers, -1, w.shape[-1])
        parts = [r.reshape(N_DEV, -1, w.shape[-1]) for r in received[k]]
        outs = _adamw_sharded(parts, w.reshape(shape2), mom_m[k].reshape(shape2), mom_v[k].reshape(shape2), name=f"adamw_{k}")
        grads[k], deltas[k], new_m[k], new_v[k] = (o.reshape(w.shape) for o in outs)

    for k in big_early:
        adamw(k)
    after = grads[big_early[-1]]

    small_w = [weights[k] for k in _SMALL]
    packed_g = _pack_small([jnp.stack(small_grads[k]).reshape(weights[k].shape) for k in _SMALL])
    (parts_small,) = _all_gather_many([packed_g], name="gather_small_grads", after=after)
    outs = _adamw_packed(parts_small, _pack_small(small_w), _pack_small([mom_m[k] for k in _SMALL]),
                         _pack_small([mom_v[k] for k in _SMALL]), name="adamw_small")
    for res, o in zip((grads, deltas, new_m, new_v), outs):
        for k, a in zip(_SMALL, _unpack_small(o, small_w)):
            res[k] = a

    after = outs[0]
    for l in reversed(range(n_layers)):
        got = _exchange_wait(late_in_flight[l], after, name=f"grads_late_wait_l{l}")
        after = got[0]
        for k, r in zip(big_late, got):
            received[k][l] = r
    for k in big_late:
        adamw(k)

    for res in (grads, deltas, new_m, new_v):
        for k in transposed:
            res[k] = jnp.swapaxes(res[k], 1, 2)
    return (loss, grad_x, *[grads[k] for k in order], *[deltas[k] for k in order],
            *[new_m[k] for k in order], *[new_v[k] for k in order])
```

```python
import functools

import jax
import jax.numpy as jnp
from jax import lax
from jax.experimental import pallas as pl
from jax.experimental.pallas import tpu as pltpu

F32 = jnp.float32
BF16 = jnp.bfloat16

N_DEV = 8
MESH_AXES = ("x", "y", "c")
EPS = 1e-6

HEAD_DIM = 64
N_Q_HEADS = 8
N_KV_HEADS = 2
GQA_GROUP = N_Q_HEADS // N_KV_HEADS
ATTN_BLOCK = 128
ATTN_SCALE = HEAD_DIM ** -0.5
ROPE_THETA = 500000.0
ROT_DIM = 16
POOL_WINDOWS = (2, 4, 8, 16)
POOL_HALO = 16
GROUP_DIM = 128
POOL_DIM = 512
ATTN_DIM = 512
KV_DIM = 128
QKV_END = POOL_DIM + ATTN_DIM + 2 * KV_DIM

ADAM_LR = 0.001
ADAM_B1 = 0.9
ADAM_B2 = 0.999
ADAM_EPS = 1e-08
ADAM_WD = 0.01
ADAM_STEP = 10

ROW_TILE = 512
TN_ROW_TILE = 2048
VMEM_LIMIT_BYTES = 56 << 20
ADAMW_BLOCK_ELEMS = 96 * 1024
NEG_BIG = -1e30

_NT = (((1,), (1,)), ((), ()))
_TN = (((0,), (0,)), ((), ()))


def _cp(*sem):
    return pltpu.CompilerParams(dimension_semantics=sem, vmem_limit_bytes=VMEM_LIMIT_BYTES)


def _resident(block, index):
    return pl.BlockSpec(block, lambda *_: index, pipeline_mode=pl.Buffered(1))


def _row_tile(m):
    return min(ROW_TILE, m)


def _sds(shape, dtype):
    return jax.ShapeDtypeStruct(shape, dtype)


def _mesh_pos():
    return lax.axis_index("x"), lax.axis_index("y"), lax.axis_index("c")


def _all_gather_many(shards, name, after=None):
    n = len(shards)

    deps = [] if after is None else [after]

    def body(*refs):
        ins, outs = refs[:n], refs[n + len(deps):2 * n + len(deps)]
        send_sems, recv_sems, local_sems = refs[2 * n + len(deps):]
        x, y, c = _mesh_pos()
        me, sibling = (x, y, c), (x, y, 1 - c)
        chips = [(1 - x, y), (x, 1 - y), (1 - x, 1 - y)]

        def slot(a, pos):
            return outs[a].at[4 * pos[0] + 2 * pos[1] + pos[2]]

        def copy(a, k, block, to, src=None):
            return pltpu.make_async_remote_copy(
                src_ref=slot(a, block) if src is None else src, dst_ref=slot(a, block),
                send_sem=send_sems.at[a, k], recv_sem=recv_sems.at[a, k],
                device_id=to, device_id_type=pl.DeviceIdType.MESH)

        mine = [pltpu.make_async_copy(ins[a], slot(a, me), local_sems.at[a]) for a in range(n)]
        for cp in mine:
            cp.start()
        first = []
        for a in range(n):
            first.append(copy(a, 0, me, sibling, src=ins[a]))
            for j, chip in enumerate(chips):
                first.append(copy(a, 1 + j, me, (*chip, c), src=ins[a]))
        for cp in first:
            cp.start()
        passed = []
        for j, chip in enumerate(chips):
            for a in range(n):
                copy(a, 1 + j, (*chip, c), me).wait_recv()
                fwd = copy(a, 4 + j, (*chip, c), sibling)
                fwd.start()
                passed.append(fwd)
        for a in range(n):
            copy(a, 0, sibling, me).wait_recv()
        for j, chip in enumerate(chips):
            for a in range(n):
                copy(a, 4 + j, (*chip, 1 - c), me).wait_recv()
        for cp in first + passed:
            cp.wait_send()
        for cp in mine:
            cp.wait()

    any_spec = pl.BlockSpec(memory_space=pl.ANY)
    return pl.pallas_call(
        body, name=name,
        out_shape=[_sds((N_DEV,) + s.shape, s.dtype) for s in shards],
        in_specs=[any_spec] * (n + len(deps)), out_specs=[any_spec] * n,
        scratch_shapes=[pltpu.SemaphoreType.DMA((n, 7)), pltpu.SemaphoreType.DMA((n, 7)),
                        pltpu.SemaphoreType.DMA((n,))],
    )(*shards, *deps)


def _direct_copies(src, land, send_sem, recv_sem, local_sem, scatter):
    x, y, c = _mesh_pos()
    me = 4 * x + 2 * y + c
    local = pltpu.make_async_copy(src.at[me] if scatter else src, land.at[me], local_sem)
    remote = []
    for k in range(1, N_DEV):
        px = 1 - x if k & 4 else x
        py = 1 - y if k & 2 else y
        pc = 1 - c if k & 1 else c
        remote.append(pltpu.make_async_remote_copy(
            src_ref=src.at[4 * px + 2 * py + pc] if scatter else src, dst_ref=land.at[me],
            send_sem=send_sem, recv_sem=recv_sem, device_id=(px, py, pc), device_id_type=pl.DeviceIdType.MESH))
    seven = land.at[pl.ds(0, N_DEV - 1)]
    drain = pltpu.make_async_remote_copy(src_ref=seven, dst_ref=seven, send_sem=send_sem, recv_sem=recv_sem,
                                         device_id=(x, y, c), device_id_type=pl.DeviceIdType.MESH)
    return local, remote, drain


_HBM_SPEC = pl.BlockSpec(memory_space=pltpu.HBM)
_SEM_SPEC = pl.BlockSpec(memory_space=pltpu.SEMAPHORE)
_DATAFLOW = pltpu.SideEffectType.DATAFLOW_SIDE_EFFECTING
_SEMS_PER_ARRAY = 3


def _exchange_start(srcs, scatter, name):
    n = len(srcs)
    n_sems = _SEMS_PER_ARRAY * n
    land_shapes = [s.shape if scatter else (N_DEV,) + s.shape for s in srcs]

    def body(*refs):
        ins, lands, sems = refs[:n], refs[n:2 * n], refs[2 * n:2 * n + n_sems]
        for a in range(n):
            local, remote, _ = _direct_copies(ins[a], lands[a], *sems[3 * a:3 * a + 3], scatter)
            local.start()
            for cp in remote:
                cp.start()
        refs[-1][...] = jnp.zeros_like(refs[-1])

    outs = pl.pallas_call(
        body, name=name,
        out_shape=(*[pltpu.SemaphoreType.DMA(())] * n_sems,
                   *[pltpu.HBM(s.shape, s.dtype) for s in srcs],
                   *[pltpu.HBM(shape, s.dtype) for shape, s in zip(land_shapes, srcs)],
                   _sds((8, 128), F32)),
        in_specs=[_HBM_SPEC] * (2 * n),
        out_specs=(*[_SEM_SPEC] * n_sems, *[_HBM_SPEC] * (2 * n), pl.BlockSpec(memory_space=pltpu.VMEM)),
        input_output_aliases={i: n_sems + i for i in range(2 * n)},
        compiler_params=pltpu.CompilerParams(has_side_effects=_DATAFLOW),
    )(*[pltpu.with_memory_space_constraint(s, pltpu.HBM) for s in srcs],
      *[pltpu.with_memory_space_constraint(lax.empty(shape, s.dtype), pltpu.HBM) for shape, s in zip(land_shapes, srcs)])
    return (outs[:n_sems], outs[n_sems:n_sems + n], outs[n_sems + n:n_sems + 2 * n], scatter), outs[-1]


def _exchange_wait(state, after, name):
    sems, srcs, lands, scatter = state
    n = len(srcs)
    n_sems = len(sems)

    def body(*refs):
        ins, zones, ss = refs[:n], refs[n:2 * n], refs[2 * n:2 * n + n_sems]
        for a in range(n):
            local, _, drain = _direct_copies(ins[a], zones[a], *ss[3 * a:3 * a + 3], scatter)
            drain.wait_send()
            drain.wait_recv()
            local.wait()

    outs = pl.pallas_call(
        body, name=name,
        out_shape=(*[pltpu.HBM(s.shape, s.dtype) for s in srcs], *[pltpu.HBM(z.shape, z.dtype) for z in lands]),
        in_specs=[_HBM_SPEC] * (2 * n) + [_SEM_SPEC] * n_sems + [pl.BlockSpec(memory_space=pl.ANY)],
        out_specs=[_HBM_SPEC] * (2 * n),
        input_output_aliases={i: i for i in range(2 * n)},
        compiler_params=pltpu.CompilerParams(has_side_effects=_DATAFLOW),
    )(*srcs, *lands, *sems, after)
    return outs[n:]


def _rms_fwd(xv, gain):
    r = lax.rsqrt(jnp.mean(xv * xv, axis=-1, keepdims=True) + EPS)
    return xv * r * gain


def _rms_bwd(dh, xv, gain):
    r = lax.rsqrt(jnp.mean(xv * xv, axis=-1, keepdims=True) + EPS)
    xn = xv * r
    dxn = dh * gain
    dx = r * (dxn - xn * jnp.mean(dxn * xn, axis=-1, keepdims=True))
    return dx, dh * xn


def _silu_parts(g):
    s = 0.5 * jnp.tanh(0.5 * g) + 0.5
    return g * s, s * (1.0 + g * (1.0 - s))


def _segment_mean(v, width):
    r = lax.broadcasted_iota(jnp.int32, (width, width), 0) >> 6
    c = lax.broadcasted_iota(jnp.int32, (width, width), 1) >> 6
    bd = (r == c).astype(BF16)
    hi = v.astype(BF16)
    lo = (v - hi.astype(F32)).astype(BF16)
    total = jnp.dot(hi, bd, preferred_element_type=F32) + jnp.dot(lo, bd, preferred_element_type=F32)
    return total * (1.0 / HEAD_DIM)


def _rope_partner(v):
    width = v.shape[1]
    half = ROT_DIM // 2
    lane = lax.broadcasted_iota(jnp.int32, v.shape, 1) & (HEAD_DIM - 1)
    up = jnp.where(lane < ROT_DIM, pltpu.roll(v, half, 1), 0.0)
    return jnp.where(lane < half, pltpu.roll(v, width - half, 1), up)


def _tile_lanes(t, width):
    return t if width == t.shape[1] else jnp.tile(t, (1, width // t.shape[1]))


def _ffn_up(x, ln, wgu, name, after=None):
    m, d = x.shape
    c = wgu.shape[-2]
    tm = _row_tile(m)
    deps = [] if after is None else [after]

    def body(*refs):
        x_ref, ln_ref, w_ref = refs[:3]
        h_ref, gu_ref, a_ref = refs[-3:]
        h = _rms_fwd(x_ref[...], ln_ref[...]).astype(BF16)
        h_ref[...] = h
        for j in range(4):
            g = lax.dot_general(h, w_ref[0, j], _NT, preferred_element_type=F32)
            u = lax.dot_general(h, w_ref[1, j], _NT, preferred_element_type=F32)
            gu_ref[0, j] = g.astype(BF16)
            gu_ref[1, j] = u.astype(BF16)
            a_ref[j] = (g * jax.nn.sigmoid(g) * u).astype(BF16)

    return pl.pallas_call(
        body, name=name, grid=(m // tm,),
        in_specs=[pl.BlockSpec((tm, d), lambda i: (i, 0)), _resident((1, d), (0, 0)),
                  _resident((2, 4, c, d), (0, 0, 0, 0))] + [pl.BlockSpec(memory_space=pl.ANY)] * len(deps),
        out_specs=[pl.BlockSpec((tm, d), lambda i: (i, 0)),
                   pl.BlockSpec((2, 4, tm, c), lambda i: (0, 0, i, 0)),
                   pl.BlockSpec((4, tm, c), lambda i: (0, i, 0))],
        out_shape=[_sds((m, d), BF16), _sds((2, 4, m, c), BF16), _sds((4, m, c), BF16)],
        compiler_params=_cp("parallel"),
    )(x, ln, wgu, *deps)


def _ffn_down(x, act, wd, name):
    m, d = x.shape
    c = act.shape[-1]
    tm = _row_tile(m)

    def body(x_ref, a_ref, w_ref, o_ref):
        acc = jnp.dot(a_ref[0], w_ref[0], preferred_element_type=F32)
        for j in range(1, 4):
            acc += jnp.dot(a_ref[j], w_ref[j], preferred_element_type=F32)
        o_ref[...] = x_ref[...] + 0.5 * acc

    return pl.pallas_call(
        body, name=name, grid=(m // tm,),
        in_specs=[pl.BlockSpec((tm, d), lambda i: (i, 0)), pl.BlockSpec((4, tm, c), lambda i: (0, i, 0)),
                  _resident((4, c, d), (0, 0, 0))],
        out_specs=pl.BlockSpec((tm, d), lambda i: (i, 0)),
        out_shape=_sds((m, d), F32),
        compiler_params=_cp("parallel"),
    )(x, act, wd)


def _ffn_bwd_act(dy, gu, wd, name, after=None):
    m, d = dy.shape
    c = gu.shape[-1]
    tm = _row_tile(m)
    deps = [] if after is None else [after]

    def body(*refs):
        dy_ref, gu_ref, w_ref = refs[:3]
        dyh_ref, dgu_ref = refs[-2:]
        dyh = (0.5 * dy_ref[...]).astype(BF16)
        dyh_ref[...] = dyh
        for j in range(4):
            da = lax.dot_general(dyh, w_ref[j], _NT, preferred_element_type=F32)
            g = gu_ref[0, j].astype(F32)
            u = gu_ref[1, j].astype(F32)
            silu, dsilu = _silu_parts(g)
            dgu_ref[0, j] = (da * u * dsilu).astype(BF16)
            dgu_ref[1, j] = (da * silu).astype(BF16)

    return pl.pallas_call(
        body, name=name, grid=(m // tm,),
        in_specs=[pl.BlockSpec((tm, d), lambda i: (i, 0)), pl.BlockSpec((2, 4, tm, c), lambda i: (0, 0, i, 0)),
                  _resident((4, c, d), (0, 0, 0))] + [pl.BlockSpec(memory_space=pl.ANY)] * len(deps),
        out_specs=[pl.BlockSpec((tm, d), lambda i: (i, 0)), pl.BlockSpec((2, 4, tm, c), lambda i: (0, 0, i, 0))],
        out_shape=[_sds((m, d), BF16), _sds((2, 4, m, c), BF16)],
        compiler_params=_cp("parallel"),
    )(dy, gu, wd, *deps)


def _ffn_bwd_x(dy, x, ln, dgu, wgu, name, after=None):
    m, d = dy.shape
    c = dgu.shape[-1]
    tm = _row_tile(m)
    deps = [] if after is None else [after]

    def body(*refs):
        dy_ref, x_ref, ln_ref, dgu_ref, w_ref = refs[:5]
        dx_ref, dln_ref = refs[-2:]

        @pl.when(pl.program_id(0) == 0)
        def _():
            dln_ref[...] = jnp.zeros_like(dln_ref)

        dh = None
        for half in range(2):
            for j in range(4):
                t = jnp.dot(dgu_ref[half, j], w_ref[half, j], preferred_element_type=F32)
                dh = t if dh is None else dh + t
        dx, dgain = _rms_bwd(dh, x_ref[...], ln_ref[...])
        dx_ref[...] = dy_ref[...] + dx
        dln_ref[...] += jnp.sum(dgain, axis=0, keepdims=True)

    return pl.pallas_call(
        body, name=name, grid=(m // tm,),
        in_specs=[pl.BlockSpec((tm, d), lambda i: (i, 0)), pl.BlockSpec((tm, d), lambda i: (i, 0)),
                  _resident((1, d), (0, 0)), pl.BlockSpec((2, 4, tm, c), lambda i: (0, 0, i, 0)),
                  _resident((2, 4, c, d), (0, 0, 0, 0))] + [pl.BlockSpec(memory_space=pl.ANY)] * len(deps),
        out_specs=[pl.BlockSpec((tm, d), lambda i: (i, 0)), pl.BlockSpec((1, d), lambda i: (0, 0))],
        out_shape=[_sds((m, d), F32), _sds((1, d), F32)],
        compiler_params=_cp("arbitrary"),
    )(dy, x, ln, dgu, wgu, *deps)


def _matmul_tn(a, b, name, out_dtype=BF16):
    ja, m, k = a.shape
    jb, _, n = b.shape
    nj = max(ja, jb)
    tm = min(TN_ROW_TILE, m)
    nm = m // tm

    def body(a_ref, b_ref, o_ref, acc):
        step = pl.program_id(1)

        @pl.when(step == 0)
        def _():
            acc[...] = jnp.zeros_like(acc)

        acc[...] += lax.dot_general(a_ref[...], b_ref[...], _TN, preferred_element_type=F32)

        @pl.when(step == nm - 1)
        def _():
            o_ref[...] = acc[...].astype(o_ref.dtype)

    return pl.pallas_call(
        body, name=name, grid=(nj, nm),
        in_specs=[pl.BlockSpec((None, tm, k), (lambda j, s: (j, s, 0)) if ja > 1 else (lambda j, s: (0, s, 0))),
                  pl.BlockSpec((None, tm, n), (lambda j, s: (j, s, 0)) if jb > 1 else (lambda j, s: (0, s, 0)))],
        out_specs=pl.BlockSpec((None, k, n), lambda j, s: (j, 0, 0)),
        out_shape=_sds((nj, k, n), out_dtype),
        scratch_shapes=[pltpu.VMEM((k, n), F32)],
        compiler_params=_cp("parallel", "arbitrary"),
    )(a, b)


def _mix_in(x, ln, w_in, name):
    m, d = x.shape
    n_in = w_in.shape[0]
    tm = _row_tile(m)

    def body(x_ref, ln_ref, w_ref, h_ref, zq_ref, zg_ref):
        h = _rms_fwd(x_ref[...], ln_ref[...]).astype(BF16)
        h_ref[...] = h
        zq_ref[...] = lax.dot_general(h, w_ref[:QKV_END, :], _NT, preferred_element_type=F32)
        zg_ref[...] = lax.dot_general(h, w_ref[QKV_END:, :], _NT, preferred_element_type=F32)

    return pl.pallas_call(
        body, name=name, grid=(m // tm,),
        in_specs=[pl.BlockSpec((tm, d), lambda i: (i, 0)), _resident((1, d), (0, 0)), _resident((n_in, d), (0, 0))],
        out_specs=[pl.BlockSpec((tm, d), lambda i: (i, 0)), pl.BlockSpec((tm, QKV_END), lambda i: (i, 0)),
                   pl.BlockSpec((tm, n_in - QKV_END), lambda i: (i, 0))],
        out_shape=[_sds((m, d), BF16), _sds((m, QKV_END), F32), _sds((m, n_in - QKV_END), F32)],
        compiler_params=_cp("parallel"),
    )(x, ln, w_in)


def _pool_fwd(zq, pool_w, scale, name):
    m = zq.shape[0]
    tm = _row_tile(m)
    halo_blocks = tm // POOL_HALO

    def body(zc_ref, zh_ref, pw_ref, sc_ref, d_ref, p_ref):
        i = pl.program_id(0)
        halo = jnp.where(i > 0, zh_ref[...], 0.0)
        ext = jnp.concatenate([halo, zc_ref[...]], axis=0)
        t = i * tm + lax.broadcasted_iota(jnp.int32, (tm, 1), 0)
        for g, w in enumerate(POOL_WINDOWS):
            lanes = slice(g * GROUP_DIM, (g + 1) * GROUP_DIM)
            e = ext[:, lanes]
            s, k = e, 1
            while k < w:
                s = s + pltpu.roll(s, k, 0)
                k *= 2
            cnt = jnp.minimum(t + 1, w).astype(F32)
            dg = (s[POOL_HALO:, :] / cnt - e[POOL_HALO:, :]).astype(BF16)
            y = jnp.dot(dg, pw_ref[g].astype(BF16), preferred_element_type=F32)
            d_ref[:, lanes] = dg
            p_ref[:, lanes] = (y * sc_ref[:, lanes]).astype(BF16)

    return pl.pallas_call(
        body, name=name, grid=(m // tm,),
        in_specs=[pl.BlockSpec((tm, POOL_DIM), lambda i: (i, 0)),
                  pl.BlockSpec((POOL_HALO, POOL_DIM), lambda i: (jnp.maximum(i * halo_blocks - 1, 0), 0)),
                  _resident((4, GROUP_DIM, GROUP_DIM), (0, 0, 0)), _resident((1, POOL_DIM), (0, 0))],
        out_specs=[pl.BlockSpec((tm, POOL_DIM), lambda i: (i, 0)), pl.BlockSpec((tm, POOL_DIM), lambda i: (i, 0))],
        out_shape=[_sds((m, POOL_DIM), BF16), _sds((m, POOL_DIM), BF16)],
        compiler_params=_cp("parallel"),
    )(zq, zq, pool_w, scale)


def _pool_bwd(dp, d, pool_w, scale, name):
    m = dp.shape[0]
    tm = _row_tile(m)
    nb = m // tm
    halo_blocks = tm // POOL_HALO
    rows = tm + POOL_HALO

    def body(dpc_ref, dph_ref, d_ref, pw_ref, sc_ref, du_ref, dpw_ref, dsc_ref):
        i = pl.program_id(0)

        @pl.when(i == 0)
        def _():
            dpw_ref[...] = jnp.zeros_like(dpw_ref)
            dsc_ref[...] = jnp.zeros_like(dsc_ref)

        halo = jnp.where(i < nb - 1, dph_ref[...], 0.0)
        dpc = dpc_ref[...]
        ext = jnp.concatenate([dpc, halo], axis=0)
        t = i * tm + lax.broadcasted_iota(jnp.int32, (rows, 1), 0)
        for g, w in enumerate(POOL_WINDOWS):
            lanes = slice(g * GROUP_DIM, (g + 1) * GROUP_DIM)
            pwb = pw_ref[g].astype(BF16)
            dyb = (ext[:, lanes] * sc_ref[:, lanes]).astype(BF16)
            dd = lax.dot_general(dyb, pwb, _NT, preferred_element_type=F32)
            cnt = jnp.minimum(t + 1, w).astype(F32)
            s, k = dd / cnt, 1
            while k < w:
                s = s + pltpu.roll(s, rows - k, 0)
                k *= 2
            du_ref[:, lanes] = (s[:tm, :] - dd[:tm, :]).astype(BF16)
            dcur = d_ref[:, lanes]
            y = jnp.dot(dcur, pwb, preferred_element_type=F32)
            dsc_ref[:, lanes] += jnp.sum(dpc[:, lanes] * y, axis=0, keepdims=True)
            dpw_ref[g] += lax.dot_general(dcur, dyb[:tm, :], _TN, preferred_element_type=F32)

    return pl.pallas_call(
        body, name=name, grid=(nb,),
        in_specs=[pl.BlockSpec((tm, POOL_DIM), lambda i: (i, 0)),
                  pl.BlockSpec((POOL_HALO, POOL_DIM), lambda i: (jnp.minimum((i + 1) * halo_blocks, nb * halo_blocks - 1), 0)),
                  pl.BlockSpec((tm, POOL_DIM), lambda i: (i, 0)),
                  _resident((4, GROUP_DIM, GROUP_DIM), (0, 0, 0)), _resident((1, POOL_DIM), (0, 0))],
        out_specs=[pl.BlockSpec((tm, POOL_DIM), lambda i: (i, 0)),
                   pl.BlockSpec((4, GROUP_DIM, GROUP_DIM), lambda i: (0, 0, 0)),
                   pl.BlockSpec((1, POOL_DIM), lambda i: (0, 0))],
        out_shape=[_sds((m, POOL_DIM), BF16), _sds((4, GROUP_DIM, GROUP_DIM), F32), _sds((1, POOL_DIM), F32)],
        compiler_params=_cp("arbitrary"),
    )(dp, dp, d, pool_w, scale)


def _qk_norm_rope(xv, gain, cos, sin):
    width = xv.shape[1]
    r = lax.rsqrt(_segment_mean(xv * xv, width) + EPS)
    y = xv * r * gain
    return y * _tile_lanes(cos, width) + _rope_partner(y) * _tile_lanes(sin, width)


def _qk_prep(zq, qgain, kgain, cos, sin, name):
    m = zq.shape[0]
    tm = _row_tile(m)

    def body(q_ref, kv_ref, qg_ref, kg_ref, cos_ref, sin_ref, qr_ref, kr_ref, v_ref):
        cos_v, sin_v = cos_ref[...], sin_ref[...]
        qr_ref[...] = (_qk_norm_rope(q_ref[...], qg_ref[...], cos_v, sin_v) * ATTN_SCALE).astype(BF16)
        kv = kv_ref[...]
        kr_ref[...] = _qk_norm_rope(kv[:, :KV_DIM], kg_ref[...], cos_v, sin_v).astype(BF16)
        v_ref[...] = kv[:, KV_DIM:].astype(BF16)

    return pl.pallas_call(
        body, name=name, grid=(m // tm,),
        in_specs=[pl.BlockSpec((tm, ATTN_DIM), lambda i: (i, 1)), pl.BlockSpec((tm, 2 * KV_DIM), lambda i: (i, 4)),
                  _resident((1, ATTN_DIM), (0, 0)), _resident((1, KV_DIM), (0, 0)),
                  pl.BlockSpec((tm, 128), lambda i: (i, 0)), pl.BlockSpec((tm, 128), lambda i: (i, 0))],
        out_specs=[pl.BlockSpec((tm, ATTN_DIM), lambda i: (i, 0)), pl.BlockSpec((tm, KV_DIM), lambda i: (i, 0)),
                   pl.BlockSpec((tm, KV_DIM), lambda i: (i, 0))],
        out_shape=[_sds((m, ATTN_DIM), BF16), _sds((m, KV_DIM), BF16), _sds((m, KV_DIM), BF16)],
        compiler_params=_cp("parallel"),
    )(zq, zq, qgain, kgain, cos, sin)


GROUP_ROWS = GQA_GROUP * ATTN_BLOCK


def _band_bias():
    qi = jnp.arange(GROUP_ROWS)[:, None] % ATTN_BLOCK
    ki = jnp.arange(2 * ATTN_BLOCK)[None, :]
    diff = qi + ATTN_BLOCK - ki
    band = (diff >= 0) & (diff < ATTN_BLOCK)
    first = band & (ki >= ATTN_BLOCK)
    return jnp.where(jnp.stack([first, band]), 0.0, NEG_BIG).astype(F32)


def _stack_group(v, kvh):
    heads = range(kvh * GQA_GROUP, (kvh + 1) * GQA_GROUP)
    return jnp.concatenate([v[:, h * HEAD_DIM:(h + 1) * HEAD_DIM] for h in heads], axis=0)


def _unstack_groups(groups):
    return jnp.concatenate([grp[g * ATTN_BLOCK:(g + 1) * ATTN_BLOCK, :] for grp in groups for g in range(GQA_GROUP)],
                           axis=1)


def _group_sinks(sk_ref, kvh):
    row_head = lax.broadcasted_iota(jnp.int32, (GROUP_ROWS, 1), 0) >> 7
    col = jnp.full((GROUP_ROWS, 1), sk_ref[kvh * GQA_GROUP], F32)
    for g in range(1, GQA_GROUP):
        col = jnp.where(row_head == g, sk_ref[kvh * GQA_GROUP + g], col)
    return col


def _head_probs(qh, kh, bias, sink):
    s = lax.dot_general(qh, kh, _NT, preferred_element_type=F32) + bias
    mx = jnp.maximum(jnp.max(s, axis=-1, keepdims=True), sink)
    p = jnp.exp(s - mx)
    es = jnp.exp(sink - mx)
    inv = 1.0 / (jnp.sum(p, axis=-1, keepdims=True) + es)
    return p * inv, es * inv


def _attn_fwd(qr, kr, vb, sinks, name):
    m = qr.shape[0]
    nb = m // ATTN_BLOCK

    def body(q_ref, kp_ref, kc_ref, vp_ref, vc_ref, sk_ref, bias_ref, o_ref):
        bias = bias_ref[jnp.minimum(pl.program_id(0), 1)]
        qv = q_ref[...]
        kk = jnp.concatenate([kp_ref[...], kc_ref[...]], axis=0)
        vv = jnp.concatenate([vp_ref[...], vc_ref[...]], axis=0)
        outs = []
        for kvh in range(N_KV_HEADS):
            kv_lanes = slice(kvh * HEAD_DIM, (kvh + 1) * HEAD_DIM)
            p, _ = _head_probs(_stack_group(qv, kvh), kk[:, kv_lanes], bias, _group_sinks(sk_ref, kvh))
            outs.append(jnp.dot(p.astype(BF16), vv[:, kv_lanes], preferred_element_type=F32))
        o_ref[...] = _unstack_groups(outs).astype(BF16)

    prev = lambda n: (jnp.maximum(n - 1, 0), 0)
    cur = lambda n: (n, 0)
    return pl.pallas_call(
        body, name=name, grid=(nb,),
        in_specs=[pl.BlockSpec((ATTN_BLOCK, ATTN_DIM), cur),
                  pl.BlockSpec((ATTN_BLOCK, KV_DIM), prev), pl.BlockSpec((ATTN_BLOCK, KV_DIM), cur),
                  pl.BlockSpec((ATTN_BLOCK, KV_DIM), prev), pl.BlockSpec((ATTN_BLOCK, KV_DIM), cur),
                  pl.BlockSpec(memory_space=pltpu.SMEM), _resident((2, GROUP_ROWS, 2 * ATTN_BLOCK), (0, 0, 0))],
        out_specs=pl.BlockSpec((ATTN_BLOCK, ATTN_DIM), cur),
        out_shape=_sds((m, ATTN_DIM), BF16),
        compiler_params=_cp("parallel"),
    )(qr, kr, kr, vb, vb, sinks, _band_bias())


def _attn_bwd(do, qr, kr, vb, sinks, name):
    m = qr.shape[0]
    nb = m // ATTN_BLOCK

    def body(do_ref, q_ref, kp_ref, kc_ref, vp_ref, vc_ref, sk_ref, bias_ref, dq_ref, dk_ref, dv_ref, ds_ref,
             carry_k, carry_v):
        n = pl.program_id(0)

        @pl.when(n == 0)
        def _():
            carry_k[...] = jnp.zeros_like(carry_k)
            carry_v[...] = jnp.zeros_like(carry_v)
            ds_ref[...] = jnp.zeros_like(ds_ref)

        @pl.when(n < nb)
        def _():
            bias = bias_ref[jnp.minimum(n, 1)]
            qv = q_ref[...]
            dov = do_ref[...]
            kk = jnp.concatenate([kp_ref[...], kc_ref[...]], axis=0)
            vv = jnp.concatenate([vp_ref[...], vc_ref[...]], axis=0)
            lane = lax.broadcasted_iota(jnp.int32, (1, 128), 1)
            dsink = jnp.zeros((1, 128), F32)
            dqs, dks, dvs = [], [], []
            for kvh in range(N_KV_HEADS):
                kv_lanes = slice(kvh * HEAD_DIM, (kvh + 1) * HEAD_DIM)
                kh, vh = kk[:, kv_lanes], vv[:, kv_lanes]
                qg = _stack_group(qv, kvh)
                dog = _stack_group(dov, kvh).astype(BF16)
                p, ps = _head_probs(qg, kh, bias, _group_sinks(sk_ref, kvh))
                dpr = lax.dot_general(dog, vh, _NT, preferred_element_type=F32)
                delta = jnp.sum(p * dpr, axis=-1, keepdims=True)
                dsb = (p * (dpr - delta)).astype(BF16)
                sink_term = ps * delta
                for g in range(GQA_GROUP):
                    rows = slice(g * ATTN_BLOCK, (g + 1) * ATTN_BLOCK)
                    dsink = dsink + jnp.where(lane == kvh * GQA_GROUP + g, -jnp.sum(sink_term[rows, :]), 0.0)
                dqs.append(jnp.dot(dsb, kh, preferred_element_type=F32) * ATTN_SCALE)
                dks.append(lax.dot_general(dsb, qg, _TN, preferred_element_type=F32))
                dvs.append(lax.dot_general(p.astype(BF16), dog, _TN, preferred_element_type=F32))
            dq_ref[...] = _unstack_groups(dqs)
            dkk = jnp.concatenate(dks, axis=1)
            dvv = jnp.concatenate(dvs, axis=1)
            dk_ref[...] = carry_k[...] + dkk[:ATTN_BLOCK, :]
            dv_ref[...] = carry_v[...] + dvv[:ATTN_BLOCK, :]
            carry_k[...] = dkk[ATTN_BLOCK:, :]
            carry_v[...] = dvv[ATTN_BLOCK:, :]
            ds_ref[...] += dsink

        @pl.when(n == nb)
        def _():
            dk_ref[...] = carry_k[...]
            dv_ref[...] = carry_v[...]

    cur = lambda n: (jnp.minimum(n, nb - 1), 0)
    prev = lambda n: (jnp.clip(n - 1, 0, nb - 1), 0)
    return pl.pallas_call(
        body, name=name, grid=(nb + 1,),
        in_specs=[pl.BlockSpec((ATTN_BLOCK, ATTN_DIM), cur), pl.BlockSpec((ATTN_BLOCK, ATTN_DIM), cur),
                  pl.BlockSpec((ATTN_BLOCK, KV_DIM), prev), pl.BlockSpec((ATTN_BLOCK, KV_DIM), cur),
                  pl.BlockSpec((ATTN_BLOCK, KV_DIM), prev), pl.BlockSpec((ATTN_BLOCK, KV_DIM), cur),
                  pl.BlockSpec(memory_space=pltpu.SMEM), _resident((2, GROUP_ROWS, 2 * ATTN_BLOCK), (0, 0, 0))],
        out_specs=[pl.BlockSpec((ATTN_BLOCK, ATTN_DIM), cur), pl.BlockSpec((ATTN_BLOCK, KV_DIM), prev),
                   pl.BlockSpec((ATTN_BLOCK, KV_DIM), prev), pl.BlockSpec((1, 128), lambda n: (0, 0))],
        out_shape=[_sds((m, ATTN_DIM), F32), _sds((m, KV_DIM), F32), _sds((m, KV_DIM), F32), _sds((1, 128), F32)],
        scratch_shapes=[pltpu.VMEM((ATTN_BLOCK, KV_DIM), F32), pltpu.VMEM((ATTN_BLOCK, KV_DIM), F32)],
        compiler_params=_cp("arbitrary"),
    )(do, qr, kr, kr, vb, vb, sinks, _band_bias())


def _qk_norm_rope_bwd(dout, xv, gain, cos, sin):
    width = xv.shape[1]
    r = lax.rsqrt(_segment_mean(xv * xv, width) + EPS)
    xn = xv * r
    dy = dout * _tile_lanes(cos, width) + _rope_partner(dout * _tile_lanes(sin, width))
    dxn = dy * gain
    dx = r * (dxn - xn * _segment_mean(dxn * xn, width))
    return dx, jnp.sum(dy * xn, axis=0, keepdims=True)


def _fold_heads(v):
    out = v[:, :HEAD_DIM]
    for h in range(1, v.shape[1] // HEAD_DIM):
        out = out + v[:, h * HEAD_DIM:(h + 1) * HEAD_DIM]
    return out


def _qk_bwd(dq, dk, dv, zq, qgain, kgain, cos, sin, name):
    m = zq.shape[0]
    tm = _row_tile(m)

    def body(dq_ref, dk_ref, dv_ref, q_ref, kv_ref, qg_ref, kg_ref, cos_ref, sin_ref, dz_ref, dqg_ref, dkg_ref):
        @pl.when(pl.program_id(0) == 0)
        def _():
            dqg_ref[...] = jnp.zeros_like(dqg_ref)
            dkg_ref[...] = jnp.zeros_like(dkg_ref)

        cos_v, sin_v = cos_ref[...], sin_ref[...]
        dxq, dgq = _qk_norm_rope_bwd(dq_ref[...], q_ref[...], qg_ref[...], cos_v, sin_v)
        dxk, dgk = _qk_norm_rope_bwd(dk_ref[...], kv_ref[:, :KV_DIM], kg_ref[...], cos_v, sin_v)
        dz_ref[:, :ATTN_DIM] = dxq.astype(BF16)
        dz_ref[:, ATTN_DIM:ATTN_DIM + KV_DIM] = dxk.astype(BF16)
        dz_ref[:, ATTN_DIM + KV_DIM:] = dv_ref[...].astype(BF16)
        dqg_ref[...] += _fold_heads(dgq)
        dkg_ref[...] += _fold_heads(dgk)

    row = lambda i: (i, 0)
    return pl.pallas_call(
        body, name=name, grid=(m // tm,),
        in_specs=[pl.BlockSpec((tm, ATTN_DIM), row), pl.BlockSpec((tm, KV_DIM), row), pl.BlockSpec((tm, KV_DIM), row),
                  pl.BlockSpec((tm, ATTN_DIM), lambda i: (i, 1)), pl.BlockSpec((tm, 2 * KV_DIM), lambda i: (i, 4)),
                  _resident((1, ATTN_DIM), (0, 0)), _resident((1, KV_DIM), (0, 0)),
                  pl.BlockSpec((tm, 128), row), pl.BlockSpec((tm, 128), row)],
        out_specs=[pl.BlockSpec((tm, ATTN_DIM + 2 * KV_DIM), row), pl.BlockSpec((1, HEAD_DIM), lambda i: (0, 0)),
                   pl.BlockSpec((1, HEAD_DIM), lambda i: (0, 0))],
        out_shape=[_sds((m, ATTN_DIM + 2 * KV_DIM), BF16), _sds((1, HEAD_DIM), F32), _sds((1, HEAD_DIM), F32)],
        compiler_params=_cp("arbitrary"),
    )(dq, dk, dv, zq, zq, qgain, kgain, cos, sin)


def _merge_fwd(x, p, o, zg, w_pb, w_ab, w_out, name):
    m, d = x.shape
    tm = _row_tile(m)

    def body(x_ref, p_ref, o_ref, zg_ref, wpb_ref, wab_ref, wo_ref, xo_ref, mix_ref):
        a = jnp.dot(p_ref[...], wpb_ref[...], preferred_element_type=F32)
        b = jnp.dot(o_ref[...], wab_ref[...], preferred_element_type=F32)
        mix = (jax.nn.sigmoid(zg_ref[:, :d]) * a + jax.nn.sigmoid(zg_ref[:, d:]) * b).astype(BF16)
        mix_ref[...] = mix
        xo_ref[...] = x_ref[...] + jnp.dot(mix, wo_ref[...], preferred_element_type=F32)

    row = lambda i: (i, 0)
    return pl.pallas_call(
        body, name=name, grid=(m // tm,),
        in_specs=[pl.BlockSpec((tm, d), row), pl.BlockSpec((tm, POOL_DIM), row), pl.BlockSpec((tm, ATTN_DIM), row),
                  pl.BlockSpec((tm, 2 * d), row), _resident((POOL_DIM, d), (0, 0)), _resident((ATTN_DIM, d), (0, 0)),
                  _resident((d, d), (0, 0))],
        out_specs=[pl.BlockSpec((tm, d), row)] * 2,
        out_shape=[_sds((m, d), F32), _sds((m, d), BF16)],
        compiler_params=_cp("parallel"),
    )(x, p, o, zg, w_pb, w_ab, w_out)


def _merge_bwd(dy, p, o, zg, w_out, w_pb, w_ab, name):
    m, d = dy.shape
    tm = _row_tile(m)

    def body(dy_ref, p_ref, o_ref, zg_ref, wo_ref, wpb_ref, wab_ref, dyb_ref, da_ref, db_ref, dp_ref, do_ref, dzg_ref):
        dyb = dy_ref[...].astype(BF16)
        dyb_ref[...] = dyb
        dmix = lax.dot_general(dyb, wo_ref[...], _NT, preferred_element_type=F32)
        gp = jax.nn.sigmoid(zg_ref[:, :d])
        ga = jax.nn.sigmoid(zg_ref[:, d:])
        da = (dmix * gp).astype(BF16)
        db = (dmix * ga).astype(BF16)
        da_ref[...] = da
        db_ref[...] = db
        a = jnp.dot(p_ref[...], wpb_ref[...], preferred_element_type=F32)
        b = jnp.dot(o_ref[...], wab_ref[...], preferred_element_type=F32)
        dzg_ref[:, :d] = (dmix * a * gp * (1.0 - gp)).astype(BF16)
        dzg_ref[:, d:] = (dmix * b * ga * (1.0 - ga)).astype(BF16)
        dp_ref[...] = lax.dot_general(da, wpb_ref[...], _NT, preferred_element_type=F32)
        do_ref[...] = lax.dot_general(db, wab_ref[...], _NT, preferred_element_type=F32)

    row = lambda i: (i, 0)
    return pl.pallas_call(
        body, name=name, grid=(m // tm,),
        in_specs=[pl.BlockSpec((tm, d), row), pl.BlockSpec((tm, POOL_DIM), row), pl.BlockSpec((tm, ATTN_DIM), row),
                  pl.BlockSpec((tm, 2 * d), row), _resident((d, d), (0, 0)), _resident((POOL_DIM, d), (0, 0)),
                  _resident((ATTN_DIM, d), (0, 0))],
        out_specs=[pl.BlockSpec((tm, d), row), pl.BlockSpec((tm, d), row), pl.BlockSpec((tm, d), row),
                   pl.BlockSpec((tm, POOL_DIM), row), pl.BlockSpec((tm, ATTN_DIM), row), pl.BlockSpec((tm, 2 * d), row)],
        out_shape=[_sds((m, d), BF16), _sds((m, d), BF16), _sds((m, d), BF16), _sds((m, POOL_DIM), F32),
                   _sds((m, ATTN_DIM), F32), _sds((m, 2 * d), BF16)],
        compiler_params=_cp("parallel"),
    )(dy, p, o, zg, w_out, w_pb, w_ab)


def _mix_bwd_x(dy, x, ln, dzp, dzqkv, dzg, w_in, name):
    m, d = dy.shape
    n_in = w_in.shape[0]
    tm = _row_tile(m)

    def body(dy_ref, x_ref, ln_ref, dzp_ref, dzq_ref, dzg_ref, w_ref, dx_ref, dln_ref):
        @pl.when(pl.program_id(0) == 0)
        def _():
            dln_ref[...] = jnp.zeros_like(dln_ref)

        dh = jnp.dot(dzp_ref[...], w_ref[:POOL_DIM, :], preferred_element_type=F32)
        dh += jnp.dot(dzq_ref[...], w_ref[POOL_DIM:QKV_END, :], preferred_element_type=F32)
        dh += jnp.dot(dzg_ref[...], w_ref[QKV_END:, :], preferred_element_type=F32)
        dx, dgain = _rms_bwd(dh, x_ref[...], ln_ref[...])
        dx_ref[...] = dy_ref[...] + dx
        dln_ref[...] += jnp.sum(dgain, axis=0, keepdims=True)

    row = lambda i: (i, 0)
    return pl.pallas_call(
        body, name=name, grid=(m // tm,),
        in_specs=[pl.BlockSpec((tm, d), row), pl.BlockSpec((tm, d), row), _resident((1, d), (0, 0)),
                  pl.BlockSpec((tm, POOL_DIM), row), pl.BlockSpec((tm, QKV_END - POOL_DIM), row),
                  pl.BlockSpec((tm, n_in - QKV_END), row), _resident((n_in, d), (0, 0))],
        out_specs=[pl.BlockSpec((tm, d), row), pl.BlockSpec((1, d), lambda i: (0, 0))],
        out_shape=[_sds((m, d), F32), _sds((1, d), F32)],
        compiler_params=_cp("arbitrary"),
    )(dy, x, ln, dzp, dzqkv, dzg, w_in)


def _loss_head(y, target, name):
    m, d = y.shape
    tm = _row_tile(m)

    def body(y_ref, t_ref, loss_ref, dy_ref):
        @pl.when(pl.program_id(0) == 0)
        def _():
            loss_ref[...] = jnp.zeros_like(loss_ref)

        diff = y_ref[...] - t_ref[...]
        dy_ref[...] = diff * (1.0 / d)
        loss_ref[...] += 0.5 * jnp.sum(jnp.mean(diff * diff, axis=-1, keepdims=True), axis=0, keepdims=True)

    row = lambda i: (i, 0)
    return pl.pallas_call(
        body, name=name, grid=(m // tm,),
        in_specs=[pl.BlockSpec((tm, d), row), pl.BlockSpec((tm, d), row)],
        out_specs=[pl.BlockSpec((1, 1), lambda i: (0, 0)), pl.BlockSpec((tm, d), row)],
        out_shape=[_sds((1, 1), F32), _sds((m, d), F32)],
        compiler_params=_cp("arbitrary"),
    )(y, target)


def _adamw_math(g, w, m, v):
    m2 = ADAM_B1 * m + (1.0 - ADAM_B1) * g
    v2 = ADAM_B2 * v + (1.0 - ADAM_B2) * (g * g)
    m_hat = m2 / (1.0 - ADAM_B1 ** ADAM_STEP)
    v_hat = v2 / (1.0 - ADAM_B2 ** ADAM_STEP)
    delta = -ADAM_LR * (m_hat / (jnp.sqrt(v_hat) + ADAM_EPS) + ADAM_WD * w)
    return delta, m2, v2


def _sum_parts(parts_ref):
    g = parts_ref[0].astype(F32)
    for s in range(1, N_DEV):
        g = g + parts_ref[s].astype(F32)
    return g


def _adamw_sharded(parts, w, m, v, name, after=None):
    n_layers, rows, cols = w.shape
    tr = next(t for t in (512, 256, 128, 64, 32, 16, 8) if rows % t == 0 and t * cols <= ADAMW_BLOCK_ELEMS)
    nr = rows // tr
    deps = [] if after is None else [after]

    def body(*refs):
        part_refs = refs[:n_layers]
        w_ref, m_ref, v_ref = refs[n_layers:n_layers + 3]
        g_out, d_out, m_out, v_out = refs[-4:]
        layer = pl.program_id(0)
        for l in range(n_layers):
            @pl.when(layer == l)
            def _(l=l):
                g = _sum_parts(part_refs[l])
                delta, m2, v2 = _adamw_math(g, w_ref[...], m_ref[...], v_ref[...])
                g_out[...] = g
                d_out[...] = delta
                m_out[...] = m2
                v_out[...] = v2

    def part_map(l):
        return lambda layer, r: (0, jnp.where(layer == l, r, jnp.where(layer < l, 0, nr - 1)), 0)

    wspec = pl.BlockSpec((None, tr, cols), lambda layer, r: (layer, r, 0))
    return pl.pallas_call(
        body, name=name, grid=(n_layers, nr),
        in_specs=([pl.BlockSpec((N_DEV, tr, cols), part_map(l)) for l in range(n_layers)] + [wspec] * 3
                  + [pl.BlockSpec(memory_space=pl.ANY)] * len(deps)),
        out_specs=[wspec] * 4,
        out_shape=[_sds(w.shape, F32)] * 4,
        compiler_params=_cp("arbitrary", "arbitrary"),
    )(*parts, w, m, v, *deps)


def _adamw_packed(parts, w, m, v, name):
    def body(p_ref, w_ref, m_ref, v_ref, g_out, d_out, m_out, v_out):
        g = _sum_parts(p_ref)
        delta, m2, v2 = _adamw_math(g, w_ref[...], m_ref[...], v_ref[...])
        g_out[...] = g
        d_out[...] = delta
        m_out[...] = m2
        v_out[...] = v2

    return pl.pallas_call(
        body, name=name, out_shape=[_sds(w.shape, F32)] * 4,
        compiler_params=pltpu.CompilerParams(vmem_limit_bytes=VMEM_LIMIT_BYTES),
    )(parts, w, m, v)


_SMALL = ("ln_ffn1", "ln_mix", "pool_w", "pool_scale", "q_norm", "k_norm", "sinks", "ln_ffn2")


def _pack_small(arrs):
    rows = []
    for a in arrs:
        flat = a.reshape(-1)
        pad = (-flat.shape[0]) % 1024
        rows.append(jnp.pad(flat, (0, pad)).reshape(-1, 128))
    return jnp.concatenate(rows, axis=0)


def _unpack_small(packed, like):
    out, r0 = [], 0
    for a in like:
        size = a.size
        nrows = (size + 1023) // 1024 * 8
        out.append(packed[r0:r0 + nrows].reshape(-1)[:size].reshape(a.shape))
        r0 += nrows
    return out


def _rope_tables(m):
    pos = jnp.arange(m, dtype=F32)
    inv_freq = ROPE_THETA ** (-jnp.arange(0, ROT_DIM, 2, dtype=F32) / ROT_DIM)
    ang = pos[:, None] * inv_freq[None, :]
    cos8, sin8 = jnp.cos(ang), jnp.sin(ang)
    rest = HEAD_DIM - ROT_DIM
    cos64 = jnp.concatenate([cos8, cos8, jnp.ones((m, rest), F32)], axis=1)
    sin64 = jnp.concatenate([-sin8, sin8, jnp.zeros((m, rest), F32)], axis=1)
    return jnp.tile(cos64, (1, 2)), jnp.tile(sin64, (1, 2))


def _to_shard_major_cols(w):
    k = w.shape[0]
    return w.reshape(k, N_DEV, -1).transpose(1, 0, 2)


def _from_shard_major_cols(w):
    return w.transpose(1, 0, 2).reshape(w.shape[1], -1)


def kernel(x, ln_ffn1, w_ffn1_gu, w_ffn1_down, ln_mix, w_in, pool_w, pool_scale, w_pool_branch, q_norm, k_norm, sinks, w_attn_branch, w_out, ln_ffn2, w_ffn2_gu, w_ffn2_down, loss_target, m_ln_ffn1, m_w_ffn1_gu, m_w_ffn1_down, m_ln_mix, m_w_in, m_pool_w, m_pool_scale, m_w_pool_branch, m_q_norm, m_k_norm, m_sinks, m_w_attn_branch, m_w_out, m_ln_ffn2, m_w_ffn2_gu, m_w_ffn2_down, v_ln_ffn1, v_w_ffn1_gu, v_w_ffn1_down, v_ln_mix, v_w_in, v_pool_w, v_pool_scale, v_w_pool_branch, v_q_norm, v_k_norm, v_sinks, v_w_attn_branch, v_w_out, v_ln_ffn2, v_w_ffn2_gu, v_w_ffn2_down):
    weights = dict(ln_ffn1=ln_ffn1, w_ffn1_gu=w_ffn1_gu, w_ffn1_down=w_ffn1_down, ln_mix=ln_mix, w_in=w_in, pool_w=pool_w,
                   pool_scale=pool_scale, w_pool_branch=w_pool_branch, q_norm=q_norm, k_norm=k_norm, sinks=sinks,
                   w_attn_branch=w_attn_branch, w_out=w_out, ln_ffn2=ln_ffn2, w_ffn2_gu=w_ffn2_gu, w_ffn2_down=w_ffn2_down)
    mom_m = dict(ln_ffn1=m_ln_ffn1, w_ffn1_gu=m_w_ffn1_gu, w_ffn1_down=m_w_ffn1_down, ln_mix=m_ln_mix, w_in=m_w_in,
                 pool_w=m_pool_w, pool_scale=m_pool_scale, w_pool_branch=m_w_pool_branch, q_norm=m_q_norm, k_norm=m_k_norm,
                 sinks=m_sinks, w_attn_branch=m_w_attn_branch, w_out=m_w_out, ln_ffn2=m_ln_ffn2, w_ffn2_gu=m_w_ffn2_gu,
                 w_ffn2_down=m_w_ffn2_down)
    mom_v = dict(ln_ffn1=v_ln_ffn1, w_ffn1_gu=v_w_ffn1_gu, w_ffn1_down=v_w_ffn1_down, ln_mix=v_ln_mix, w_in=v_w_in,
                 pool_w=v_pool_w, pool_scale=v_pool_scale, w_pool_branch=v_w_pool_branch, q_norm=v_q_norm, k_norm=v_k_norm,
                 sinks=v_sinks, w_attn_branch=v_w_attn_branch, w_out=v_w_out, ln_ffn2=v_ln_ffn2, w_ffn2_gu=v_w_ffn2_gu,
                 w_ffn2_down=v_w_ffn2_down)
    order = ("ln_ffn1", "w_ffn1_gu", "w_ffn1_down", "ln_mix", "w_in", "pool_w", "pool_scale", "w_pool_branch", "q_norm",
             "k_norm", "sinks", "w_attn_branch", "w_out", "ln_ffn2", "w_ffn2_gu", "w_ffn2_down")
    big = ("w_ffn1_gu", "w_ffn1_down", "w_in", "w_pool_branch", "w_attn_branch", "w_out", "w_ffn2_gu", "w_ffn2_down")

    transposed = ("w_ffn1_gu", "w_ffn2_gu", "w_in")
    for group in (weights, mom_m, mom_v):
        for k in transposed:
            group[k] = jnp.swapaxes(group[k], 1, 2)

    n_layers = ln_ffn1.shape[0]
    seq, d = x.shape[-2], x.shape[-1]
    xs = x.reshape(seq, d)
    target = loss_target.reshape(seq, d)
    cos, sin = _rope_tables(seq)

    def layer_shards(l):
        return [weights[k][l].astype(BF16) for k in big]

    def layer_weights(l, full):
        g = dict(zip(big, full))
        c_ff = g["w_ffn1_gu"].shape[-2]
        return dict(
            gu1=g["w_ffn1_gu"].reshape(2, 4, c_ff, d), down1=g["w_ffn1_down"].reshape(4, c_ff, d),
            gu2=g["w_ffn2_gu"].reshape(2, 4, c_ff, d), down2=g["w_ffn2_down"].reshape(4, c_ff, d),
            w_in=g["w_in"].reshape(-1, d), w_pb=_from_shard_major_cols(g["w_pool_branch"]),
            w_ab=_from_shard_major_cols(g["w_attn_branch"]), w_out=g["w_out"].reshape(d, d),
            ln1=ln_ffn1[l][None], ln_mix=ln_mix[l][None], ln2=ln_ffn2[l][None], pool_w=pool_w[l],
            pool_scale=pool_scale[l][None], sinks=sinks[l],
            qgain=jnp.tile(q_norm[l], N_Q_HEADS)[None], kgain=jnp.tile(k_norm[l], N_KV_HEADS)[None])

    gathered = [layer_weights(0, _all_gather_many(layer_shards(0), name="gather_weights_l0"))]
    saved = []
    cur = xs
    for l in range(n_layers):
        lw = gathered[l]
        s = dict(x0=cur)
        in_flight, token = None, None
        if l + 1 < n_layers:
            in_flight, token = _exchange_start(layer_shards(l + 1), scatter=False, name=f"gather_start_l{l + 1}")
        s["h1"], s["gu1"], act1 = _ffn_up(cur, lw["ln1"], lw["gu1"], name=f"ffn1_up_l{l}", after=token)
        s["act1"] = act1
        x1 = _ffn_down(cur, act1, lw["down1"], name=f"ffn1_down_l{l}")
        s["x1"] = x1
        s["h2"], zq, zg = _mix_in(x1, lw["ln_mix"], lw["w_in"], name=f"mix_in_l{l}")
        s["zq"], s["zg"] = zq, zg
        s["d"], s["p"] = _pool_fwd(zq, lw["pool_w"], lw["pool_scale"], name=f"pool_fwd_l{l}")
        s["qr"], s["kr"], s["vb"] = _qk_prep(zq, lw["qgain"], lw["kgain"], cos, sin, name=f"qk_prep_l{l}")
        s["o"] = _attn_fwd(s["qr"], s["kr"], s["vb"], lw["sinks"], name=f"attn_fwd_l{l}")
        x2, s["mix"] = _merge_fwd(x1, s["p"], s["o"], zg, lw["w_pb"], lw["w_ab"], lw["w_out"],
                                                  name=f"merge_fwd_l{l}")
        s["x2"] = x2
        s["h3"], s["gu2"], act2 = _ffn_up(x2, lw["ln2"], lw["gu2"], name=f"ffn2_up_l{l}")
        s["act2"] = act2
        cur = _ffn_down(x2, act2, lw["down2"], name=f"ffn2_down_l{l}")
        saved.append(s)
        if in_flight is not None:
            gathered.append(layer_weights(l + 1, _exchange_wait(in_flight, cur, name=f"gather_wait_l{l + 1}")))

    loss_local, dy = _loss_head(cur, target, name="loss_head")
    loss = lax.psum(loss_local[0, 0], MESH_AXES)

    small_grads = {k: [None] * n_layers for k in _SMALL}
    received = {k: [None] * n_layers for k in big}
    big_late = ("w_ffn1_gu", "w_ffn1_down")
    big_early = tuple(k for k in big if k not in big_late)
    early_in_flight, late_in_flight = [None] * n_layers, [None] * n_layers
    token = None
    for l in reversed(range(n_layers)):
        lw, s = gathered[l], saved[l]
        c_ff = lw["gu1"].shape[-2]

        def ffn_weight_grads(dyh, dgu, h, act, tag):
            dw_down = _matmul_tn(act, dyh[None], name=f"{tag}_dw_down_l{l}")
            dw_gu = _matmul_tn(dgu.reshape(N_DEV, seq, c_ff), h[None], name=f"{tag}_dw_gu_l{l}")
            return dw_gu, dw_down.reshape(N_DEV, c_ff // 2, d)

        dyh, dgu = _ffn_bwd_act(dy, s["gu2"], lw["down2"], name=f"ffn2_bwd_act_l{l}", after=token)
        dx2, dln2 = _ffn_bwd_x(dy, s["x2"], lw["ln2"], dgu, lw["gu2"], name=f"ffn2_bwd_x_l{l}")
        dw_gu2, dw_down2 = ffn_weight_grads(dyh, dgu, s["h3"], s["act2"], "ffn2")

        dyb, da, db, dp, do, dzg = _merge_bwd(dx2, s["p"], s["o"], s["zg"], lw["w_out"], lw["w_pb"], lw["w_ab"],
                                              name=f"merge_bwd_l{l}")
        dw_out = _matmul_tn(s["mix"][None], dyb[None], name=f"dw_out_l{l}")[0]
        dw_pb = _matmul_tn(s["p"][None], da[None], name=f"dw_pb_l{l}")[0]
        dw_ab = _matmul_tn(s["o"][None], db[None], name=f"dw_ab_l{l}")[0]
        dzp, dpw, dsc = _pool_bwd(dp, s["d"], lw["pool_w"], lw["pool_scale"], name=f"pool_bwd_l{l}")
        dq, dk, dv, dsinks = _attn_bwd(do, s["qr"], s["kr"], s["vb"], lw["sinks"], name=f"attn_bwd_l{l}")
        dzqkv, dqg, dkg = _qk_bwd(dq, dk, dv, s["zq"], lw["qgain"], lw["kgain"], cos, sin, name=f"qk_bwd_l{l}")
        dw_in = jnp.concatenate([_matmul_tn(dzp[None], s["h2"][None], name=f"dw_in_pool_l{l}")[0],
                                 _matmul_tn(dzqkv[None], s["h2"][None], name=f"dw_in_qkv_l{l}")[0],
                                 _matmul_tn(dzg[None], s["h2"][None], name=f"dw_in_gate_l{l}")[0]], axis=0)
        dx1, dlnm = _mix_bwd_x(dx2, s["x1"], lw["ln_mix"], dzp, dzqkv, dzg, lw["w_in"], name=f"mix_bwd_x_l{l}")

        partial = dict(w_in=dw_in.reshape(N_DEV, -1, d), w_pool_branch=_to_shard_major_cols(dw_pb),
                       w_attn_branch=_to_shard_major_cols(dw_ab), w_out=dw_out.reshape(N_DEV, d // N_DEV, d),
                       w_ffn2_gu=dw_gu2, w_ffn2_down=dw_down2)
        early_in_flight[l], token = _exchange_start([partial[k] for k in big_early], scatter=True,
                                                    name=f"grads_early_start_l{l}")

        dyh, dgu = _ffn_bwd_act(dx1, s["gu1"], lw["down1"], name=f"ffn1_bwd_act_l{l}", after=token)
        dw_gu1, dw_down1 = ffn_weight_grads(dyh, dgu, s["h1"], s["act1"], "ffn1")
        late_in_flight[l], token = _exchange_start([dw_gu1, dw_down1], scatter=True, name=f"grads_late_start_l{l}")
        dy, dln1 = _ffn_bwd_x(dx1, s["x0"], lw["ln1"], dgu, lw["gu1"], name=f"ffn1_bwd_x_l{l}", after=token)
        small_grads["ln_ffn1"][l] = dln1[0]
        small_grads["ln_mix"][l] = dlnm[0]
        small_grads["ln_ffn2"][l] = dln2[0]
        small_grads["pool_w"][l] = dpw
        small_grads["pool_scale"][l] = dsc[0]
        small_grads["q_norm"][l] = dqg[0]
        small_grads["k_norm"][l] = dkg[0]
        small_grads["sinks"][l] = dsinks[0, :N_Q_HEADS]

    grad_x = dy.reshape(x.shape)

    small_w = [weights[k] for k in _SMALL]
    packed_g = _pack_small([jnp.stack(small_grads[k]).reshape(weights[k].shape) for k in _SMALL])
    small_in_flight, after = _exchange_start([packed_g], scatter=False, name="small_grads_start")
    for l in reversed(range(n_layers)):
        got = _exchange_wait(early_in_flight[l], after, name=f"grads_early_wait_l{l}")
        after = got[0]
        for k, r in zip(big_early, got):
            received[k][l] = r

    grads, deltas, new_m, new_v = {}, {}, {}, {}

    def adamw(k, after):
        w = weights[k]
        shape2 = (n_layers, -1, w.shape[-1])
        parts = [r.reshape(N_DEV, -1, w.shape[-1]) for r in received[k]]
        outs = _adamw_sharded(parts, w.reshape(shape2), mom_m[k].reshape(shape2), mom_v[k].reshape(shape2),
                              name=f"adamw_{k}", after=after)
        grads[k], deltas[k], new_m[k], new_v[k] = (o.reshape(w.shape) for o in outs)
        return outs[0]

    after = None
    for k in big_early:
        after = adamw(k, after)
    for l in reversed(range(n_layers)):
        got = _exchange_wait(late_in_flight[l], after, name=f"grads_late_wait_l{l}")
        after = got[0]
        for k, r in zip(big_late, got):
            received[k][l] = r
    after = None
    for k in big_late:
        after = adamw(k, after)

    (parts_small,) = _exchange_wait(small_in_flight, after, name="small_grads_wait")
    outs = _adamw_packed(parts_small, _pack_small(small_w), _pack_small([mom_m[k] for k in _SMALL]),
                         _pack_small([mom_v[k] for k in _SMALL]), name="adamw_small")
    for res, o in zip((grads, deltas, new_m, new_v), outs):
        for k, a in zip(_SMALL, _unpack_small(o, small_w)):
            res[k] = a

    for res in (grads, deltas, new_m, new_v):
        for k in transposed:
            res[k] = jnp.swapaxes(res[k], 1, 2)
    return (loss, grad_x, *[grads[k] for k in order], *[deltas[k] for k in order],
            *[new_m[k] for k in order], *[new_v[k] for k in order])
```

```python
import functools

import jax
import jax.numpy as jnp
from jax import lax
from jax.experimental import pallas as pl
from jax.experimental.pallas import tpu as pltpu

F32 = jnp.float32
BF16 = jnp.bfloat16

N_DEV = 8
MESH_AXES = ("x", "y", "c")
EPS = 1e-6

HEAD_DIM = 64
N_Q_HEADS = 8
N_KV_HEADS = 2
GQA_GROUP = N_Q_HEADS // N_KV_HEADS
ATTN_BLOCK = 128
ATTN_SCALE = HEAD_DIM ** -0.5
ROPE_THETA = 500000.0
ROT_DIM = 16
POOL_WINDOWS = (2, 4, 8, 16)
POOL_HALO = 16
GROUP_DIM = 128
POOL_DIM = 512
ATTN_DIM = 512
KV_DIM = 128
QKV_END = POOL_DIM + ATTN_DIM + 2 * KV_DIM

ADAM_LR = 0.001
ADAM_B1 = 0.9
ADAM_B2 = 0.999
ADAM_EPS = 1e-08
ADAM_WD = 0.01
ADAM_STEP = 10

ROW_TILE = 512
TN_ROW_TILE = 2048
FFN_CHUNK = 256
FFN_BWD_ROW_TILE = 256
VMEM_LIMIT_BYTES = 56 << 20
ADAMW_BLOCK_ELEMS = 96 * 1024
NEG_BIG = -1e30

_NT = (((1,), (1,)), ((), ()))
_TN = (((0,), (0,)), ((), ()))


def _cp(*sem):
    return pltpu.CompilerParams(dimension_semantics=sem, vmem_limit_bytes=VMEM_LIMIT_BYTES)


def _resident(block, index):
    return pl.BlockSpec(block, lambda *_: index, pipeline_mode=pl.Buffered(1))


def _row_tile(m):
    return min(ROW_TILE, m)


def _sds(shape, dtype):
    return jax.ShapeDtypeStruct(shape, dtype)


def _mesh_pos():
    return lax.axis_index("x"), lax.axis_index("y"), lax.axis_index("c")


def _all_gather_many(shards, name, after=None):
    n = len(shards)

    deps = [] if after is None else [after]

    def body(*refs):
        ins, outs = refs[:n], refs[n + len(deps):2 * n + len(deps)]
        send_sems, recv_sems, local_sems = refs[2 * n + len(deps):]
        x, y, c = _mesh_pos()
        me, sibling = (x, y, c), (x, y, 1 - c)
        chips = [(1 - x, y), (x, 1 - y), (1 - x, 1 - y)]

        def slot(a, pos):
            return outs[a].at[4 * pos[0] + 2 * pos[1] + pos[2]]

        def copy(a, k, block, to, src=None):
            return pltpu.make_async_remote_copy(
                src_ref=slot(a, block) if src is None else src, dst_ref=slot(a, block),
                send_sem=send_sems.at[a, k], recv_sem=recv_sems.at[a, k],
                device_id=to, device_id_type=pl.DeviceIdType.MESH)

        mine = [pltpu.make_async_copy(ins[a], slot(a, me), local_sems.at[a]) for a in range(n)]
        for cp in mine:
            cp.start()
        first = []
        for a in range(n):
            first.append(copy(a, 0, me, sibling, src=ins[a]))
            for j, chip in enumerate(chips):
                first.append(copy(a, 1 + j, me, (*chip, c), src=ins[a]))
        for cp in first:
            cp.start()
        passed = []
        for j, chip in enumerate(chips):
            for a in range(n):
                copy(a, 1 + j, (*chip, c), me).wait_recv()
                fwd = copy(a, 4 + j, (*chip, c), sibling)
                fwd.start()
                passed.append(fwd)
        for a in range(n):
            copy(a, 0, sibling, me).wait_recv()
        for j, chip in enumerate(chips):
            for a in range(n):
                copy(a, 4 + j, (*chip, 1 - c), me).wait_recv()
        for cp in first + passed:
            cp.wait_send()
        for cp in mine:
            cp.wait()

    any_spec = pl.BlockSpec(memory_space=pl.ANY)
    return pl.pallas_call(
        body, name=name,
        out_shape=[_sds((N_DEV,) + s.shape, s.dtype) for s in shards],
        in_specs=[any_spec] * (n + len(deps)), out_specs=[any_spec] * n,
        scratch_shapes=[pltpu.SemaphoreType.DMA((n, 7)), pltpu.SemaphoreType.DMA((n, 7)),
                        pltpu.SemaphoreType.DMA((n,))],
    )(*shards, *deps)


def _direct_copies(src, land, send_sem, recv_sem, local_sem, scatter):
    x, y, c = _mesh_pos()
    me = 4 * x + 2 * y + c
    local = pltpu.make_async_copy(src.at[me] if scatter else src, land.at[me], local_sem)
    remote = []
    for k in range(1, N_DEV):
        px = 1 - x if k & 4 else x
        py = 1 - y if k & 2 else y
        pc = 1 - c if k & 1 else c
        remote.append(pltpu.make_async_remote_copy(
            src_ref=src.at[4 * px + 2 * py + pc] if scatter else src, dst_ref=land.at[me],
            send_sem=send_sem, recv_sem=recv_sem, device_id=(px, py, pc), device_id_type=pl.DeviceIdType.MESH))
    seven = land.at[pl.ds(0, N_DEV - 1)]
    drain = pltpu.make_async_remote_copy(src_ref=seven, dst_ref=seven, send_sem=send_sem, recv_sem=recv_sem,
                                         device_id=(x, y, c), device_id_type=pl.DeviceIdType.MESH)
    return local, remote, drain


_HBM_SPEC = pl.BlockSpec(memory_space=pltpu.HBM)
_SEM_SPEC = pl.BlockSpec(memory_space=pltpu.SEMAPHORE)
_DATAFLOW = pltpu.SideEffectType.DATAFLOW_SIDE_EFFECTING
_SEMS_PER_ARRAY = 3


def _exchange_start(srcs, scatter, name):
    n = len(srcs)
    n_sems = _SEMS_PER_ARRAY * n
    land_shapes = [s.shape if scatter else (N_DEV,) + s.shape for s in srcs]

    def body(*refs):
        ins, lands, sems = refs[:n], refs[n:2 * n], refs[2 * n:2 * n + n_sems]
        for a in range(n):
            local, remote, _ = _direct_copies(ins[a], lands[a], *sems[3 * a:3 * a + 3], scatter)
            local.start()
            for cp in remote:
                cp.start()
        refs[-1][...] = jnp.zeros_like(refs[-1])

    outs = pl.pallas_call(
        body, name=name,
        out_shape=(*[pltpu.SemaphoreType.DMA(())] * n_sems,
                   *[pltpu.HBM(s.shape, s.dtype) for s in srcs],
                   *[pltpu.HBM(shape, s.dtype) for shape, s in zip(land_shapes, srcs)],
                   _sds((8, 128), F32)),
        in_specs=[_HBM_SPEC] * (2 * n),
        out_specs=(*[_SEM_SPEC] * n_sems, *[_HBM_SPEC] * (2 * n), pl.BlockSpec(memory_space=pltpu.VMEM)),
        input_output_aliases={i: n_sems + i for i in range(2 * n)},
        compiler_params=pltpu.CompilerParams(has_side_effects=_DATAFLOW),
    )(*[pltpu.with_memory_space_constraint(s, pltpu.HBM) for s in srcs],
      *[pltpu.with_memory_space_constraint(lax.empty(shape, s.dtype), pltpu.HBM) for shape, s in zip(land_shapes, srcs)])
    return (outs[:n_sems], outs[n_sems:n_sems + n], outs[n_sems + n:n_sems + 2 * n], scatter), outs[-1]


def _exchange_wait(state, after, name):
    sems, srcs, lands, scatter = state
    n = len(srcs)
    n_sems = len(sems)

    def body(*refs):
        ins, zones, ss = refs[:n], refs[n:2 * n], refs[2 * n:2 * n + n_sems]
        for a in range(n):
            local, _, drain = _direct_copies(ins[a], zones[a], *ss[3 * a:3 * a + 3], scatter)
            drain.wait_send()
            drain.wait_recv()
            local.wait()

    outs = pl.pallas_call(
        body, name=name,
        out_shape=(*[pltpu.HBM(s.shape, s.dtype) for s in srcs], *[pltpu.HBM(z.shape, z.dtype) for z in lands]),
        in_specs=[_HBM_SPEC] * (2 * n) + [_SEM_SPEC] * n_sems + [pl.BlockSpec(memory_space=pl.ANY)],
        out_specs=[_HBM_SPEC] * (2 * n),
        input_output_aliases={i: i for i in range(2 * n)},
        compiler_params=pltpu.CompilerParams(has_side_effects=_DATAFLOW),
    )(*srcs, *lands, *sems, after)
    return outs[n:]


def _rms_fwd(xv, gain):
    r = lax.rsqrt(jnp.mean(xv * xv, axis=-1, keepdims=True) + EPS)
    return xv * r * gain


def _rms_bwd(dh, xv, gain):
    r = lax.rsqrt(jnp.mean(xv * xv, axis=-1, keepdims=True) + EPS)
    xn = xv * r
    dxn = dh * gain
    dx = r * (dxn - xn * jnp.mean(dxn * xn, axis=-1, keepdims=True))
    return dx, dh * xn


def _silu_parts(g):
    s = 0.5 * jnp.tanh(0.5 * g) + 0.5
    return g * s, s * (1.0 + g * (1.0 - s))


def _segment_mean(v, width):
    r = lax.broadcasted_iota(jnp.int32, (width, width), 0) >> 6
    c = lax.broadcasted_iota(jnp.int32, (width, width), 1) >> 6
    bd = (r == c).astype(BF16)
    hi = v.astype(BF16)
    lo = (v - hi.astype(F32)).astype(BF16)
    total = jnp.dot(hi, bd, preferred_element_type=F32) + jnp.dot(lo, bd, preferred_element_type=F32)
    return total * (1.0 / HEAD_DIM)


def _rope_partner(v):
    width = v.shape[1]
    half = ROT_DIM // 2
    lane = lax.broadcasted_iota(jnp.int32, v.shape, 1) & (HEAD_DIM - 1)
    up = jnp.where(lane < ROT_DIM, pltpu.roll(v, half, 1), 0.0)
    return jnp.where(lane < half, pltpu.roll(v, width - half, 1), up)


def _tile_lanes(t, width):
    return t if width == t.shape[1] else jnp.tile(t, (1, width // t.shape[1]))


def _ffn_chunks(f):
    return [slice(j * FFN_CHUNK, (j + 1) * FFN_CHUNK) for j in range(f // FFN_CHUNK)]


def _ffn_up(x, ln, wgu, name, after=None):
    m, d = x.shape
    f = wgu.shape[1]
    tm = _row_tile(m)
    deps = [] if after is None else [after]

    def body(*refs):
        x_ref, ln_ref, w_ref = refs[:3]
        h_ref, gu_ref, a_ref = refs[-3:]
        h = _rms_fwd(x_ref[...], ln_ref[...]).astype(BF16)
        h_ref[...] = h
        for cols in _ffn_chunks(f):
            g = lax.dot_general(h, w_ref[0, cols, :], _NT, preferred_element_type=F32)
            u = lax.dot_general(h, w_ref[1, cols, :], _NT, preferred_element_type=F32)
            gu_ref[0, :, cols] = g.astype(BF16)
            gu_ref[1, :, cols] = u.astype(BF16)
            a_ref[:, cols] = (g * jax.nn.sigmoid(g) * u).astype(BF16)

    return pl.pallas_call(
        body, name=name, grid=(m // tm,),
        in_specs=[pl.BlockSpec((tm, d), lambda i: (i, 0)), _resident((1, d), (0, 0)),
                  _resident((2, f, d), (0, 0, 0))] + [pl.BlockSpec(memory_space=pl.ANY)] * len(deps),
        out_specs=[pl.BlockSpec((tm, d), lambda i: (i, 0)), pl.BlockSpec((2, tm, f), lambda i: (0, i, 0)),
                   pl.BlockSpec((tm, f), lambda i: (i, 0))],
        out_shape=[_sds((m, d), BF16), _sds((2, m, f), BF16), _sds((m, f), BF16)],
        compiler_params=_cp("parallel"),
    )(x, ln, wgu, *deps)


def _ffn_down(x, act, wd, name):
    m, d = x.shape
    f = act.shape[-1]
    tm = _row_tile(m)

    def body(x_ref, a_ref, w_ref, o_ref):
        o_ref[...] = x_ref[...] + 0.5 * jnp.dot(a_ref[...], w_ref[...], preferred_element_type=F32)

    return pl.pallas_call(
        body, name=name, grid=(m // tm,),
        in_specs=[pl.BlockSpec((tm, d), lambda i: (i, 0)), pl.BlockSpec((tm, f), lambda i: (i, 0)),
                  _resident((f, d), (0, 0))],
        out_specs=pl.BlockSpec((tm, d), lambda i: (i, 0)),
        out_shape=_sds((m, d), F32),
        compiler_params=_cp("parallel"),
    )(x, act, wd)


def _ffn_bwd(dy, x, ln, gu, wd, wgu, name, after=None):
    m, d = dy.shape
    f = gu.shape[-1]
    tm = min(FFN_BWD_ROW_TILE, m)
    deps = [] if after is None else [after]

    def body(*refs):
        dy_ref, x_ref, ln_ref, gu_ref, wd_ref, wgu_ref = refs[:6]
        dyh_ref, dgu_ref, dx_ref, dln_ref = refs[-4:]

        @pl.when(pl.program_id(0) == 0)
        def _():
            dln_ref[...] = jnp.zeros_like(dln_ref)

        dyh = (0.5 * dy_ref[...]).astype(BF16)
        dyh_ref[...] = dyh
        dh = jnp.zeros((tm, d), F32)
        for cols in _ffn_chunks(f):
            da = lax.dot_general(dyh, wd_ref[cols, :], _NT, preferred_element_type=F32)
            g = gu_ref[0, :, cols].astype(F32)
            u = gu_ref[1, :, cols].astype(F32)
            silu, dsilu = _silu_parts(g)
            dg = (da * u * dsilu).astype(BF16)
            du = (da * silu).astype(BF16)
            dgu_ref[0, :, cols] = dg
            dgu_ref[1, :, cols] = du
            dh += jnp.dot(dg, wgu_ref[0, cols, :], preferred_element_type=F32)
            dh += jnp.dot(du, wgu_ref[1, cols, :], preferred_element_type=F32)
        dx, dgain = _rms_bwd(dh, x_ref[...], ln_ref[...])
        dx_ref[...] = dy_ref[...] + dx
        dln_ref[...] += jnp.sum(dgain, axis=0, keepdims=True)

    row = lambda i: (i, 0)
    return pl.pallas_call(
        body, name=name, grid=(m // tm,),
        in_specs=[pl.BlockSpec((tm, d), row), pl.BlockSpec((tm, d), row), _resident((1, d), (0, 0)),
                  pl.BlockSpec((2, tm, f), lambda i: (0, i, 0)), _resident((f, d), (0, 0)),
                  _resident((2, f, d), (0, 0, 0))] + [pl.BlockSpec(memory_space=pl.ANY)] * len(deps),
        out_specs=[pl.BlockSpec((tm, d), row), pl.BlockSpec((2, tm, f), lambda i: (0, i, 0)),
                   pl.BlockSpec((tm, d), row), pl.BlockSpec((1, d), lambda i: (0, 0))],
        out_shape=[_sds((m, d), BF16), _sds((2, m, f), BF16), _sds((m, d), F32), _sds((1, d), F32)],
        compiler_params=_cp("arbitrary"),
    )(dy, x, ln, gu, wd, wgu, *deps)


def _matmul_tn(a, b, name, a_chunk=None, out_dtype=BF16):
    ja, m, k = a.shape
    jb, _, n = b.shape
    nj = max(ja, jb)
    kc = k if a_chunk is None else a_chunk
    tm = min(TN_ROW_TILE, m)
    nm = m // tm

    def body(a_ref, b_ref, o_ref, acc):
        step = pl.program_id(2)

        @pl.when(step == 0)
        def _():
            acc[...] = jnp.zeros_like(acc)

        acc[...] += lax.dot_general(a_ref[...], b_ref[...], _TN, preferred_element_type=F32)

        @pl.when(step == nm - 1)
        def _():
            o_ref[...] = acc[...].astype(o_ref.dtype)

    return pl.pallas_call(
        body, name=name, grid=(nj, k // kc, nm),
        in_specs=[pl.BlockSpec((None, tm, kc), (lambda j, c, s: (j, s, c)) if ja > 1 else (lambda j, c, s: (0, s, c))),
                  pl.BlockSpec((None, tm, n), (lambda j, c, s: (j, s, 0)) if jb > 1 else (lambda j, c, s: (0, s, 0)))],
        out_specs=pl.BlockSpec((None, kc, n), lambda j, c, s: (j, c, 0)),
        out_shape=_sds((nj, k, n), out_dtype),
        scratch_shapes=[pltpu.VMEM((kc, n), F32)],
        compiler_params=_cp("parallel", "parallel", "arbitrary"),
    )(a, b)


def _mix_in(x, ln, w_in, name):
    m, d = x.shape
    n_in = w_in.shape[0]
    tm = _row_tile(m)

    def body(x_ref, ln_ref, w_ref, h_ref, zq_ref, zg_ref):
        h = _rms_fwd(x_ref[...], ln_ref[...]).astype(BF16)
        h_ref[...] = h
        zq_ref[...] = lax.dot_general(h, w_ref[:QKV_END, :], _NT, preferred_element_type=F32)
        zg_ref[...] = lax.dot_general(h, w_ref[QKV_END:, :], _NT, preferred_element_type=F32)

    return pl.pallas_call(
        body, name=name, grid=(m // tm,),
        in_specs=[pl.BlockSpec((tm, d), lambda i: (i, 0)), _resident((1, d), (0, 0)), _resident((n_in, d), (0, 0))],
        out_specs=[pl.BlockSpec((tm, d), lambda i: (i, 0)), pl.BlockSpec((tm, QKV_END), lambda i: (i, 0)),
                   pl.BlockSpec((tm, n_in - QKV_END), lambda i: (i, 0))],
        out_shape=[_sds((m, d), BF16), _sds((m, QKV_END), F32), _sds((m, n_in - QKV_END), F32)],
        compiler_params=_cp("parallel"),
    )(x, ln, w_in)


def _pool_fwd(zq, pool_w, scale, name):
    m = zq.shape[0]
    tm = _row_tile(m)
    halo_blocks = tm // POOL_HALO

    def body(zc_ref, zh_ref, pw_ref, sc_ref, d_ref, p_ref):
        i = pl.program_id(0)
        halo = jnp.where(i > 0, zh_ref[...], 0.0)
        ext = jnp.concatenate([halo, zc_ref[...]], axis=0)
        t = i * tm + lax.broadcasted_iota(jnp.int32, (tm, 1), 0)
        for g, w in enumerate(POOL_WINDOWS):
            lanes = slice(g * GROUP_DIM, (g + 1) * GROUP_DIM)
            e = ext[:, lanes]
            s, k = e, 1
            while k < w:
                s = s + pltpu.roll(s, k, 0)
                k *= 2
            cnt = jnp.minimum(t + 1, w).astype(F32)
            dg = (s[POOL_HALO:, :] / cnt - e[POOL_HALO:, :]).astype(BF16)
            y = jnp.dot(dg, pw_ref[g].astype(BF16), preferred_element_type=F32)
            d_ref[:, lanes] = dg
            p_ref[:, lanes] = (y * sc_ref[:, lanes]).astype(BF16)

    return pl.pallas_call(
        body, name=name, grid=(m // tm,),
        in_specs=[pl.BlockSpec((tm, POOL_DIM), lambda i: (i, 0)),
                  pl.BlockSpec((POOL_HALO, POOL_DIM), lambda i: (jnp.maximum(i * halo_blocks - 1, 0), 0)),
                  _resident((4, GROUP_DIM, GROUP_DIM), (0, 0, 0)), _resident((1, POOL_DIM), (0, 0))],
        out_specs=[pl.BlockSpec((tm, POOL_DIM), lambda i: (i, 0)), pl.BlockSpec((tm, POOL_DIM), lambda i: (i, 0))],
        out_shape=[_sds((m, POOL_DIM), BF16), _sds((m, POOL_DIM), BF16)],
        compiler_params=_cp("parallel"),
    )(zq, zq, pool_w, scale)


def _pool_bwd(dp, d, pool_w, scale, name):
    m = dp.shape[0]
    tm = _row_tile(m)
    nb = m // tm
    halo_blocks = tm // POOL_HALO
    rows = tm + POOL_HALO

    def body(dpc_ref, dph_ref, d_ref, pw_ref, sc_ref, du_ref, dpw_ref, dsc_ref):
        i = pl.program_id(0)

        @pl.when(i == 0)
        def _():
            dpw_ref[...] = jnp.zeros_like(dpw_ref)
            dsc_ref[...] = jnp.zeros_like(dsc_ref)

        halo = jnp.where(i < nb - 1, dph_ref[...], 0.0)
        dpc = dpc_ref[...]
        ext = jnp.concatenate([dpc, halo], axis=0)
        t = i * tm + lax.broadcasted_iota(jnp.int32, (rows, 1), 0)
        for g, w in enumerate(POOL_WINDOWS):
            lanes = slice(g * GROUP_DIM, (g + 1) * GROUP_DIM)
            pwb = pw_ref[g].astype(BF16)
            dyb = (ext[:, lanes] * sc_ref[:, lanes]).astype(BF16)
            dd = lax.dot_general(dyb, pwb, _NT, preferred_element_type=F32)
            cnt = jnp.minimum(t + 1, w).astype(F32)
            s, k = dd / cnt, 1
            while k < w:
                s = s + pltpu.roll(s, rows - k, 0)
                k *= 2
            du_ref[:, lanes] = (s[:tm, :] - dd[:tm, :]).astype(BF16)
            dcur = d_ref[:, lanes]
            y = jnp.dot(dcur, pwb, preferred_element_type=F32)
            dsc_ref[:, lanes] += jnp.sum(dpc[:, lanes] * y, axis=0, keepdims=True)
            dpw_ref[g] += lax.dot_general(dcur, dyb[:tm, :], _TN, preferred_element_type=F32)

    return pl.pallas_call(
        body, name=name, grid=(nb,),
        in_specs=[pl.BlockSpec((tm, POOL_DIM), lambda i: (i, 0)),
                  pl.BlockSpec((POOL_HALO, POOL_DIM), lambda i: (jnp.minimum((i + 1) * halo_blocks, nb * halo_blocks - 1), 0)),
                  pl.BlockSpec((tm, POOL_DIM), lambda i: (i, 0)),
                  _resident((4, GROUP_DIM, GROUP_DIM), (0, 0, 0)), _resident((1, POOL_DIM), (0, 0))],
        out_specs=[pl.BlockSpec((tm, POOL_DIM), lambda i: (i, 0)),
                   pl.BlockSpec((4, GROUP_DIM, GROUP_DIM), lambda i: (0, 0, 0)),
                   pl.BlockSpec((1, POOL_DIM), lambda i: (0, 0))],
        out_shape=[_sds((m, POOL_DIM), BF16), _sds((4, GROUP_DIM, GROUP_DIM), F32), _sds((1, POOL_DIM), F32)],
        compiler_params=_cp("arbitrary"),
    )(dp, dp, d, pool_w, scale)


def _qk_norm_rope(xv, gain, cos, sin):
    width = xv.shape[1]
    r = lax.rsqrt(_segment_mean(xv * xv, width) + EPS)
    y = xv * r * gain
    return y * _tile_lanes(cos, width) + _rope_partner(y) * _tile_lanes(sin, width)


def _qk_prep(zq, qgain, kgain, cos, sin, name):
    m = zq.shape[0]
    tm = _row_tile(m)

    def body(q_ref, kv_ref, qg_ref, kg_ref, cos_ref, sin_ref, qr_ref, kr_ref, v_ref):
        cos_v, sin_v = cos_ref[...], sin_ref[...]
        qr_ref[...] = (_qk_norm_rope(q_ref[...], qg_ref[...], cos_v, sin_v) * ATTN_SCALE).astype(BF16)
        kv = kv_ref[...]
        kr_ref[...] = _qk_norm_rope(kv[:, :KV_DIM], kg_ref[...], cos_v, sin_v).astype(BF16)
        v_ref[...] = kv[:, KV_DIM:].astype(BF16)

    return pl.pallas_call(
        body, name=name, grid=(m // tm,),
        in_specs=[pl.BlockSpec((tm, ATTN_DIM), lambda i: (i, 1)), pl.BlockSpec((tm, 2 * KV_DIM), lambda i: (i, 4)),
                  _resident((1, ATTN_DIM), (0, 0)), _resident((1, KV_DIM), (0, 0)),
                  pl.BlockSpec((tm, 128), lambda i: (i, 0)), pl.BlockSpec((tm, 128), lambda i: (i, 0))],
        out_specs=[pl.BlockSpec((tm, ATTN_DIM), lambda i: (i, 0)), pl.BlockSpec((tm, KV_DIM), lambda i: (i, 0)),
                   pl.BlockSpec((tm, KV_DIM), lambda i: (i, 0))],
        out_shape=[_sds((m, ATTN_DIM), BF16), _sds((m, KV_DIM), BF16), _sds((m, KV_DIM), BF16)],
        compiler_params=_cp("parallel"),
    )(zq, zq, qgain, kgain, cos, sin)


GROUP_ROWS = GQA_GROUP * ATTN_BLOCK


def _band_bias():
    qi = jnp.arange(GROUP_ROWS)[:, None] % ATTN_BLOCK
    ki = jnp.arange(2 * ATTN_BLOCK)[None, :]
    diff = qi + ATTN_BLOCK - ki
    band = (diff >= 0) & (diff < ATTN_BLOCK)
    first = band & (ki >= ATTN_BLOCK)
    return jnp.where(jnp.stack([first, band]), 0.0, NEG_BIG).astype(F32)


def _stack_group(v, kvh):
    heads = range(kvh * GQA_GROUP, (kvh + 1) * GQA_GROUP)
    return jnp.concatenate([v[:, h * HEAD_DIM:(h + 1) * HEAD_DIM] for h in heads], axis=0)


def _unstack_groups(groups):
    return jnp.concatenate([grp[g * ATTN_BLOCK:(g + 1) * ATTN_BLOCK, :] for grp in groups for g in range(GQA_GROUP)],
                           axis=1)


def _group_sinks(sk_ref, kvh):
    row_head = lax.broadcasted_iota(jnp.int32, (GROUP_ROWS, 1), 0) >> 7
    col = jnp.full((GROUP_ROWS, 1), sk_ref[kvh * GQA_GROUP], F32)
    for g in range(1, GQA_GROUP):
        col = jnp.where(row_head == g, sk_ref[kvh * GQA_GROUP + g], col)
    return col


def _head_probs(qh, kh, bias, sink):
    s = lax.dot_general(qh, kh, _NT, preferred_element_type=F32) + bias
    mx = jnp.maximum(jnp.max(s, axis=-1, keepdims=True), sink)
    p = jnp.exp(s - mx)
    es = jnp.exp(sink - mx)
    inv = 1.0 / (jnp.sum(p, axis=-1, keepdims=True) + es)
    return p * inv, es * inv


def _attn_fwd(qr, kr, vb, sinks, name):
    m = qr.shape[0]
    nb = m // ATTN_BLOCK

    def body(q_ref, kp_ref, kc_ref, vp_ref, vc_ref, sk_ref, bias_ref, o_ref):
        bias = bias_ref[jnp.minimum(pl.program_id(0), 1)]
        qv = q_ref[...]
        kk = jnp.concatenate([kp_ref[...], kc_ref[...]], axis=0)
        vv = jnp.concatenate([vp_ref[...], vc_ref[...]], axis=0)
        outs = []
        for kvh in range(N_KV_HEADS):
            kv_lanes = slice(kvh * HEAD_DIM, (kvh + 1) * HEAD_DIM)
            p, _ = _head_probs(_stack_group(qv, kvh), kk[:, kv_lanes], bias, _group_sinks(sk_ref, kvh))
            outs.append(jnp.dot(p.astype(BF16), vv[:, kv_lanes], preferred_element_type=F32))
        o_ref[...] = _unstack_groups(outs).astype(BF16)

    prev = lambda n: (jnp.maximum(n - 1, 0), 0)
    cur = lambda n: (n, 0)
    return pl.pallas_call(
        body, name=name, grid=(nb,),
        in_specs=[pl.BlockSpec((ATTN_BLOCK, ATTN_DIM), cur),
                  pl.BlockSpec((ATTN_BLOCK, KV_DIM), prev), pl.BlockSpec((ATTN_BLOCK, KV_DIM), cur),
                  pl.BlockSpec((ATTN_BLOCK, KV_DIM), prev), pl.BlockSpec((ATTN_BLOCK, KV_DIM), cur),
                  pl.BlockSpec(memory_space=pltpu.SMEM), _resident((2, GROUP_ROWS, 2 * ATTN_BLOCK), (0, 0, 0))],
        out_specs=pl.BlockSpec((ATTN_BLOCK, ATTN_DIM), cur),
        out_shape=_sds((m, ATTN_DIM), BF16),
        compiler_params=_cp("parallel"),
    )(qr, kr, kr, vb, vb, sinks, _band_bias())


def _attn_bwd(do, qr, kr, vb, sinks, name):
    m = qr.shape[0]
    nb = m // ATTN_BLOCK

    def body(do_ref, q_ref, kp_ref, kc_ref, vp_ref, vc_ref, sk_ref, bias_ref, dq_ref, dk_ref, dv_ref, ds_ref,
             carry_k, carry_v):
        n = pl.program_id(0)

        @pl.when(n == 0)
        def _():
            carry_k[...] = jnp.zeros_like(carry_k)
            carry_v[...] = jnp.zeros_like(carry_v)
            ds_ref[...] = jnp.zeros_like(ds_ref)

        @pl.when(n < nb)
        def _():
            bias = bias_ref[jnp.minimum(n, 1)]
            qv = q_ref[...]
            dov = do_ref[...]
            kk = jnp.concatenate([kp_ref[...], kc_ref[...]], axis=0)
            vv = jnp.concatenate([vp_ref[...], vc_ref[...]], axis=0)
            lane = lax.broadcasted_iota(jnp.int32, (1, 128), 1)
            dsink = jnp.zeros((1, 128), F32)
            dqs, dks, dvs = [], [], []
            for kvh in range(N_KV_HEADS):
                kv_lanes = slice(kvh * HEAD_DIM, (kvh + 1) * HEAD_DIM)
                kh, vh = kk[:, kv_lanes], vv[:, kv_lanes]
                qg = _stack_group(qv, kvh)
                dog = _stack_group(dov, kvh).astype(BF16)
                p, ps = _head_probs(qg, kh, bias, _group_sinks(sk_ref, kvh))
                dpr = lax.dot_general(dog, vh, _NT, preferred_element_type=F32)
                delta = jnp.sum(p * dpr, axis=-1, keepdims=True)
                dsb = (p * (dpr - delta)).astype(BF16)
                sink_term = ps * delta
                for g in range(GQA_GROUP):
                    rows = slice(g * ATTN_BLOCK, (g + 1) * ATTN_BLOCK)
                    dsink = dsink + jnp.where(lane == kvh * GQA_GROUP + g, -jnp.sum(sink_term[rows, :]), 0.0)
                dqs.append(jnp.dot(dsb, kh, preferred_element_type=F32) * ATTN_SCALE)
                dks.append(lax.dot_general(dsb, qg, _TN, preferred_element_type=F32))
                dvs.append(lax.dot_general(p.astype(BF16), dog, _TN, preferred_element_type=F32))
            dq_ref[...] = _unstack_groups(dqs)
            dkk = jnp.concatenate(dks, axis=1)
            dvv = jnp.concatenate(dvs, axis=1)
            dk_ref[...] = carry_k[...] + dkk[:ATTN_BLOCK, :]
            dv_ref[...] = carry_v[...] + dvv[:ATTN_BLOCK, :]
            carry_k[...] = dkk[ATTN_BLOCK:, :]
            carry_v[...] = dvv[ATTN_BLOCK:, :]
            ds_ref[...] += dsink

        @pl.when(n == nb)
        def _():
            dk_ref[...] = carry_k[...]
            dv_ref[...] = carry_v[...]

    cur = lambda n: (jnp.minimum(n, nb - 1), 0)
    prev = lambda n: (jnp.clip(n - 1, 0, nb - 1), 0)
    return pl.pallas_call(
        body, name=name, grid=(nb + 1,),
        in_specs=[pl.BlockSpec((ATTN_BLOCK, ATTN_DIM), cur), pl.BlockSpec((ATTN_BLOCK, ATTN_DIM), cur),
                  pl.BlockSpec((ATTN_BLOCK, KV_DIM), prev), pl.BlockSpec((ATTN_BLOCK, KV_DIM), cur),
                  pl.BlockSpec((ATTN_BLOCK, KV_DIM), prev), pl.BlockSpec((ATTN_BLOCK, KV_DIM), cur),
                  pl.BlockSpec(memory_space=pltpu.SMEM), _resident((2, GROUP_ROWS, 2 * ATTN_BLOCK), (0, 0, 0))],
        out_specs=[pl.BlockSpec((ATTN_BLOCK, ATTN_DIM), cur), pl.BlockSpec((ATTN_BLOCK, KV_DIM), prev),
                   pl.BlockSpec((ATTN_BLOCK, KV_DIM), prev), pl.BlockSpec((1, 128), lambda n: (0, 0))],
        out_shape=[_sds((m, ATTN_DIM), F32), _sds((m, KV_DIM), F32), _sds((m, KV_DIM), F32), _sds((1, 128), F32)],
        scratch_shapes=[pltpu.VMEM((ATTN_BLOCK, KV_DIM), F32), pltpu.VMEM((ATTN_BLOCK, KV_DIM), F32)],
        compiler_params=_cp("arbitrary"),
    )(do, qr, kr, kr, vb, vb, sinks, _band_bias())


def _qk_norm_rope_bwd(dout, xv, gain, cos, sin):
    width = xv.shape[1]
    r = lax.rsqrt(_segment_mean(xv * xv, width) + EPS)
    xn = xv * r
    dy = dout * _tile_lanes(cos, width) + _rope_partner(dout * _tile_lanes(sin, width))
    dxn = dy * gain
    dx = r * (dxn - xn * _segment_mean(dxn * xn, width))
    return dx, jnp.sum(dy * xn, axis=0, keepdims=True)


def _fold_heads(v):
    out = v[:, :HEAD_DIM]
    for h in range(1, v.shape[1] // HEAD_DIM):
        out = out + v[:, h * HEAD_DIM:(h + 1) * HEAD_DIM]
    return out


def _qk_bwd(dq, dk, dv, zq, qgain, kgain, cos, sin, name):
    m = zq.shape[0]
    tm = _row_tile(m)

    def body(dq_ref, dk_ref, dv_ref, q_ref, kv_ref, qg_ref, kg_ref, cos_ref, sin_ref, dz_ref, dqg_ref, dkg_ref):
        @pl.when(pl.program_id(0) == 0)
        def _():
            dqg_ref[...] = jnp.zeros_like(dqg_ref)
            dkg_ref[...] = jnp.zeros_like(dkg_ref)

        cos_v, sin_v = cos_ref[...], sin_ref[...]
        dxq, dgq = _qk_norm_rope_bwd(dq_ref[...], q_ref[...], qg_ref[...], cos_v, sin_v)
        dxk, dgk = _qk_norm_rope_bwd(dk_ref[...], kv_ref[:, :KV_DIM], kg_ref[...], cos_v, sin_v)
        dz_ref[:, :ATTN_DIM] = dxq.astype(BF16)
        dz_ref[:, ATTN_DIM:ATTN_DIM + KV_DIM] = dxk.astype(BF16)
        dz_ref[:, ATTN_DIM + KV_DIM:] = dv_ref[...].astype(BF16)
        dqg_ref[...] += _fold_heads(dgq)
        dkg_ref[...] += _fold_heads(dgk)

    row = lambda i: (i, 0)
    return pl.pallas_call(
        body, name=name, grid=(m // tm,),
        in_specs=[pl.BlockSpec((tm, ATTN_DIM), row), pl.BlockSpec((tm, KV_DIM), row), pl.BlockSpec((tm, KV_DIM), row),
                  pl.BlockSpec((tm, ATTN_DIM), lambda i: (i, 1)), pl.BlockSpec((tm, 2 * KV_DIM), lambda i: (i, 4)),
                  _resident((1, ATTN_DIM), (0, 0)), _resident((1, KV_DIM), (0, 0)),
                  pl.BlockSpec((tm, 128), row), pl.BlockSpec((tm, 128), row)],
        out_specs=[pl.BlockSpec((tm, ATTN_DIM + 2 * KV_DIM), row), pl.BlockSpec((1, HEAD_DIM), lambda i: (0, 0)),
                   pl.BlockSpec((1, HEAD_DIM), lambda i: (0, 0))],
        out_shape=[_sds((m, ATTN_DIM + 2 * KV_DIM), BF16), _sds((1, HEAD_DIM), F32), _sds((1, HEAD_DIM), F32)],
        compiler_params=_cp("arbitrary"),
    )(dq, dk, dv, zq, zq, qgain, kgain, cos, sin)


def _merge_fwd(x, p, o, zg, w_pb, w_ab, w_out, name):
    m, d = x.shape
    tm = _row_tile(m)

    def body(x_ref, p_ref, o_ref, zg_ref, wpb_ref, wab_ref, wo_ref, xo_ref, mix_ref):
        a = jnp.dot(p_ref[...], wpb_ref[...], preferred_element_type=F32)
        b = jnp.dot(o_ref[...], wab_ref[...], preferred_element_type=F32)
        mix = (jax.nn.sigmoid(zg_ref[:, :d]) * a + jax.nn.sigmoid(zg_ref[:, d:]) * b).astype(BF16)
        mix_ref[...] = mix
        xo_ref[...] = x_ref[...] + jnp.dot(mix, wo_ref[...], preferred_element_type=F32)

    row = lambda i: (i, 0)
    return pl.pallas_call(
        body, name=name, grid=(m // tm,),
        in_specs=[pl.BlockSpec((tm, d), row), pl.BlockSpec((tm, POOL_DIM), row), pl.BlockSpec((tm, ATTN_DIM), row),
                  pl.BlockSpec((tm, 2 * d), row), _resident((POOL_DIM, d), (0, 0)), _resident((ATTN_DIM, d), (0, 0)),
                  _resident((d, d), (0, 0))],
        out_specs=[pl.BlockSpec((tm, d), row)] * 2,
        out_shape=[_sds((m, d), F32), _sds((m, d), BF16)],
        compiler_params=_cp("parallel"),
    )(x, p, o, zg, w_pb, w_ab, w_out)


def _merge_bwd(dy, p, o, zg, w_out, w_pb, w_ab, name):
    m, d = dy.shape
    tm = _row_tile(m)

    def body(dy_ref, p_ref, o_ref, zg_ref, wo_ref, wpb_ref, wab_ref, dyb_ref, da_ref, db_ref, dp_ref, do_ref, dzg_ref):
        dyb = dy_ref[...].astype(BF16)
        dyb_ref[...] = dyb
        dmix = lax.dot_general(dyb, wo_ref[...], _NT, preferred_element_type=F32)
        gp = jax.nn.sigmoid(zg_ref[:, :d])
        ga = jax.nn.sigmoid(zg_ref[:, d:])
        da = (dmix * gp).astype(BF16)
        db = (dmix * ga).astype(BF16)
        da_ref[...] = da
        db_ref[...] = db
        a = jnp.dot(p_ref[...], wpb_ref[...], preferred_element_type=F32)
        b = jnp.dot(o_ref[...], wab_ref[...], preferred_element_type=F32)
        dzg_ref[:, :d] = (dmix * a * gp * (1.0 - gp)).astype(BF16)
        dzg_ref[:, d:] = (dmix * b * ga * (1.0 - ga)).astype(BF16)
        dp_ref[...] = lax.dot_general(da, wpb_ref[...], _NT, preferred_element_type=F32)
        do_ref[...] = lax.dot_general(db, wab_ref[...], _NT, preferred_element_type=F32)

    row = lambda i: (i, 0)
    return pl.pallas_call(
        body, name=name, grid=(m // tm,),
        in_specs=[pl.BlockSpec((tm, d), row), pl.BlockSpec((tm, POOL_DIM), row), pl.BlockSpec((tm, ATTN_DIM), row),
                  pl.BlockSpec((tm, 2 * d), row), _resident((d, d), (0, 0)), _resident((POOL_DIM, d), (0, 0)),
                  _resident((ATTN_DIM, d), (0, 0))],
        out_specs=[pl.BlockSpec((tm, d), row), pl.BlockSpec((tm, d), row), pl.BlockSpec((tm, d), row),
                   pl.BlockSpec((tm, POOL_DIM), row), pl.BlockSpec((tm, ATTN_DIM), row), pl.BlockSpec((tm, 2 * d), row)],
        out_shape=[_sds((m, d), BF16), _sds((m, d), BF16), _sds((m, d), BF16), _sds((m, POOL_DIM), F32),
                   _sds((m, ATTN_DIM), F32), _sds((m, 2 * d), BF16)],
        compiler_params=_cp("parallel"),
    )(dy, p, o, zg, w_out, w_pb, w_ab)


def _mix_bwd_x(dy, x, ln, dzp, dzqkv, dzg, w_in, name):
    m, d = dy.shape
    n_in = w_in.shape[0]
    tm = _row_tile(m)

    def body(dy_ref, x_ref, ln_ref, dzp_ref, dzq_ref, dzg_ref, w_ref, dx_ref, dln_ref):
        @pl.when(pl.program_id(0) == 0)
        def _():
            dln_ref[...] = jnp.zeros_like(dln_ref)

        dh = jnp.dot(dzp_ref[...], w_ref[:POOL_DIM, :], preferred_element_type=F32)
        dh += jnp.dot(dzq_ref[...], w_ref[POOL_DIM:QKV_END, :], preferred_element_type=F32)
        dh += jnp.dot(dzg_ref[...], w_ref[QKV_END:, :], preferred_element_type=F32)
        dx, dgain = _rms_bwd(dh, x_ref[...], ln_ref[...])
        dx_ref[...] = dy_ref[...] + dx
        dln_ref[...] += jnp.sum(dgain, axis=0, keepdims=True)

    row = lambda i: (i, 0)
    return pl.pallas_call(
        body, name=name, grid=(m // tm,),
        in_specs=[pl.BlockSpec((tm, d), row), pl.BlockSpec((tm, d), row), _resident((1, d), (0, 0)),
                  pl.BlockSpec((tm, POOL_DIM), row), pl.BlockSpec((tm, QKV_END - POOL_DIM), row),
                  pl.BlockSpec((tm, n_in - QKV_END), row), _resident((n_in, d), (0, 0))],
        out_specs=[pl.BlockSpec((tm, d), row), pl.BlockSpec((1, d), lambda i: (0, 0))],
        out_shape=[_sds((m, d), F32), _sds((1, d), F32)],
        compiler_params=_cp("arbitrary"),
    )(dy, x, ln, dzp, dzqkv, dzg, w_in)


def _loss_head(y, target, name):
    m, d = y.shape
    tm = _row_tile(m)

    def body(y_ref, t_ref, loss_ref, dy_ref):
        @pl.when(pl.program_id(0) == 0)
        def _():
            loss_ref[...] = jnp.zeros_like(loss_ref)

        diff = y_ref[...] - t_ref[...]
        dy_ref[...] = diff * (1.0 / d)
        loss_ref[...] += 0.5 * jnp.sum(jnp.mean(diff * diff, axis=-1, keepdims=True), axis=0, keepdims=True)

    row = lambda i: (i, 0)
    return pl.pallas_call(
        body, name=name, grid=(m // tm,),
        in_specs=[pl.BlockSpec((tm, d), row), pl.BlockSpec((tm, d), row)],
        out_specs=[pl.BlockSpec((1, 1), lambda i: (0, 0)), pl.BlockSpec((tm, d), row)],
        out_shape=[_sds((1, 1), F32), _sds((m, d), F32)],
        compiler_params=_cp("arbitrary"),
    )(y, target)


def _adamw_math(g, w, m, v):
    m2 = ADAM_B1 * m + (1.0 - ADAM_B1) * g
    v2 = ADAM_B2 * v + (1.0 - ADAM_B2) * (g * g)
    m_hat = m2 / (1.0 - ADAM_B1 ** ADAM_STEP)
    v_hat = v2 / (1.0 - ADAM_B2 ** ADAM_STEP)
    delta = -ADAM_LR * (m_hat / (jnp.sqrt(v_hat) + ADAM_EPS) + ADAM_WD * w)
    return delta, m2, v2


def _sum_parts(parts_ref):
    g = parts_ref[0].astype(F32)
    for s in range(1, N_DEV):
        g = g + parts_ref[s].astype(F32)
    return g


def _adamw_sharded(parts, w, m, v, name, after=None):
    n_layers, rows, cols = w.shape
    tr = next(t for t in (512, 256, 128, 64, 32, 16, 8) if rows % t == 0 and t * cols <= ADAMW_BLOCK_ELEMS)
    nr = rows // tr
    deps = [] if after is None else [after]

    def body(*refs):
        part_refs = refs[:n_layers]
        w_ref, m_ref, v_ref = refs[n_layers:n_layers + 3]
        g_out, d_out, m_out, v_out = refs[-4:]
        layer = pl.program_id(0)
        for l in range(n_layers):
            @pl.when(layer == l)
            def _(l=l):
                g = _sum_parts(part_refs[l])
                delta, m2, v2 = _adamw_math(g, w_ref[...], m_ref[...], v_ref[...])
                g_out[...] = g
                d_out[...] = delta
                m_out[...] = m2
                v_out[...] = v2

    def part_map(l):
        return lambda layer, r: (0, jnp.where(layer == l, r, jnp.where(layer < l, 0, nr - 1)), 0)

    wspec = pl.BlockSpec((None, tr, cols), lambda layer, r: (layer, r, 0))
    return pl.pallas_call(
        body, name=name, grid=(n_layers, nr),
        in_specs=([pl.BlockSpec((N_DEV, tr, cols), part_map(l)) for l in range(n_layers)] + [wspec] * 3
                  + [pl.BlockSpec(memory_space=pl.ANY)] * len(deps)),
        out_specs=[wspec] * 4,
        out_shape=[_sds(w.shape, F32)] * 4,
        compiler_params=_cp("arbitrary", "arbitrary"),
    )(*parts, w, m, v, *deps)


def _adamw_packed(parts, w, m, v, name):
    def body(p_ref, w_ref, m_ref, v_ref, g_out, d_out, m_out, v_out):
        g = _sum_parts(p_ref)
        delta, m2, v2 = _adamw_math(g, w_ref[...], m_ref[...], v_ref[...])
        g_out[...] = g
        d_out[...] = delta
        m_out[...] = m2
        v_out[...] = v2

    return pl.pallas_call(
        body, name=name, out_shape=[_sds(w.shape, F32)] * 4,
        compiler_params=pltpu.CompilerParams(vmem_limit_bytes=VMEM_LIMIT_BYTES),
    )(parts, w, m, v)


_SMALL = ("ln_ffn1", "ln_mix", "pool_w", "pool_scale", "q_norm", "k_norm", "sinks", "ln_ffn2")


def _pack_small(arrs):
    rows = []
    for a in arrs:
        flat = a.reshape(-1)
        pad = (-flat.shape[0]) % 1024
        rows.append(jnp.pad(flat, (0, pad)).reshape(-1, 128))
    return jnp.concatenate(rows, axis=0)


def _unpack_small(packed, like):
    out, r0 = [], 0
    for a in like:
        size = a.size
        nrows = (size + 1023) // 1024 * 8
        out.append(packed[r0:r0 + nrows].reshape(-1)[:size].reshape(a.shape))
        r0 += nrows
    return out


def _rope_tables(m):
    pos = jnp.arange(m, dtype=F32)
    inv_freq = ROPE_THETA ** (-jnp.arange(0, ROT_DIM, 2, dtype=F32) / ROT_DIM)
    ang = pos[:, None] * inv_freq[None, :]
    cos8, sin8 = jnp.cos(ang), jnp.sin(ang)
    rest = HEAD_DIM - ROT_DIM
    cos64 = jnp.concatenate([cos8, cos8, jnp.ones((m, rest), F32)], axis=1)
    sin64 = jnp.concatenate([-sin8, sin8, jnp.zeros((m, rest), F32)], axis=1)
    return jnp.tile(cos64, (1, 2)), jnp.tile(sin64, (1, 2))


def _to_shard_major_cols(w):
    k = w.shape[0]
    return w.reshape(k, N_DEV, -1).transpose(1, 0, 2)


def _from_shard_major_cols(w):
    return w.transpose(1, 0, 2).reshape(w.shape[1], -1)


def kernel(x, ln_ffn1, w_ffn1_gu, w_ffn1_down, ln_mix, w_in, pool_w, pool_scale, w_pool_branch, q_norm, k_norm, sinks, w_attn_branch, w_out, ln_ffn2, w_ffn2_gu, w_ffn2_down, loss_target, m_ln_ffn1, m_w_ffn1_gu, m_w_ffn1_down, m_ln_mix, m_w_in, m_pool_w, m_pool_scale, m_w_pool_branch, m_q_norm, m_k_norm, m_sinks, m_w_attn_branch, m_w_out, m_ln_ffn2, m_w_ffn2_gu, m_w_ffn2_down, v_ln_ffn1, v_w_ffn1_gu, v_w_ffn1_down, v_ln_mix, v_w_in, v_pool_w, v_pool_scale, v_w_pool_branch, v_q_norm, v_k_norm, v_sinks, v_w_attn_branch, v_w_out, v_ln_ffn2, v_w_ffn2_gu, v_w_ffn2_down):
    weights = dict(ln_ffn1=ln_ffn1, w_ffn1_gu=w_ffn1_gu, w_ffn1_down=w_ffn1_down, ln_mix=ln_mix, w_in=w_in, pool_w=pool_w,
                   pool_scale=pool_scale, w_pool_branch=w_pool_branch, q_norm=q_norm, k_norm=k_norm, sinks=sinks,
                   w_attn_branch=w_attn_branch, w_out=w_out, ln_ffn2=ln_ffn2, w_ffn2_gu=w_ffn2_gu, w_ffn2_down=w_ffn2_down)
    mom_m = dict(ln_ffn1=m_ln_ffn1, w_ffn1_gu=m_w_ffn1_gu, w_ffn1_down=m_w_ffn1_down, ln_mix=m_ln_mix, w_in=m_w_in,
                 pool_w=m_pool_w, pool_scale=m_pool_scale, w_pool_branch=m_w_pool_branch, q_norm=m_q_norm, k_norm=m_k_norm,
                 sinks=m_sinks, w_attn_branch=m_w_attn_branch, w_out=m_w_out, ln_ffn2=m_ln_ffn2, w_ffn2_gu=m_w_ffn2_gu,
                 w_ffn2_down=m_w_ffn2_down)
    mom_v = dict(ln_ffn1=v_ln_ffn1, w_ffn1_gu=v_w_ffn1_gu, w_ffn1_down=v_w_ffn1_down, ln_mix=v_ln_mix, w_in=v_w_in,
                 pool_w=v_pool_w, pool_scale=v_pool_scale, w_pool_branch=v_w_pool_branch, q_norm=v_q_norm, k_norm=v_k_norm,
                 sinks=v_sinks, w_attn_branch=v_w_attn_branch, w_out=v_w_out, ln_ffn2=v_ln_ffn2, w_ffn2_gu=v_w_ffn2_gu,
                 w_ffn2_down=v_w_ffn2_down)
    order = ("ln_ffn1", "w_ffn1_gu", "w_ffn1_down", "ln_mix", "w_in", "pool_w", "pool_scale", "w_pool_branch", "q_norm",
             "k_norm", "sinks", "w_attn_branch", "w_out", "ln_ffn2", "w_ffn2_gu", "w_ffn2_down")
    big = ("w_ffn1_gu", "w_ffn1_down", "w_in", "w_pool_branch", "w_attn_branch", "w_out", "w_ffn2_gu", "w_ffn2_down")

    transposed = ("w_ffn1_gu", "w_ffn2_gu", "w_in")
    for group in (weights, mom_m, mom_v):
        for k in transposed:
            group[k] = jnp.swapaxes(group[k], 1, 2)

    n_layers = ln_ffn1.shape[0]
    seq, d = x.shape[-2], x.shape[-1]
    xs = x.reshape(seq, d)
    target = loss_target.reshape(seq, d)
    cos, sin = _rope_tables(seq)

    def layer_shards(l):
        return [weights[k][l].astype(BF16) for k in big]

    def layer_weights(l, full):
        g = dict(zip(big, full))
        return dict(
            gu1=g["w_ffn1_gu"].reshape(2, -1, d), down1=g["w_ffn1_down"].reshape(-1, d),
            gu2=g["w_ffn2_gu"].reshape(2, -1, d), down2=g["w_ffn2_down"].reshape(-1, d),
            w_in=g["w_in"].reshape(-1, d), w_pb=_from_shard_major_cols(g["w_pool_branch"]),
            w_ab=_from_shard_major_cols(g["w_attn_branch"]), w_out=g["w_out"].reshape(d, d),
            ln1=ln_ffn1[l][None], ln_mix=ln_mix[l][None], ln2=ln_ffn2[l][None], pool_w=pool_w[l],
            pool_scale=pool_scale[l][None], sinks=sinks[l],
            qgain=jnp.tile(q_norm[l], N_Q_HEADS)[None], kgain=jnp.tile(k_norm[l], N_KV_HEADS)[None])

    gathered = [layer_weights(0, _all_gather_many(layer_shards(0), name="gather_weights_l0"))]
    saved = []
    cur = xs
    for l in range(n_layers):
        lw = gathered[l]
        s = dict(x0=cur)
        in_flight, token = None, None
        if l + 1 < n_layers:
            in_flight, token = _exchange_start(layer_shards(l + 1), scatter=False, name=f"gather_start_l{l + 1}")
        s["h1"], s["gu1"], act1 = _ffn_up(cur, lw["ln1"], lw["gu1"], name=f"ffn1_up_l{l}", after=token)
        s["act1"] = act1
        x1 = _ffn_down(cur, act1, lw["down1"], name=f"ffn1_down_l{l}")
        s["x1"] = x1
        s["h2"], zq, zg = _mix_in(x1, lw["ln_mix"], lw["w_in"], name=f"mix_in_l{l}")
        s["zq"], s["zg"] = zq, zg
        s["d"], s["p"] = _pool_fwd(zq, lw["pool_w"], lw["pool_scale"], name=f"pool_fwd_l{l}")
        s["qr"], s["kr"], s["vb"] = _qk_prep(zq, lw["qgain"], lw["kgain"], cos, sin, name=f"qk_prep_l{l}")
        s["o"] = _attn_fwd(s["qr"], s["kr"], s["vb"], lw["sinks"], name=f"attn_fwd_l{l}")
        x2, s["mix"] = _merge_fwd(x1, s["p"], s["o"], zg, lw["w_pb"], lw["w_ab"], lw["w_out"],
                                                  name=f"merge_fwd_l{l}")
        s["x2"] = x2
        s["h3"], s["gu2"], act2 = _ffn_up(x2, lw["ln2"], lw["gu2"], name=f"ffn2_up_l{l}")
        s["act2"] = act2
        cur = _ffn_down(x2, act2, lw["down2"], name=f"ffn2_down_l{l}")
        saved.append(s)
        if in_flight is not None:
            gathered.append(layer_weights(l + 1, _exchange_wait(in_flight, cur, name=f"gather_wait_l{l + 1}")))

    loss_local, dy = _loss_head(cur, target, name="loss_head")
    loss = lax.psum(loss_local[0, 0], MESH_AXES)

    small_grads = {k: [None] * n_layers for k in _SMALL}
    received = {k: [None] * n_layers for k in big}
    big_late = ("w_ffn1_gu", "w_ffn1_down")
    big_early = tuple(k for k in big if k not in big_late)
    early_in_flight, late_in_flight = [None] * n_layers, [None] * n_layers
    token = None
    for l in reversed(range(n_layers)):
        lw, s = gathered[l], saved[l]
        d_ff = lw["down1"].shape[0]

        def ffn_weight_grads(dyh, dgu, h, act, tag):
            dw_down = _matmul_tn(act[None], dyh[None], name=f"{tag}_dw_down_l{l}", a_chunk=d_ff // 2)
            dw_gu = _matmul_tn(dgu, h[None], name=f"{tag}_dw_gu_l{l}", a_chunk=d_ff // 2)
            return dw_gu.reshape(N_DEV, -1, d), dw_down.reshape(N_DEV, -1, d)

        dyh, dgu, dx2, dln2 = _ffn_bwd(dy, s["x2"], lw["ln2"], s["gu2"], lw["down2"], lw["gu2"],
                                       name=f"ffn2_bwd_l{l}", after=token)
        dw_gu2, dw_down2 = ffn_weight_grads(dyh, dgu, s["h3"], s["act2"], "ffn2")

        dyb, da, db, dp, do, dzg = _merge_bwd(dx2, s["p"], s["o"], s["zg"], lw["w_out"], lw["w_pb"], lw["w_ab"],
                                              name=f"merge_bwd_l{l}")
        dw_out = _matmul_tn(s["mix"][None], dyb[None], name=f"dw_out_l{l}")[0]
        dw_pb = _matmul_tn(s["p"][None], da[None], name=f"dw_pb_l{l}")[0]
        dw_ab = _matmul_tn(s["o"][None], db[None], name=f"dw_ab_l{l}")[0]
        dzp, dpw, dsc = _pool_bwd(dp, s["d"], lw["pool_w"], lw["pool_scale"], name=f"pool_bwd_l{l}")
        dq, dk, dv, dsinks = _attn_bwd(do, s["qr"], s["kr"], s["vb"], lw["sinks"], name=f"attn_bwd_l{l}")
        dzqkv, dqg, dkg = _qk_bwd(dq, dk, dv, s["zq"], lw["qgain"], lw["kgain"], cos, sin, name=f"qk_bwd_l{l}")
        dw_in = jnp.concatenate([_matmul_tn(dzp[None], s["h2"][None], name=f"dw_in_pool_l{l}")[0],
                                 _matmul_tn(dzqkv[None], s["h2"][None], name=f"dw_in_qkv_l{l}")[0],
                                 _matmul_tn(dzg[None], s["h2"][None], name=f"dw_in_gate_l{l}")[0]], axis=0)
        dx1, dlnm = _mix_bwd_x(dx2, s["x1"], lw["ln_mix"], dzp, dzqkv, dzg, lw["w_in"], name=f"mix_bwd_x_l{l}")

        partial = dict(w_in=dw_in.reshape(N_DEV, -1, d), w_pool_branch=_to_shard_major_cols(dw_pb),
                       w_attn_branch=_to_shard_major_cols(dw_ab), w_out=dw_out.reshape(N_DEV, d // N_DEV, d),
                       w_ffn2_gu=dw_gu2, w_ffn2_down=dw_down2)
        early_in_flight[l], token = _exchange_start([partial[k] for k in big_early], scatter=True,
                                                    name=f"grads_early_start_l{l}")

        dyh, dgu, dy, dln1 = _ffn_bwd(dx1, s["x0"], lw["ln1"], s["gu1"], lw["down1"], lw["gu1"],
                                      name=f"ffn1_bwd_l{l}", after=token)
        dw_gu1, dw_down1 = ffn_weight_grads(dyh, dgu, s["h1"], s["act1"], "ffn1")
        late_in_flight[l], token = _exchange_start([dw_gu1, dw_down1], scatter=True, name=f"grads_late_start_l{l}")
        small_grads["ln_ffn1"][l] = dln1[0]
        small_grads["ln_mix"][l] = dlnm[0]
        small_grads["ln_ffn2"][l] = dln2[0]
        small_grads["pool_w"][l] = dpw
        small_grads["pool_scale"][l] = dsc[0]
        small_grads["q_norm"][l] = dqg[0]
        small_grads["k_norm"][l] = dkg[0]
        small_grads["sinks"][l] = dsinks[0, :N_Q_HEADS]

    grad_x = dy.reshape(x.shape)

    small_w = [weights[k] for k in _SMALL]
    packed_g = _pack_small([jnp.stack(small_grads[k]).reshape(weights[k].shape) for k in _SMALL])
    small_in_flight, after = _exchange_start([packed_g], scatter=False, name="small_grads_start")
    for l in reversed(range(n_layers)):
        got = _exchange_wait(early_in_flight[l], after, name=f"grads_early_wait_l{l}")
        after = got[0]
        for k, r in zip(big_early, got):
            received[k][l] = r

    grads, deltas, new_m, new_v = {}, {}, {}, {}

    def adamw(k, after):
        w = weights[k]
        shape2 = (n_layers, -1, w.shape[-1])
        parts = [r.reshape(N_DEV, -1, w.shape[-1]) for r in received[k]]
        outs = _adamw_sharded(parts, w.reshape(shape2), mom_m[k].reshape(shape2), mom_v[k].reshape(shape2),
                              name=f"adamw_{k}", after=after)
        grads[k], deltas[k], new_m[k], new_v[k] = (o.reshape(w.shape) for o in outs)
        return outs[0]

    after = None
    for k in big_early:
        after = adamw(k, after)
    for l in reversed(range(n_layers)):
        got = _exchange_wait(late_in_flight[l], after, name=f"grads_late_wait_l{l}")
        after = got[0]
        for k, r in zip(big_late, got):
            received[k][l] = r
    after = None
    for k in big_late:
        after = adamw(k, after)

    (parts_small,) = _exchange_wait(small_in_flight, after, name="small_grads_wait")
    outs = _adamw_packed(parts_small, _pack_small(small_w), _pack_small([mom_m[k] for k in _SMALL]),
                         _pack_small([mom_v[k] for k in _SMALL]), name="adamw_small")
    for res, o in zip((grads, deltas, new_m, new_v), outs):
        for k, a in zip(_SMALL, _unpack_small(o, small_w)):
            res[k] = a

    for res in (grads, deltas, new_m, new_v):
        for k in transposed:
            res[k] = jnp.swapaxes(res[k], 1, 2)
    return (loss, grad_x, *[grads[k] for k in order], *[deltas[k] for k in order],
            *[new_m[k] for k in order], *[new_v[k] for k in order])
```

```python
import functools

import jax
import jax.numpy as jnp
from jax import lax
from jax.experimental import pallas as pl
from jax.experimental.pallas import tpu as pltpu

F32 = jnp.float32
BF16 = jnp.bfloat16

N_DEV = 8
MESH_AXES = ("x", "y", "c")
EPS = 1e-6

HEAD_DIM = 64
N_Q_HEADS = 8
N_KV_HEADS = 2
GQA_GROUP = N_Q_HEADS // N_KV_HEADS
ATTN_BLOCK = 128
ATTN_SCALE = HEAD_DIM ** -0.5
ROPE_THETA = 500000.0
ROT_DIM = 16
POOL_WINDOWS = (2, 4, 8, 16)
POOL_HALO = 16
GROUP_DIM = 128
POOL_DIM = 512
ATTN_DIM = 512
KV_DIM = 128
QKV_END = POOL_DIM + ATTN_DIM + 2 * KV_DIM

ADAM_LR = 0.001
ADAM_B1 = 0.9
ADAM_B2 = 0.999
ADAM_EPS = 1e-08
ADAM_WD = 0.01
ADAM_STEP = 10

ROW_TILE = 512
TN_ROW_TILE = 2048
FFN_CHUNK = 256
FFN_BWD_ROW_TILE = 256
VMEM_LIMIT_BYTES = 56 << 20
ADAMW_BLOCK_ELEMS = 96 * 1024
NEG_BIG = -1e30

_NT = (((1,), (1,)), ((), ()))
_TN = (((0,), (0,)), ((), ()))


def _cp(*sem):
    return pltpu.CompilerParams(dimension_semantics=sem, vmem_limit_bytes=VMEM_LIMIT_BYTES)


def _resident(block, index):
    return pl.BlockSpec(block, lambda *_: index, pipeline_mode=pl.Buffered(1))


def _row_tile(m):
    return min(ROW_TILE, m)


def _sds(shape, dtype):
    return jax.ShapeDtypeStruct(shape, dtype)


def _mesh_pos():
    return lax.axis_index("x"), lax.axis_index("y"), lax.axis_index("c")


def _all_gather_many(shards, name, after=None):
    n = len(shards)

    deps = [] if after is None else [after]

    def body(*refs):
        ins, outs = refs[:n], refs[n + len(deps):2 * n + len(deps)]
        send_sems, recv_sems, local_sems = refs[2 * n + len(deps):]
        x, y, c = _mesh_pos()
        me, sibling = (x, y, c), (x, y, 1 - c)
        chips = [(1 - x, y), (x, 1 - y), (1 - x, 1 - y)]

        def slot(a, pos):
            return outs[a].at[4 * pos[0] + 2 * pos[1] + pos[2]]

        def copy(a, k, block, to, src=None):
            return pltpu.make_async_remote_copy(
                src_ref=slot(a, block) if src is None else src, dst_ref=slot(a, block),
                send_sem=send_sems.at[a, k], recv_sem=recv_sems.at[a, k],
                device_id=to, device_id_type=pl.DeviceIdType.MESH)

        mine = [pltpu.make_async_copy(ins[a], slot(a, me), local_sems.at[a]) for a in range(n)]
        for cp in mine:
            cp.start()
        first = []
        for a in range(n):
            first.append(copy(a, 0, me, sibling, src=ins[a]))
            for j, chip in enumerate(chips):
                first.append(copy(a, 1 + j, me, (*chip, c), src=ins[a]))
        for cp in first:
            cp.start()
        passed = []
        for j, chip in enumerate(chips):
            for a in range(n):
                copy(a, 1 + j, (*chip, c), me).wait_recv()
                fwd = copy(a, 4 + j, (*chip, c), sibling)
                fwd.start()
                passed.append(fwd)
        for a in range(n):
            copy(a, 0, sibling, me).wait_recv()
        for j, chip in enumerate(chips):
            for a in range(n):
                copy(a, 4 + j, (*chip, 1 - c), me).wait_recv()
        for cp in first + passed:
            cp.wait_send()
        for cp in mine:
            cp.wait()

    any_spec = pl.BlockSpec(memory_space=pl.ANY)
    return pl.pallas_call(
        body, name=name,
        out_shape=[_sds((N_DEV,) + s.shape, s.dtype) for s in shards],
        in_specs=[any_spec] * (n + len(deps)), out_specs=[any_spec] * n,
        scratch_shapes=[pltpu.SemaphoreType.DMA((n, 7)), pltpu.SemaphoreType.DMA((n, 7)),
                        pltpu.SemaphoreType.DMA((n,))],
    )(*shards, *deps)


def _direct_copies(src, land, send_sem, recv_sem, local_sem, scatter):
    x, y, c = _mesh_pos()
    me = 4 * x + 2 * y + c
    local = pltpu.make_async_copy(src.at[me] if scatter else src, land.at[me], local_sem)
    remote = []
    for k in range(1, N_DEV):
        px = 1 - x if k & 4 else x
        py = 1 - y if k & 2 else y
        pc = 1 - c if k & 1 else c
        remote.append(pltpu.make_async_remote_copy(
            src_ref=src.at[4 * px + 2 * py + pc] if scatter else src, dst_ref=land.at[me],
            send_sem=send_sem, recv_sem=recv_sem, device_id=(px, py, pc), device_id_type=pl.DeviceIdType.MESH))
    seven = land.at[pl.ds(0, N_DEV - 1)]
    drain = pltpu.make_async_remote_copy(src_ref=seven, dst_ref=seven, send_sem=send_sem, recv_sem=recv_sem,
                                         device_id=(x, y, c), device_id_type=pl.DeviceIdType.MESH)
    return local, remote, drain


_HBM_SPEC = pl.BlockSpec(memory_space=pltpu.HBM)
_SEM_SPEC = pl.BlockSpec(memory_space=pltpu.SEMAPHORE)
_DATAFLOW = pltpu.SideEffectType.DATAFLOW_SIDE_EFFECTING
_SEMS_PER_ARRAY = 3


def _exchange_start(srcs, scatter, name):
    n = len(srcs)
    n_sems = _SEMS_PER_ARRAY * n
    land_shapes = [s.shape if scatter else (N_DEV,) + s.shape for s in srcs]

    def body(*refs):
        ins, lands, sems = refs[:n], refs[n:2 * n], refs[2 * n:2 * n + n_sems]
        for a in range(n):
            local, remote, _ = _direct_copies(ins[a], lands[a], *sems[3 * a:3 * a + 3], scatter)
            local.start()
            for cp in remote:
                cp.start()
        refs[-1][...] = jnp.zeros_like(refs[-1])

    outs = pl.pallas_call(
        body, name=name,
        out_shape=(*[pltpu.SemaphoreType.DMA(())] * n_sems,
                   *[pltpu.HBM(s.shape, s.dtype) for s in srcs],
                   *[pltpu.HBM(shape, s.dtype) for shape, s in zip(land_shapes, srcs)],
                   _sds((8, 128), F32)),
        in_specs=[_HBM_SPEC] * (2 * n),
        out_specs=(*[_SEM_SPEC] * n_sems, *[_HBM_SPEC] * (2 * n), pl.BlockSpec(memory_space=pltpu.VMEM)),
        input_output_aliases={i: n_sems + i for i in range(2 * n)},
        compiler_params=pltpu.CompilerParams(has_side_effects=_DATAFLOW),
    )(*[pltpu.with_memory_space_constraint(s, pltpu.HBM) for s in srcs],
      *[pltpu.with_memory_space_constraint(lax.empty(shape, s.dtype), pltpu.HBM) for shape, s in zip(land_shapes, srcs)])
    return (outs[:n_sems], outs[n_sems:n_sems + n], outs[n_sems + n:n_sems + 2 * n], scatter), outs[-1]


def _exchange_wait(state, after, name):
    sems, srcs, lands, scatter = state
    n = len(srcs)
    n_sems = len(sems)

    def body(*refs):
        ins, zones, ss = refs[:n], refs[n:2 * n], refs[2 * n:2 * n + n_sems]
        for a in range(n):
            local, _, drain = _direct_copies(ins[a], zones[a], *ss[3 * a:3 * a + 3], scatter)
            drain.wait_send()
            drain.wait_recv()
            local.wait()

    outs = pl.pallas_call(
        body, name=name,
        out_shape=(*[pltpu.HBM(s.shape, s.dtype) for s in srcs], *[pltpu.HBM(z.shape, z.dtype) for z in lands]),
        in_specs=[_HBM_SPEC] * (2 * n) + [_SEM_SPEC] * n_sems + [pl.BlockSpec(memory_space=pl.ANY)],
        out_specs=[_HBM_SPEC] * (2 * n),
        input_output_aliases={i: i for i in range(2 * n)},
        compiler_params=pltpu.CompilerParams(has_side_effects=_DATAFLOW),
    )(*srcs, *lands, *sems, after)
    return outs[n:]


def _rms_fwd(xv, gain):
    r = lax.rsqrt(jnp.mean(xv * xv, axis=-1, keepdims=True) + EPS)
    return xv * r * gain


def _rms_bwd(dh, xv, gain):
    r = lax.rsqrt(jnp.mean(xv * xv, axis=-1, keepdims=True) + EPS)
    xn = xv * r
    dxn = dh * gain
    dx = r * (dxn - xn * jnp.mean(dxn * xn, axis=-1, keepdims=True))
    return dx, dh * xn


def _sigmoid(v):
    return 0.5 * jnp.tanh(0.5 * v) + 0.5


def _silu_parts(g):
    s = _sigmoid(g)
    return g * s, s * (1.0 + g * (1.0 - s))


def _segment_mean(v, width):
    r = lax.broadcasted_iota(jnp.int32, (width, width), 0) >> 6
    c = lax.broadcasted_iota(jnp.int32, (width, width), 1) >> 6
    bd = (r == c).astype(BF16)
    hi = v.astype(BF16)
    lo = (v - hi.astype(F32)).astype(BF16)
    total = jnp.dot(hi, bd, preferred_element_type=F32) + jnp.dot(lo, bd, preferred_element_type=F32)
    return total * (1.0 / HEAD_DIM)


def _rope_partner(v):
    width = v.shape[1]
    half = ROT_DIM // 2
    lane = lax.broadcasted_iota(jnp.int32, v.shape, 1) & (HEAD_DIM - 1)
    up = jnp.where(lane < ROT_DIM, pltpu.roll(v, half, 1), 0.0)
    return jnp.where(lane < half, pltpu.roll(v, width - half, 1), up)


def _tile_lanes(t, width):
    return t if width == t.shape[1] else jnp.tile(t, (1, width // t.shape[1]))


def _ffn_chunks(f):
    return [slice(j * FFN_CHUNK, (j + 1) * FFN_CHUNK) for j in range(f // FFN_CHUNK)]


def _ffn_up(x, ln, wgu, name, after=None):
    m, d = x.shape
    f = wgu.shape[1]
    tm = _row_tile(m)
    deps = [] if after is None else [after]

    def body(*refs):
        x_ref, ln_ref, w_ref = refs[:3]
        h_ref, gu_ref, a_ref = refs[-3:]
        h = _rms_fwd(x_ref[...], ln_ref[...]).astype(BF16)
        h_ref[...] = h
        for cols in _ffn_chunks(f):
            g = lax.dot_general(h, w_ref[0, cols, :], _NT, preferred_element_type=F32)
            u = lax.dot_general(h, w_ref[1, cols, :], _NT, preferred_element_type=F32)
            gu_ref[0, :, cols] = g.astype(BF16)
            gu_ref[1, :, cols] = u.astype(BF16)
            a_ref[:, cols] = (g * _sigmoid(g) * u).astype(BF16)

    return pl.pallas_call(
        body, name=name, grid=(m // tm,),
        in_specs=[pl.BlockSpec((tm, d), lambda i: (i, 0)), _resident((1, d), (0, 0)),
                  _resident((2, f, d), (0, 0, 0))] + [pl.BlockSpec(memory_space=pl.ANY)] * len(deps),
        out_specs=[pl.BlockSpec((tm, d), lambda i: (i, 0)), pl.BlockSpec((2, tm, f), lambda i: (0, i, 0)),
                   pl.BlockSpec((tm, f), lambda i: (i, 0))],
        out_shape=[_sds((m, d), BF16), _sds((2, m, f), BF16), _sds((m, f), BF16)],
        compiler_params=_cp("parallel"),
    )(x, ln, wgu, *deps)


def _ffn_down(x, act, wd, name):
    m, d = x.shape
    f = act.shape[-1]
    tm = _row_tile(m)

    def body(x_ref, a_ref, w_ref, o_ref):
        o_ref[...] = x_ref[...] + 0.5 * jnp.dot(a_ref[...], w_ref[...], preferred_element_type=F32)

    return pl.pallas_call(
        body, name=name, grid=(m // tm,),
        in_specs=[pl.BlockSpec((tm, d), lambda i: (i, 0)), pl.BlockSpec((tm, f), lambda i: (i, 0)),
                  _resident((f, d), (0, 0))],
        out_specs=pl.BlockSpec((tm, d), lambda i: (i, 0)),
        out_shape=_sds((m, d), F32),
        compiler_params=_cp("parallel"),
    )(x, act, wd)


def _ffn_bwd(dy, x, ln, gu, wd, wgu, name, after=None):
    m, d = dy.shape
    f = gu.shape[-1]
    tm = min(FFN_BWD_ROW_TILE, m)
    deps = [] if after is None else [after]

    def body(*refs):
        dy_ref, x_ref, ln_ref, gu_ref, wd_ref, wgu_ref = refs[:6]
        dyh_ref, dgu_ref, dx_ref, dln_ref = refs[-4:]

        @pl.when(pl.program_id(0) == 0)
        def _():
            dln_ref[...] = jnp.zeros_like(dln_ref)

        dyh = (0.5 * dy_ref[...]).astype(BF16)
        dyh_ref[...] = dyh
        dgs, dus = [], []
        for cols in _ffn_chunks(f):
            da = lax.dot_general(dyh, wd_ref[cols, :], _NT, preferred_element_type=F32)
            g = gu_ref[0, :, cols].astype(F32)
            u = gu_ref[1, :, cols].astype(F32)
            silu, dsilu = _silu_parts(g)
            dgs.append((da * u * dsilu).astype(BF16))
            dus.append((da * silu).astype(BF16))
            dgu_ref[0, :, cols] = dgs[-1]
            dgu_ref[1, :, cols] = dus[-1]
        dh = jnp.dot(jnp.concatenate(dgs, axis=1), wgu_ref[0], preferred_element_type=F32)
        dh += jnp.dot(jnp.concatenate(dus, axis=1), wgu_ref[1], preferred_element_type=F32)
        dx, dgain = _rms_bwd(dh, x_ref[...], ln_ref[...])
        dx_ref[...] = dy_ref[...] + dx
        dln_ref[...] += jnp.sum(dgain, axis=0, keepdims=True)

    row = lambda i: (i, 0)
    return pl.pallas_call(
        body, name=name, grid=(m // tm,),
        in_specs=[pl.BlockSpec((tm, d), row), pl.BlockSpec((tm, d), row), _resident((1, d), (0, 0)),
                  pl.BlockSpec((2, tm, f), lambda i: (0, i, 0)), _resident((f, d), (0, 0)),
                  _resident((2, f, d), (0, 0, 0))] + [pl.BlockSpec(memory_space=pl.ANY)] * len(deps),
        out_specs=[pl.BlockSpec((tm, d), row), pl.BlockSpec((2, tm, f), lambda i: (0, i, 0)),
                   pl.BlockSpec((tm, d), row), pl.BlockSpec((1, d), lambda i: (0, 0))],
        out_shape=[_sds((m, d), BF16), _sds((2, m, f), BF16), _sds((m, d), F32), _sds((1, d), F32)],
        compiler_params=_cp("arbitrary"),
    )(dy, x, ln, gu, wd, wgu, *deps)


def _matmul_tn(a, b, name, a_chunk=None, out_dtype=BF16):
    ja, m, k = a.shape
    jb, _, n = b.shape
    nj = max(ja, jb)
    kc = k if a_chunk is None else a_chunk
    tm = min(TN_ROW_TILE, m)
    nm = m // tm

    def body(a_ref, b_ref, o_ref, acc):
        step = pl.program_id(2)

        @pl.when(step == 0)
        def _():
            acc[...] = jnp.zeros_like(acc)

        acc[...] += lax.dot_general(a_ref[...], b_ref[...], _TN, preferred_element_type=F32)

        @pl.when(step == nm - 1)
        def _():
            o_ref[...] = acc[...].astype(o_ref.dtype)

    return pl.pallas_call(
        body, name=name, grid=(nj, k // kc, nm),
        in_specs=[pl.BlockSpec((None, tm, kc), (lambda j, c, s: (j, s, c)) if ja > 1 else (lambda j, c, s: (0, s, c))),
                  pl.BlockSpec((None, tm, n), (lambda j, c, s: (j, s, 0)) if jb > 1 else (lambda j, c, s: (0, s, 0)))],
        out_specs=pl.BlockSpec((None, kc, n), lambda j, c, s: (j, c, 0)),
        out_shape=_sds((nj, k, n), out_dtype),
        scratch_shapes=[pltpu.VMEM((kc, n), F32)],
        compiler_params=_cp("parallel", "parallel", "arbitrary"),
    )(a, b)


def _mix_in(x, ln, w_in, name):
    m, d = x.shape
    n_in = w_in.shape[0]
    tm = _row_tile(m)

    def body(x_ref, ln_ref, w_ref, h_ref, zq_ref, zg_ref):
        h = _rms_fwd(x_ref[...], ln_ref[...]).astype(BF16)
        h_ref[...] = h
        zq_ref[...] = lax.dot_general(h, w_ref[:QKV_END, :], _NT, preferred_element_type=F32)
        zg_ref[...] = lax.dot_general(h, w_ref[QKV_END:, :], _NT, preferred_element_type=F32)

    return pl.pallas_call(
        body, name=name, grid=(m // tm,),
        in_specs=[pl.BlockSpec((tm, d), lambda i: (i, 0)), _resident((1, d), (0, 0)), _resident((n_in, d), (0, 0))],
        out_specs=[pl.BlockSpec((tm, d), lambda i: (i, 0)), pl.BlockSpec((tm, QKV_END), lambda i: (i, 0)),
                   pl.BlockSpec((tm, n_in - QKV_END), lambda i: (i, 0))],
        out_shape=[_sds((m, d), BF16), _sds((m, QKV_END), F32), _sds((m, n_in - QKV_END), F32)],
        compiler_params=_cp("parallel"),
    )(x, ln, w_in)


def _pool_fwd(zq, pool_w, scale, name):
    m = zq.shape[0]
    tm = _row_tile(m)
    halo_blocks = tm // POOL_HALO

    def body(zc_ref, zh_ref, pw_ref, sc_ref, d_ref, p_ref):
        i = pl.program_id(0)
        halo = jnp.where(i > 0, zh_ref[...], 0.0)
        ext = jnp.concatenate([halo, zc_ref[...]], axis=0)
        t = i * tm + lax.broadcasted_iota(jnp.int32, (tm, 1), 0)
        for g, w in enumerate(POOL_WINDOWS):
            lanes = slice(g * GROUP_DIM, (g + 1) * GROUP_DIM)
            e = ext[:, lanes]
            s, k = e, 1
            while k < w:
                s = s + pltpu.roll(s, k, 0)
                k *= 2
            cnt = jnp.minimum(t + 1, w).astype(F32)
            dg = (s[POOL_HALO:, :] / cnt - e[POOL_HALO:, :]).astype(BF16)
            y = jnp.dot(dg, pw_ref[g].astype(BF16), preferred_element_type=F32)
            d_ref[:, lanes] = dg
            p_ref[:, lanes] = (y * sc_ref[:, lanes]).astype(BF16)

    return pl.pallas_call(
        body, name=name, grid=(m // tm,),
        in_specs=[pl.BlockSpec((tm, POOL_DIM), lambda i: (i, 0)),
                  pl.BlockSpec((POOL_HALO, POOL_DIM), lambda i: (jnp.maximum(i * halo_blocks - 1, 0), 0)),
                  _resident((4, GROUP_DIM, GROUP_DIM), (0, 0, 0)), _resident((1, POOL_DIM), (0, 0))],
        out_specs=[pl.BlockSpec((tm, POOL_DIM), lambda i: (i, 0)), pl.BlockSpec((tm, POOL_DIM), lambda i: (i, 0))],
        out_shape=[_sds((m, POOL_DIM), BF16), _sds((m, POOL_DIM), BF16)],
        compiler_params=_cp("parallel"),
    )(zq, zq, pool_w, scale)


def _pool_bwd(dp, d, pool_w, scale, name):
    m = dp.shape[0]
    tm = _row_tile(m)
    nb = m // tm
    halo_blocks = tm // POOL_HALO
    rows = tm + POOL_HALO

    def body(dpc_ref, dph_ref, d_ref, pw_ref, sc_ref, du_ref, dpw_ref, dsc_ref):
        i = pl.program_id(0)

        @pl.when(i == 0)
        def _():
            dpw_ref[...] = jnp.zeros_like(dpw_ref)
            dsc_ref[...] = jnp.zeros_like(dsc_ref)

        halo = jnp.where(i < nb - 1, dph_ref[...], 0.0)
        dpc = dpc_ref[...]
        ext = jnp.concatenate([dpc, halo], axis=0)
        t = i * tm + lax.broadcasted_iota(jnp.int32, (rows, 1), 0)
        for g, w in enumerate(POOL_WINDOWS):
            lanes = slice(g * GROUP_DIM, (g + 1) * GROUP_DIM)
            pwb = pw_ref[g].astype(BF16)
            dyb = (ext[:, lanes] * sc_ref[:, lanes]).astype(BF16)
            dd = lax.dot_general(dyb, pwb, _NT, preferred_element_type=F32)
            cnt = jnp.minimum(t + 1, w).astype(F32)
            s, k = dd / cnt, 1
            while k < w:
                s = s + pltpu.roll(s, rows - k, 0)
                k *= 2
            du_ref[:, lanes] = (s[:tm, :] - dd[:tm, :]).astype(BF16)
            dcur = d_ref[:, lanes]
            y = jnp.dot(dcur, pwb, preferred_element_type=F32)
            dsc_ref[:, lanes] += jnp.sum(dpc[:, lanes] * y, axis=0, keepdims=True)
            dpw_ref[g] += lax.dot_general(dcur, dyb[:tm, :], _TN, preferred_element_type=F32)

    return pl.pallas_call(
        body, name=name, grid=(nb,),
        in_specs=[pl.BlockSpec((tm, POOL_DIM), lambda i: (i, 0)),
                  pl.BlockSpec((POOL_HALO, POOL_DIM), lambda i: (jnp.minimum((i + 1) * halo_blocks, nb * halo_blocks - 1), 0)),
                  pl.BlockSpec((tm, POOL_DIM), lambda i: (i, 0)),
                  _resident((4, GROUP_DIM, GROUP_DIM), (0, 0, 0)), _resident((1, POOL_DIM), (0, 0))],
        out_specs=[pl.BlockSpec((tm, POOL_DIM), lambda i: (i, 0)),
                   pl.BlockSpec((4, GROUP_DIM, GROUP_DIM), lambda i: (0, 0, 0)),
                   pl.BlockSpec((1, POOL_DIM), lambda i: (0, 0))],
        out_shape=[_sds((m, POOL_DIM), BF16), _sds((4, GROUP_DIM, GROUP_DIM), F32), _sds((1, POOL_DIM), F32)],
        compiler_params=_cp("arbitrary"),
    )(dp, dp, d, pool_w, scale)


def _qk_norm_rope(xv, gain, cos, sin):
    width = xv.shape[1]
    r = lax.rsqrt(_segment_mean(xv * xv, width) + EPS)
    y = xv * r * gain
    return y * _tile_lanes(cos, width) + _rope_partner(y) * _tile_lanes(sin, width)


def _qk_prep(zq, qgain, kgain, cos, sin, name):
    m = zq.shape[0]
    tm = _row_tile(m)

    def body(q_ref, kv_ref, qg_ref, kg_ref, cos_ref, sin_ref, qr_ref, kr_ref, v_ref):
        cos_v, sin_v = cos_ref[...], sin_ref[...]
        qr_ref[...] = (_qk_norm_rope(q_ref[...], qg_ref[...], cos_v, sin_v) * ATTN_SCALE).astype(BF16)
        kv = kv_ref[...]
        kr_ref[...] = _qk_norm_rope(kv[:, :KV_DIM], kg_ref[...], cos_v, sin_v).astype(BF16)
        v_ref[...] = kv[:, KV_DIM:].astype(BF16)

    return pl.pallas_call(
        body, name=name, grid=(m // tm,),
        in_specs=[pl.BlockSpec((tm, ATTN_DIM), lambda i: (i, 1)), pl.BlockSpec((tm, 2 * KV_DIM), lambda i: (i, 4)),
                  _resident((1, ATTN_DIM), (0, 0)), _resident((1, KV_DIM), (0, 0)),
                  pl.BlockSpec((tm, 128), lambda i: (i, 0)), pl.BlockSpec((tm, 128), lambda i: (i, 0))],
        out_specs=[pl.BlockSpec((tm, ATTN_DIM), lambda i: (i, 0)), pl.BlockSpec((tm, KV_DIM), lambda i: (i, 0)),
                   pl.BlockSpec((tm, KV_DIM), lambda i: (i, 0))],
        out_shape=[_sds((m, ATTN_DIM), BF16), _sds((m, KV_DIM), BF16), _sds((m, KV_DIM), BF16)],
        compiler_params=_cp("parallel"),
    )(zq, zq, qgain, kgain, cos, sin)


GROUP_ROWS = GQA_GROUP * ATTN_BLOCK


def _band_bias():
    qi = jnp.arange(GROUP_ROWS)[:, None] % ATTN_BLOCK
    ki = jnp.arange(2 * ATTN_BLOCK)[None, :]
    diff = qi + ATTN_BLOCK - ki
    band = (diff >= 0) & (diff < ATTN_BLOCK)
    first = band & (ki >= ATTN_BLOCK)
    return jnp.where(jnp.stack([first, band]), 0.0, NEG_BIG).astype(F32)


def _stack_group(v, kvh):
    heads = range(kvh * GQA_GROUP, (kvh + 1) * GQA_GROUP)
    return jnp.concatenate([v[:, h * HEAD_DIM:(h + 1) * HEAD_DIM] for h in heads], axis=0)


def _unstack_groups(groups):
    return jnp.concatenate([grp[g * ATTN_BLOCK:(g + 1) * ATTN_BLOCK, :] for grp in groups for g in range(GQA_GROUP)],
                           axis=1)


def _group_sinks(sk_ref, kvh):
    row_head = lax.broadcasted_iota(jnp.int32, (GROUP_ROWS, 1), 0) >> 7
    col = jnp.full((GROUP_ROWS, 1), sk_ref[kvh * GQA_GROUP], F32)
    for g in range(1, GQA_GROUP):
        col = jnp.where(row_head == g, sk_ref[kvh * GQA_GROUP + g], col)
    return col


def _head_probs(qh, kh, bias, sink):
    s = lax.dot_general(qh, kh, _NT, preferred_element_type=F32) + bias
    mx = jnp.maximum(jnp.max(s, axis=-1, keepdims=True), sink)
    p = jnp.exp(s - mx)
    es = jnp.exp(sink - mx)
    inv = 1.0 / (jnp.sum(p, axis=-1, keepdims=True) + es)
    return p * inv, es * inv


def _attn_fwd(qr, kr, vb, sinks, name):
    m = qr.shape[0]
    nb = m // ATTN_BLOCK

    def body(q_ref, kp_ref, kc_ref, vp_ref, vc_ref, sk_ref, bias_ref, o_ref):
        bias = bias_ref[jnp.minimum(pl.program_id(0), 1)]
        qv = q_ref[...]
        kk = jnp.concatenate([kp_ref[...], kc_ref[...]], axis=0)
        vv = jnp.concatenate([vp_ref[...], vc_ref[...]], axis=0)
        outs = []
        for kvh in range(N_KV_HEADS):
            kv_lanes = slice(kvh * HEAD_DIM, (kvh + 1) * HEAD_DIM)
            p, _ = _head_probs(_stack_group(qv, kvh), kk[:, kv_lanes], bias, _group_sinks(sk_ref, kvh))
            outs.append(jnp.dot(p.astype(BF16), vv[:, kv_lanes], preferred_element_type=F32))
        o_ref[...] = _unstack_groups(outs).astype(BF16)

    prev = lambda n: (jnp.maximum(n - 1, 0), 0)
    cur = lambda n: (n, 0)
    return pl.pallas_call(
        body, name=name, grid=(nb,),
        in_specs=[pl.BlockSpec((ATTN_BLOCK, ATTN_DIM), cur),
                  pl.BlockSpec((ATTN_BLOCK, KV_DIM), prev), pl.BlockSpec((ATTN_BLOCK, KV_DIM), cur),
                  pl.BlockSpec((ATTN_BLOCK, KV_DIM), prev), pl.BlockSpec((ATTN_BLOCK, KV_DIM), cur),
                  pl.BlockSpec(memory_space=pltpu.SMEM), _resident((2, GROUP_ROWS, 2 * ATTN_BLOCK), (0, 0, 0))],
        out_specs=pl.BlockSpec((ATTN_BLOCK, ATTN_DIM), cur),
        out_shape=_sds((m, ATTN_DIM), BF16),
        compiler_params=_cp("parallel"),
    )(qr, kr, kr, vb, vb, sinks, _band_bias())


def _attn_bwd(do, qr, kr, vb, sinks, name):
    m = qr.shape[0]
    nb = m // ATTN_BLOCK

    def body(do_ref, q_ref, kp_ref, kc_ref, vp_ref, vc_ref, sk_ref, bias_ref, dq_ref, dk_ref, dv_ref, ds_ref,
             carry_k, carry_v):
        n = pl.program_id(0)

        @pl.when(n == 0)
        def _():
            carry_k[...] = jnp.zeros_like(carry_k)
            carry_v[...] = jnp.zeros_like(carry_v)
            ds_ref[...] = jnp.zeros_like(ds_ref)

        @pl.when(n < nb)
        def _():
            bias = bias_ref[jnp.minimum(n, 1)]
            qv = q_ref[...]
            dov = do_ref[...]
            kk = jnp.concatenate([kp_ref[...], kc_ref[...]], axis=0)
            vv = jnp.concatenate([vp_ref[...], vc_ref[...]], axis=0)
            lane = lax.broadcasted_iota(jnp.int32, (1, 128), 1)
            dsink = jnp.zeros((1, 128), F32)
            dqs, dks, dvs = [], [], []
            for kvh in range(N_KV_HEADS):
                kv_lanes = slice(kvh * HEAD_DIM, (kvh + 1) * HEAD_DIM)
                kh, vh = kk[:, kv_lanes], vv[:, kv_lanes]
                qg = _stack_group(qv, kvh)
                dog = _stack_group(dov, kvh).astype(BF16)
                p, ps = _head_probs(qg, kh, bias, _group_sinks(sk_ref, kvh))
                dpr = lax.dot_general(dog, vh, _NT, preferred_element_type=F32)
                delta = jnp.sum(p * dpr, axis=-1, keepdims=True)
                dsb = (p * (dpr - delta)).astype(BF16)
                sink_term = ps * delta
                for g in range(GQA_GROUP):
                    rows = slice(g * ATTN_BLOCK, (g + 1) * ATTN_BLOCK)
                    dsink = dsink + jnp.where(lane == kvh * GQA_GROUP + g, -jnp.sum(sink_term[rows, :]), 0.0)
                dqs.append(jnp.dot(dsb, kh, preferred_element_type=F32) * ATTN_SCALE)
                dks.append(lax.dot_general(dsb, qg, _TN, preferred_element_type=F32))
                dvs.append(lax.dot_general(p.astype(BF16), dog, _TN, preferred_element_type=F32))
            dq_ref[...] = _unstack_groups(dqs)
            dkk = jnp.concatenate(dks, axis=1)
            dvv = jnp.concatenate(dvs, axis=1)
            dk_ref[...] = carry_k[...] + dkk[:ATTN_BLOCK, :]
            dv_ref[...] = carry_v[...] + dvv[:ATTN_BLOCK, :]
            carry_k[...] = dkk[ATTN_BLOCK:, :]
            carry_v[...] = dvv[ATTN_BLOCK:, :]
            ds_ref[...] += dsink

        @pl.when(n == nb)
        def _():
            dk_ref[...] = carry_k[...]
            dv_ref[...] = carry_v[...]

    cur = lambda n: (jnp.minimum(n, nb - 1), 0)
    prev = lambda n: (jnp.clip(n - 1, 0, nb - 1), 0)
    return pl.pallas_call(
        body, name=name, grid=(nb + 1,),
        in_specs=[pl.BlockSpec((ATTN_BLOCK, ATTN_DIM), cur), pl.BlockSpec((ATTN_BLOCK, ATTN_DIM), cur),
                  pl.BlockSpec((ATTN_BLOCK, KV_DIM), prev), pl.BlockSpec((ATTN_BLOCK, KV_DIM), cur),
                  pl.BlockSpec((ATTN_BLOCK, KV_DIM), prev), pl.BlockSpec((ATTN_BLOCK, KV_DIM), cur),
                  pl.BlockSpec(memory_space=pltpu.SMEM), _resident((2, GROUP_ROWS, 2 * ATTN_BLOCK), (0, 0, 0))],
        out_specs=[pl.BlockSpec((ATTN_BLOCK, ATTN_DIM), cur), pl.BlockSpec((ATTN_BLOCK, KV_DIM), prev),
                   pl.BlockSpec((ATTN_BLOCK, KV_DIM), prev), pl.BlockSpec((1, 128), lambda n: (0, 0))],
        out_shape=[_sds((m, ATTN_DIM), F32), _sds((m, KV_DIM), F32), _sds((m, KV_DIM), F32), _sds((1, 128), F32)],
        scratch_shapes=[pltpu.VMEM((ATTN_BLOCK, KV_DIM), F32), pltpu.VMEM((ATTN_BLOCK, KV_DIM), F32)],
        compiler_params=_cp("arbitrary"),
    )(do, qr, kr, kr, vb, vb, sinks, _band_bias())


def _qk_norm_rope_bwd(dout, xv, gain, cos, sin):
    width = xv.shape[1]
    r = lax.rsqrt(_segment_mean(xv * xv, width) + EPS)
    xn = xv * r
    dy = dout * _tile_lanes(cos, width) + _rope_partner(dout * _tile_lanes(sin, width))
    dxn = dy * gain
    dx = r * (dxn - xn * _segment_mean(dxn * xn, width))
    return dx, jnp.sum(dy * xn, axis=0, keepdims=True)


def _fold_heads(v):
    out = v[:, :HEAD_DIM]
    for h in range(1, v.shape[1] // HEAD_DIM):
        out = out + v[:, h * HEAD_DIM:(h + 1) * HEAD_DIM]
    return out


def _qk_bwd(dq, dk, dv, zq, qgain, kgain, cos, sin, name):
    m = zq.shape[0]
    tm = _row_tile(m)

    def body(dq_ref, dk_ref, dv_ref, q_ref, kv_ref, qg_ref, kg_ref, cos_ref, sin_ref, dz_ref, dqg_ref, dkg_ref):
        @pl.when(pl.program_id(0) == 0)
        def _():
            dqg_ref[...] = jnp.zeros_like(dqg_ref)
            dkg_ref[...] = jnp.zeros_like(dkg_ref)

        cos_v, sin_v = cos_ref[...], sin_ref[...]
        dxq, dgq = _qk_norm_rope_bwd(dq_ref[...], q_ref[...], qg_ref[...], cos_v, sin_v)
        dxk, dgk = _qk_norm_rope_bwd(dk_ref[...], kv_ref[:, :KV_DIM], kg_ref[...], cos_v, sin_v)
        dz_ref[:, :ATTN_DIM] = dxq.astype(BF16)
        dz_ref[:, ATTN_DIM:ATTN_DIM + KV_DIM] = dxk.astype(BF16)
        dz_ref[:, ATTN_DIM + KV_DIM:] = dv_ref[...].astype(BF16)
        dqg_ref[...] += _fold_heads(dgq)
        dkg_ref[...] += _fold_heads(dgk)

    row = lambda i: (i, 0)
    return pl.pallas_call(
        body, name=name, grid=(m // tm,),
        in_specs=[pl.BlockSpec((tm, ATTN_DIM), row), pl.BlockSpec((tm, KV_DIM), row), pl.BlockSpec((tm, KV_DIM), row),
                  pl.BlockSpec((tm, ATTN_DIM), lambda i: (i, 1)), pl.BlockSpec((tm, 2 * KV_DIM), lambda i: (i, 4)),
                  _resident((1, ATTN_DIM), (0, 0)), _resident((1, KV_DIM), (0, 0)),
                  pl.BlockSpec((tm, 128), row), pl.BlockSpec((tm, 128), row)],
        out_specs=[pl.BlockSpec((tm, ATTN_DIM + 2 * KV_DIM), row), pl.BlockSpec((1, HEAD_DIM), lambda i: (0, 0)),
                   pl.BlockSpec((1, HEAD_DIM), lambda i: (0, 0))],
        out_shape=[_sds((m, ATTN_DIM + 2 * KV_DIM), BF16), _sds((1, HEAD_DIM), F32), _sds((1, HEAD_DIM), F32)],
        compiler_params=_cp("arbitrary"),
    )(dq, dk, dv, zq, zq, qgain, kgain, cos, sin)


def _merge_fwd(x, p, o, zg, w_pb, w_ab, w_out, name):
    m, d = x.shape
    tm = _row_tile(m)

    def body(x_ref, p_ref, o_ref, zg_ref, wpb_ref, wab_ref, wo_ref, xo_ref, mix_ref):
        a = jnp.dot(p_ref[...], wpb_ref[...], preferred_element_type=F32)
        b = jnp.dot(o_ref[...], wab_ref[...], preferred_element_type=F32)
        mix = (_sigmoid(zg_ref[:, :d]) * a + _sigmoid(zg_ref[:, d:]) * b).astype(BF16)
        mix_ref[...] = mix
        xo_ref[...] = x_ref[...] + jnp.dot(mix, wo_ref[...], preferred_element_type=F32)

    row = lambda i: (i, 0)
    return pl.pallas_call(
        body, name=name, grid=(m // tm,),
        in_specs=[pl.BlockSpec((tm, d), row), pl.BlockSpec((tm, POOL_DIM), row), pl.BlockSpec((tm, ATTN_DIM), row),
                  pl.BlockSpec((tm, 2 * d), row), _resident((POOL_DIM, d), (0, 0)), _resident((ATTN_DIM, d), (0, 0)),
                  _resident((d, d), (0, 0))],
        out_specs=[pl.BlockSpec((tm, d), row)] * 2,
        out_shape=[_sds((m, d), F32), _sds((m, d), BF16)],
        compiler_params=_cp("parallel"),
    )(x, p, o, zg, w_pb, w_ab, w_out)


def _merge_bwd(dy, p, o, zg, w_out, w_pb, w_ab, name):
    m, d = dy.shape
    tm = _row_tile(m)

    def body(dy_ref, p_ref, o_ref, zg_ref, wo_ref, wpb_ref, wab_ref, dyb_ref, da_ref, db_ref, dp_ref, do_ref, dzg_ref):
        dyb = dy_ref[...].astype(BF16)
        dyb_ref[...] = dyb
        dmix = lax.dot_general(dyb, wo_ref[...], _NT, preferred_element_type=F32)
        gp = _sigmoid(zg_ref[:, :d])
        ga = _sigmoid(zg_ref[:, d:])
        da = (dmix * gp).astype(BF16)
        db = (dmix * ga).astype(BF16)
        da_ref[...] = da
        db_ref[...] = db
        a = jnp.dot(p_ref[...], wpb_ref[...], preferred_element_type=F32)
        b = jnp.dot(o_ref[...], wab_ref[...], preferred_element_type=F32)
        dzg_ref[:, :d] = (dmix * a * gp * (1.0 - gp)).astype(BF16)
        dzg_ref[:, d:] = (dmix * b * ga * (1.0 - ga)).astype(BF16)
        dp_ref[...] = lax.dot_general(da, wpb_ref[...], _NT, preferred_element_type=F32)
        do_ref[...] = lax.dot_general(db, wab_ref[...], _NT, preferred_element_type=F32)

    row = lambda i: (i, 0)
    return pl.pallas_call(
        body, name=name, grid=(m // tm,),
        in_specs=[pl.BlockSpec((tm, d), row), pl.BlockSpec((tm, POOL_DIM), row), pl.BlockSpec((tm, ATTN_DIM), row),
                  pl.BlockSpec((tm, 2 * d), row), _resident((d, d), (0, 0)), _resident((POOL_DIM, d), (0, 0)),
                  _resident((ATTN_DIM, d), (0, 0))],
        out_specs=[pl.BlockSpec((tm, d), row), pl.BlockSpec((tm, d), row), pl.BlockSpec((tm, d), row),
                   pl.BlockSpec((tm, POOL_DIM), row), pl.BlockSpec((tm, ATTN_DIM), row), pl.BlockSpec((tm, 2 * d), row)],
        out_shape=[_sds((m, d), BF16), _sds((m, d), BF16), _sds((m, d), BF16), _sds((m, POOL_DIM), F32),
                   _sds((m, ATTN_DIM), F32), _sds((m, 2 * d), BF16)],
        compiler_params=_cp("parallel"),
    )(dy, p, o, zg, w_out, w_pb, w_ab)


def _mix_bwd_x(dy, x, ln, dzp, dzqkv, dzg, w_in, name):
    m, d = dy.shape
    n_in = w_in.shape[0]
    tm = _row_tile(m)

    def body(dy_ref, x_ref, ln_ref, dzp_ref, dzq_ref, dzg_ref, w_ref, dx_ref, dln_ref):
        @pl.when(pl.program_id(0) == 0)
        def _():
            dln_ref[...] = jnp.zeros_like(dln_ref)

        dh = jnp.dot(dzp_ref[...], w_ref[:POOL_DIM, :], preferred_element_type=F32)
        dh += jnp.dot(dzq_ref[...], w_ref[POOL_DIM:QKV_END, :], preferred_element_type=F32)
        dh += jnp.dot(dzg_ref[...], w_ref[QKV_END:, :], preferred_element_type=F32)
        dx, dgain = _rms_bwd(dh, x_ref[...], ln_ref[...])
        dx_ref[...] = dy_ref[...] + dx
        dln_ref[...] += jnp.sum(dgain, axis=0, keepdims=True)

    row = lambda i: (i, 0)
    return pl.pallas_call(
        body, name=name, grid=(m // tm,),
        in_specs=[pl.BlockSpec((tm, d), row), pl.BlockSpec((tm, d), row), _resident((1, d), (0, 0)),
                  pl.BlockSpec((tm, POOL_DIM), row), pl.BlockSpec((tm, QKV_END - POOL_DIM), row),
                  pl.BlockSpec((tm, n_in - QKV_END), row), _resident((n_in, d), (0, 0))],
        out_specs=[pl.BlockSpec((tm, d), row), pl.BlockSpec((1, d), lambda i: (0, 0))],
        out_shape=[_sds((m, d), F32), _sds((1, d), F32)],
        compiler_params=_cp("arbitrary"),
    )(dy, x, ln, dzp, dzqkv, dzg, w_in)


def _loss_head(y, target, name):
    m, d = y.shape
    tm = _row_tile(m)

    def body(y_ref, t_ref, loss_ref, dy_ref):
        @pl.when(pl.program_id(0) == 0)
        def _():
            loss_ref[...] = jnp.zeros_like(loss_ref)

        diff = y_ref[...] - t_ref[...]
        dy_ref[...] = diff * (1.0 / d)
        loss_ref[...] += 0.5 * jnp.sum(jnp.mean(diff * diff, axis=-1, keepdims=True), axis=0, keepdims=True)

    row = lambda i: (i, 0)
    return pl.pallas_call(
        body, name=name, grid=(m // tm,),
        in_specs=[pl.BlockSpec((tm, d), row), pl.BlockSpec((tm, d), row)],
        out_specs=[pl.BlockSpec((1, 1), lambda i: (0, 0)), pl.BlockSpec((tm, d), row)],
        out_shape=[_sds((1, 1), F32), _sds((m, d), F32)],
        compiler_params=_cp("arbitrary"),
    )(y, target)


def _adamw_math(g, w, m, v):
    m2 = ADAM_B1 * m + (1.0 - ADAM_B1) * g
    v2 = ADAM_B2 * v + (1.0 - ADAM_B2) * (g * g)
    m_hat = m2 / (1.0 - ADAM_B1 ** ADAM_STEP)
    v_hat = v2 / (1.0 - ADAM_B2 ** ADAM_STEP)
    delta = -ADAM_LR * (m_hat / (jnp.sqrt(v_hat) + ADAM_EPS) + ADAM_WD * w)
    return delta, m2, v2


def _sum_parts(parts_ref):
    g = parts_ref[0].astype(F32)
    for s in range(1, N_DEV):
        g = g + parts_ref[s].astype(F32)
    return g


def _adamw_sharded(parts, w, m, v, name, after=None):
    n_layers, rows, cols = w.shape
    tr = next(t for t in (512, 256, 128, 64, 32, 16, 8) if rows % t == 0 and t * cols <= ADAMW_BLOCK_ELEMS)
    nr = rows // tr
    deps = [] if after is None else [after]

    def body(*refs):
        part_refs = refs[:n_layers]
        w_ref, m_ref, v_ref = refs[n_layers:n_layers + 3]
        g_out, d_out, m_out, v_out = refs[-4:]
        layer = pl.program_id(0)
        for l in range(n_layers):
            @pl.when(layer == l)
            def _(l=l):
                g = _sum_parts(part_refs[l])
                delta, m2, v2 = _adamw_math(g, w_ref[...], m_ref[...], v_ref[...])
                g_out[...] = g
                d_out[...] = delta
                m_out[...] = m2
                v_out[...] = v2

    def part_map(l):
        return lambda layer, r: (0, jnp.where(layer == l, r, jnp.where(layer < l, 0, nr - 1)), 0)

    wspec = pl.BlockSpec((None, tr, cols), lambda layer, r: (layer, r, 0))
    return pl.pallas_call(
        body, name=name, grid=(n_layers, nr),
        in_specs=([pl.BlockSpec((N_DEV, tr, cols), part_map(l)) for l in range(n_layers)] + [wspec] * 3
                  + [pl.BlockSpec(memory_space=pl.ANY)] * len(deps)),
        out_specs=[wspec] * 4,
        out_shape=[_sds(w.shape, F32)] * 4,
        compiler_params=_cp("arbitrary", "arbitrary"),
    )(*parts, w, m, v, *deps)


def _adamw_packed(parts, w, m, v, name):
    def body(p_ref, w_ref, m_ref, v_ref, g_out, d_out, m_out, v_out):
        g = _sum_parts(p_ref)
        delta, m2, v2 = _adamw_math(g, w_ref[...], m_ref[...], v_ref[...])
        g_out[...] = g
        d_out[...] = delta
        m_out[...] = m2
        v_out[...] = v2

    return pl.pallas_call(
        body, name=name, out_shape=[_sds(w.shape, F32)] * 4,
        compiler_params=pltpu.CompilerParams(vmem_limit_bytes=VMEM_LIMIT_BYTES),
    )(parts, w, m, v)


_SMALL = ("ln_ffn1", "ln_mix", "pool_w", "pool_scale", "q_norm", "k_norm", "sinks", "ln_ffn2")


def _pack_small(arrs):
    rows = []
    for a in arrs:
        flat = a.reshape(-1)
        pad = (-flat.shape[0]) % 1024
        rows.append(jnp.pad(flat, (0, pad)).reshape(-1, 128))
    return jnp.concatenate(rows, axis=0)


def _unpack_small(packed, like):
    out, r0 = [], 0
    for a in like:
        size = a.size
        nrows = (size + 1023) // 1024 * 8
        out.append(packed[r0:r0 + nrows].reshape(-1)[:size].reshape(a.shape))
        r0 += nrows
    return out


def _rope_tables(m):
    pos = jnp.arange(m, dtype=F32)
    inv_freq = ROPE_THETA ** (-jnp.arange(0, ROT_DIM, 2, dtype=F32) / ROT_DIM)
    ang = pos[:, None] * inv_freq[None, :]
    cos8, sin8 = jnp.cos(ang), jnp.sin(ang)
    rest = HEAD_DIM - ROT_DIM
    cos64 = jnp.concatenate([cos8, cos8, jnp.ones((m, rest), F32)], axis=1)
    sin64 = jnp.concatenate([-sin8, sin8, jnp.zeros((m, rest), F32)], axis=1)
    return jnp.tile(cos64, (1, 2)), jnp.tile(sin64, (1, 2))


def _to_shard_major_cols(w):
    k = w.shape[0]
    return w.reshape(k, N_DEV, -1).transpose(1, 0, 2)


def _from_shard_major_cols(w):
    return w.transpose(1, 0, 2).reshape(w.shape[1], -1)


def kernel(x, ln_ffn1, w_ffn1_gu, w_ffn1_down, ln_mix, w_in, pool_w, pool_scale, w_pool_branch, q_norm, k_norm, sinks, w_attn_branch, w_out, ln_ffn2, w_ffn2_gu, w_ffn2_down, loss_target, m_ln_ffn1, m_w_ffn1_gu, m_w_ffn1_down, m_ln_mix, m_w_in, m_pool_w, m_pool_scale, m_w_pool_branch, m_q_norm, m_k_norm, m_sinks, m_w_attn_branch, m_w_out, m_ln_ffn2, m_w_ffn2_gu, m_w_ffn2_down, v_ln_ffn1, v_w_ffn1_gu, v_w_ffn1_down, v_ln_mix, v_w_in, v_pool_w, v_pool_scale, v_w_pool_branch, v_q_norm, v_k_norm, v_sinks, v_w_attn_branch, v_w_out, v_ln_ffn2, v_w_ffn2_gu, v_w_ffn2_down):
    weights = dict(ln_ffn1=ln_ffn1, w_ffn1_gu=w_ffn1_gu, w_ffn1_down=w_ffn1_down, ln_mix=ln_mix, w_in=w_in, pool_w=pool_w,
                   pool_scale=pool_scale, w_pool_branch=w_pool_branch, q_norm=q_norm, k_norm=k_norm, sinks=sinks,
                   w_attn_branch=w_attn_branch, w_out=w_out, ln_ffn2=ln_ffn2, w_ffn2_gu=w_ffn2_gu, w_ffn2_down=w_ffn2_down)
    mom_m = dict(ln_ffn1=m_ln_ffn1, w_ffn1_gu=m_w_ffn1_gu, w_ffn1_down=m_w_ffn1_down, ln_mix=m_ln_mix, w_in=m_w_in,
                 pool_w=m_pool_w, pool_scale=m_pool_scale, w_pool_branch=m_w_pool_branch, q_norm=m_q_norm, k_norm=m_k_norm,
                 sinks=m_sinks, w_attn_branch=m_w_attn_branch, w_out=m_w_out, ln_ffn2=m_ln_ffn2, w_ffn2_gu=m_w_ffn2_gu,
                 w_ffn2_down=m_w_ffn2_down)
    mom_v = dict(ln_ffn1=v_ln_ffn1, w_ffn1_gu=v_w_ffn1_gu, w_ffn1_down=v_w_ffn1_down, ln_mix=v_ln_mix, w_in=v_w_in,
                 pool_w=v_pool_w, pool_scale=v_pool_scale, w_pool_branch=v_w_pool_branch, q_norm=v_q_norm, k_norm=v_k_norm,
                 sinks=v_sinks, w_attn_branch=v_w_attn_branch, w_out=v_w_out, ln_ffn2=v_ln_ffn2, w_ffn2_gu=v_w_ffn2_gu,
                 w_ffn2_down=v_w_ffn2_down)
    order = ("ln_ffn1", "w_ffn1_gu", "w_ffn1_down", "ln_mix", "w_in", "pool_w", "pool_scale", "w_pool_branch", "q_norm",
             "k_norm", "sinks", "w_attn_branch", "w_out", "ln_ffn2", "w_ffn2_gu", "w_ffn2_down")
    big = ("w_ffn1_gu", "w_ffn1_down", "w_in", "w_pool_branch", "w_attn_branch", "w_out", "w_ffn2_gu", "w_ffn2_down")

    transposed = ("w_ffn1_gu", "w_ffn2_gu", "w_in")
    for group in (weights, mom_m, mom_v):
        for k in transposed:
            group[k] = jnp.swapaxes(group[k], 1, 2)

    n_layers = ln_ffn1.shape[0]
    seq, d = x.shape[-2], x.shape[-1]
    xs = x.reshape(seq, d)
    target = loss_target.reshape(seq, d)
    cos, sin = _rope_tables(seq)

    def layer_shards(l):
        return [weights[k][l].astype(BF16) for k in big]

    def layer_weights(l, full):
        g = dict(zip(big, full))
        return dict(
            gu1=g["w_ffn1_gu"].reshape(2, -1, d), down1=g["w_ffn1_down"].reshape(-1, d),
            gu2=g["w_ffn2_gu"].reshape(2, -1, d), down2=g["w_ffn2_down"].reshape(-1, d),
            w_in=g["w_in"].reshape(-1, d), w_pb=_from_shard_major_cols(g["w_pool_branch"]),
            w_ab=_from_shard_major_cols(g["w_attn_branch"]), w_out=g["w_out"].reshape(d, d),
            ln1=ln_ffn1[l][None], ln_mix=ln_mix[l][None], ln2=ln_ffn2[l][None], pool_w=pool_w[l],
            pool_scale=pool_scale[l][None], sinks=sinks[l],
            qgain=jnp.tile(q_norm[l], N_Q_HEADS)[None], kgain=jnp.tile(k_norm[l], N_KV_HEADS)[None])

    gathered = [layer_weights(0, _all_gather_many(layer_shards(0), name="gather_weights_l0"))]
    saved = []
    cur = xs
    for l in range(n_layers):
        lw = gathered[l]
        s = dict(x0=cur)
        in_flight, token = None, None
        if l + 1 < n_layers:
            in_flight, token = _exchange_start(layer_shards(l + 1), scatter=False, name=f"gather_start_l{l + 1}")
        s["h1"], s["gu1"], act1 = _ffn_up(cur, lw["ln1"], lw["gu1"], name=f"ffn1_up_l{l}", after=token)
        s["act1"] = act1
        x1 = _ffn_down(cur, act1, lw["down1"], name=f"ffn1_down_l{l}")
        s["x1"] = x1
        s["h2"], zq, zg = _mix_in(x1, lw["ln_mix"], lw["w_in"], name=f"mix_in_l{l}")
        s["zq"], s["zg"] = zq, zg
        s["d"], s["p"] = _pool_fwd(zq, lw["pool_w"], lw["pool_scale"], name=f"pool_fwd_l{l}")
        s["qr"], s["kr"], s["vb"] = _qk_prep(zq, lw["qgain"], lw["kgain"], cos, sin, name=f"qk_prep_l{l}")
        s["o"] = _attn_fwd(s["qr"], s["kr"], s["vb"], lw["sinks"], name=f"attn_fwd_l{l}")
        x2, s["mix"] = _merge_fwd(x1, s["p"], s["o"], zg, lw["w_pb"], lw["w_ab"], lw["w_out"],
                                                  name=f"merge_fwd_l{l}")
        s["x2"] = x2
        s["h3"], s["gu2"], act2 = _ffn_up(x2, lw["ln2"], lw["gu2"], name=f"ffn2_up_l{l}")
        s["act2"] = act2
        cur = _ffn_down(x2, act2, lw["down2"], name=f"ffn2_down_l{l}")
        saved.append(s)
        if in_flight is not None:
            gathered.append(layer_weights(l + 1, _exchange_wait(in_flight, cur, name=f"gather_wait_l{l + 1}")))

    loss_local, dy = _loss_head(cur, target, name="loss_head")
    loss = lax.psum(loss_local[0, 0], MESH_AXES)

    small_grads = {k: [None] * n_layers for k in _SMALL}
    received = {k: [None] * n_layers for k in big}
    big_late = ("w_ffn1_gu", "w_ffn1_down")
    big_early = tuple(k for k in big if k not in big_late)
    early_in_flight, late_in_flight = [None] * n_layers, [None] * n_layers
    token = None
    for l in reversed(range(n_layers)):
        lw, s = gathered[l], saved[l]
        d_ff = lw["down1"].shape[0]

        def ffn_weight_grads(dyh, dgu, h, act, tag):
            dw_down = _matmul_tn(act[None], dyh[None], name=f"{tag}_dw_down_l{l}", a_chunk=d_ff // 2)
            dw_gu = _matmul_tn(dgu, h[None], name=f"{tag}_dw_gu_l{l}", a_chunk=d_ff // 2)
            return dw_gu.reshape(N_DEV, -1, d), dw_down.reshape(N_DEV, -1, d)

        dyh, dgu, dx2, dln2 = _ffn_bwd(dy, s["x2"], lw["ln2"], s["gu2"], lw["down2"], lw["gu2"],
                                       name=f"ffn2_bwd_l{l}", after=token)
        dw_gu2, dw_down2 = ffn_weight_grads(dyh, dgu, s["h3"], s["act2"], "ffn2")

        dyb, da, db, dp, do, dzg = _merge_bwd(dx2, s["p"], s["o"], s["zg"], lw["w_out"], lw["w_pb"], lw["w_ab"],
                                              name=f"merge_bwd_l{l}")
        dw_out = _matmul_tn(s["mix"][None], dyb[None], name=f"dw_out_l{l}")[0]
        dw_pb = _matmul_tn(s["p"][None], da[None], name=f"dw_pb_l{l}")[0]
        dw_ab = _matmul_tn(s["o"][None], db[None], name=f"dw_ab_l{l}")[0]
        dzp, dpw, dsc = _pool_bwd(dp, s["d"], lw["pool_w"], lw["pool_scale"], name=f"pool_bwd_l{l}")
        dq, dk, dv, dsinks = _attn_bwd(do, s["qr"], s["kr"], s["vb"], lw["sinks"], name=f"attn_bwd_l{l}")
        dzqkv, dqg, dkg = _qk_bwd(dq, dk, dv, s["zq"], lw["qgain"], lw["kgain"], cos, sin, name=f"qk_bwd_l{l}")
        dw_in = jnp.concatenate([_matmul_tn(dzp[None], s["h2"][None], name=f"dw_in_pool_l{l}")[0],
                                 _matmul_tn(dzqkv[None], s["h2"][None], name=f"dw_in_qkv_l{l}")[0],
                                 _matmul_tn(dzg[None], s["h2"][None], name=f"dw_in_gate_l{l}")[0]], axis=0)
        dx1, dlnm = _mix_bwd_x(dx2, s["x1"], lw["ln_mix"], dzp, dzqkv, dzg, lw["w_in"], name=f"mix_bwd_x_l{l}")

        partial = dict(w_in=dw_in.reshape(N_DEV, -1, d), w_pool_branch=_to_shard_major_cols(dw_pb),
                       w_attn_branch=_to_shard_major_cols(dw_ab), w_out=dw_out.reshape(N_DEV, d // N_DEV, d),
                       w_ffn2_gu=dw_gu2, w_ffn2_down=dw_down2)
        early_in_flight[l], token = _exchange_start([partial[k] for k in big_early], scatter=True,
                                                    name=f"grads_early_start_l{l}")

        dyh, dgu, dy, dln1 = _ffn_bwd(dx1, s["x0"], lw["ln1"], s["gu1"], lw["down1"], lw["gu1"],
                                      name=f"ffn1_bwd_l{l}", after=token)
        dw_gu1, dw_down1 = ffn_weight_grads(dyh, dgu, s["h1"], s["act1"], "ffn1")
        late_in_flight[l], token = _exchange_start([dw_gu1, dw_down1], scatter=True, name=f"grads_late_start_l{l}")
        small_grads["ln_ffn1"][l] = dln1[0]
        small_grads["ln_mix"][l] = dlnm[0]
        small_grads["ln_ffn2"][l] = dln2[0]
        small_grads["pool_w"][l] = dpw
        small_grads["pool_scale"][l] = dsc[0]
        small_grads["q_norm"][l] = dqg[0]
        small_grads["k_norm"][l] = dkg[0]
        small_grads["sinks"][l] = dsinks[0, :N_Q_HEADS]

    grad_x = dy.reshape(x.shape)

    small_w = [weights[k] for k in _SMALL]
    packed_g = _pack_small([jnp.stack(small_grads[k]).reshape(weights[k].shape) for k in _SMALL])
    small_in_flight, after = _exchange_start([packed_g], scatter=False, name="small_grads_start")
    for l in reversed(range(n_layers)):
        got = _exchange_wait(early_in_flight[l], after, name=f"grads_early_wait_l{l}")
        after = got[0]
        for k, r in zip(big_early, got):
            received[k][l] = r

    grads, deltas, new_m, new_v = {}, {}, {}, {}

    def adamw(k, after):
        w = weights[k]
        shape2 = (n_layers, -1, w.shape[-1])
        parts = [r.reshape(N_DEV, -1, w.shape[-1]) for r in received[k]]
        outs = _adamw_sharded(parts, w.reshape(shape2), mom_m[k].reshape(shape2), mom_v[k].reshape(shape2),
                              name=f"adamw_{k}", after=after)
        grads[k], deltas[k], new_m[k], new_v[k] = (o.reshape(w.shape) for o in outs)
        return outs[0]

    after = None
    for k in big_early:
        after = adamw(k, after)
    for l in reversed(range(n_layers)):
        got = _exchange_wait(late_in_flight[l], after, name=f"grads_late_wait_l{l}")
        after = got[0]
        for k, r in zip(big_late, got):
            received[k][l] = r
    after = None
    for k in big_late:
        after = adamw(k, after)

    (parts_small,) = _exchange_wait(small_in_flight, after, name="small_grads_wait")
    outs = _adamw_packed(parts_small, _pack_small(small_w), _pack_small([mom_m[k] for k in _SMALL]),
                         _pack_small([mom_v[k] for k in _SMALL]), name="adamw_small")
    for res, o in zip((grads, deltas, new_m, new_v), outs):
        for k, a in zip(_SMALL, _unpack_small(o, small_w)):
            res[k] = a

    for res in (grads, deltas, new_m, new_v):
        for k in transposed:
            res[k] = jnp.swapaxes(res[k], 1, 2)
    return (loss, grad_x, *[grads[k] for k in order], *[deltas[k] for k in order],
            *[new_m[k] for k in order], *[new_v[k] for k in order])
```

```python
import functools

import jax
import jax.numpy as jnp
from jax import lax
from jax.experimental import pallas as pl
from jax.experimental.pallas import tpu as pltpu

F32 = jnp.float32
BF16 = jnp.bfloat16

N_DEV = 8
MESH_AXES = ("x", "y", "c")
EPS = 1e-6

HEAD_DIM = 64
N_Q_HEADS = 8
N_KV_HEADS = 2
GQA_GROUP = N_Q_HEADS // N_KV_HEADS
ATTN_BLOCK = 128
ATTN_SCALE = HEAD_DIM ** -0.5
ROPE_THETA = 500000.0
ROT_DIM = 16
POOL_WINDOWS = (2, 4, 8, 16)
POOL_HALO = 16
GROUP_DIM = 128
POOL_DIM = 512
ATTN_DIM = 512
KV_DIM = 128
QKV_END = POOL_DIM + ATTN_DIM + 2 * KV_DIM

ADAM_LR = 0.001
ADAM_B1 = 0.9
ADAM_B2 = 0.999
ADAM_EPS = 1e-08
ADAM_WD = 0.01
ADAM_STEP = 10

ROW_TILE = 512
TN_ROW_TILE = 2048
FFN_CHUNK = 256
FFN_BWD_ROW_TILE = 256
VMEM_LIMIT_BYTES = 56 << 20
ADAMW_BLOCK_ELEMS = 192 * 1024
NEG_BIG = -1e30

_NT = (((1,), (1,)), ((), ()))
_TN = (((0,), (0,)), ((), ()))


def _cp(*sem):
    return pltpu.CompilerParams(dimension_semantics=sem, vmem_limit_bytes=VMEM_LIMIT_BYTES)


def _resident(block, index):
    return pl.BlockSpec(block, lambda *_: index, pipeline_mode=pl.Buffered(1))


def _row_tile(m):
    return min(ROW_TILE, m)


def _sds(shape, dtype):
    return jax.ShapeDtypeStruct(shape, dtype)


def _mesh_pos():
    return lax.axis_index("x"), lax.axis_index("y"), lax.axis_index("c")


def _all_gather_many(shards, name, after=None):
    n = len(shards)

    deps = [] if after is None else [after]

    def body(*refs):
        ins, outs = refs[:n], refs[n + len(deps):2 * n + len(deps)]
        send_sems, recv_sems, local_sems = refs[2 * n + len(deps):]
        x, y, c = _mesh_pos()
        me, sibling = (x, y, c), (x, y, 1 - c)
        chips = [(1 - x, y), (x, 1 - y), (1 - x, 1 - y)]

        def slot(a, pos):
            return outs[a].at[4 * pos[0] + 2 * pos[1] + pos[2]]

        def copy(a, k, block, to, src=None):
            return pltpu.make_async_remote_copy(
                src_ref=slot(a, block) if src is None else src, dst_ref=slot(a, block),
                send_sem=send_sems.at[a, k], recv_sem=recv_sems.at[a, k],
                device_id=to, device_id_type=pl.DeviceIdType.MESH)

        mine = [pltpu.make_async_copy(ins[a], slot(a, me), local_sems.at[a]) for a in range(n)]
        for cp in mine:
            cp.start()
        first = []
        for a in range(n):
            first.append(copy(a, 0, me, sibling, src=ins[a]))
            for j, chip in enumerate(chips):
                first.append(copy(a, 1 + j, me, (*chip, c), src=ins[a]))
        for cp in first:
            cp.start()
        passed = []
        for j, chip in enumerate(chips):
            for a in range(n):
                copy(a, 1 + j, (*chip, c), me).wait_recv()
                fwd = copy(a, 4 + j, (*chip, c), sibling)
                fwd.start()
                passed.append(fwd)
        for a in range(n):
            copy(a, 0, sibling, me).wait_recv()
        for j, chip in enumerate(chips):
            for a in range(n):
                copy(a, 4 + j, (*chip, 1 - c), me).wait_recv()
        for cp in first + passed:
            cp.wait_send()
        for cp in mine:
            cp.wait()

    any_spec = pl.BlockSpec(memory_space=pl.ANY)
    return pl.pallas_call(
        body, name=name,
        out_shape=[_sds((N_DEV,) + s.shape, s.dtype) for s in shards],
        in_specs=[any_spec] * (n + len(deps)), out_specs=[any_spec] * n,
        scratch_shapes=[pltpu.SemaphoreType.DMA((n, 7)), pltpu.SemaphoreType.DMA((n, 7)),
                        pltpu.SemaphoreType.DMA((n,))],
    )(*shards, *deps)


def _direct_copies(src, land, send_sem, recv_sem, local_sem, scatter):
    x, y, c = _mesh_pos()
    me = 4 * x + 2 * y + c
    local = pltpu.make_async_copy(src.at[me] if scatter else src, land.at[me], local_sem)
    remote = []
    for k in range(1, N_DEV):
        px = 1 - x if k & 4 else x
        py = 1 - y if k & 2 else y
        pc = 1 - c if k & 1 else c
        remote.append(pltpu.make_async_remote_copy(
            src_ref=src.at[4 * px + 2 * py + pc] if scatter else src, dst_ref=land.at[me],
            send_sem=send_sem, recv_sem=recv_sem, device_id=(px, py, pc), device_id_type=pl.DeviceIdType.MESH))
    seven = land.at[pl.ds(0, N_DEV - 1)]
    drain = pltpu.make_async_remote_copy(src_ref=seven, dst_ref=seven, send_sem=send_sem, recv_sem=recv_sem,
                                         device_id=(x, y, c), device_id_type=pl.DeviceIdType.MESH)
    return local, remote, drain


_HBM_SPEC = pl.BlockSpec(memory_space=pltpu.HBM)
_SEM_SPEC = pl.BlockSpec(memory_space=pltpu.SEMAPHORE)
_DATAFLOW = pltpu.SideEffectType.DATAFLOW_SIDE_EFFECTING
_SEMS_PER_ARRAY = 3


def _exchange_start(srcs, scatter, name):
    n = len(srcs)
    n_sems = _SEMS_PER_ARRAY * n
    land_shapes = [s.shape if scatter else (N_DEV,) + s.shape for s in srcs]

    def body(*refs):
        ins, lands, sems = refs[:n], refs[n:2 * n], refs[2 * n:2 * n + n_sems]
        for a in range(n):
            local, remote, _ = _direct_copies(ins[a], lands[a], *sems[3 * a:3 * a + 3], scatter)
            local.start()
            for cp in remote:
                cp.start()
        refs[-1][...] = jnp.zeros_like(refs[-1])

    outs = pl.pallas_call(
        body, name=name,
        out_shape=(*[pltpu.SemaphoreType.DMA(())] * n_sems,
                   *[pltpu.HBM(s.shape, s.dtype) for s in srcs],
                   *[pltpu.HBM(shape, s.dtype) for shape, s in zip(land_shapes, srcs)],
                   _sds((8, 128), F32)),
        in_specs=[_HBM_SPEC] * (2 * n),
        out_specs=(*[_SEM_SPEC] * n_sems, *[_HBM_SPEC] * (2 * n), pl.BlockSpec(memory_space=pltpu.VMEM)),
        input_output_aliases={i: n_sems + i for i in range(2 * n)},
        compiler_params=pltpu.CompilerParams(has_side_effects=_DATAFLOW),
    )(*[pltpu.with_memory_space_constraint(s, pltpu.HBM) for s in srcs],
      *[pltpu.with_memory_space_constraint(lax.empty(shape, s.dtype), pltpu.HBM) for shape, s in zip(land_shapes, srcs)])
    return (outs[:n_sems], outs[n_sems:n_sems + n], outs[n_sems + n:n_sems + 2 * n], scatter), outs[-1]


def _exchange_wait(state, after, name):
    sems, srcs, lands, scatter = state
    n = len(srcs)
    n_sems = len(sems)

    def body(*refs):
        ins, zones, ss = refs[:n], refs[n:2 * n], refs[2 * n:2 * n + n_sems]
        for a in range(n):
            local, _, drain = _direct_copies(ins[a], zones[a], *ss[3 * a:3 * a + 3], scatter)
            drain.wait_send()
            drain.wait_recv()
            local.wait()

    outs = pl.pallas_call(
        body, name=name,
        out_shape=(*[pltpu.HBM(s.shape, s.dtype) for s in srcs], *[pltpu.HBM(z.shape, z.dtype) for z in lands]),
        in_specs=[_HBM_SPEC] * (2 * n) + [_SEM_SPEC] * n_sems + [pl.BlockSpec(memory_space=pl.ANY)],
        out_specs=[_HBM_SPEC] * (2 * n),
        input_output_aliases={i: i for i in range(2 * n)},
        compiler_params=pltpu.CompilerParams(has_side_effects=_DATAFLOW),
    )(*srcs, *lands, *sems, after)
    return outs[n:]


def _rms_fwd(xv, gain):
    r = lax.rsqrt(jnp.mean(xv * xv, axis=-1, keepdims=True) + EPS)
    return xv * r * gain


def _rms_bwd(dh, xv, gain):
    r = lax.rsqrt(jnp.mean(xv * xv, axis=-1, keepdims=True) + EPS)
    xn = xv * r
    dxn = dh * gain
    dx = r * (dxn - xn * jnp.mean(dxn * xn, axis=-1, keepdims=True))
    return dx, dh * xn


def _sigmoid(v):
    return 0.5 * jnp.tanh(0.5 * v) + 0.5


def _silu_parts(g):
    s = _sigmoid(g)
    return g * s, s * (1.0 + g * (1.0 - s))


def _segment_mean(v, width):
    r = lax.broadcasted_iota(jnp.int32, (width, width), 0) >> 6
    c = lax.broadcasted_iota(jnp.int32, (width, width), 1) >> 6
    bd = (r == c).astype(BF16)
    hi = v.astype(BF16)
    lo = (v - hi.astype(F32)).astype(BF16)
    total = jnp.dot(hi, bd, preferred_element_type=F32) + jnp.dot(lo, bd, preferred_element_type=F32)
    return total * (1.0 / HEAD_DIM)


def _rope_partner(v):
    width = v.shape[1]
    half = ROT_DIM // 2
    lane = lax.broadcasted_iota(jnp.int32, v.shape, 1) & (HEAD_DIM - 1)
    up = jnp.where(lane < ROT_DIM, pltpu.roll(v, half, 1), 0.0)
    return jnp.where(lane < half, pltpu.roll(v, width - half, 1), up)


def _tile_lanes(t, width):
    return t if width == t.shape[1] else jnp.tile(t, (1, width // t.shape[1]))


def _ffn_chunks(f):
    return [slice(j * FFN_CHUNK, (j + 1) * FFN_CHUNK) for j in range(f // FFN_CHUNK)]


def _ffn_up(x, ln, wgu, name, after=None):
    m, d = x.shape
    f = wgu.shape[1]
    tm = _row_tile(m)
    deps = [] if after is None else [after]

    def body(*refs):
        x_ref, ln_ref, w_ref = refs[:3]
        h_ref, gu_ref, a_ref = refs[-3:]
        h = _rms_fwd(x_ref[...], ln_ref[...]).astype(BF16)
        h_ref[...] = h
        for cols in _ffn_chunks(f):
            g = lax.dot_general(h, w_ref[0, cols, :], _NT, preferred_element_type=F32)
            u = lax.dot_general(h, w_ref[1, cols, :], _NT, preferred_element_type=F32)
            gu_ref[0, :, cols] = g.astype(BF16)
            gu_ref[1, :, cols] = u.astype(BF16)
            a_ref[:, cols] = (g * _sigmoid(g) * u).astype(BF16)

    return pl.pallas_call(
        body, name=name, grid=(m // tm,),
        in_specs=[pl.BlockSpec((tm, d), lambda i: (i, 0)), _resident((1, d), (0, 0)),
                  _resident((2, f, d), (0, 0, 0))] + [pl.BlockSpec(memory_space=pl.ANY)] * len(deps),
        out_specs=[pl.BlockSpec((tm, d), lambda i: (i, 0)), pl.BlockSpec((2, tm, f), lambda i: (0, i, 0)),
                   pl.BlockSpec((tm, f), lambda i: (i, 0))],
        out_shape=[_sds((m, d), BF16), _sds((2, m, f), BF16), _sds((m, f), BF16)],
        compiler_params=_cp("parallel"),
    )(x, ln, wgu, *deps)


def _ffn_down(x, act, wd, name):
    m, d = x.shape
    f = act.shape[-1]
    tm = _row_tile(m)

    def body(x_ref, a_ref, w_ref, o_ref):
        o_ref[...] = x_ref[...] + 0.5 * jnp.dot(a_ref[...], w_ref[...], preferred_element_type=F32)

    return pl.pallas_call(
        body, name=name, grid=(m // tm,),
        in_specs=[pl.BlockSpec((tm, d), lambda i: (i, 0)), pl.BlockSpec((tm, f), lambda i: (i, 0)),
                  _resident((f, d), (0, 0))],
        out_specs=pl.BlockSpec((tm, d), lambda i: (i, 0)),
        out_shape=_sds((m, d), F32),
        compiler_params=_cp("parallel"),
    )(x, act, wd)


def _ffn_bwd(dy, x, ln, gu, wd, wgu, name, after=None):
    m, d = dy.shape
    f = gu.shape[-1]
    tm = min(FFN_BWD_ROW_TILE, m)
    deps = [] if after is None else [after]

    def body(*refs):
        dy_ref, x_ref, ln_ref, gu_ref, wd_ref, wgu_ref = refs[:6]
        dyh_ref, dgu_ref, dx_ref, dln_ref = refs[-4:]

        @pl.when(pl.program_id(0) == 0)
        def _():
            dln_ref[...] = jnp.zeros_like(dln_ref)

        dyh = (0.5 * dy_ref[...]).astype(BF16)
        dyh_ref[...] = dyh
        dgs, dus = [], []
        for cols in _ffn_chunks(f):
            da = lax.dot_general(dyh, wd_ref[cols, :], _NT, preferred_element_type=F32)
            g = gu_ref[0, :, cols].astype(F32)
            u = gu_ref[1, :, cols].astype(F32)
            silu, dsilu = _silu_parts(g)
            dgs.append((da * u * dsilu).astype(BF16))
            dus.append((da * silu).astype(BF16))
            dgu_ref[0, :, cols] = dgs[-1]
            dgu_ref[1, :, cols] = dus[-1]
        dh = jnp.dot(jnp.concatenate(dgs, axis=1), wgu_ref[0], preferred_element_type=F32)
        dh += jnp.dot(jnp.concatenate(dus, axis=1), wgu_ref[1], preferred_element_type=F32)
        dx, dgain = _rms_bwd(dh, x_ref[...], ln_ref[...])
        dx_ref[...] = dy_ref[...] + dx
        dln_ref[...] += jnp.sum(dgain, axis=0, keepdims=True)

    row = lambda i: (i, 0)
    return pl.pallas_call(
        body, name=name, grid=(m // tm,),
        in_specs=[pl.BlockSpec((tm, d), row), pl.BlockSpec((tm, d), row), _resident((1, d), (0, 0)),
                  pl.BlockSpec((2, tm, f), lambda i: (0, i, 0)), _resident((f, d), (0, 0)),
                  _resident((2, f, d), (0, 0, 0))] + [pl.BlockSpec(memory_space=pl.ANY)] * len(deps),
        out_specs=[pl.BlockSpec((tm, d), row), pl.BlockSpec((2, tm, f), lambda i: (0, i, 0)),
                   pl.BlockSpec((tm, d), row), pl.BlockSpec((1, d), lambda i: (0, 0))],
        out_shape=[_sds((m, d), BF16), _sds((2, m, f), BF16), _sds((m, d), F32), _sds((1, d), F32)],
        compiler_params=_cp("arbitrary"),
    )(dy, x, ln, gu, wd, wgu, *deps)


def _matmul_tn(a, b, name, a_chunk=None, out_dtype=BF16):
    ja, m, k = a.shape
    jb, _, n = b.shape
    nj = max(ja, jb)
    kc = k if a_chunk is None else a_chunk
    tm = min(TN_ROW_TILE, m)
    nm = m // tm

    def body(a_ref, b_ref, o_ref, acc):
        step = pl.program_id(2)

        @pl.when(step == 0)
        def _():
            acc[...] = jnp.zeros_like(acc)

        acc[...] += lax.dot_general(a_ref[...], b_ref[...], _TN, preferred_element_type=F32)

        @pl.when(step == nm - 1)
        def _():
            o_ref[...] = acc[...].astype(o_ref.dtype)

    return pl.pallas_call(
        body, name=name, grid=(nj, k // kc, nm),
        in_specs=[pl.BlockSpec((None, tm, kc), (lambda j, c, s: (j, s, c)) if ja > 1 else (lambda j, c, s: (0, s, c))),
                  pl.BlockSpec((None, tm, n), (lambda j, c, s: (j, s, 0)) if jb > 1 else (lambda j, c, s: (0, s, 0)))],
        out_specs=pl.BlockSpec((None, kc, n), lambda j, c, s: (j, c, 0)),
        out_shape=_sds((nj, k, n), out_dtype),
        scratch_shapes=[pltpu.VMEM((kc, n), F32)],
        compiler_params=_cp("parallel", "parallel", "arbitrary"),
    )(a, b)


def _mix_in(x, ln, w_in, name):
    m, d = x.shape
    n_in = w_in.shape[0]
    tm = _row_tile(m)

    def body(x_ref, ln_ref, w_ref, h_ref, zq_ref, zg_ref):
        h = _rms_fwd(x_ref[...], ln_ref[...]).astype(BF16)
        h_ref[...] = h
        zq_ref[...] = lax.dot_general(h, w_ref[:QKV_END, :], _NT, preferred_element_type=F32)
        zg_ref[...] = lax.dot_general(h, w_ref[QKV_END:, :], _NT, preferred_element_type=F32).astype(BF16)

    return pl.pallas_call(
        body, name=name, grid=(m // tm,),
        in_specs=[pl.BlockSpec((tm, d), lambda i: (i, 0)), _resident((1, d), (0, 0)), _resident((n_in, d), (0, 0))],
        out_specs=[pl.BlockSpec((tm, d), lambda i: (i, 0)), pl.BlockSpec((tm, QKV_END), lambda i: (i, 0)),
                   pl.BlockSpec((tm, n_in - QKV_END), lambda i: (i, 0))],
        out_shape=[_sds((m, d), BF16), _sds((m, QKV_END), F32), _sds((m, n_in - QKV_END), BF16)],
        compiler_params=_cp("parallel"),
    )(x, ln, w_in)


def _pool_fwd(zq, pool_w, scale, name):
    m = zq.shape[0]
    tm = _row_tile(m)
    halo_blocks = tm // POOL_HALO

    def body(zc_ref, zh_ref, pw_ref, sc_ref, d_ref, p_ref):
        i = pl.program_id(0)
        halo = jnp.where(i > 0, zh_ref[...], 0.0)
        ext = jnp.concatenate([halo, zc_ref[...]], axis=0)
        t = i * tm + lax.broadcasted_iota(jnp.int32, (tm, 1), 0)
        for g, w in enumerate(POOL_WINDOWS):
            lanes = slice(g * GROUP_DIM, (g + 1) * GROUP_DIM)
            e = ext[:, lanes]
            s, k = e, 1
            while k < w:
                s = s + pltpu.roll(s, k, 0)
                k *= 2
            cnt = jnp.minimum(t + 1, w).astype(F32)
            dg = (s[POOL_HALO:, :] / cnt - e[POOL_HALO:, :]).astype(BF16)
            y = jnp.dot(dg, pw_ref[g].astype(BF16), preferred_element_type=F32)
            d_ref[:, lanes] = dg
            p_ref[:, lanes] = (y * sc_ref[:, lanes]).astype(BF16)

    return pl.pallas_call(
        body, name=name, grid=(m // tm,),
        in_specs=[pl.BlockSpec((tm, POOL_DIM), lambda i: (i, 0)),
                  pl.BlockSpec((POOL_HALO, POOL_DIM), lambda i: (jnp.maximum(i * halo_blocks - 1, 0), 0)),
                  _resident((4, GROUP_DIM, GROUP_DIM), (0, 0, 0)), _resident((1, POOL_DIM), (0, 0))],
        out_specs=[pl.BlockSpec((tm, POOL_DIM), lambda i: (i, 0)), pl.BlockSpec((tm, POOL_DIM), lambda i: (i, 0))],
        out_shape=[_sds((m, POOL_DIM), BF16), _sds((m, POOL_DIM), BF16)],
        compiler_params=_cp("parallel"),
    )(zq, zq, pool_w, scale)


def _pool_bwd(dp, d, pool_w, scale, name):
    m = dp.shape[0]
    tm = _row_tile(m)
    nb = m // tm
    halo_blocks = tm // POOL_HALO
    rows = tm + POOL_HALO

    def body(dpc_ref, dph_ref, d_ref, pw_ref, sc_ref, du_ref, dpw_ref, dsc_ref):
        i = pl.program_id(0)

        @pl.when(i == 0)
        def _():
            dpw_ref[...] = jnp.zeros_like(dpw_ref)
            dsc_ref[...] = jnp.zeros_like(dsc_ref)

        halo = jnp.where(i < nb - 1, dph_ref[...], 0.0)
        dpc = dpc_ref[...]
        ext = jnp.concatenate([dpc, halo], axis=0)
        t = i * tm + lax.broadcasted_iota(jnp.int32, (rows, 1), 0)
        for g, w in enumerate(POOL_WINDOWS):
            lanes = slice(g * GROUP_DIM, (g + 1) * GROUP_DIM)
            pwb = pw_ref[g].astype(BF16)
            dyb = (ext[:, lanes] * sc_ref[:, lanes]).astype(BF16)
            dd = lax.dot_general(dyb, pwb, _NT, preferred_element_type=F32)
            cnt = jnp.minimum(t + 1, w).astype(F32)
            s, k = dd / cnt, 1
            while k < w:
                s = s + pltpu.roll(s, rows - k, 0)
                k *= 2
            du_ref[:, lanes] = (s[:tm, :] - dd[:tm, :]).astype(BF16)
            dcur = d_ref[:, lanes]
            y = jnp.dot(dcur, pwb, preferred_element_type=F32)
            dsc_ref[:, lanes] += jnp.sum(dpc[:, lanes] * y, axis=0, keepdims=True)
            dpw_ref[g] += lax.dot_general(dcur, dyb[:tm, :], _TN, preferred_element_type=F32)

    return pl.pallas_call(
        body, name=name, grid=(nb,),
        in_specs=[pl.BlockSpec((tm, POOL_DIM), lambda i: (i, 0)),
                  pl.BlockSpec((POOL_HALO, POOL_DIM), lambda i: (jnp.minimum((i + 1) * halo_blocks, nb * halo_blocks - 1), 0)),
                  pl.BlockSpec((tm, POOL_DIM), lambda i: (i, 0)),
                  _resident((4, GROUP_DIM, GROUP_DIM), (0, 0, 0)), _resident((1, POOL_DIM), (0, 0))],
        out_specs=[pl.BlockSpec((tm, POOL_DIM), lambda i: (i, 0)),
                   pl.BlockSpec((4, GROUP_DIM, GROUP_DIM), lambda i: (0, 0, 0)),
                   pl.BlockSpec((1, POOL_DIM), lambda i: (0, 0))],
        out_shape=[_sds((m, POOL_DIM), BF16), _sds((4, GROUP_DIM, GROUP_DIM), F32), _sds((1, POOL_DIM), F32)],
        compiler_params=_cp("arbitrary"),
    )(dp, dp, d, pool_w, scale)


def _qk_norm_rope(xv, gain, cos, sin):
    width = xv.shape[1]
    r = lax.rsqrt(_segment_mean(xv * xv, width) + EPS)
    y = xv * r * gain
    return y * _tile_lanes(cos, width) + _rope_partner(y) * _tile_lanes(sin, width)


def _qk_prep(zq, qgain, kgain, cos, sin, name):
    m = zq.shape[0]
    tm = _row_tile(m)

    def body(q_ref, kv_ref, qg_ref, kg_ref, cos_ref, sin_ref, qr_ref, kr_ref, v_ref):
        cos_v, sin_v = cos_ref[...], sin_ref[...]
        qr_ref[...] = (_qk_norm_rope(q_ref[...], qg_ref[...], cos_v, sin_v) * ATTN_SCALE).astype(BF16)
        kv = kv_ref[...]
        kr_ref[...] = _qk_norm_rope(kv[:, :KV_DIM], kg_ref[...], cos_v, sin_v).astype(BF16)
        v_ref[...] = kv[:, KV_DIM:].astype(BF16)

    return pl.pallas_call(
        body, name=name, grid=(m // tm,),
        in_specs=[pl.BlockSpec((tm, ATTN_DIM), lambda i: (i, 1)), pl.BlockSpec((tm, 2 * KV_DIM), lambda i: (i, 4)),
                  _resident((1, ATTN_DIM), (0, 0)), _resident((1, KV_DIM), (0, 0)),
                  pl.BlockSpec((tm, 128), lambda i: (i, 0)), pl.BlockSpec((tm, 128), lambda i: (i, 0))],
        out_specs=[pl.BlockSpec((tm, ATTN_DIM), lambda i: (i, 0)), pl.BlockSpec((tm, KV_DIM), lambda i: (i, 0)),
                   pl.BlockSpec((tm, KV_DIM), lambda i: (i, 0))],
        out_shape=[_sds((m, ATTN_DIM), BF16), _sds((m, KV_DIM), BF16), _sds((m, KV_DIM), BF16)],
        compiler_params=_cp("parallel"),
    )(zq, zq, qgain, kgain, cos, sin)


STACK_ROWS = N_Q_HEADS * ATTN_BLOCK


def _band_bias():
    qi = jnp.arange(STACK_ROWS)[:, None] % ATTN_BLOCK
    ki = jnp.arange(2 * ATTN_BLOCK)[None, :]
    diff = qi + ATTN_BLOCK - ki
    band = (diff >= 0) & (diff < ATTN_BLOCK)
    first = band & (ki >= ATTN_BLOCK)
    return jnp.where(jnp.stack([first, band]), 0.0, NEG_BIG).astype(F32)


def _stack_heads(v):
    zeros = jnp.zeros((ATTN_BLOCK, HEAD_DIM), v.dtype)
    rows = []
    for h in range(N_Q_HEADS):
        qh = v[:, h * HEAD_DIM:(h + 1) * HEAD_DIM]
        rows.append(jnp.concatenate([qh, zeros] if h < GQA_GROUP else [zeros, qh], axis=1))
    return jnp.concatenate(rows, axis=0)


def _unstack_heads(stacked):
    parts = []
    for h in range(N_Q_HEADS):
        lanes = slice(0, HEAD_DIM) if h < GQA_GROUP else slice(HEAD_DIM, 2 * HEAD_DIM)
        parts.append(stacked[h * ATTN_BLOCK:(h + 1) * ATTN_BLOCK, lanes])
    return jnp.concatenate(parts, axis=1)


def _stacked_sinks(sk_ref):
    row_head = lax.broadcasted_iota(jnp.int32, (STACK_ROWS, 1), 0) >> 7
    col = jnp.full((STACK_ROWS, 1), sk_ref[0], F32)
    for h in range(1, N_Q_HEADS):
        col = jnp.where(row_head == h, sk_ref[h], col)
    return col


def _head_probs(qh, kh, bias, sink):
    s = lax.dot_general(qh, kh, _NT, preferred_element_type=F32) + bias
    mx = jnp.maximum(jnp.max(s, axis=-1, keepdims=True), sink)
    p = jnp.exp(s - mx)
    es = jnp.exp(sink - mx)
    inv = 1.0 / (jnp.sum(p, axis=-1, keepdims=True) + es)
    return p * inv, es * inv


def _attn_fwd(qr, kr, vb, sinks, name):
    m = qr.shape[0]
    nb = m // ATTN_BLOCK

    def body(q_ref, kp_ref, kc_ref, vp_ref, vc_ref, sk_ref, bias_ref, o_ref):
        bias = bias_ref[jnp.minimum(pl.program_id(0), 1)]
        qv = q_ref[...]
        kk = jnp.concatenate([kp_ref[...], kc_ref[...]], axis=0)
        vv = jnp.concatenate([vp_ref[...], vc_ref[...]], axis=0)
        p, _ = _head_probs(_stack_heads(qv), kk, bias, _stacked_sinks(sk_ref))
        o_ref[...] = _unstack_heads(jnp.dot(p.astype(BF16), vv, preferred_element_type=F32)).astype(BF16)

    prev = lambda n: (jnp.maximum(n - 1, 0), 0)
    cur = lambda n: (n, 0)
    return pl.pallas_call(
        body, name=name, grid=(nb,),
        in_specs=[pl.BlockSpec((ATTN_BLOCK, ATTN_DIM), cur),
                  pl.BlockSpec((ATTN_BLOCK, KV_DIM), prev), pl.BlockSpec((ATTN_BLOCK, KV_DIM), cur),
                  pl.BlockSpec((ATTN_BLOCK, KV_DIM), prev), pl.BlockSpec((ATTN_BLOCK, KV_DIM), cur),
                  pl.BlockSpec(memory_space=pltpu.SMEM), _resident((2, STACK_ROWS, 2 * ATTN_BLOCK), (0, 0, 0))],
        out_specs=pl.BlockSpec((ATTN_BLOCK, ATTN_DIM), cur),
        out_shape=_sds((m, ATTN_DIM), BF16),
        compiler_params=_cp("parallel"),
    )(qr, kr, kr, vb, vb, sinks, _band_bias())


def _attn_bwd(do, qr, kr, vb, sinks, name):
    m = qr.shape[0]
    nb = m // ATTN_BLOCK

    def body(do_ref, q_ref, kp_ref, kc_ref, vp_ref, vc_ref, sk_ref, bias_ref, dq_ref, dk_ref, dv_ref, ds_ref,
             carry_k, carry_v):
        n = pl.program_id(0)

        @pl.when(n == 0)
        def _():
            carry_k[...] = jnp.zeros_like(carry_k)
            carry_v[...] = jnp.zeros_like(carry_v)
            ds_ref[...] = jnp.zeros_like(ds_ref)

        @pl.when(n < nb)
        def _():
            bias = bias_ref[jnp.minimum(n, 1)]
            qv = q_ref[...]
            dov = do_ref[...]
            kk = jnp.concatenate([kp_ref[...], kc_ref[...]], axis=0)
            vv = jnp.concatenate([vp_ref[...], vc_ref[...]], axis=0)
            lane = lax.broadcasted_iota(jnp.int32, (1, 128), 1)
            qs = _stack_heads(qv)
            dos = _stack_heads(dov.astype(BF16))
            p, ps = _head_probs(qs, kk, bias, _stacked_sinks(sk_ref))
            dpr = lax.dot_general(dos, vv, _NT, preferred_element_type=F32)
            delta = jnp.sum(p * dpr, axis=-1, keepdims=True)
            dsb = (p * (dpr - delta)).astype(BF16)
            sink_term = ps * delta
            dsink = jnp.zeros((1, 128), F32)
            for h in range(N_Q_HEADS):
                rows = slice(h * ATTN_BLOCK, (h + 1) * ATTN_BLOCK)
                dsink = dsink + jnp.where(lane == h, -jnp.sum(sink_term[rows, :]), 0.0)
            dq_ref[...] = _unstack_heads(jnp.dot(dsb, kk, preferred_element_type=F32) * ATTN_SCALE)
            dkk = lax.dot_general(dsb, qs, _TN, preferred_element_type=F32)
            dvv = lax.dot_general(p.astype(BF16), dos, _TN, preferred_element_type=F32)
            dk_ref[...] = carry_k[...] + dkk[:ATTN_BLOCK, :]
            dv_ref[...] = carry_v[...] + dvv[:ATTN_BLOCK, :]
            carry_k[...] = dkk[ATTN_BLOCK:, :]
            carry_v[...] = dvv[ATTN_BLOCK:, :]
            ds_ref[...] += dsink

        @pl.when(n == nb)
        def _():
            dk_ref[...] = carry_k[...]
            dv_ref[...] = carry_v[...]

    cur = lambda n: (jnp.minimum(n, nb - 1), 0)
    prev = lambda n: (jnp.clip(n - 1, 0, nb - 1), 0)
    return pl.pallas_call(
        body, name=name, grid=(nb + 1,),
        in_specs=[pl.BlockSpec((ATTN_BLOCK, ATTN_DIM), cur), pl.BlockSpec((ATTN_BLOCK, ATTN_DIM), cur),
                  pl.BlockSpec((ATTN_BLOCK, KV_DIM), prev), pl.BlockSpec((ATTN_BLOCK, KV_DIM), cur),
                  pl.BlockSpec((ATTN_BLOCK, KV_DIM), prev), pl.BlockSpec((ATTN_BLOCK, KV_DIM), cur),
                  pl.BlockSpec(memory_space=pltpu.SMEM), _resident((2, STACK_ROWS, 2 * ATTN_BLOCK), (0, 0, 0))],
        out_specs=[pl.BlockSpec((ATTN_BLOCK, ATTN_DIM), cur), pl.BlockSpec((ATTN_BLOCK, KV_DIM), prev),
                   pl.BlockSpec((ATTN_BLOCK, KV_DIM), prev), pl.BlockSpec((1, 128), lambda n: (0, 0))],
        out_shape=[_sds((m, ATTN_DIM), F32), _sds((m, KV_DIM), F32), _sds((m, KV_DIM), F32), _sds((1, 128), F32)],
        scratch_shapes=[pltpu.VMEM((ATTN_BLOCK, KV_DIM), F32), pltpu.VMEM((ATTN_BLOCK, KV_DIM), F32)],
        compiler_params=_cp("arbitrary"),
    )(do, qr, kr, kr, vb, vb, sinks, _band_bias())


def _qk_norm_rope_bwd(dout, xv, gain, cos, sin):
    width = xv.shape[1]
    r = lax.rsqrt(_segment_mean(xv * xv, width) + EPS)
    xn = xv * r
    dy = dout * _tile_lanes(cos, width) + _rope_partner(dout * _tile_lanes(sin, width))
    dxn = dy * gain
    dx = r * (dxn - xn * _segment_mean(dxn * xn, width))
    return dx, jnp.sum(dy * xn, axis=0, keepdims=True)


def _fold_heads(v):
    out = v[:, :HEAD_DIM]
    for h in range(1, v.shape[1] // HEAD_DIM):
        out = out + v[:, h * HEAD_DIM:(h + 1) * HEAD_DIM]
    return out


def _qk_bwd(dq, dk, dv, zq, qgain, kgain, cos, sin, name):
    m = zq.shape[0]
    tm = _row_tile(m)

    def body(dq_ref, dk_ref, dv_ref, q_ref, kv_ref, qg_ref, kg_ref, cos_ref, sin_ref, dz_ref, dqg_ref, dkg_ref):
        @pl.when(pl.program_id(0) == 0)
        def _():
            dqg_ref[...] = jnp.zeros_like(dqg_ref)
            dkg_ref[...] = jnp.zeros_like(dkg_ref)

        cos_v, sin_v = cos_ref[...], sin_ref[...]
        dxq, dgq = _qk_norm_rope_bwd(dq_ref[...], q_ref[...], qg_ref[...], cos_v, sin_v)
        dxk, dgk = _qk_norm_rope_bwd(dk_ref[...], kv_ref[:, :KV_DIM], kg_ref[...], cos_v, sin_v)
        dz_ref[:, :ATTN_DIM] = dxq.astype(BF16)
        dz_ref[:, ATTN_DIM:ATTN_DIM + KV_DIM] = dxk.astype(BF16)
        dz_ref[:, ATTN_DIM + KV_DIM:] = dv_ref[...].astype(BF16)
        dqg_ref[...] += _fold_heads(dgq)
        dkg_ref[...] += _fold_heads(dgk)

    row = lambda i: (i, 0)
    return pl.pallas_call(
        body, name=name, grid=(m // tm,),
        in_specs=[pl.BlockSpec((tm, ATTN_DIM), row), pl.BlockSpec((tm, KV_DIM), row), pl.BlockSpec((tm, KV_DIM), row),
                  pl.BlockSpec((tm, ATTN_DIM), lambda i: (i, 1)), pl.BlockSpec((tm, 2 * KV_DIM), lambda i: (i, 4)),
                  _resident((1, ATTN_DIM), (0, 0)), _resident((1, KV_DIM), (0, 0)),
                  pl.BlockSpec((tm, 128), row), pl.BlockSpec((tm, 128), row)],
        out_specs=[pl.BlockSpec((tm, ATTN_DIM + 2 * KV_DIM), row), pl.BlockSpec((1, HEAD_DIM), lambda i: (0, 0)),
                   pl.BlockSpec((1, HEAD_DIM), lambda i: (0, 0))],
        out_shape=[_sds((m, ATTN_DIM + 2 * KV_DIM), BF16), _sds((1, HEAD_DIM), F32), _sds((1, HEAD_DIM), F32)],
        compiler_params=_cp("arbitrary"),
    )(dq, dk, dv, zq, zq, qgain, kgain, cos, sin)


def _merge_fwd(x, p, o, zg, w_pb, w_ab, w_out, name):
    m, d = x.shape
    tm = _row_tile(m)

    def body(x_ref, p_ref, o_ref, zg_ref, wpb_ref, wab_ref, wo_ref, xo_ref, mix_ref):
        a = jnp.dot(p_ref[...], wpb_ref[...], preferred_element_type=F32)
        b = jnp.dot(o_ref[...], wab_ref[...], preferred_element_type=F32)
        mix = (_sigmoid(zg_ref[:, :d].astype(F32)) * a + _sigmoid(zg_ref[:, d:].astype(F32)) * b).astype(BF16)
        mix_ref[...] = mix
        xo_ref[...] = x_ref[...] + jnp.dot(mix, wo_ref[...], preferred_element_type=F32)

    row = lambda i: (i, 0)
    return pl.pallas_call(
        body, name=name, grid=(m // tm,),
        in_specs=[pl.BlockSpec((tm, d), row), pl.BlockSpec((tm, POOL_DIM), row), pl.BlockSpec((tm, ATTN_DIM), row),
                  pl.BlockSpec((tm, 2 * d), row), _resident((POOL_DIM, d), (0, 0)), _resident((ATTN_DIM, d), (0, 0)),
                  _resident((d, d), (0, 0))],
        out_specs=[pl.BlockSpec((tm, d), row)] * 2,
        out_shape=[_sds((m, d), F32), _sds((m, d), BF16)],
        compiler_params=_cp("parallel"),
    )(x, p, o, zg, w_pb, w_ab, w_out)


def _merge_bwd(dy, p, o, zg, w_out, w_pb, w_ab, name):
    m, d = dy.shape
    tm = _row_tile(m)

    def body(dy_ref, p_ref, o_ref, zg_ref, wo_ref, wpb_ref, wab_ref, dyb_ref, da_ref, db_ref, dp_ref, do_ref, dzg_ref):
        dyb = dy_ref[...].astype(BF16)
        dyb_ref[...] = dyb
        dmix = lax.dot_general(dyb, wo_ref[...], _NT, preferred_element_type=F32)
        gp = _sigmoid(zg_ref[:, :d].astype(F32))
        ga = _sigmoid(zg_ref[:, d:].astype(F32))
        da = (dmix * gp).astype(BF16)
        db = (dmix * ga).astype(BF16)
        da_ref[...] = da
        db_ref[...] = db
        a = jnp.dot(p_ref[...], wpb_ref[...], preferred_element_type=F32)
        b = jnp.dot(o_ref[...], wab_ref[...], preferred_element_type=F32)
        dzg_ref[:, :d] = (dmix * a * gp * (1.0 - gp)).astype(BF16)
        dzg_ref[:, d:] = (dmix * b * ga * (1.0 - ga)).astype(BF16)
        dp_ref[...] = lax.dot_general(da, wpb_ref[...], _NT, preferred_element_type=F32)
        do_ref[...] = lax.dot_general(db, wab_ref[...], _NT, preferred_element_type=F32).astype(BF16)

    row = lambda i: (i, 0)
    return pl.pallas_call(
        body, name=name, grid=(m // tm,),
        in_specs=[pl.BlockSpec((tm, d), row), pl.BlockSpec((tm, POOL_DIM), row), pl.BlockSpec((tm, ATTN_DIM), row),
                  pl.BlockSpec((tm, 2 * d), row), _resident((d, d), (0, 0)), _resident((POOL_DIM, d), (0, 0)),
                  _resident((ATTN_DIM, d), (0, 0))],
        out_specs=[pl.BlockSpec((tm, d), row), pl.BlockSpec((tm, d), row), pl.BlockSpec((tm, d), row),
                   pl.BlockSpec((tm, POOL_DIM), row), pl.BlockSpec((tm, ATTN_DIM), row), pl.BlockSpec((tm, 2 * d), row)],
        out_shape=[_sds((m, d), BF16), _sds((m, d), BF16), _sds((m, d), BF16), _sds((m, POOL_DIM), F32),
                   _sds((m, ATTN_DIM), BF16), _sds((m, 2 * d), BF16)],
        compiler_params=_cp("parallel"),
    )(dy, p, o, zg, w_out, w_pb, w_ab)


def _mix_bwd_x(dy, x, ln, dzp, dzqkv, dzg, w_in, name):
    m, d = dy.shape
    n_in = w_in.shape[0]
    tm = _row_tile(m)

    def body(dy_ref, x_ref, ln_ref, dzp_ref, dzq_ref, dzg_ref, w_ref, dx_ref, dln_ref):
        @pl.when(pl.program_id(0) == 0)
        def _():
            dln_ref[...] = jnp.zeros_like(dln_ref)

        dh = jnp.dot(dzp_ref[...], w_ref[:POOL_DIM, :], preferred_element_type=F32)
        dh += jnp.dot(dzq_ref[...], w_ref[POOL_DIM:QKV_END, :], preferred_element_type=F32)
        dh += jnp.dot(dzg_ref[...], w_ref[QKV_END:, :], preferred_element_type=F32)
        dx, dgain = _rms_bwd(dh, x_ref[...], ln_ref[...])
        dx_ref[...] = dy_ref[...] + dx
        dln_ref[...] += jnp.sum(dgain, axis=0, keepdims=True)

    row = lambda i: (i, 0)
    return pl.pallas_call(
        body, name=name, grid=(m // tm,),
        in_specs=[pl.BlockSpec((tm, d), row), pl.BlockSpec((tm, d), row), _resident((1, d), (0, 0)),
                  pl.BlockSpec((tm, POOL_DIM), row), pl.BlockSpec((tm, QKV_END - POOL_DIM), row),
                  pl.BlockSpec((tm, n_in - QKV_END), row), _resident((n_in, d), (0, 0))],
        out_specs=[pl.BlockSpec((tm, d), row), pl.BlockSpec((1, d), lambda i: (0, 0))],
        out_shape=[_sds((m, d), F32), _sds((1, d), F32)],
        compiler_params=_cp("arbitrary"),
    )(dy, x, ln, dzp, dzqkv, dzg, w_in)


def _loss_head(y, target, name):
    m, d = y.shape
    tm = _row_tile(m)

    def body(y_ref, t_ref, loss_ref, dy_ref):
        @pl.when(pl.program_id(0) == 0)
        def _():
            loss_ref[...] = jnp.zeros_like(loss_ref)

        diff = y_ref[...] - t_ref[...]
        dy_ref[...] = diff * (1.0 / d)
        loss_ref[...] += 0.5 * jnp.sum(jnp.mean(diff * diff, axis=-1, keepdims=True), axis=0, keepdims=True)

    row = lambda i: (i, 0)
    return pl.pallas_call(
        body, name=name, grid=(m // tm,),
        in_specs=[pl.BlockSpec((tm, d), row), pl.BlockSpec((tm, d), row)],
        out_specs=[pl.BlockSpec((1, 1), lambda i: (0, 0)), pl.BlockSpec((tm, d), row)],
        out_shape=[_sds((1, 1), F32), _sds((m, d), F32)],
        compiler_params=_cp("arbitrary"),
    )(y, target)


def _adamw_math(g, w, m, v):
    m2 = ADAM_B1 * m + (1.0 - ADAM_B1) * g
    v2 = ADAM_B2 * v + (1.0 - ADAM_B2) * (g * g)
    m_hat = m2 / (1.0 - ADAM_B1 ** ADAM_STEP)
    v_hat = v2 / (1.0 - ADAM_B2 ** ADAM_STEP)
    delta = -ADAM_LR * (m_hat / (jnp.sqrt(v_hat) + ADAM_EPS) + ADAM_WD * w)
    return delta, m2, v2


def _sum_parts(parts_ref):
    g = parts_ref[0].astype(F32)
    for s in range(1, N_DEV):
        g = g + parts_ref[s].astype(F32)
    return g


def _adamw_sharded(parts, w, m, v, name, after=None):
    n_layers, rows, cols = w.shape
    tr = max(t for t in range(16, rows + 1, 16) if rows % t == 0 and t * cols <= ADAMW_BLOCK_ELEMS)
    nr = rows // tr
    deps = [] if after is None else [after]

    def body(*refs):
        part_refs = refs[:n_layers]
        w_ref, m_ref, v_ref = refs[n_layers:n_layers + 3]
        g_out, d_out, m_out, v_out = refs[-4:]
        layer = pl.program_id(0)
        for l in range(n_layers):
            @pl.when(layer == l)
            def _(l=l):
                g = _sum_parts(part_refs[l])
                delta, m2, v2 = _adamw_math(g, w_ref[...], m_ref[...], v_ref[...])
                g_out[...] = g
                d_out[...] = delta
                m_out[...] = m2
                v_out[...] = v2

    def part_map(l):
        return lambda layer, r: (0, jnp.where(layer == l, r, jnp.where(layer < l, 0, nr - 1)), 0)

    wspec = pl.BlockSpec((None, tr, cols), lambda layer, r: (layer, r, 0))
    return pl.pallas_call(
        body, name=name, grid=(n_layers, nr),
        in_specs=([pl.BlockSpec((N_DEV, tr, cols), part_map(l)) for l in range(n_layers)] + [wspec] * 3
                  + [pl.BlockSpec(memory_space=pl.ANY)] * len(deps)),
        out_specs=[wspec] * 4,
        out_shape=[_sds(w.shape, F32)] * 4,
        compiler_params=_cp("arbitrary", "arbitrary"),
    )(*parts, w, m, v, *deps)


def _adamw_packed(parts, w, m, v, name):
    def body(p_ref, w_ref, m_ref, v_ref, g_out, d_out, m_out, v_out):
        g = _sum_parts(p_ref)
        delta, m2, v2 = _adamw_math(g, w_ref[...], m_ref[...], v_ref[...])
        g_out[...] = g
        d_out[...] = delta
        m_out[...] = m2
        v_out[...] = v2

    return pl.pallas_call(
        body, name=name, out_shape=[_sds(w.shape, F32)] * 4,
        compiler_params=pltpu.CompilerParams(vmem_limit_bytes=VMEM_LIMIT_BYTES),
    )(parts, w, m, v)


_SMALL = ("ln_ffn1", "ln_mix", "pool_w", "pool_scale", "q_norm", "k_norm", "sinks", "ln_ffn2")


def _pack_small(arrs):
    rows = []
    for a in arrs:
        flat = a.reshape(-1)
        pad = (-flat.shape[0]) % 1024
        rows.append(jnp.pad(flat, (0, pad)).reshape(-1, 128))
    return jnp.concatenate(rows, axis=0)


def _unpack_small(packed, like):
    out, r0 = [], 0
    for a in like:
        size = a.size
        nrows = (size + 1023) // 1024 * 8
        out.append(packed[r0:r0 + nrows].reshape(-1)[:size].reshape(a.shape))
        r0 += nrows
    return out


def _rope_tables(m):
    pos = jnp.arange(m, dtype=F32)
    inv_freq = ROPE_THETA ** (-jnp.arange(0, ROT_DIM, 2, dtype=F32) / ROT_DIM)
    ang = pos[:, None] * inv_freq[None, :]
    cos8, sin8 = jnp.cos(ang), jnp.sin(ang)
    rest = HEAD_DIM - ROT_DIM
    cos64 = jnp.concatenate([cos8, cos8, jnp.ones((m, rest), F32)], axis=1)
    sin64 = jnp.concatenate([-sin8, sin8, jnp.zeros((m, rest), F32)], axis=1)
    return jnp.tile(cos64, (1, 2)), jnp.tile(sin64, (1, 2))


def _to_shard_major_cols(w):
    k = w.shape[0]
    return w.reshape(k, N_DEV, -1).transpose(1, 0, 2)


def _from_shard_major_cols(w):
    return w.transpose(1, 0, 2).reshape(w.shape[1], -1)


def kernel(x, ln_ffn1, w_ffn1_gu, w_ffn1_down, ln_mix, w_in, pool_w, pool_scale, w_pool_branch, q_norm, k_norm, sinks, w_attn_branch, w_out, ln_ffn2, w_ffn2_gu, w_ffn2_down, loss_target, m_ln_ffn1, m_w_ffn1_gu, m_w_ffn1_down, m_ln_mix, m_w_in, m_pool_w, m_pool_scale, m_w_pool_branch, m_q_norm, m_k_norm, m_sinks, m_w_attn_branch, m_w_out, m_ln_ffn2, m_w_ffn2_gu, m_w_ffn2_down, v_ln_ffn1, v_w_ffn1_gu, v_w_ffn1_down, v_ln_mix, v_w_in, v_pool_w, v_pool_scale, v_w_pool_branch, v_q_norm, v_k_norm, v_sinks, v_w_attn_branch, v_w_out, v_ln_ffn2, v_w_ffn2_gu, v_w_ffn2_down):
    weights = dict(ln_ffn1=ln_ffn1, w_ffn1_gu=w_ffn1_gu, w_ffn1_down=w_ffn1_down, ln_mix=ln_mix, w_in=w_in, pool_w=pool_w,
                   pool_scale=pool_scale, w_pool_branch=w_pool_branch, q_norm=q_norm, k_norm=k_norm, sinks=sinks,
                   w_attn_branch=w_attn_branch, w_out=w_out, ln_ffn2=ln_ffn2, w_ffn2_gu=w_ffn2_gu, w_ffn2_down=w_ffn2_down)
    mom_m = dict(ln_ffn1=m_ln_ffn1, w_ffn1_gu=m_w_ffn1_gu, w_ffn1_down=m_w_ffn1_down, ln_mix=m_ln_mix, w_in=m_w_in,
                 pool_w=m_pool_w, pool_scale=m_pool_scale, w_pool_branch=m_w_pool_branch, q_norm=m_q_norm, k_norm=m_k_norm,
                 sinks=m_sinks, w_attn_branch=m_w_attn_branch, w_out=m_w_out, ln_ffn2=m_ln_ffn2, w_ffn2_gu=m_w_ffn2_gu,
                 w_ffn2_down=m_w_ffn2_down)
    mom_v = dict(ln_ffn1=v_ln_ffn1, w_ffn1_gu=v_w_ffn1_gu, w_ffn1_down=v_w_ffn1_down, ln_mix=v_ln_mix, w_in=v_w_in,
                 pool_w=v_pool_w, pool_scale=v_pool_scale, w_pool_branch=v_w_pool_branch, q_norm=v_q_norm, k_norm=v_k_norm,
                 sinks=v_sinks, w_attn_branch=v_w_attn_branch, w_out=v_w_out, ln_ffn2=v_ln_ffn2, w_ffn2_gu=v_w_ffn2_gu,
                 w_ffn2_down=v_w_ffn2_down)
    order = ("ln_ffn1", "w_ffn1_gu", "w_ffn1_down", "ln_mix", "w_in", "pool_w", "pool_scale", "w_pool_branch", "q_norm",
             "k_norm", "sinks", "w_attn_branch", "w_out", "ln_ffn2", "w_ffn2_gu", "w_ffn2_down")
    big = ("w_ffn1_gu", "w_ffn1_down", "w_in", "w_pool_branch", "w_attn_branch", "w_out", "w_ffn2_gu", "w_ffn2_down")

    transposed = ("w_ffn1_gu", "w_ffn2_gu", "w_in")
    for group in (weights, mom_m, mom_v):
        for k in transposed:
            group[k] = jnp.swapaxes(group[k], 1, 2)

    n_layers = ln_ffn1.shape[0]
    seq, d = x.shape[-2], x.shape[-1]
    xs = x.reshape(seq, d)
    target = loss_target.reshape(seq, d)
    cos, sin = _rope_tables(seq)

    def layer_shards(l):
        return [weights[k][l].astype(BF16) for k in big]

    def layer_weights(l, full):
        g = dict(zip(big, full))
        return dict(
            gu1=g["w_ffn1_gu"].reshape(2, -1, d), down1=g["w_ffn1_down"].reshape(-1, d),
            gu2=g["w_ffn2_gu"].reshape(2, -1, d), down2=g["w_ffn2_down"].reshape(-1, d),
            w_in=g["w_in"].reshape(-1, d), w_pb=_from_shard_major_cols(g["w_pool_branch"]),
            w_ab=_from_shard_major_cols(g["w_attn_branch"]), w_out=g["w_out"].reshape(d, d),
            ln1=ln_ffn1[l][None], ln_mix=ln_mix[l][None], ln2=ln_ffn2[l][None], pool_w=pool_w[l],
            pool_scale=pool_scale[l][None], sinks=sinks[l],
            qgain=jnp.tile(q_norm[l], N_Q_HEADS)[None], kgain=jnp.tile(k_norm[l], N_KV_HEADS)[None])

    gathered = [layer_weights(0, _all_gather_many(layer_shards(0), name="gather_weights_l0"))]
    saved = []
    cur = xs
    for l in range(n_layers):
        lw = gathered[l]
        s = dict(x0=cur)
        in_flight, token = None, None
        if l + 1 < n_layers:
            in_flight, token = _exchange_start(layer_shards(l + 1), scatter=False, name=f"gather_start_l{l + 1}")
        s["h1"], s["gu1"], act1 = _ffn_up(cur, lw["ln1"], lw["gu1"], name=f"ffn1_up_l{l}", after=token)
        s["act1"] = act1
        x1 = _ffn_down(cur, act1, lw["down1"], name=f"ffn1_down_l{l}")
        s["x1"] = x1
        s["h2"], zq, zg = _mix_in(x1, lw["ln_mix"], lw["w_in"], name=f"mix_in_l{l}")
        s["zq"], s["zg"] = zq, zg
        s["d"], s["p"] = _pool_fwd(zq, lw["pool_w"], lw["pool_scale"], name=f"pool_fwd_l{l}")
        s["qr"], s["kr"], s["vb"] = _qk_prep(zq, lw["qgain"], lw["kgain"], cos, sin, name=f"qk_prep_l{l}")
        s["o"] = _attn_fwd(s["qr"], s["kr"], s["vb"], lw["sinks"], name=f"attn_fwd_l{l}")
        x2, s["mix"] = _merge_fwd(x1, s["p"], s["o"], zg, lw["w_pb"], lw["w_ab"], lw["w_out"],
                                                  name=f"merge_fwd_l{l}")
        s["x2"] = x2
        s["h3"], s["gu2"], act2 = _ffn_up(x2, lw["ln2"], lw["gu2"], name=f"ffn2_up_l{l}")
        s["act2"] = act2
        cur = _ffn_down(x2, act2, lw["down2"], name=f"ffn2_down_l{l}")
        saved.append(s)
        if in_flight is not None:
            gathered.append(layer_weights(l + 1, _exchange_wait(in_flight, cur, name=f"gather_wait_l{l + 1}")))

    loss_local, dy = _loss_head(cur, target, name="loss_head")
    loss = lax.psum(loss_local[0, 0], MESH_AXES)

    small_grads = {k: [None] * n_layers for k in _SMALL}
    received = {k: [None] * n_layers for k in big}
    big_late = ("w_ffn1_gu", "w_ffn1_down")
    big_early = tuple(k for k in big if k not in big_late)
    early_in_flight, late_in_flight = [None] * n_layers, [None] * n_layers
    token = None
    for l in reversed(range(n_layers)):
        lw, s = gathered[l], saved[l]
        d_ff = lw["down1"].shape[0]

        def ffn_weight_grads(dyh, dgu, h, act, tag):
            dw_down = _matmul_tn(act[None], dyh[None], name=f"{tag}_dw_down_l{l}", a_chunk=d_ff // 2)
            dw_gu = _matmul_tn(dgu, h[None], name=f"{tag}_dw_gu_l{l}", a_chunk=d_ff // 2)
            return dw_gu.reshape(N_DEV, -1, d), dw_down.reshape(N_DEV, -1, d)

        dyh, dgu, dx2, dln2 = _ffn_bwd(dy, s["x2"], lw["ln2"], s["gu2"], lw["down2"], lw["gu2"],
                                       name=f"ffn2_bwd_l{l}", after=token)
        dw_gu2, dw_down2 = ffn_weight_grads(dyh, dgu, s["h3"], s["act2"], "ffn2")

        dyb, da, db, dp, do, dzg = _merge_bwd(dx2, s["p"], s["o"], s["zg"], lw["w_out"], lw["w_pb"], lw["w_ab"],
                                              name=f"merge_bwd_l{l}")
        dw_out = _matmul_tn(s["mix"][None], dyb[None], name=f"dw_out_l{l}")[0]
        dw_pb = _matmul_tn(s["p"][None], da[None], name=f"dw_pb_l{l}")[0]
        dw_ab = _matmul_tn(s["o"][None], db[None], name=f"dw_ab_l{l}")[0]
        dzp, dpw, dsc = _pool_bwd(dp, s["d"], lw["pool_w"], lw["pool_scale"], name=f"pool_bwd_l{l}")
        dq, dk, dv, dsinks = _attn_bwd(do, s["qr"], s["kr"], s["vb"], lw["sinks"], name=f"attn_bwd_l{l}")
        dzqkv, dqg, dkg = _qk_bwd(dq, dk, dv, s["zq"], lw["qgain"], lw["kgain"], cos, sin, name=f"qk_bwd_l{l}")
        dw_in = jnp.concatenate([_matmul_tn(dzp[None], s["h2"][None], name=f"dw_in_pool_l{l}")[0],
                                 _matmul_tn(dzqkv[None], s["h2"][None], name=f"dw_in_qkv_l{l}")[0],
                                 _matmul_tn(dzg[None], s["h2"][None], name=f"dw_in_gate_l{l}")[0]], axis=0)
        dx1, dlnm = _mix_bwd_x(dx2, s["x1"], lw["ln_mix"], dzp, dzqkv, dzg, lw["w_in"], name=f"mix_bwd_x_l{l}")

        partial = dict(w_in=dw_in.reshape(N_DEV, -1, d), w_pool_branch=_to_shard_major_cols(dw_pb),
                       w_attn_branch=_to_shard_major_cols(dw_ab), w_out=dw_out.reshape(N_DEV, d // N_DEV, d),
                       w_ffn2_gu=dw_gu2, w_ffn2_down=dw_down2)
        early_in_flight[l], token = _exchange_start([partial[k] for k in big_early], scatter=True,
                                                    name=f"grads_early_start_l{l}")

        dyh, dgu, dy, dln1 = _ffn_bwd(dx1, s["x0"], lw["ln1"], s["gu1"], lw["down1"], lw["gu1"],
                                      name=f"ffn1_bwd_l{l}", after=token)
        dw_gu1, dw_down1 = ffn_weight_grads(dyh, dgu, s["h1"], s["act1"], "ffn1")
        late_in_flight[l], token = _exchange_start([dw_gu1, dw_down1], scatter=True, name=f"grads_late_start_l{l}")
        small_grads["ln_ffn1"][l] = dln1[0]
        small_grads["ln_mix"][l] = dlnm[0]
        small_grads["ln_ffn2"][l] = dln2[0]
        small_grads["pool_w"][l] = dpw
        small_grads["pool_scale"][l] = dsc[0]
        small_grads["q_norm"][l] = dqg[0]
        small_grads["k_norm"][l] = dkg[0]
        small_grads["sinks"][l] = dsinks[0, :N_Q_HEADS]

    grad_x = dy.reshape(x.shape)

    small_w = [weights[k] for k in _SMALL]
    packed_g = _pack_small([jnp.stack(small_grads[k]).reshape(weights[k].shape) for k in _SMALL])
    small_in_flight, after = _exchange_start([packed_g], scatter=False, name="small_grads_start")
    for l in reversed(range(n_layers)):
        got = _exchange_wait(early_in_flight[l], after, name=f"grads_early_wait_l{l}")
        after = got[0]
        for k, r in zip(big_early, got):
            received[k][l] = r

    grads, deltas, new_m, new_v = {}, {}, {}, {}

    def adamw(k, after):
        w = weights[k]
        shape2 = (n_layers, -1, w.shape[-1])
        parts = [r.reshape(N_DEV, -1, w.shape[-1]) for r in received[k]]
        outs = _adamw_sharded(parts, w.reshape(shape2), mom_m[k].reshape(shape2), mom_v[k].reshape(shape2),
                              name=f"adamw_{k}", after=after)
        grads[k], deltas[k], new_m[k], new_v[k] = (o.reshape(w.shape) for o in outs)
        return outs[0]

    after = None
    for k in big_early:
        after = adamw(k, after)
    for l in reversed(range(n_layers)):
        got = _exchange_wait(late_in_flight[l], after, name=f"grads_late_wait_l{l}")
        after = got[0]
        for k, r in zip(big_late, got):
            received[k][l] = r
    after = None
    for k in big_late:
        after = adamw(k, after)

    (parts_small,) = _exchange_wait(small_in_flight, after, name="small_grads_wait")
    outs = _adamw_packed(parts_small, _pack_small(small_w), _pack_small([mom_m[k] for k in _SMALL]),
                         _pack_small([mom_v[k] for k in _SMALL]), name="adamw_small")
    for res, o in zip((grads, deltas, new_m, new_v), outs):
        for k, a in zip(_SMALL, _unpack_small(o, small_w)):
            res[k] = a

    for res in (grads, deltas, new_m, new_v):
        for k in transposed:
            res[k] = jnp.swapaxes(res[k], 1, 2)
    return (loss, grad_x, *[grads[k] for k in order], *[deltas[k] for k in order],
            *[new_m[k] for k in order], *[new_v[k] for k in order])
```

```python
import functools

import jax
import jax.numpy as jnp
from jax import lax
from jax.experimental import pallas as pl
from jax.experimental.pallas import tpu as pltpu

F32 = jnp.float32
BF16 = jnp.bfloat16

N_DEV = 8
MESH_AXES = ("x", "y", "c")
EPS = 1e-6

HEAD_DIM = 64
N_Q_HEADS = 8
N_KV_HEADS = 2
GQA_GROUP = N_Q_HEADS // N_KV_HEADS
ATTN_BLOCK = 128
ATTN_SCALE = HEAD_DIM ** -0.5
ROPE_THETA = 500000.0
ROT_DIM = 16
POOL_WINDOWS = (2, 4, 8, 16)
POOL_HALO = 16
GROUP_DIM = 128
POOL_DIM = 512
ATTN_DIM = 512
KV_DIM = 128
QKV_END = POOL_DIM + ATTN_DIM + 2 * KV_DIM

ADAM_LR = 0.001
ADAM_B1 = 0.9
ADAM_B2 = 0.999
ADAM_EPS = 1e-08
ADAM_WD = 0.01
ADAM_STEP = 10

ROW_TILE = 512
TN_ROW_TILE = 2048
FFN_CHUNK = 256
FFN_BWD_ROW_TILE = 256
VMEM_LIMIT_BYTES = 56 << 20
ADAMW_BLOCK_ELEMS = 192 * 1024
NEG_BIG = -1e30

_NT = (((1,), (1,)), ((), ()))
_TN = (((0,), (0,)), ((), ()))


def _cp(*sem):
    return pltpu.CompilerParams(dimension_semantics=sem, vmem_limit_bytes=VMEM_LIMIT_BYTES)


def _resident(block, index):
    return pl.BlockSpec(block, lambda *_: index, pipeline_mode=pl.Buffered(1))


def _row_tile(m):
    return min(ROW_TILE, m)


def _sds(shape, dtype):
    return jax.ShapeDtypeStruct(shape, dtype)


def _mesh_pos():
    return lax.axis_index("x"), lax.axis_index("y"), lax.axis_index("c")


def _all_gather_many(shards, name, after=None):
    n = len(shards)

    deps = [] if after is None else [after]

    def body(*refs):
        ins, outs = refs[:n], refs[n + len(deps):2 * n + len(deps)]
        send_sems, recv_sems, local_sems = refs[2 * n + len(deps):]
        x, y, c = _mesh_pos()
        me, sibling = (x, y, c), (x, y, 1 - c)
        chips = [(1 - x, y), (x, 1 - y), (1 - x, 1 - y)]

        def slot(a, pos):
            return outs[a].at[4 * pos[0] + 2 * pos[1] + pos[2]]

        def copy(a, k, block, to, src=None):
            return pltpu.make_async_remote_copy(
                src_ref=slot(a, block) if src is None else src, dst_ref=slot(a, block),
                send_sem=send_sems.at[a, k], recv_sem=recv_sems.at[a, k],
                device_id=to, device_id_type=pl.DeviceIdType.MESH)

        mine = [pltpu.make_async_copy(ins[a], slot(a, me), local_sems.at[a]) for a in range(n)]
        for cp in mine:
            cp.start()
        first = []
        for a in range(n):
            first.append(copy(a, 0, me, sibling, src=ins[a]))
            for j, chip in enumerate(chips):
                first.append(copy(a, 1 + j, me, (*chip, c), src=ins[a]))
        for cp in first:
            cp.start()
        passed = []
        for j, chip in enumerate(chips):
            for a in range(n):
                copy(a, 1 + j, (*chip, c), me).wait_recv()
                fwd = copy(a, 4 + j, (*chip, c), sibling)
                fwd.start()
                passed.append(fwd)
        for a in range(n):
            copy(a, 0, sibling, me).wait_recv()
        for j, chip in enumerate(chips):
            for a in range(n):
                copy(a, 4 + j, (*chip, 1 - c), me).wait_recv()
        for cp in first + passed:
            cp.wait_send()
        for cp in mine:
            cp.wait()

    any_spec = pl.BlockSpec(memory_space=pl.ANY)
    return pl.pallas_call(
        body, name=name,
        out_shape=[_sds((N_DEV,) + s.shape, s.dtype) for s in shards],
        in_specs=[any_spec] * (n + len(deps)), out_specs=[any_spec] * n,
        scratch_shapes=[pltpu.SemaphoreType.DMA((n, 7)), pltpu.SemaphoreType.DMA((n, 7)),
                        pltpu.SemaphoreType.DMA((n,))],
    )(*shards, *deps)


_ALL_PEERS = tuple(range(1, N_DEV))
_CHIP_PEERS = (1, 2, 4, 6)


def _direct_copies(src, land, send_sem, recv_sem, local_sem, scatter, peers=_ALL_PEERS):
    x, y, c = _mesh_pos()
    me = 4 * x + 2 * y + c
    local = pltpu.make_async_copy(src.at[me] if scatter else src, land.at[me], local_sem)
    remote = []
    for k in peers:
        px = 1 - x if k & 4 else x
        py = 1 - y if k & 2 else y
        pc = 1 - c if k & 1 else c
        remote.append(pltpu.make_async_remote_copy(
            src_ref=src.at[4 * px + 2 * py + pc] if scatter else src, dst_ref=land.at[me],
            send_sem=send_sem, recv_sem=recv_sem, device_id=(px, py, pc), device_id_type=pl.DeviceIdType.MESH))
    every = land.at[pl.ds(0, len(peers))]
    drain = pltpu.make_async_remote_copy(src_ref=every, dst_ref=every, send_sem=send_sem, recv_sem=recv_sem,
                                         device_id=(x, y, c), device_id_type=pl.DeviceIdType.MESH)
    return local, remote, drain


_HBM_SPEC = pl.BlockSpec(memory_space=pltpu.HBM)
_SEM_SPEC = pl.BlockSpec(memory_space=pltpu.SEMAPHORE)
_DATAFLOW = pltpu.SideEffectType.DATAFLOW_SIDE_EFFECTING
_SEMS_PER_ARRAY = 3


def _exchange_start(srcs, scatter, name, peers=_ALL_PEERS, after=None):
    n = len(srcs)
    deps = [] if after is None else [after]
    n_sems = _SEMS_PER_ARRAY * n
    land_shapes = [s.shape if scatter else (N_DEV,) + s.shape for s in srcs]

    def body(*refs):
        ins, lands = refs[:n], refs[n:2 * n]
        sems = refs[2 * n + len(deps):2 * n + len(deps) + n_sems]
        for a in range(n):
            local, remote, _ = _direct_copies(ins[a], lands[a], *sems[3 * a:3 * a + 3], scatter, peers)
            local.start()
            for cp in remote:
                cp.start()
        refs[-1][...] = jnp.zeros_like(refs[-1])

    outs = pl.pallas_call(
        body, name=name,
        out_shape=(*[pltpu.SemaphoreType.DMA(())] * n_sems,
                   *[pltpu.HBM(s.shape, s.dtype) for s in srcs],
                   *[pltpu.HBM(shape, s.dtype) for shape, s in zip(land_shapes, srcs)],
                   _sds((8, 128), F32)),
        in_specs=[_HBM_SPEC] * (2 * n) + [pl.BlockSpec(memory_space=pl.ANY)] * len(deps),
        out_specs=(*[_SEM_SPEC] * n_sems, *[_HBM_SPEC] * (2 * n), pl.BlockSpec(memory_space=pltpu.VMEM)),
        input_output_aliases={i: n_sems + i for i in range(2 * n)},
        compiler_params=pltpu.CompilerParams(has_side_effects=_DATAFLOW),
    )(*[pltpu.with_memory_space_constraint(s, pltpu.HBM) for s in srcs],
      *[pltpu.with_memory_space_constraint(lax.empty(shape, s.dtype), pltpu.HBM) for shape, s in zip(land_shapes, srcs)],
      *deps)
    return (outs[:n_sems], outs[n_sems:n_sems + n], outs[n_sems + n:n_sems + 2 * n], scatter, peers), outs[-1]


def _exchange_wait(state, after, name):
    sems, srcs, lands, scatter, peers = state
    n = len(srcs)
    n_sems = len(sems)

    def body(*refs):
        ins, zones, ss = refs[:n], refs[n:2 * n], refs[2 * n:2 * n + n_sems]
        for a in range(n):
            local, _, drain = _direct_copies(ins[a], zones[a], *ss[3 * a:3 * a + 3], scatter, peers)
            drain.wait_send()
            drain.wait_recv()
            local.wait()

    outs = pl.pallas_call(
        body, name=name,
        out_shape=(*[pltpu.HBM(s.shape, s.dtype) for s in srcs], *[pltpu.HBM(z.shape, z.dtype) for z in lands]),
        in_specs=[_HBM_SPEC] * (2 * n) + [_SEM_SPEC] * n_sems + [pl.BlockSpec(memory_space=pl.ANY)],
        out_specs=[_HBM_SPEC] * (2 * n),
        input_output_aliases={i: i for i in range(2 * n)},
        compiler_params=pltpu.CompilerParams(has_side_effects=_DATAFLOW),
    )(*srcs, *lands, *sems, after)
    return outs[n:]


def _forward_to_sibling(lands, name):
    n = len(lands)

    def body(*refs):
        zones = refs[n:2 * n]
        send_sems, recv_sems = refs[2 * n:]
        x, y, c = _mesh_pos()
        chips = [(1 - x, y), (x, 1 - y), (1 - x, 1 - y)]
        sends, recvs = [], []
        for a in range(n):
            for j, (px, py) in enumerate(chips):
                mine = zones[a].at[4 * px + 2 * py + c]
                theirs = zones[a].at[4 * px + 2 * py + 1 - c]
                sends.append(pltpu.make_async_remote_copy(
                    src_ref=mine, dst_ref=mine, send_sem=send_sems.at[a, j], recv_sem=recv_sems.at[a, j],
                    device_id=(x, y, 1 - c), device_id_type=pl.DeviceIdType.MESH))
                recvs.append(pltpu.make_async_remote_copy(
                    src_ref=theirs, dst_ref=theirs, send_sem=send_sems.at[a, j], recv_sem=recv_sems.at[a, j],
                    device_id=(x, y, 1 - c), device_id_type=pl.DeviceIdType.MESH))
        for cp in sends:
            cp.start()
        for cp in recvs:
            cp.wait_recv()
        for cp in sends:
            cp.wait_send()

    any_spec = pl.BlockSpec(memory_space=pl.ANY)
    return pl.pallas_call(
        body, name=name,
        out_shape=[_sds(z.shape, z.dtype) for z in lands],
        in_specs=[any_spec] * n, out_specs=[any_spec] * n,
        input_output_aliases={a: a for a in range(n)},
        scratch_shapes=[pltpu.SemaphoreType.DMA((n, 3)), pltpu.SemaphoreType.DMA((n, 3))],
    )(*lands)


def _rms_fwd(xv, gain):
    r = lax.rsqrt(jnp.mean(xv * xv, axis=-1, keepdims=True) + EPS)
    return xv * r * gain


def _rms_bwd(dh, xv, gain):
    r = lax.rsqrt(jnp.mean(xv * xv, axis=-1, keepdims=True) + EPS)
    xn = xv * r
    dxn = dh * gain
    dx = r * (dxn - xn * jnp.mean(dxn * xn, axis=-1, keepdims=True))
    return dx, dh * xn


def _sigmoid(v):
    return 0.5 * jnp.tanh(0.5 * v) + 0.5


def _silu_parts(g):
    s = _sigmoid(g)
    return g * s, s * (1.0 + g * (1.0 - s))


def _segment_mean(v, width):
    r = lax.broadcasted_iota(jnp.int32, (width, width), 0) >> 6
    c = lax.broadcasted_iota(jnp.int32, (width, width), 1) >> 6
    bd = (r == c).astype(BF16)
    hi = v.astype(BF16)
    lo = (v - hi.astype(F32)).astype(BF16)
    total = jnp.dot(hi, bd, preferred_element_type=F32) + jnp.dot(lo, bd, preferred_element_type=F32)
    return total * (1.0 / HEAD_DIM)


def _rope_partner(v):
    width = v.shape[1]
    half = ROT_DIM // 2
    lane = lax.broadcasted_iota(jnp.int32, v.shape, 1) & (HEAD_DIM - 1)
    up = jnp.where(lane < ROT_DIM, pltpu.roll(v, half, 1), 0.0)
    return jnp.where(lane < half, pltpu.roll(v, width - half, 1), up)


def _tile_lanes(t, width):
    return t if width == t.shape[1] else jnp.tile(t, (1, width // t.shape[1]))


def _ffn_chunks(f):
    return [slice(j * FFN_CHUNK, (j + 1) * FFN_CHUNK) for j in range(f // FFN_CHUNK)]


def _ffn_up(x, ln, wgu, name, after=None):
    m, d = x.shape
    f = wgu.shape[1]
    tm = _row_tile(m)
    deps = [] if after is None else [after]

    def body(*refs):
        x_ref, ln_ref, w_ref = refs[:3]
        h_ref, gu_ref, a_ref = refs[-3:]
        h = _rms_fwd(x_ref[...], ln_ref[...]).astype(BF16)
        h_ref[...] = h
        for cols in _ffn_chunks(f):
            g = lax.dot_general(h, w_ref[0, cols, :], _NT, preferred_element_type=F32)
            u = lax.dot_general(h, w_ref[1, cols, :], _NT, preferred_element_type=F32)
            gu_ref[0, :, cols] = g.astype(BF16)
            gu_ref[1, :, cols] = u.astype(BF16)
            a_ref[:, cols] = (g * _sigmoid(g) * u).astype(BF16)

    return pl.pallas_call(
        body, name=name, grid=(m // tm,),
        in_specs=[pl.BlockSpec((tm, d), lambda i: (i, 0)), _resident((1, d), (0, 0)),
                  _resident((2, f, d), (0, 0, 0))] + [pl.BlockSpec(memory_space=pl.ANY)] * len(deps),
        out_specs=[pl.BlockSpec((tm, d), lambda i: (i, 0)), pl.BlockSpec((2, tm, f), lambda i: (0, i, 0)),
                   pl.BlockSpec((tm, f), lambda i: (i, 0))],
        out_shape=[_sds((m, d), BF16), _sds((2, m, f), BF16), _sds((m, f), BF16)],
        compiler_params=_cp("parallel"),
    )(x, ln, wgu, *deps)


def _ffn_down(x, act, wd, name):
    m, d = x.shape
    f = act.shape[-1]
    tm = _row_tile(m)

    def body(x_ref, a_ref, w_ref, o_ref):
        o_ref[...] = x_ref[...] + 0.5 * jnp.dot(a_ref[...], w_ref[...], preferred_element_type=F32)

    return pl.pallas_call(
        body, name=name, grid=(m // tm,),
        in_specs=[pl.BlockSpec((tm, d), lambda i: (i, 0)), pl.BlockSpec((tm, f), lambda i: (i, 0)),
                  _resident((f, d), (0, 0))],
        out_specs=pl.BlockSpec((tm, d), lambda i: (i, 0)),
        out_shape=_sds((m, d), F32),
        compiler_params=_cp("parallel"),
    )(x, act, wd)


def _ffn_bwd(dy, x, ln, gu, wd, wgu, name, after=None):
    m, d = dy.shape
    f = gu.shape[-1]
    tm = min(FFN_BWD_ROW_TILE, m)
    deps = [] if after is None else [after]

    def body(*refs):
        dy_ref, x_ref, ln_ref, gu_ref, wd_ref, wgu_ref = refs[:6]
        dyh_ref, dgu_ref, dx_ref, dln_ref = refs[-4:]

        @pl.when(pl.program_id(0) == 0)
        def _():
            dln_ref[...] = jnp.zeros_like(dln_ref)

        dyh = (0.5 * dy_ref[...]).astype(BF16)
        dyh_ref[...] = dyh
        dgs, dus = [], []
        for cols in _ffn_chunks(f):
            da = lax.dot_general(dyh, wd_ref[cols, :], _NT, preferred_element_type=F32)
            g = gu_ref[0, :, cols].astype(F32)
            u = gu_ref[1, :, cols].astype(F32)
            silu, dsilu = _silu_parts(g)
            dgs.append((da * u * dsilu).astype(BF16))
            dus.append((da * silu).astype(BF16))
            dgu_ref[0, :, cols] = dgs[-1]
            dgu_ref[1, :, cols] = dus[-1]
        dh = jnp.dot(jnp.concatenate(dgs, axis=1), wgu_ref[0], preferred_element_type=F32)
        dh += jnp.dot(jnp.concatenate(dus, axis=1), wgu_ref[1], preferred_element_type=F32)
        dx, dgain = _rms_bwd(dh, x_ref[...], ln_ref[...])
        dx_ref[...] = dy_ref[...] + dx
        dln_ref[...] += jnp.sum(dgain, axis=0, keepdims=True)

    row = lambda i: (i, 0)
    return pl.pallas_call(
        body, name=name, grid=(m // tm,),
        in_specs=[pl.BlockSpec((tm, d), row), pl.BlockSpec((tm, d), row), _resident((1, d), (0, 0)),
                  pl.BlockSpec((2, tm, f), lambda i: (0, i, 0)), _resident((f, d), (0, 0)),
                  _resident((2, f, d), (0, 0, 0))] + [pl.BlockSpec(memory_space=pl.ANY)] * len(deps),
        out_specs=[pl.BlockSpec((tm, d), row), pl.BlockSpec((2, tm, f), lambda i: (0, i, 0)),
                   pl.BlockSpec((tm, d), row), pl.BlockSpec((1, d), lambda i: (0, 0))],
        out_shape=[_sds((m, d), BF16), _sds((2, m, f), BF16), _sds((m, d), F32), _sds((1, d), F32)],
        compiler_params=_cp("arbitrary"),
    )(dy, x, ln, gu, wd, wgu, *deps)


def _matmul_tn(a, b, name, a_chunk=None, out_dtype=BF16):
    ja, m, k = a.shape
    jb, _, n = b.shape
    nj = max(ja, jb)
    kc = k if a_chunk is None else a_chunk
    tm = min(TN_ROW_TILE, m)
    nm = m // tm

    def body(a_ref, b_ref, o_ref, acc):
        step = pl.program_id(2)

        @pl.when(step == 0)
        def _():
            acc[...] = jnp.zeros_like(acc)

        acc[...] += lax.dot_general(a_ref[...], b_ref[...], _TN, preferred_element_type=F32)

        @pl.when(step == nm - 1)
        def _():
            o_ref[...] = acc[...].astype(o_ref.dtype)

    return pl.pallas_call(
        body, name=name, grid=(nj, k // kc, nm),
        in_specs=[pl.BlockSpec((None, tm, kc), (lambda j, c, s: (j, s, c)) if ja > 1 else (lambda j, c, s: (0, s, c))),
                  pl.BlockSpec((None, tm, n), (lambda j, c, s: (j, s, 0)) if jb > 1 else (lambda j, c, s: (0, s, 0)))],
        out_specs=pl.BlockSpec((None, kc, n), lambda j, c, s: (j, c, 0)),
        out_shape=_sds((nj, k, n), out_dtype),
        scratch_shapes=[pltpu.VMEM((kc, n), F32)],
        compiler_params=_cp("parallel", "parallel", "arbitrary"),
    )(a, b)


def _mix_in(x, ln, w_in, name):
    m, d = x.shape
    n_in = w_in.shape[0]
    tm = _row_tile(m)

    def body(x_ref, ln_ref, w_ref, h_ref, zq_ref, zg_ref):
        h = _rms_fwd(x_ref[...], ln_ref[...]).astype(BF16)
        h_ref[...] = h
        zq_ref[...] = lax.dot_general(h, w_ref[:QKV_END, :], _NT, preferred_element_type=F32)
        zg_ref[...] = lax.dot_general(h, w_ref[QKV_END:, :], _NT, preferred_element_type=F32).astype(BF16)

    return pl.pallas_call(
        body, name=name, grid=(m // tm,),
        in_specs=[pl.BlockSpec((tm, d), lambda i: (i, 0)), _resident((1, d), (0, 0)), _resident((n_in, d), (0, 0))],
        out_specs=[pl.BlockSpec((tm, d), lambda i: (i, 0)), pl.BlockSpec((tm, QKV_END), lambda i: (i, 0)),
                   pl.BlockSpec((tm, n_in - QKV_END), lambda i: (i, 0))],
        out_shape=[_sds((m, d), BF16), _sds((m, QKV_END), F32), _sds((m, n_in - QKV_END), BF16)],
        compiler_params=_cp("parallel"),
    )(x, ln, w_in)


def _pool_fwd(zq, pool_w, scale, name):
    m = zq.shape[0]
    tm = _row_tile(m)
    halo_blocks = tm // POOL_HALO

    def body(zc_ref, zh_ref, pw_ref, sc_ref, d_ref, p_ref):
        i = pl.program_id(0)
        halo = jnp.where(i > 0, zh_ref[...], 0.0)
        ext = jnp.concatenate([halo, zc_ref[...]], axis=0)
        t = i * tm + lax.broadcasted_iota(jnp.int32, (tm, 1), 0)
        for g, w in enumerate(POOL_WINDOWS):
            lanes = slice(g * GROUP_DIM, (g + 1) * GROUP_DIM)
            e = ext[:, lanes]
            s, k = e, 1
            while k < w:
                s = s + pltpu.roll(s, k, 0)
                k *= 2
            cnt = jnp.minimum(t + 1, w).astype(F32)
            dg = (s[POOL_HALO:, :] / cnt - e[POOL_HALO:, :]).astype(BF16)
            y = jnp.dot(dg, pw_ref[g].astype(BF16), preferred_element_type=F32)
            d_ref[:, lanes] = dg
            p_ref[:, lanes] = (y * sc_ref[:, lanes]).astype(BF16)

    return pl.pallas_call(
        body, name=name, grid=(m // tm,),
        in_specs=[pl.BlockSpec((tm, POOL_DIM), lambda i: (i, 0)),
                  pl.BlockSpec((POOL_HALO, POOL_DIM), lambda i: (jnp.maximum(i * halo_blocks - 1, 0), 0)),
                  _resident((4, GROUP_DIM, GROUP_DIM), (0, 0, 0)), _resident((1, POOL_DIM), (0, 0))],
        out_specs=[pl.BlockSpec((tm, POOL_DIM), lambda i: (i, 0)), pl.BlockSpec((tm, POOL_DIM), lambda i: (i, 0))],
        out_shape=[_sds((m, POOL_DIM), BF16), _sds((m, POOL_DIM), BF16)],
        compiler_params=_cp("parallel"),
    )(zq, zq, pool_w, scale)


def _pool_bwd(dp, d, pool_w, scale, name):
    m = dp.shape[0]
    tm = _row_tile(m)
    nb = m // tm
    halo_blocks = tm // POOL_HALO
    rows = tm + POOL_HALO

    def body(dpc_ref, dph_ref, d_ref, pw_ref, sc_ref, du_ref, dpw_ref, dsc_ref):
        i = pl.program_id(0)

        @pl.when(i == 0)
        def _():
            dpw_ref[...] = jnp.zeros_like(dpw_ref)
            dsc_ref[...] = jnp.zeros_like(dsc_ref)

        halo = jnp.where(i < nb - 1, dph_ref[...], 0.0)
        dpc = dpc_ref[...]
        ext = jnp.concatenate([dpc, halo], axis=0)
        t = i * tm + lax.broadcasted_iota(jnp.int32, (rows, 1), 0)
        for g, w in enumerate(POOL_WINDOWS):
            lanes = slice(g * GROUP_DIM, (g + 1) * GROUP_DIM)
            pwb = pw_ref[g].astype(BF16)
            dyb = (ext[:, lanes] * sc_ref[:, lanes]).astype(BF16)
            dd = lax.dot_general(dyb, pwb, _NT, preferred_element_type=F32)
            cnt = jnp.minimum(t + 1, w).astype(F32)
            s, k = dd / cnt, 1
            while k < w:
                s = s + pltpu.roll(s, rows - k, 0)
                k *= 2
            du_ref[:, lanes] = (s[:tm, :] - dd[:tm, :]).astype(BF16)
            dcur = d_ref[:, lanes]
            y = jnp.dot(dcur, pwb, preferred_element_type=F32)
            dsc_ref[:, lanes] += jnp.sum(dpc[:, lanes] * y, axis=0, keepdims=True)
            dpw_ref[g] += lax.dot_general(dcur, dyb[:tm, :], _TN, preferred_element_type=F32)

    return pl.pallas_call(
        body, name=name, grid=(nb,),
        in_specs=[pl.BlockSpec((tm, POOL_DIM), lambda i: (i, 0)),
                  pl.BlockSpec((POOL_HALO, POOL_DIM), lambda i: (jnp.minimum((i + 1) * halo_blocks, nb * halo_blocks - 1), 0)),
                  pl.BlockSpec((tm, POOL_DIM), lambda i: (i, 0)),
                  _resident((4, GROUP_DIM, GROUP_DIM), (0, 0, 0)), _resident((1, POOL_DIM), (0, 0))],
        out_specs=[pl.BlockSpec((tm, POOL_DIM), lambda i: (i, 0)),
                   pl.BlockSpec((4, GROUP_DIM, GROUP_DIM), lambda i: (0, 0, 0)),
                   pl.BlockSpec((1, POOL_DIM), lambda i: (0, 0))],
        out_shape=[_sds((m, POOL_DIM), BF16), _sds((4, GROUP_DIM, GROUP_DIM), F32), _sds((1, POOL_DIM), F32)],
        compiler_params=_cp("arbitrary"),
    )(dp, dp, d, pool_w, scale)


def _qk_norm_rope(xv, gain, cos, sin):
    width = xv.shape[1]
    r = lax.rsqrt(_segment_mean(xv * xv, width) + EPS)
    y = xv * r * gain
    return y * _tile_lanes(cos, width) + _rope_partner(y) * _tile_lanes(sin, width)


def _qk_prep(zq, qgain, kgain, cos, sin, name):
    m = zq.shape[0]
    tm = _row_tile(m)

    def body(q_ref, kv_ref, qg_ref, kg_ref, cos_ref, sin_ref, qr_ref, kr_ref, v_ref):
        cos_v, sin_v = cos_ref[...], sin_ref[...]
        qr_ref[...] = (_qk_norm_rope(q_ref[...], qg_ref[...], cos_v, sin_v) * ATTN_SCALE).astype(BF16)
        kv = kv_ref[...]
        kr_ref[...] = _qk_norm_rope(kv[:, :KV_DIM], kg_ref[...], cos_v, sin_v).astype(BF16)
        v_ref[...] = kv[:, KV_DIM:].astype(BF16)

    return pl.pallas_call(
        body, name=name, grid=(m // tm,),
        in_specs=[pl.BlockSpec((tm, ATTN_DIM), lambda i: (i, 1)), pl.BlockSpec((tm, 2 * KV_DIM), lambda i: (i, 4)),
                  _resident((1, ATTN_DIM), (0, 0)), _resident((1, KV_DIM), (0, 0)),
                  pl.BlockSpec((tm, 128), lambda i: (i, 0)), pl.BlockSpec((tm, 128), lambda i: (i, 0))],
        out_specs=[pl.BlockSpec((tm, ATTN_DIM), lambda i: (i, 0)), pl.BlockSpec((tm, KV_DIM), lambda i: (i, 0)),
                   pl.BlockSpec((tm, KV_DIM), lambda i: (i, 0))],
        out_shape=[_sds((m, ATTN_DIM), BF16), _sds((m, KV_DIM), BF16), _sds((m, KV_DIM), BF16)],
        compiler_params=_cp("parallel"),
    )(zq, zq, qgain, kgain, cos, sin)


STACK_ROWS = N_Q_HEADS * ATTN_BLOCK


def _band_bias():
    qi = jnp.arange(STACK_ROWS)[:, None] % ATTN_BLOCK
    ki = jnp.arange(2 * ATTN_BLOCK)[None, :]
    diff = qi + ATTN_BLOCK - ki
    band = (diff >= 0) & (diff < ATTN_BLOCK)
    first = band & (ki >= ATTN_BLOCK)
    return jnp.where(jnp.stack([first, band]), 0.0, NEG_BIG).astype(F32)


def _stack_heads(v):
    zeros = jnp.zeros((ATTN_BLOCK, HEAD_DIM), v.dtype)
    rows = []
    for h in range(N_Q_HEADS):
        qh = v[:, h * HEAD_DIM:(h + 1) * HEAD_DIM]
        rows.append(jnp.concatenate([qh, zeros] if h < GQA_GROUP else [zeros, qh], axis=1))
    return jnp.concatenate(rows, axis=0)


def _unstack_heads(stacked):
    parts = []
    for h in range(N_Q_HEADS):
        lanes = slice(0, HEAD_DIM) if h < GQA_GROUP else slice(HEAD_DIM, 2 * HEAD_DIM)
        parts.append(stacked[h * ATTN_BLOCK:(h + 1) * ATTN_BLOCK, lanes])
    return jnp.concatenate(parts, axis=1)


def _stacked_sinks(sk_ref):
    row_head = lax.broadcasted_iota(jnp.int32, (STACK_ROWS, 1), 0) >> 7
    col = jnp.full((STACK_ROWS, 1), sk_ref[0], F32)
    for h in range(1, N_Q_HEADS):
        col = jnp.where(row_head == h, sk_ref[h], col)
    return col


def _head_probs(qh, kh, bias, sink):
    s = lax.dot_general(qh, kh, _NT, preferred_element_type=F32) + bias
    mx = jnp.maximum(jnp.max(s, axis=-1, keepdims=True), sink)
    p = jnp.exp(s - mx)
    es = jnp.exp(sink - mx)
    inv = 1.0 / (jnp.sum(p, axis=-1, keepdims=True) + es)
    return p * inv, es * inv


def _attn_fwd(qr, kr, vb, sinks, name):
    m = qr.shape[0]
    nb = m // ATTN_BLOCK

    def body(q_ref, kp_ref, kc_ref, vp_ref, vc_ref, sk_ref, bias_ref, o_ref):
        bias = bias_ref[jnp.minimum(pl.program_id(0), 1)]
        qv = q_ref[...]
        kk = jnp.concatenate([kp_ref[...], kc_ref[...]], axis=0)
        vv = jnp.concatenate([vp_ref[...], vc_ref[...]], axis=0)
        p, _ = _head_probs(_stack_heads(qv), kk, bias, _stacked_sinks(sk_ref))
        o_ref[...] = _unstack_heads(jnp.dot(p.astype(BF16), vv, preferred_element_type=F32)).astype(BF16)

    prev = lambda n: (jnp.maximum(n - 1, 0), 0)
    cur = lambda n: (n, 0)
    return pl.pallas_call(
        body, name=name, grid=(nb,),
        in_specs=[pl.BlockSpec((ATTN_BLOCK, ATTN_DIM), cur),
                  pl.BlockSpec((ATTN_BLOCK, KV_DIM), prev), pl.BlockSpec((ATTN_BLOCK, KV_DIM), cur),
                  pl.BlockSpec((ATTN_BLOCK, KV_DIM), prev), pl.BlockSpec((ATTN_BLOCK, KV_DIM), cur),
                  pl.BlockSpec(memory_space=pltpu.SMEM), _resident((2, STACK_ROWS, 2 * ATTN_BLOCK), (0, 0, 0))],
        out_specs=pl.BlockSpec((ATTN_BLOCK, ATTN_DIM), cur),
        out_shape=_sds((m, ATTN_DIM), BF16),
        compiler_params=_cp("parallel"),
    )(qr, kr, kr, vb, vb, sinks, _band_bias())


def _attn_bwd(do, qr, kr, vb, sinks, name):
    m = qr.shape[0]
    nb = m // ATTN_BLOCK

    def body(do_ref, q_ref, kp_ref, kc_ref, vp_ref, vc_ref, sk_ref, bias_ref, dq_ref, dk_ref, dv_ref, ds_ref,
             carry_k, carry_v):
        n = pl.program_id(0)

        @pl.when(n == 0)
        def _():
            carry_k[...] = jnp.zeros_like(carry_k)
            carry_v[...] = jnp.zeros_like(carry_v)
            ds_ref[...] = jnp.zeros_like(ds_ref)

        @pl.when(n < nb)
        def _():
            bias = bias_ref[jnp.minimum(n, 1)]
            qv = q_ref[...]
            dov = do_ref[...]
            kk = jnp.concatenate([kp_ref[...], kc_ref[...]], axis=0)
            vv = jnp.concatenate([vp_ref[...], vc_ref[...]], axis=0)
            lane = lax.broadcasted_iota(jnp.int32, (1, 128), 1)
            qs = _stack_heads(qv)
            dos = _stack_heads(dov.astype(BF16))
            p, ps = _head_probs(qs, kk, bias, _stacked_sinks(sk_ref))
            dpr = lax.dot_general(dos, vv, _NT, preferred_element_type=F32)
            delta = jnp.sum(p * dpr, axis=-1, keepdims=True)
            dsb = (p * (dpr - delta)).astype(BF16)
            sink_term = ps * delta
            dsink = jnp.zeros((1, 128), F32)
            for h in range(N_Q_HEADS):
                rows = slice(h * ATTN_BLOCK, (h + 1) * ATTN_BLOCK)
                dsink = dsink + jnp.where(lane == h, -jnp.sum(sink_term[rows, :]), 0.0)
            dq_ref[...] = _unstack_heads(jnp.dot(dsb, kk, preferred_element_type=F32) * ATTN_SCALE)
            dkk = lax.dot_general(dsb, qs, _TN, preferred_element_type=F32)
            dvv = lax.dot_general(p.astype(BF16), dos, _TN, preferred_element_type=F32)
            dk_ref[...] = carry_k[...] + dkk[:ATTN_BLOCK, :]
            dv_ref[...] = carry_v[...] + dvv[:ATTN_BLOCK, :]
            carry_k[...] = dkk[ATTN_BLOCK:, :]
            carry_v[...] = dvv[ATTN_BLOCK:, :]
            ds_ref[...] += dsink

        @pl.when(n == nb)
        def _():
            dk_ref[...] = carry_k[...]
            dv_ref[...] = carry_v[...]

    cur = lambda n: (jnp.minimum(n, nb - 1), 0)
    prev = lambda n: (jnp.clip(n - 1, 0, nb - 1), 0)
    return pl.pallas_call(
        body, name=name, grid=(nb + 1,),
        in_specs=[pl.BlockSpec((ATTN_BLOCK, ATTN_DIM), cur), pl.BlockSpec((ATTN_BLOCK, ATTN_DIM), cur),
                  pl.BlockSpec((ATTN_BLOCK, KV_DIM), prev), pl.BlockSpec((ATTN_BLOCK, KV_DIM), cur),
                  pl.BlockSpec((ATTN_BLOCK, KV_DIM), prev), pl.BlockSpec((ATTN_BLOCK, KV_DIM), cur),
                  pl.BlockSpec(memory_space=pltpu.SMEM), _resident((2, STACK_ROWS, 2 * ATTN_BLOCK), (0, 0, 0))],
        out_specs=[pl.BlockSpec((ATTN_BLOCK, ATTN_DIM), cur), pl.BlockSpec((ATTN_BLOCK, KV_DIM), prev),
                   pl.BlockSpec((ATTN_BLOCK, KV_DIM), prev), pl.BlockSpec((1, 128), lambda n: (0, 0))],
        out_shape=[_sds((m, ATTN_DIM), F32), _sds((m, KV_DIM), F32), _sds((m, KV_DIM), F32), _sds((1, 128), F32)],
        scratch_shapes=[pltpu.VMEM((ATTN_BLOCK, KV_DIM), F32), pltpu.VMEM((ATTN_BLOCK, KV_DIM), F32)],
        compiler_params=_cp("arbitrary"),
    )(do, qr, kr, kr, vb, vb, sinks, _band_bias())


def _qk_norm_rope_bwd(dout, xv, gain, cos, sin):
    width = xv.shape[1]
    r = lax.rsqrt(_segment_mean(xv * xv, width) + EPS)
    xn = xv * r
    dy = dout * _tile_lanes(cos, width) + _rope_partner(dout * _tile_lanes(sin, width))
    dxn = dy * gain
    dx = r * (dxn - xn * _segment_mean(dxn * xn, width))
    return dx, jnp.sum(dy * xn, axis=0, keepdims=True)


def _fold_heads(v):
    out = v[:, :HEAD_DIM]
    for h in range(1, v.shape[1] // HEAD_DIM):
        out = out + v[:, h * HEAD_DIM:(h + 1) * HEAD_DIM]
    return out


def _qk_bwd(dq, dk, dv, zq, qgain, kgain, cos, sin, name):
    m = zq.shape[0]
    tm = _row_tile(m)

    def body(dq_ref, dk_ref, dv_ref, q_ref, kv_ref, qg_ref, kg_ref, cos_ref, sin_ref, dz_ref, dqg_ref, dkg_ref):
        @pl.when(pl.program_id(0) == 0)
        def _():
            dqg_ref[...] = jnp.zeros_like(dqg_ref)
            dkg_ref[...] = jnp.zeros_like(dkg_ref)

        cos_v, sin_v = cos_ref[...], sin_ref[...]
        dxq, dgq = _qk_norm_rope_bwd(dq_ref[...], q_ref[...], qg_ref[...], cos_v, sin_v)
        dxk, dgk = _qk_norm_rope_bwd(dk_ref[...], kv_ref[:, :KV_DIM], kg_ref[...], cos_v, sin_v)
        dz_ref[:, :ATTN_DIM] = dxq.astype(BF16)
        dz_ref[:, ATTN_DIM:ATTN_DIM + KV_DIM] = dxk.astype(BF16)
        dz_ref[:, ATTN_DIM + KV_DIM:] = dv_ref[...].astype(BF16)
        dqg_ref[...] += _fold_heads(dgq)
        dkg_ref[...] += _fold_heads(dgk)

    row = lambda i: (i, 0)
    return pl.pallas_call(
        body, name=name, grid=(m // tm,),
        in_specs=[pl.BlockSpec((tm, ATTN_DIM), row), pl.BlockSpec((tm, KV_DIM), row), pl.BlockSpec((tm, KV_DIM), row),
                  pl.BlockSpec((tm, ATTN_DIM), lambda i: (i, 1)), pl.BlockSpec((tm, 2 * KV_DIM), lambda i: (i, 4)),
                  _resident((1, ATTN_DIM), (0, 0)), _resident((1, KV_DIM), (0, 0)),
                  pl.BlockSpec((tm, 128), row), pl.BlockSpec((tm, 128), row)],
        out_specs=[pl.BlockSpec((tm, ATTN_DIM + 2 * KV_DIM), row), pl.BlockSpec((1, HEAD_DIM), lambda i: (0, 0)),
                   pl.BlockSpec((1, HEAD_DIM), lambda i: (0, 0))],
        out_shape=[_sds((m, ATTN_DIM + 2 * KV_DIM), BF16), _sds((1, HEAD_DIM), F32), _sds((1, HEAD_DIM), F32)],
        compiler_params=_cp("arbitrary"),
    )(dq, dk, dv, zq, zq, qgain, kgain, cos, sin)


def _merge_fwd(x, p, o, zg, w_pb, w_ab, w_out, name):
    m, d = x.shape
    tm = _row_tile(m)

    def body(x_ref, p_ref, o_ref, zg_ref, wpb_ref, wab_ref, wo_ref, xo_ref, mix_ref):
        a = jnp.dot(p_ref[...], wpb_ref[...], preferred_element_type=F32)
        b = jnp.dot(o_ref[...], wab_ref[...], preferred_element_type=F32)
        mix = (_sigmoid(zg_ref[:, :d].astype(F32)) * a + _sigmoid(zg_ref[:, d:].astype(F32)) * b).astype(BF16)
        mix_ref[...] = mix
        xo_ref[...] = x_ref[...] + jnp.dot(mix, wo_ref[...], preferred_element_type=F32)

    row = lambda i: (i, 0)
    return pl.pallas_call(
        body, name=name, grid=(m // tm,),
        in_specs=[pl.BlockSpec((tm, d), row), pl.BlockSpec((tm, POOL_DIM), row), pl.BlockSpec((tm, ATTN_DIM), row),
                  pl.BlockSpec((tm, 2 * d), row), _resident((POOL_DIM, d), (0, 0)), _resident((ATTN_DIM, d), (0, 0)),
                  _resident((d, d), (0, 0))],
        out_specs=[pl.BlockSpec((tm, d), row)] * 2,
        out_shape=[_sds((m, d), F32), _sds((m, d), BF16)],
        compiler_params=_cp("parallel"),
    )(x, p, o, zg, w_pb, w_ab, w_out)


def _merge_bwd(dy, p, o, zg, w_out, w_pb, w_ab, name):
    m, d = dy.shape
    tm = _row_tile(m)

    def body(dy_ref, p_ref, o_ref, zg_ref, wo_ref, wpb_ref, wab_ref, dyb_ref, da_ref, db_ref, dp_ref, do_ref, dzg_ref):
        dyb = dy_ref[...].astype(BF16)
        dyb_ref[...] = dyb
        dmix = lax.dot_general(dyb, wo_ref[...], _NT, preferred_element_type=F32)
        gp = _sigmoid(zg_ref[:, :d].astype(F32))
        ga = _sigmoid(zg_ref[:, d:].astype(F32))
        da = (dmix * gp).astype(BF16)
        db = (dmix * ga).astype(BF16)
        da_ref[...] = da
        db_ref[...] = db
        a = jnp.dot(p_ref[...], wpb_ref[...], preferred_element_type=F32)
        b = jnp.dot(o_ref[...], wab_ref[...], preferred_element_type=F32)
        dzg_ref[:, :d] = (dmix * a * gp * (1.0 - gp)).astype(BF16)
        dzg_ref[:, d:] = (dmix * b * ga * (1.0 - ga)).astype(BF16)
        dp_ref[...] = lax.dot_general(da, wpb_ref[...], _NT, preferred_element_type=F32)
        do_ref[...] = lax.dot_general(db, wab_ref[...], _NT, preferred_element_type=F32).astype(BF16)

    row = lambda i: (i, 0)
    return pl.pallas_call(
        body, name=name, grid=(m // tm,),
        in_specs=[pl.BlockSpec((tm, d), row), pl.BlockSpec((tm, POOL_DIM), row), pl.BlockSpec((tm, ATTN_DIM), row),
                  pl.BlockSpec((tm, 2 * d), row), _resident((d, d), (0, 0)), _resident((POOL_DIM, d), (0, 0)),
                  _resident((ATTN_DIM, d), (0, 0))],
        out_specs=[pl.BlockSpec((tm, d), row), pl.BlockSpec((tm, d), row), pl.BlockSpec((tm, d), row),
                   pl.BlockSpec((tm, POOL_DIM), row), pl.BlockSpec((tm, ATTN_DIM), row), pl.BlockSpec((tm, 2 * d), row)],
        out_shape=[_sds((m, d), BF16), _sds((m, d), BF16), _sds((m, d), BF16), _sds((m, POOL_DIM), F32),
                   _sds((m, ATTN_DIM), BF16), _sds((m, 2 * d), BF16)],
        compiler_params=_cp("parallel"),
    )(dy, p, o, zg, w_out, w_pb, w_ab)


def _mix_bwd_x(dy, x, ln, dzp, dzqkv, dzg, w_in, name):
    m, d = dy.shape
    n_in = w_in.shape[0]
    tm = _row_tile(m)

    def body(dy_ref, x_ref, ln_ref, dzp_ref, dzq_ref, dzg_ref, w_ref, dx_ref, dln_ref):
        @pl.when(pl.program_id(0) == 0)
        def _():
            dln_ref[...] = jnp.zeros_like(dln_ref)

        dh = jnp.dot(dzp_ref[...], w_ref[:POOL_DIM, :], preferred_element_type=F32)
        dh += jnp.dot(dzq_ref[...], w_ref[POOL_DIM:QKV_END, :], preferred_element_type=F32)
        dh += jnp.dot(dzg_ref[...], w_ref[QKV_END:, :], preferred_element_type=F32)
        dx, dgain = _rms_bwd(dh, x_ref[...], ln_ref[...])
        dx_ref[...] = dy_ref[...] + dx
        dln_ref[...] += jnp.sum(dgain, axis=0, keepdims=True)

    row = lambda i: (i, 0)
    return pl.pallas_call(
        body, name=name, grid=(m // tm,),
        in_specs=[pl.BlockSpec((tm, d), row), pl.BlockSpec((tm, d), row), _resident((1, d), (0, 0)),
                  pl.BlockSpec((tm, POOL_DIM), row), pl.BlockSpec((tm, QKV_END - POOL_DIM), row),
                  pl.BlockSpec((tm, n_in - QKV_END), row), _resident((n_in, d), (0, 0))],
        out_specs=[pl.BlockSpec((tm, d), row), pl.BlockSpec((1, d), lambda i: (0, 0))],
        out_shape=[_sds((m, d), F32), _sds((1, d), F32)],
        compiler_params=_cp("arbitrary"),
    )(dy, x, ln, dzp, dzqkv, dzg, w_in)


def _loss_head(y, target, name):
    m, d = y.shape
    tm = _row_tile(m)

    def body(y_ref, t_ref, loss_ref, dy_ref):
        @pl.when(pl.program_id(0) == 0)
        def _():
            loss_ref[...] = jnp.zeros_like(loss_ref)

        diff = y_ref[...] - t_ref[...]
        dy_ref[...] = diff * (1.0 / d)
        loss_ref[...] += 0.5 * jnp.sum(jnp.mean(diff * diff, axis=-1, keepdims=True), axis=0, keepdims=True)

    row = lambda i: (i, 0)
    return pl.pallas_call(
        body, name=name, grid=(m // tm,),
        in_specs=[pl.BlockSpec((tm, d), row), pl.BlockSpec((tm, d), row)],
        out_specs=[pl.BlockSpec((1, 1), lambda i: (0, 0)), pl.BlockSpec((tm, d), row)],
        out_shape=[_sds((1, 1), F32), _sds((m, d), F32)],
        compiler_params=_cp("arbitrary"),
    )(y, target)


def _adamw_math(g, w, m, v):
    m2 = ADAM_B1 * m + (1.0 - ADAM_B1) * g
    v2 = ADAM_B2 * v + (1.0 - ADAM_B2) * (g * g)
    m_hat = m2 / (1.0 - ADAM_B1 ** ADAM_STEP)
    v_hat = v2 / (1.0 - ADAM_B2 ** ADAM_STEP)
    delta = -ADAM_LR * (m_hat / (jnp.sqrt(v_hat) + ADAM_EPS) + ADAM_WD * w)
    return delta, m2, v2


def _sum_parts(parts_ref):
    g = parts_ref[0].astype(F32)
    for s in range(1, N_DEV):
        g = g + parts_ref[s].astype(F32)
    return g


def _adamw_sharded(parts, w, m, v, name, after=None):
    n_layers, rows, cols = w.shape
    tr = max(t for t in range(16, rows + 1, 16) if rows % t == 0 and t * cols <= ADAMW_BLOCK_ELEMS)
    nr = rows // tr
    deps = [] if after is None else [after]

    def body(*refs):
        part_refs = refs[:n_layers]
        w_ref, m_ref, v_ref = refs[n_layers:n_layers + 3]
        g_out, d_out, m_out, v_out = refs[-4:]
        layer = pl.program_id(0)
        for l in range(n_layers):
            @pl.when(layer == l)
            def _(l=l):
                g = _sum_parts(part_refs[l])
                delta, m2, v2 = _adamw_math(g, w_ref[...], m_ref[...], v_ref[...])
                g_out[...] = g
                d_out[...] = delta
                m_out[...] = m2
                v_out[...] = v2

    def part_map(l):
        return lambda layer, r: (0, jnp.where(layer == l, r, jnp.where(layer < l, 0, nr - 1)), 0)

    wspec = pl.BlockSpec((None, tr, cols), lambda layer, r: (layer, r, 0))
    return pl.pallas_call(
        body, name=name, grid=(n_layers, nr),
        in_specs=([pl.BlockSpec((N_DEV, tr, cols), part_map(l)) for l in range(n_layers)] + [wspec] * 3
                  + [pl.BlockSpec(memory_space=pl.ANY)] * len(deps)),
        out_specs=[wspec] * 4,
        out_shape=[_sds(w.shape, F32)] * 4,
        compiler_params=_cp("arbitrary", "arbitrary"),
    )(*parts, w, m, v, *deps)


def _adamw_packed(parts, w, m, v, name):
    def body(p_ref, w_ref, m_ref, v_ref, g_out, d_out, m_out, v_out):
        g = _sum_parts(p_ref)
        delta, m2, v2 = _adamw_math(g, w_ref[...], m_ref[...], v_ref[...])
        g_out[...] = g
        d_out[...] = delta
        m_out[...] = m2
        v_out[...] = v2

    return pl.pallas_call(
        body, name=name, out_shape=[_sds(w.shape, F32)] * 4,
        compiler_params=pltpu.CompilerParams(vmem_limit_bytes=VMEM_LIMIT_BYTES),
    )(parts, w, m, v)


_SMALL = ("ln_ffn1", "ln_mix", "pool_w", "pool_scale", "q_norm", "k_norm", "sinks", "ln_ffn2")


def _pack_small(arrs):
    rows = []
    for a in arrs:
        flat = a.reshape(-1)
        pad = (-flat.shape[0]) % 1024
        rows.append(jnp.pad(flat, (0, pad)).reshape(-1, 128))
    return jnp.concatenate(rows, axis=0)


def _unpack_small(packed, like):
    out, r0 = [], 0
    for a in like:
        size = a.size
        nrows = (size + 1023) // 1024 * 8
        out.append(packed[r0:r0 + nrows].reshape(-1)[:size].reshape(a.shape))
        r0 += nrows
    return out


def _rope_tables(m):
    pos = jnp.arange(m, dtype=F32)
    inv_freq = ROPE_THETA ** (-jnp.arange(0, ROT_DIM, 2, dtype=F32) / ROT_DIM)
    ang = pos[:, None] * inv_freq[None, :]
    cos8, sin8 = jnp.cos(ang), jnp.sin(ang)
    rest = HEAD_DIM - ROT_DIM
    cos64 = jnp.concatenate([cos8, cos8, jnp.ones((m, rest), F32)], axis=1)
    sin64 = jnp.concatenate([-sin8, sin8, jnp.zeros((m, rest), F32)], axis=1)
    return jnp.tile(cos64, (1, 2)), jnp.tile(sin64, (1, 2))


def _to_shard_major_cols(w):
    k = w.shape[0]
    return w.reshape(k, N_DEV, -1).transpose(1, 0, 2)


def _from_shard_major_cols(w):
    return w.transpose(1, 0, 2).reshape(w.shape[1], -1)


def kernel(x, ln_ffn1, w_ffn1_gu, w_ffn1_down, ln_mix, w_in, pool_w, pool_scale, w_pool_branch, q_norm, k_norm, sinks, w_attn_branch, w_out, ln_ffn2, w_ffn2_gu, w_ffn2_down, loss_target, m_ln_ffn1, m_w_ffn1_gu, m_w_ffn1_down, m_ln_mix, m_w_in, m_pool_w, m_pool_scale, m_w_pool_branch, m_q_norm, m_k_norm, m_sinks, m_w_attn_branch, m_w_out, m_ln_ffn2, m_w_ffn2_gu, m_w_ffn2_down, v_ln_ffn1, v_w_ffn1_gu, v_w_ffn1_down, v_ln_mix, v_w_in, v_pool_w, v_pool_scale, v_w_pool_branch, v_q_norm, v_k_norm, v_sinks, v_w_attn_branch, v_w_out, v_ln_ffn2, v_w_ffn2_gu, v_w_ffn2_down):
    weights = dict(ln_ffn1=ln_ffn1, w_ffn1_gu=w_ffn1_gu, w_ffn1_down=w_ffn1_down, ln_mix=ln_mix, w_in=w_in, pool_w=pool_w,
                   pool_scale=pool_scale, w_pool_branch=w_pool_branch, q_norm=q_norm, k_norm=k_norm, sinks=sinks,
                   w_attn_branch=w_attn_branch, w_out=w_out, ln_ffn2=ln_ffn2, w_ffn2_gu=w_ffn2_gu, w_ffn2_down=w_ffn2_down)
    mom_m = dict(ln_ffn1=m_ln_ffn1, w_ffn1_gu=m_w_ffn1_gu, w_ffn1_down=m_w_ffn1_down, ln_mix=m_ln_mix, w_in=m_w_in,
                 pool_w=m_pool_w, pool_scale=m_pool_scale, w_pool_branch=m_w_pool_branch, q_norm=m_q_norm, k_norm=m_k_norm,
                 sinks=m_sinks, w_attn_branch=m_w_attn_branch, w_out=m_w_out, ln_ffn2=m_ln_ffn2, w_ffn2_gu=m_w_ffn2_gu,
                 w_ffn2_down=m_w_ffn2_down)
    mom_v = dict(ln_ffn1=v_ln_ffn1, w_ffn1_gu=v_w_ffn1_gu, w_ffn1_down=v_w_ffn1_down, ln_mix=v_ln_mix, w_in=v_w_in,
                 pool_w=v_pool_w, pool_scale=v_pool_scale, w_pool_branch=v_w_pool_branch, q_norm=v_q_norm, k_norm=v_k_norm,
                 sinks=v_sinks, w_attn_branch=v_w_attn_branch, w_out=v_w_out, ln_ffn2=v_ln_ffn2, w_ffn2_gu=v_w_ffn2_gu,
                 w_ffn2_down=v_w_ffn2_down)
    order = ("ln_ffn1", "w_ffn1_gu", "w_ffn1_down", "ln_mix", "w_in", "pool_w", "pool_scale", "w_pool_branch", "q_norm",
             "k_norm", "sinks", "w_attn_branch", "w_out", "ln_ffn2", "w_ffn2_gu", "w_ffn2_down")
    big = ("w_ffn1_gu", "w_ffn1_down", "w_in", "w_pool_branch", "w_attn_branch", "w_out", "w_ffn2_gu", "w_ffn2_down")

    transposed = ("w_ffn1_gu", "w_ffn2_gu", "w_in")
    for group in (weights, mom_m, mom_v):
        for k in transposed:
            group[k] = jnp.swapaxes(group[k], 1, 2)

    n_layers = ln_ffn1.shape[0]
    seq, d = x.shape[-2], x.shape[-1]
    xs = x.reshape(seq, d)
    target = loss_target.reshape(seq, d)
    cos, sin = _rope_tables(seq)

    first_keys = ("w_ffn1_gu", "w_ffn1_down")
    rest_keys = tuple(k for k in big if k not in first_keys)

    def layer_shards(l, keys=big):
        return [weights[k][l].astype(BF16) for k in keys]

    def first_weights(l, g):
        return dict(gu1=g["w_ffn1_gu"].reshape(2, -1, d), down1=g["w_ffn1_down"].reshape(-1, d), ln1=ln_ffn1[l][None])

    def rest_weights(l, g):
        return dict(
            gu2=g["w_ffn2_gu"].reshape(2, -1, d), down2=g["w_ffn2_down"].reshape(-1, d),
            w_in=g["w_in"].reshape(-1, d), w_pb=_from_shard_major_cols(g["w_pool_branch"]),
            w_ab=_from_shard_major_cols(g["w_attn_branch"]), w_out=g["w_out"].reshape(d, d),
            ln_mix=ln_mix[l][None], ln2=ln_ffn2[l][None], pool_w=pool_w[l],
            pool_scale=pool_scale[l][None], sinks=sinks[l],
            qgain=jnp.tile(q_norm[l], N_Q_HEADS)[None], kgain=jnp.tile(k_norm[l], N_KV_HEADS)[None])

    def layer_weights(l, full):
        g = dict(zip(big, full))
        return {**first_weights(l, g), **rest_weights(l, g)}

    got = _all_gather_many(layer_shards(0, first_keys), name="gather_first_l0")
    gathered = [first_weights(0, dict(zip(first_keys, got)))]
    rest_in_flight, token = _exchange_start(layer_shards(0, rest_keys), scatter=False, name="gather_start_l0",
                                            peers=_CHIP_PEERS)
    saved = []
    cur = xs
    for l in range(n_layers):
        lw = gathered[l]
        s = dict(x0=cur)
        in_flight = None
        if l + 1 < n_layers:
            in_flight, token = _exchange_start(layer_shards(l + 1), scatter=False, name=f"gather_start_l{l + 1}",
                                               after=token)
        s["h1"], s["gu1"], act1 = _ffn_up(cur, lw["ln1"], lw["gu1"], name=f"ffn1_up_l{l}", after=token)
        s["act1"] = act1
        x1 = _ffn_down(cur, act1, lw["down1"], name=f"ffn1_down_l{l}")
        s["x1"] = x1
        if l == 0:
            got = _forward_to_sibling(_exchange_wait(rest_in_flight, x1, name="gather_wait_l0"), name="gather_forward_l0")
            lw.update(rest_weights(0, dict(zip(rest_keys, got))))
        s["h2"], zq, zg = _mix_in(x1, lw["ln_mix"], lw["w_in"], name=f"mix_in_l{l}")
        s["zq"], s["zg"] = zq, zg
        s["d"], s["p"] = _pool_fwd(zq, lw["pool_w"], lw["pool_scale"], name=f"pool_fwd_l{l}")
        s["qr"], s["kr"], s["vb"] = _qk_prep(zq, lw["qgain"], lw["kgain"], cos, sin, name=f"qk_prep_l{l}")
        s["o"] = _attn_fwd(s["qr"], s["kr"], s["vb"], lw["sinks"], name=f"attn_fwd_l{l}")
        x2, s["mix"] = _merge_fwd(x1, s["p"], s["o"], zg, lw["w_pb"], lw["w_ab"], lw["w_out"],
                                                  name=f"merge_fwd_l{l}")
        s["x2"] = x2
        s["h3"], s["gu2"], act2 = _ffn_up(x2, lw["ln2"], lw["gu2"], name=f"ffn2_up_l{l}")
        s["act2"] = act2
        cur = _ffn_down(x2, act2, lw["down2"], name=f"ffn2_down_l{l}")
        saved.append(s)
        if in_flight is not None:
            gathered.append(layer_weights(l + 1, _exchange_wait(in_flight, cur, name=f"gather_wait_l{l + 1}")))

    loss_local, dy = _loss_head(cur, target, name="loss_head")
    loss = lax.psum(loss_local[0, 0], MESH_AXES)

    small_grads = {k: [None] * n_layers for k in _SMALL}
    received = {k: [None] * n_layers for k in big}
    big_late = ("w_ffn1_gu", "w_ffn1_down")
    big_early = tuple(k for k in big if k not in big_late)
    early_in_flight, late_in_flight = [None] * n_layers, [None] * n_layers
    token = None
    for l in reversed(range(n_layers)):
        lw, s = gathered[l], saved[l]
        d_ff = lw["down1"].shape[0]

        def ffn_weight_grads(dyh, dgu, h, act, tag):
            dw_down = _matmul_tn(act[None], dyh[None], name=f"{tag}_dw_down_l{l}", a_chunk=d_ff // 2)
            dw_gu = _matmul_tn(dgu, h[None], name=f"{tag}_dw_gu_l{l}", a_chunk=d_ff // 2)
            return dw_gu.reshape(N_DEV, -1, d), dw_down.reshape(N_DEV, -1, d)

        dyh, dgu, dx2, dln2 = _ffn_bwd(dy, s["x2"], lw["ln2"], s["gu2"], lw["down2"], lw["gu2"],
                                       name=f"ffn2_bwd_l{l}", after=token)
        dw_gu2, dw_down2 = ffn_weight_grads(dyh, dgu, s["h3"], s["act2"], "ffn2")

        dyb, da, db, dp, do, dzg = _merge_bwd(dx2, s["p"], s["o"], s["zg"], lw["w_out"], lw["w_pb"], lw["w_ab"],
                                              name=f"merge_bwd_l{l}")
        dw_out = _matmul_tn(s["mix"][None], dyb[None], name=f"dw_out_l{l}")[0]
        dw_pb = _matmul_tn(s["p"][None], da[None], name=f"dw_pb_l{l}")[0]
        dw_ab = _matmul_tn(s["o"][None], db[None], name=f"dw_ab_l{l}")[0]
        dzp, dpw, dsc = _pool_bwd(dp, s["d"], lw["pool_w"], lw["pool_scale"], name=f"pool_bwd_l{l}")
        dq, dk, dv, dsinks = _attn_bwd(do, s["qr"], s["kr"], s["vb"], lw["sinks"], name=f"attn_bwd_l{l}")
        dzqkv, dqg, dkg = _qk_bwd(dq, dk, dv, s["zq"], lw["qgain"], lw["kgain"], cos, sin, name=f"qk_bwd_l{l}")
        dw_in = jnp.concatenate([_matmul_tn(dzp[None], s["h2"][None], name=f"dw_in_pool_l{l}")[0],
                                 _matmul_tn(dzqkv[None], s["h2"][None], name=f"dw_in_qkv_l{l}")[0],
                                 _matmul_tn(dzg[None], s["h2"][None], name=f"dw_in_gate_l{l}")[0]], axis=0)
        dx1, dlnm = _mix_bwd_x(dx2, s["x1"], lw["ln_mix"], dzp, dzqkv, dzg, lw["w_in"], name=f"mix_bwd_x_l{l}")

        partial = dict(w_in=dw_in.reshape(N_DEV, -1, d), w_pool_branch=_to_shard_major_cols(dw_pb),
                       w_attn_branch=_to_shard_major_cols(dw_ab), w_out=dw_out.reshape(N_DEV, d // N_DEV, d),
                       w_ffn2_gu=dw_gu2, w_ffn2_down=dw_down2)
        early_in_flight[l], token = _exchange_start([partial[k] for k in big_early], scatter=True,
                                                    name=f"grads_early_start_l{l}")

        dyh, dgu, dy, dln1 = _ffn_bwd(dx1, s["x0"], lw["ln1"], s["gu1"], lw["down1"], lw["gu1"],
                                      name=f"ffn1_bwd_l{l}", after=token)
        dw_gu1, dw_down1 = ffn_weight_grads(dyh, dgu, s["h1"], s["act1"], "ffn1")
        late_in_flight[l], token = _exchange_start([dw_gu1, dw_down1], scatter=True, name=f"grads_late_start_l{l}")
        small_grads["ln_ffn1"][l] = dln1[0]
        small_grads["ln_mix"][l] = dlnm[0]
        small_grads["ln_ffn2"][l] = dln2[0]
        small_grads["pool_w"][l] = dpw
        small_grads["pool_scale"][l] = dsc[0]
        small_grads["q_norm"][l] = dqg[0]
        small_grads["k_norm"][l] = dkg[0]
        small_grads["sinks"][l] = dsinks[0, :N_Q_HEADS]

    grad_x = dy.reshape(x.shape)

    small_w = [weights[k] for k in _SMALL]
    packed_g = _pack_small([jnp.stack(small_grads[k]).reshape(weights[k].shape) for k in _SMALL])
    small_in_flight, after = _exchange_start([packed_g], scatter=False, name="small_grads_start")
    for l in reversed(range(n_layers)):
        got = _exchange_wait(early_in_flight[l], after, name=f"grads_early_wait_l{l}")
        after = got[0]
        for k, r in zip(big_early, got):
            received[k][l] = r

    grads, deltas, new_m, new_v = {}, {}, {}, {}

    def adamw(k, after):
        w = weights[k]
        shape2 = (n_layers, -1, w.shape[-1])
        parts = [r.reshape(N_DEV, -1, w.shape[-1]) for r in received[k]]
        outs = _adamw_sharded(parts, w.reshape(shape2), mom_m[k].reshape(shape2), mom_v[k].reshape(shape2),
                              name=f"adamw_{k}", after=after)
        grads[k], deltas[k], new_m[k], new_v[k] = (o.reshape(w.shape) for o in outs)
        return outs[0]

    after = None
    for k in big_early:
        after = adamw(k, after)
    for l in reversed(range(n_layers)):
        got = _exchange_wait(late_in_flight[l], after, name=f"grads_late_wait_l{l}")
        after = got[0]
        for k, r in zip(big_late, got):
            received[k][l] = r
    after = None
    for k in big_late:
        after = adamw(k, after)

    (parts_small,) = _exchange_wait(small_in_flight, after, name="small_grads_wait")
    outs = _adamw_packed(parts_small, _pack_small(small_w), _pack_small([mom_m[k] for k in _SMALL]),
                         _pack_small([mom_v[k] for k in _SMALL]), name="adamw_small")
    for res, o in zip((grads, deltas, new_m, new_v), outs):
        for k, a in zip(_SMALL, _unpack_small(o, small_w)):
            res[k] = a

    for res in (grads, deltas, new_m, new_v):
        for k in transposed:
            res[k] = jnp.swapaxes(res[k], 1, 2)
    return (loss, grad_x, *[grads[k] for k in order], *[deltas[k] for k in order],
            *[new_m[k] for k in order], *[new_v[k] for k in order])
```

```python
import functools

import jax
import jax.numpy as jnp
from jax import lax
from jax.experimental import pallas as pl
from jax.experimental.pallas import tpu as pltpu

F32 = jnp.float32
BF16 = jnp.bfloat16

N_DEV = 8
MESH_AXES = ("x", "y", "c")
EPS = 1e-6

HEAD_DIM = 64
N_Q_HEADS = 8
N_KV_HEADS = 2
GQA_GROUP = N_Q_HEADS // N_KV_HEADS
ATTN_BLOCK = 128
ATTN_SCALE = HEAD_DIM ** -0.5
ROPE_THETA = 500000.0
ROT_DIM = 16
POOL_WINDOWS = (2, 4, 8, 16)
POOL_HALO = 16
GROUP_DIM = 128
POOL_DIM = 512
ATTN_DIM = 512
KV_DIM = 128
QKV_END = POOL_DIM + ATTN_DIM + 2 * KV_DIM

ADAM_LR = 0.001
ADAM_B1 = 0.9
ADAM_B2 = 0.999
ADAM_EPS = 1e-08
ADAM_WD = 0.01
ADAM_STEP = 10

ROW_TILE = 512
TN_ROW_TILE = 2048
FFN_CHUNK = 256
FFN_BWD_ROW_TILE = 256
VMEM_LIMIT_BYTES = 56 << 20
ADAMW_BLOCK_ELEMS = 192 * 1024
NEG_BIG = -1e30

_NT = (((1,), (1,)), ((), ()))
_TN = (((0,), (0,)), ((), ()))


def _cp(*sem):
    return pltpu.CompilerParams(dimension_semantics=sem, vmem_limit_bytes=VMEM_LIMIT_BYTES)


def _resident(block, index):
    return pl.BlockSpec(block, lambda *_: index, pipeline_mode=pl.Buffered(1))


def _row_tile(m):
    return min(ROW_TILE, m)


def _sds(shape, dtype):
    return jax.ShapeDtypeStruct(shape, dtype)


def _mesh_pos():
    return lax.axis_index("x"), lax.axis_index("y"), lax.axis_index("c")


def _all_gather_many(shards, name, after=None):
    n = len(shards)

    deps = [] if after is None else [after]

    def body(*refs):
        ins, outs = refs[:n], refs[n + len(deps):2 * n + len(deps)]
        send_sems, recv_sems, local_sems = refs[2 * n + len(deps):]
        x, y, c = _mesh_pos()
        me, sibling = (x, y, c), (x, y, 1 - c)
        chips = [(1 - x, y), (x, 1 - y), (1 - x, 1 - y)]

        def slot(a, pos):
            return outs[a].at[4 * pos[0] + 2 * pos[1] + pos[2]]

        def copy(a, k, block, to, src=None):
            return pltpu.make_async_remote_copy(
                src_ref=slot(a, block) if src is None else src, dst_ref=slot(a, block),
                send_sem=send_sems.at[a, k], recv_sem=recv_sems.at[a, k],
                device_id=to, device_id_type=pl.DeviceIdType.MESH)

        mine = [pltpu.make_async_copy(ins[a], slot(a, me), local_sems.at[a]) for a in range(n)]
        for cp in mine:
            cp.start()
        first = []
        for a in range(n):
            first.append(copy(a, 0, me, sibling, src=ins[a]))
            for j, chip in enumerate(chips):
                first.append(copy(a, 1 + j, me, (*chip, c), src=ins[a]))
        for cp in first:
            cp.start()
        passed = []
        for j, chip in enumerate(chips):
            for a in range(n):
                copy(a, 1 + j, (*chip, c), me).wait_recv()
                fwd = copy(a, 4 + j, (*chip, c), sibling)
                fwd.start()
                passed.append(fwd)
        for a in range(n):
            copy(a, 0, sibling, me).wait_recv()
        for j, chip in enumerate(chips):
            for a in range(n):
                copy(a, 4 + j, (*chip, 1 - c), me).wait_recv()
        for cp in first + passed:
            cp.wait_send()
        for cp in mine:
            cp.wait()

    any_spec = pl.BlockSpec(memory_space=pl.ANY)
    return pl.pallas_call(
        body, name=name,
        out_shape=[_sds((N_DEV,) + s.shape, s.dtype) for s in shards],
        in_specs=[any_spec] * (n + len(deps)), out_specs=[any_spec] * n,
        scratch_shapes=[pltpu.SemaphoreType.DMA((n, 7)), pltpu.SemaphoreType.DMA((n, 7)),
                        pltpu.SemaphoreType.DMA((n,))],
    )(*shards, *deps)


_ALL_PEERS = tuple(range(1, N_DEV))
_CHIP_PEERS = (1, 2, 4, 6)


def _direct_copies(src, land, send_sem, recv_sem, local_sem, scatter, peers=_ALL_PEERS):
    x, y, c = _mesh_pos()
    me = 4 * x + 2 * y + c
    local = pltpu.make_async_copy(src.at[me] if scatter else src, land.at[me], local_sem)
    remote = []
    for k in peers:
        px = 1 - x if k & 4 else x
        py = 1 - y if k & 2 else y
        pc = 1 - c if k & 1 else c
        remote.append(pltpu.make_async_remote_copy(
            src_ref=src.at[4 * px + 2 * py + pc] if scatter else src, dst_ref=land.at[me],
            send_sem=send_sem, recv_sem=recv_sem, device_id=(px, py, pc), device_id_type=pl.DeviceIdType.MESH))
    every = land.at[pl.ds(0, len(peers))]
    drain = pltpu.make_async_remote_copy(src_ref=every, dst_ref=every, send_sem=send_sem, recv_sem=recv_sem,
                                         device_id=(x, y, c), device_id_type=pl.DeviceIdType.MESH)
    return local, remote, drain


_HBM_SPEC = pl.BlockSpec(memory_space=pltpu.HBM)
_SEM_SPEC = pl.BlockSpec(memory_space=pltpu.SEMAPHORE)
_DATAFLOW = pltpu.SideEffectType.DATAFLOW_SIDE_EFFECTING
_SEMS_PER_ARRAY = 3


def _exchange_start(srcs, scatter, name, peers=_ALL_PEERS, after=None):
    n = len(srcs)
    deps = [] if after is None else [after]
    n_sems = _SEMS_PER_ARRAY * n
    land_shapes = [s.shape if scatter else (N_DEV,) + s.shape for s in srcs]

    def body(*refs):
        ins, lands = refs[:n], refs[n:2 * n]
        sems = refs[2 * n + len(deps):2 * n + len(deps) + n_sems]
        for a in range(n):
            local, remote, _ = _direct_copies(ins[a], lands[a], *sems[3 * a:3 * a + 3], scatter, peers)
            local.start()
            for cp in remote:
                cp.start()
        refs[-1][...] = jnp.zeros_like(refs[-1])

    outs = pl.pallas_call(
        body, name=name,
        out_shape=(*[pltpu.SemaphoreType.DMA(())] * n_sems,
                   *[pltpu.HBM(s.shape, s.dtype) for s in srcs],
                   *[pltpu.HBM(shape, s.dtype) for shape, s in zip(land_shapes, srcs)],
                   _sds((8, 128), F32)),
        in_specs=[_HBM_SPEC] * (2 * n) + [pl.BlockSpec(memory_space=pl.ANY)] * len(deps),
        out_specs=(*[_SEM_SPEC] * n_sems, *[_HBM_SPEC] * (2 * n), pl.BlockSpec(memory_space=pltpu.VMEM)),
        input_output_aliases={i: n_sems + i for i in range(2 * n)},
        compiler_params=pltpu.CompilerParams(has_side_effects=_DATAFLOW),
    )(*[pltpu.with_memory_space_constraint(s, pltpu.HBM) for s in srcs],
      *[pltpu.with_memory_space_constraint(lax.empty(shape, s.dtype), pltpu.HBM) for shape, s in zip(land_shapes, srcs)],
      *deps)
    return (outs[:n_sems], outs[n_sems:n_sems + n], outs[n_sems + n:n_sems + 2 * n], scatter, peers), outs[-1]


def _exchange_wait(state, after, name):
    sems, srcs, lands, scatter, peers = state
    n = len(srcs)
    n_sems = len(sems)

    def body(*refs):
        ins, zones, ss = refs[:n], refs[n:2 * n], refs[2 * n:2 * n + n_sems]
        for a in range(n):
            local, _, drain = _direct_copies(ins[a], zones[a], *ss[3 * a:3 * a + 3], scatter, peers)
            drain.wait_send()
            drain.wait_recv()
            local.wait()

    outs = pl.pallas_call(
        body, name=name,
        out_shape=(*[pltpu.HBM(s.shape, s.dtype) for s in srcs], *[pltpu.HBM(z.shape, z.dtype) for z in lands]),
        in_specs=[_HBM_SPEC] * (2 * n) + [_SEM_SPEC] * n_sems + [pl.BlockSpec(memory_space=pl.ANY)],
        out_specs=[_HBM_SPEC] * (2 * n),
        input_output_aliases={i: i for i in range(2 * n)},
        compiler_params=pltpu.CompilerParams(has_side_effects=_DATAFLOW),
    )(*srcs, *lands, *sems, after)
    return outs[n:]


def _forward_to_sibling(lands, name):
    n = len(lands)

    def body(*refs):
        zones = refs[n:2 * n]
        send_sems, recv_sems = refs[2 * n:]
        x, y, c = _mesh_pos()
        chips = [(1 - x, y), (x, 1 - y), (1 - x, 1 - y)]
        sends, recvs = [], []
        for a in range(n):
            for j, (px, py) in enumerate(chips):
                mine = zones[a].at[4 * px + 2 * py + c]
                theirs = zones[a].at[4 * px + 2 * py + 1 - c]
                sends.append(pltpu.make_async_remote_copy(
                    src_ref=mine, dst_ref=mine, send_sem=send_sems.at[a, j], recv_sem=recv_sems.at[a, j],
                    device_id=(x, y, 1 - c), device_id_type=pl.DeviceIdType.MESH))
                recvs.append(pltpu.make_async_remote_copy(
                    src_ref=theirs, dst_ref=theirs, send_sem=send_sems.at[a, j], recv_sem=recv_sems.at[a, j],
                    device_id=(x, y, 1 - c), device_id_type=pl.DeviceIdType.MESH))
        for cp in sends:
            cp.start()
        for cp in recvs:
            cp.wait_recv()
        for cp in sends:
            cp.wait_send()

    any_spec = pl.BlockSpec(memory_space=pl.ANY)
    return pl.pallas_call(
        body, name=name,
        out_shape=[_sds(z.shape, z.dtype) for z in lands],
        in_specs=[any_spec] * n, out_specs=[any_spec] * n,
        input_output_aliases={a: a for a in range(n)},
        scratch_shapes=[pltpu.SemaphoreType.DMA((n, 3)), pltpu.SemaphoreType.DMA((n, 3))],
    )(*lands)


def _rms_fwd(xv, gain):
    r = lax.rsqrt(jnp.mean(xv * xv, axis=-1, keepdims=True) + EPS)
    return xv * r * gain


def _rms_bwd(dh, xv, gain):
    r = lax.rsqrt(jnp.mean(xv * xv, axis=-1, keepdims=True) + EPS)
    xn = xv * r
    dxn = dh * gain
    dx = r * (dxn - xn * jnp.mean(dxn * xn, axis=-1, keepdims=True))
    return dx, dh * xn


def _sigmoid(v):
    return 0.5 * jnp.tanh(0.5 * v) + 0.5


def _silu_parts(g):
    s = _sigmoid(g)
    return g * s, s * (1.0 + g * (1.0 - s))


def _segment_mean(v, width):
    r = lax.broadcasted_iota(jnp.int32, (width, width), 0) >> 6
    c = lax.broadcasted_iota(jnp.int32, (width, width), 1) >> 6
    bd = (r == c).astype(BF16)
    hi = v.astype(BF16)
    lo = (v - hi.astype(F32)).astype(BF16)
    total = jnp.dot(hi, bd, preferred_element_type=F32) + jnp.dot(lo, bd, preferred_element_type=F32)
    return total * (1.0 / HEAD_DIM)


def _rope_partner(v):
    width = v.shape[1]
    half = ROT_DIM // 2
    lane = lax.broadcasted_iota(jnp.int32, v.shape, 1) & (HEAD_DIM - 1)
    up = jnp.where(lane < ROT_DIM, pltpu.roll(v, half, 1), 0.0)
    return jnp.where(lane < half, pltpu.roll(v, width - half, 1), up)


def _tile_lanes(t, width):
    return t if width == t.shape[1] else jnp.tile(t, (1, width // t.shape[1]))


def _ffn_chunks(f):
    return [slice(j * FFN_CHUNK, (j + 1) * FFN_CHUNK) for j in range(f // FFN_CHUNK)]


def _ffn_up(x, ln, wgu, name, after=None):
    m, d = x.shape
    f = wgu.shape[1]
    tm = _row_tile(m)
    deps = [] if after is None else [after]

    def body(*refs):
        x_ref, ln_ref, w_ref = refs[:3]
        h_ref, gu_ref, a_ref = refs[-3:]
        h = _rms_fwd(x_ref[...], ln_ref[...]).astype(BF16)
        h_ref[...] = h
        for cols in _ffn_chunks(f):
            g = lax.dot_general(h, w_ref[0, cols, :], _NT, preferred_element_type=F32)
            u = lax.dot_general(h, w_ref[1, cols, :], _NT, preferred_element_type=F32)
            gu_ref[0, :, cols] = g.astype(BF16)
            gu_ref[1, :, cols] = u.astype(BF16)
            a_ref[:, cols] = (g * _sigmoid(g) * u).astype(BF16)

    return pl.pallas_call(
        body, name=name, grid=(m // tm,),
        in_specs=[pl.BlockSpec((tm, d), lambda i: (i, 0)), _resident((1, d), (0, 0)),
                  _resident((2, f, d), (0, 0, 0))] + [pl.BlockSpec(memory_space=pl.ANY)] * len(deps),
        out_specs=[pl.BlockSpec((tm, d), lambda i: (i, 0)), pl.BlockSpec((2, tm, f), lambda i: (0, i, 0)),
                   pl.BlockSpec((tm, f), lambda i: (i, 0))],
        out_shape=[_sds((m, d), BF16), _sds((2, m, f), BF16), _sds((m, f), BF16)],
        compiler_params=_cp("parallel"),
    )(x, ln, wgu, *deps)


def _ffn_down(x, act, wd, name):
    m, d = x.shape
    f = act.shape[-1]
    tm = _row_tile(m)

    def body(x_ref, a_ref, w_ref, o_ref):
        o_ref[...] = x_ref[...] + 0.5 * jnp.dot(a_ref[...], w_ref[...], preferred_element_type=F32)

    return pl.pallas_call(
        body, name=name, grid=(m // tm,),
        in_specs=[pl.BlockSpec((tm, d), lambda i: (i, 0)), pl.BlockSpec((tm, f), lambda i: (i, 0)),
                  _resident((f, d), (0, 0))],
        out_specs=pl.BlockSpec((tm, d), lambda i: (i, 0)),
        out_shape=_sds((m, d), F32),
        compiler_params=_cp("parallel"),
    )(x, act, wd)


def _ffn_bwd(dy, x, ln, gu, wd, wgu, name, after=None):
    m, d = dy.shape
    f = gu.shape[-1]
    tm = min(FFN_BWD_ROW_TILE, m)
    deps = [] if after is None else [after]

    def body(*refs):
        dy_ref, x_ref, ln_ref, gu_ref, wd_ref, wgu_ref = refs[:6]
        dyh_ref, dgu_ref, dx_ref, dln_ref = refs[-4:]

        @pl.when(pl.program_id(0) == 0)
        def _():
            dln_ref[...] = jnp.zeros_like(dln_ref)

        dyh = (0.5 * dy_ref[...]).astype(BF16)
        dyh_ref[...] = dyh
        dgs, dus = [], []
        for cols in _ffn_chunks(f):
            da = lax.dot_general(dyh, wd_ref[cols, :], _NT, preferred_element_type=F32)
            g = gu_ref[0, :, cols].astype(F32)
            u = gu_ref[1, :, cols].astype(F32)
            silu, dsilu = _silu_parts(g)
            dgs.append((da * u * dsilu).astype(BF16))
            dus.append((da * silu).astype(BF16))
            dgu_ref[0, :, cols] = dgs[-1]
            dgu_ref[1, :, cols] = dus[-1]
        dh = jnp.dot(jnp.concatenate(dgs, axis=1), wgu_ref[0], preferred_element_type=F32)
        dh += jnp.dot(jnp.concatenate(dus, axis=1), wgu_ref[1], preferred_element_type=F32)
        dx, dgain = _rms_bwd(dh, x_ref[...], ln_ref[...])
        dx_ref[...] = dy_ref[...] + dx
        dln_ref[...] += jnp.sum(dgain, axis=0, keepdims=True)

    row = lambda i: (i, 0)
    return pl.pallas_call(
        body, name=name, grid=(m // tm,),
        in_specs=[pl.BlockSpec((tm, d), row), pl.BlockSpec((tm, d), row), _resident((1, d), (0, 0)),
                  pl.BlockSpec((2, tm, f), lambda i: (0, i, 0)), _resident((f, d), (0, 0)),
                  _resident((2, f, d), (0, 0, 0))] + [pl.BlockSpec(memory_space=pl.ANY)] * len(deps),
        out_specs=[pl.BlockSpec((tm, d), row), pl.BlockSpec((2, tm, f), lambda i: (0, i, 0)),
                   pl.BlockSpec((tm, d), row), pl.BlockSpec((1, d), lambda i: (0, 0))],
        out_shape=[_sds((m, d), BF16), _sds((2, m, f), BF16), _sds((m, d), F32), _sds((1, d), F32)],
        compiler_params=_cp("arbitrary"),
    )(dy, x, ln, gu, wd, wgu, *deps)


def _matmul_tn(a, b, name, a_chunk=None, out_dtype=BF16, after=None):
    ja, m, k = a.shape
    jb, _, n = b.shape
    nj = max(ja, jb)
    kc = k if a_chunk is None else a_chunk
    tm = min(TN_ROW_TILE, m)
    nm = m // tm
    deps = [] if after is None else [after]

    def body(*refs):
        a_ref, b_ref = refs[:2]
        o_ref, acc = refs[-2:]
        step = pl.program_id(2)

        @pl.when(step == 0)
        def _():
            acc[...] = jnp.zeros_like(acc)

        acc[...] += lax.dot_general(a_ref[...], b_ref[...], _TN, preferred_element_type=F32)

        @pl.when(step == nm - 1)
        def _():
            o_ref[...] = acc[...].astype(o_ref.dtype)

    return pl.pallas_call(
        body, name=name, grid=(nj, k // kc, nm),
        in_specs=[pl.BlockSpec((None, tm, kc), (lambda j, c, s: (j, s, c)) if ja > 1 else (lambda j, c, s: (0, s, c))),
                  pl.BlockSpec((None, tm, n), (lambda j, c, s: (j, s, 0)) if jb > 1 else (lambda j, c, s: (0, s, 0)))]
        + [pl.BlockSpec(memory_space=pl.ANY)] * len(deps),
        out_specs=pl.BlockSpec((None, kc, n), lambda j, c, s: (j, c, 0)),
        out_shape=_sds((nj, k, n), out_dtype),
        scratch_shapes=[pltpu.VMEM((kc, n), F32)],
        compiler_params=_cp("parallel", "parallel", "arbitrary"),
    )(a, b, *deps)


def _mix_in(x, ln, w_in, name):
    m, d = x.shape
    n_in = w_in.shape[0]
    tm = _row_tile(m)

    def body(x_ref, ln_ref, w_ref, h_ref, zq_ref, zg_ref):
        h = _rms_fwd(x_ref[...], ln_ref[...]).astype(BF16)
        h_ref[...] = h
        zq_ref[...] = lax.dot_general(h, w_ref[:QKV_END, :], _NT, preferred_element_type=F32)
        zg_ref[...] = lax.dot_general(h, w_ref[QKV_END:, :], _NT, preferred_element_type=F32).astype(BF16)

    return pl.pallas_call(
        body, name=name, grid=(m // tm,),
        in_specs=[pl.BlockSpec((tm, d), lambda i: (i, 0)), _resident((1, d), (0, 0)), _resident((n_in, d), (0, 0))],
        out_specs=[pl.BlockSpec((tm, d), lambda i: (i, 0)), pl.BlockSpec((tm, QKV_END), lambda i: (i, 0)),
                   pl.BlockSpec((tm, n_in - QKV_END), lambda i: (i, 0))],
        out_shape=[_sds((m, d), BF16), _sds((m, QKV_END), F32), _sds((m, n_in - QKV_END), BF16)],
        compiler_params=_cp("parallel"),
    )(x, ln, w_in)


def _pool_fwd(zq, pool_w, scale, name):
    m = zq.shape[0]
    tm = _row_tile(m)
    halo_blocks = tm // POOL_HALO

    def body(zc_ref, zh_ref, pw_ref, sc_ref, d_ref, p_ref):
        i = pl.program_id(0)
        halo = jnp.where(i > 0, zh_ref[...], 0.0)
        ext = jnp.concatenate([halo, zc_ref[...]], axis=0)
        t = i * tm + lax.broadcasted_iota(jnp.int32, (tm, 1), 0)
        for g, w in enumerate(POOL_WINDOWS):
            lanes = slice(g * GROUP_DIM, (g + 1) * GROUP_DIM)
            e = ext[:, lanes]
            s, k = e, 1
            while k < w:
                s = s + pltpu.roll(s, k, 0)
                k *= 2
            cnt = jnp.minimum(t + 1, w).astype(F32)
            dg = (s[POOL_HALO:, :] / cnt - e[POOL_HALO:, :]).astype(BF16)
            y = jnp.dot(dg, pw_ref[g].astype(BF16), preferred_element_type=F32)
            d_ref[:, lanes] = dg
            p_ref[:, lanes] = (y * sc_ref[:, lanes]).astype(BF16)

    return pl.pallas_call(
        body, name=name, grid=(m // tm,),
        in_specs=[pl.BlockSpec((tm, POOL_DIM), lambda i: (i, 0)),
                  pl.BlockSpec((POOL_HALO, POOL_DIM), lambda i: (jnp.maximum(i * halo_blocks - 1, 0), 0)),
                  _resident((4, GROUP_DIM, GROUP_DIM), (0, 0, 0)), _resident((1, POOL_DIM), (0, 0))],
        out_specs=[pl.BlockSpec((tm, POOL_DIM), lambda i: (i, 0)), pl.BlockSpec((tm, POOL_DIM), lambda i: (i, 0))],
        out_shape=[_sds((m, POOL_DIM), BF16), _sds((m, POOL_DIM), BF16)],
        compiler_params=_cp("parallel"),
    )(zq, zq, pool_w, scale)


def _pool_bwd(dp, d, pool_w, scale, name):
    m = dp.shape[0]
    tm = _row_tile(m)
    nb = m // tm
    halo_blocks = tm // POOL_HALO
    rows = tm + POOL_HALO

    def body(dpc_ref, dph_ref, d_ref, pw_ref, sc_ref, du_ref, dpw_ref, dsc_ref):
        i = pl.program_id(0)

        @pl.when(i == 0)
        def _():
            dpw_ref[...] = jnp.zeros_like(dpw_ref)
            dsc_ref[...] = jnp.zeros_like(dsc_ref)

        halo = jnp.where(i < nb - 1, dph_ref[...], 0.0)
        dpc = dpc_ref[...]
        ext = jnp.concatenate([dpc, halo], axis=0)
        t = i * tm + lax.broadcasted_iota(jnp.int32, (rows, 1), 0)
        for g, w in enumerate(POOL_WINDOWS):
            lanes = slice(g * GROUP_DIM, (g + 1) * GROUP_DIM)
            pwb = pw_ref[g].astype(BF16)
            dyb = (ext[:, lanes] * sc_ref[:, lanes]).astype(BF16)
            dd = lax.dot_general(dyb, pwb, _NT, preferred_element_type=F32)
            cnt = jnp.minimum(t + 1, w).astype(F32)
            s, k = dd / cnt, 1
            while k < w:
                s = s + pltpu.roll(s, rows - k, 0)
                k *= 2
            du_ref[:, lanes] = (s[:tm, :] - dd[:tm, :]).astype(BF16)
            dcur = d_ref[:, lanes]
            y = jnp.dot(dcur, pwb, preferred_element_type=F32)
            dsc_ref[:, lanes] += jnp.sum(dpc[:, lanes] * y, axis=0, keepdims=True)
            dpw_ref[g] += lax.dot_general(dcur, dyb[:tm, :], _TN, preferred_element_type=F32)

    return pl.pallas_call(
        body, name=name, grid=(nb,),
        in_specs=[pl.BlockSpec((tm, POOL_DIM), lambda i: (i, 0)),
                  pl.BlockSpec((POOL_HALO, POOL_DIM), lambda i: (jnp.minimum((i + 1) * halo_blocks, nb * halo_blocks - 1), 0)),
                  pl.BlockSpec((tm, POOL_DIM), lambda i: (i, 0)),
                  _resident((4, GROUP_DIM, GROUP_DIM), (0, 0, 0)), _resident((1, POOL_DIM), (0, 0))],
        out_specs=[pl.BlockSpec((tm, POOL_DIM), lambda i: (i, 0)),
                   pl.BlockSpec((4, GROUP_DIM, GROUP_DIM), lambda i: (0, 0, 0)),
                   pl.BlockSpec((1, POOL_DIM), lambda i: (0, 0))],
        out_shape=[_sds((m, POOL_DIM), BF16), _sds((4, GROUP_DIM, GROUP_DIM), F32), _sds((1, POOL_DIM), F32)],
        compiler_params=_cp("arbitrary"),
    )(dp, dp, d, pool_w, scale)


def _qk_norm_rope(xv, gain, cos, sin):
    width = xv.shape[1]
    r = lax.rsqrt(_segment_mean(xv * xv, width) + EPS)
    y = xv * r * gain
    return y * _tile_lanes(cos, width) + _rope_partner(y) * _tile_lanes(sin, width)


def _qk_prep(zq, qgain, kgain, cos, sin, name):
    m = zq.shape[0]
    tm = _row_tile(m)

    def body(q_ref, kv_ref, qg_ref, kg_ref, cos_ref, sin_ref, qr_ref, kr_ref, v_ref):
        cos_v, sin_v = cos_ref[...], sin_ref[...]
        qr_ref[...] = (_qk_norm_rope(q_ref[...], qg_ref[...], cos_v, sin_v) * ATTN_SCALE).astype(BF16)
        kv = kv_ref[...]
        kr_ref[...] = _qk_norm_rope(kv[:, :KV_DIM], kg_ref[...], cos_v, sin_v).astype(BF16)
        v_ref[...] = kv[:, KV_DIM:].astype(BF16)

    return pl.pallas_call(
        body, name=name, grid=(m // tm,),
        in_specs=[pl.BlockSpec((tm, ATTN_DIM), lambda i: (i, 1)), pl.BlockSpec((tm, 2 * KV_DIM), lambda i: (i, 4)),
                  _resident((1, ATTN_DIM), (0, 0)), _resident((1, KV_DIM), (0, 0)),
                  pl.BlockSpec((tm, 128), lambda i: (i, 0)), pl.BlockSpec((tm, 128), lambda i: (i, 0))],
        out_specs=[pl.BlockSpec((tm, ATTN_DIM), lambda i: (i, 0)), pl.BlockSpec((tm, KV_DIM), lambda i: (i, 0)),
                   pl.BlockSpec((tm, KV_DIM), lambda i: (i, 0))],
        out_shape=[_sds((m, ATTN_DIM), BF16), _sds((m, KV_DIM), BF16), _sds((m, KV_DIM), BF16)],
        compiler_params=_cp("parallel"),
    )(zq, zq, qgain, kgain, cos, sin)


STACK_ROWS = N_Q_HEADS * ATTN_BLOCK


def _band_bias():
    qi = jnp.arange(STACK_ROWS)[:, None] % ATTN_BLOCK
    ki = jnp.arange(2 * ATTN_BLOCK)[None, :]
    diff = qi + ATTN_BLOCK - ki
    band = (diff >= 0) & (diff < ATTN_BLOCK)
    first = band & (ki >= ATTN_BLOCK)
    return jnp.where(jnp.stack([first, band]), 0.0, NEG_BIG).astype(F32)


def _band_tables():
    qi = jnp.arange(STACK_ROWS)[:, None] % ATTN_BLOCK
    use_prev = jnp.arange(ATTN_BLOCK)[None, :] > qi
    return jnp.stack([use_prev.astype(F32), jnp.where(use_prev, NEG_BIG, 0.0).astype(F32),
                      jnp.zeros((STACK_ROWS, ATTN_BLOCK), F32)])


def _band_merge(use_prev, from_prev, from_cur):
    return jnp.where(use_prev, from_prev, from_cur)


def _band_split(use_prev, merged):
    return jnp.where(use_prev, merged, 0.0).astype(BF16), jnp.where(use_prev, 0.0, merged).astype(BF16)


def _stack_heads(v):
    zeros = jnp.zeros((ATTN_BLOCK, HEAD_DIM), v.dtype)
    rows = []
    for h in range(N_Q_HEADS):
        qh = v[:, h * HEAD_DIM:(h + 1) * HEAD_DIM]
        rows.append(jnp.concatenate([qh, zeros] if h < GQA_GROUP else [zeros, qh], axis=1))
    return jnp.concatenate(rows, axis=0)


def _unstack_heads(stacked):
    parts = []
    for h in range(N_Q_HEADS):
        lanes = slice(0, HEAD_DIM) if h < GQA_GROUP else slice(HEAD_DIM, 2 * HEAD_DIM)
        parts.append(stacked[h * ATTN_BLOCK:(h + 1) * ATTN_BLOCK, lanes])
    return jnp.concatenate(parts, axis=1)


def _stacked_sinks(sk_ref):
    row_head = lax.broadcasted_iota(jnp.int32, (STACK_ROWS, 1), 0) >> 7
    col = jnp.full((STACK_ROWS, 1), sk_ref[0], F32)
    for h in range(1, N_Q_HEADS):
        col = jnp.where(row_head == h, sk_ref[h], col)
    return col


def _softmax_with_sink(s, sink):
    mx = jnp.maximum(jnp.max(s, axis=-1, keepdims=True), sink)
    p = jnp.exp(s - mx)
    es = jnp.exp(sink - mx)
    inv = 1.0 / (jnp.sum(p, axis=-1, keepdims=True) + es)
    return p * inv, es * inv


def _merged_logits(qh, kp, kc, use_prev, bias):
    return _band_merge(use_prev, lax.dot_general(qh, kp, _NT, preferred_element_type=F32),
                       lax.dot_general(qh, kc, _NT, preferred_element_type=F32)) + bias


def _attn_fwd(qr, kr, vb, sinks, name):
    m = qr.shape[0]
    nb = m // ATTN_BLOCK

    def body(q_ref, kp_ref, kc_ref, vp_ref, vc_ref, sk_ref, bias_ref, o_ref):
        kk = jnp.concatenate([kp_ref[...], kc_ref[...]], axis=0)
        vv = jnp.concatenate([vp_ref[...], vc_ref[...]], axis=0)
        s = lax.dot_general(_stack_heads(q_ref[...]), kk, _NT, preferred_element_type=F32)
        p, _ = _softmax_with_sink(s + bias_ref[jnp.minimum(pl.program_id(0), 1)], _stacked_sinks(sk_ref))
        o_ref[...] = _unstack_heads(jnp.dot(p.astype(BF16), vv, preferred_element_type=F32)).astype(BF16)

    prev = lambda n: (jnp.maximum(n - 1, 0), 0)
    cur = lambda n: (n, 0)
    return pl.pallas_call(
        body, name=name, grid=(nb,),
        in_specs=[pl.BlockSpec((ATTN_BLOCK, ATTN_DIM), cur),
                  pl.BlockSpec((ATTN_BLOCK, KV_DIM), prev), pl.BlockSpec((ATTN_BLOCK, KV_DIM), cur),
                  pl.BlockSpec((ATTN_BLOCK, KV_DIM), prev), pl.BlockSpec((ATTN_BLOCK, KV_DIM), cur),
                  pl.BlockSpec(memory_space=pltpu.SMEM), _resident((2, STACK_ROWS, 2 * ATTN_BLOCK), (0, 0, 0))],
        out_specs=pl.BlockSpec((ATTN_BLOCK, ATTN_DIM), cur),
        out_shape=_sds((m, ATTN_DIM), BF16),
        compiler_params=_cp("parallel"),
    )(qr, kr, kr, vb, vb, sinks, _band_bias())


def _attn_bwd(do, qr, kr, vb, sinks, name):
    m = qr.shape[0]
    nb = m // ATTN_BLOCK

    def body(do_ref, q_ref, kp_ref, kc_ref, vp_ref, vc_ref, sk_ref, tab_ref, dq_ref, dk_ref, dv_ref, ds_ref,
             carry_k, carry_v):
        n = pl.program_id(0)

        @pl.when(n == 0)
        def _():
            carry_k[...] = jnp.zeros_like(carry_k)
            carry_v[...] = jnp.zeros_like(carry_v)
            ds_ref[...] = jnp.zeros_like(ds_ref)

        @pl.when(n < nb)
        def _():
            use_prev = tab_ref[0] > 0.5
            bias = tab_ref[1 + jnp.minimum(n, 1)]
            kp, kc, vp, vc = kp_ref[...], kc_ref[...], vp_ref[...], vc_ref[...]
            lane = lax.broadcasted_iota(jnp.int32, (1, 128), 1)
            qs = _stack_heads(q_ref[...])
            dos = _stack_heads(do_ref[...].astype(BF16))
            p, ps = _softmax_with_sink(_merged_logits(qs, kp, kc, use_prev, bias), _stacked_sinks(sk_ref))
            dpr = _band_merge(use_prev, lax.dot_general(dos, vp, _NT, preferred_element_type=F32),
                              lax.dot_general(dos, vc, _NT, preferred_element_type=F32))
            delta = jnp.sum(p * dpr, axis=-1, keepdims=True)
            ds_prev, ds_cur = _band_split(use_prev, p * (dpr - delta))
            p_prev, p_cur = _band_split(use_prev, p)
            sink_term = ps * delta
            dsink = jnp.zeros((1, 128), F32)
            for h in range(N_Q_HEADS):
                rows = slice(h * ATTN_BLOCK, (h + 1) * ATTN_BLOCK)
                dsink = dsink + jnp.where(lane == h, -jnp.sum(sink_term[rows, :]), 0.0)
            dq = jnp.dot(ds_prev, kp, preferred_element_type=F32) + jnp.dot(ds_cur, kc, preferred_element_type=F32)
            dq_ref[...] = _unstack_heads(dq * ATTN_SCALE)
            dk_ref[...] = carry_k[...] + lax.dot_general(ds_prev, qs, _TN, preferred_element_type=F32)
            dv_ref[...] = carry_v[...] + lax.dot_general(p_prev, dos, _TN, preferred_element_type=F32)
            carry_k[...] = lax.dot_general(ds_cur, qs, _TN, preferred_element_type=F32)
            carry_v[...] = lax.dot_general(p_cur, dos, _TN, preferred_element_type=F32)
            ds_ref[...] += dsink

        @pl.when(n == nb)
        def _():
            dk_ref[...] = carry_k[...]
            dv_ref[...] = carry_v[...]

    cur = lambda n: (jnp.minimum(n, nb - 1), 0)
    prev = lambda n: (jnp.clip(n - 1, 0, nb - 1), 0)
    return pl.pallas_call(
        body, name=name, grid=(nb + 1,),
        in_specs=[pl.BlockSpec((ATTN_BLOCK, ATTN_DIM), cur), pl.BlockSpec((ATTN_BLOCK, ATTN_DIM), cur),
                  pl.BlockSpec((ATTN_BLOCK, KV_DIM), prev), pl.BlockSpec((ATTN_BLOCK, KV_DIM), cur),
                  pl.BlockSpec((ATTN_BLOCK, KV_DIM), prev), pl.BlockSpec((ATTN_BLOCK, KV_DIM), cur),
                  pl.BlockSpec(memory_space=pltpu.SMEM), _resident((3, STACK_ROWS, ATTN_BLOCK), (0, 0, 0))],
        out_specs=[pl.BlockSpec((ATTN_BLOCK, ATTN_DIM), cur), pl.BlockSpec((ATTN_BLOCK, KV_DIM), prev),
                   pl.BlockSpec((ATTN_BLOCK, KV_DIM), prev), pl.BlockSpec((1, 128), lambda n: (0, 0))],
        out_shape=[_sds((m, ATTN_DIM), F32), _sds((m, KV_DIM), F32), _sds((m, KV_DIM), F32), _sds((1, 128), F32)],
        scratch_shapes=[pltpu.VMEM((ATTN_BLOCK, KV_DIM), F32), pltpu.VMEM((ATTN_BLOCK, KV_DIM), F32)],
        compiler_params=_cp("arbitrary"),
    )(do, qr, kr, kr, vb, vb, sinks, _band_tables())


def _qk_norm_rope_bwd(dout, xv, gain, cos, sin):
    width = xv.shape[1]
    r = lax.rsqrt(_segment_mean(xv * xv, width) + EPS)
    xn = xv * r
    dy = dout * _tile_lanes(cos, width) + _rope_partner(dout * _tile_lanes(sin, width))
    dxn = dy * gain
    dx = r * (dxn - xn * _segment_mean(dxn * xn, width))
    return dx, jnp.sum(dy * xn, axis=0, keepdims=True)


def _fold_heads(v):
    out = v[:, :HEAD_DIM]
    for h in range(1, v.shape[1] // HEAD_DIM):
        out = out + v[:, h * HEAD_DIM:(h + 1) * HEAD_DIM]
    return out


def _qk_bwd(dq, dk, dv, zq, qgain, kgain, cos, sin, name):
    m = zq.shape[0]
    tm = _row_tile(m)

    def body(dq_ref, dk_ref, dv_ref, q_ref, kv_ref, qg_ref, kg_ref, cos_ref, sin_ref, dz_ref, dqg_ref, dkg_ref):
        @pl.when(pl.program_id(0) == 0)
        def _():
            dqg_ref[...] = jnp.zeros_like(dqg_ref)
            dkg_ref[...] = jnp.zeros_like(dkg_ref)

        cos_v, sin_v = cos_ref[...], sin_ref[...]
        dxq, dgq = _qk_norm_rope_bwd(dq_ref[...], q_ref[...], qg_ref[...], cos_v, sin_v)
        dxk, dgk = _qk_norm_rope_bwd(dk_ref[...], kv_ref[:, :KV_DIM], kg_ref[...], cos_v, sin_v)
        dz_ref[:, :ATTN_DIM] = dxq.astype(BF16)
        dz_ref[:, ATTN_DIM:ATTN_DIM + KV_DIM] = dxk.astype(BF16)
        dz_ref[:, ATTN_DIM + KV_DIM:] = dv_ref[...].astype(BF16)
        dqg_ref[...] += _fold_heads(dgq)
        dkg_ref[...] += _fold_heads(dgk)

    row = lambda i: (i, 0)
    return pl.pallas_call(
        body, name=name, grid=(m // tm,),
        in_specs=[pl.BlockSpec((tm, ATTN_DIM), row), pl.BlockSpec((tm, KV_DIM), row), pl.BlockSpec((tm, KV_DIM), row),
                  pl.BlockSpec((tm, ATTN_DIM), lambda i: (i, 1)), pl.BlockSpec((tm, 2 * KV_DIM), lambda i: (i, 4)),
                  _resident((1, ATTN_DIM), (0, 0)), _resident((1, KV_DIM), (0, 0)),
                  pl.BlockSpec((tm, 128), row), pl.BlockSpec((tm, 128), row)],
        out_specs=[pl.BlockSpec((tm, ATTN_DIM + 2 * KV_DIM), row), pl.BlockSpec((1, HEAD_DIM), lambda i: (0, 0)),
                   pl.BlockSpec((1, HEAD_DIM), lambda i: (0, 0))],
        out_shape=[_sds((m, ATTN_DIM + 2 * KV_DIM), BF16), _sds((1, HEAD_DIM), F32), _sds((1, HEAD_DIM), F32)],
        compiler_params=_cp("arbitrary"),
    )(dq, dk, dv, zq, zq, qgain, kgain, cos, sin)


def _merge_fwd(x, p, o, zg, w_pb, w_ab, w_out, name):
    m, d = x.shape
    tm = _row_tile(m)

    def body(x_ref, p_ref, o_ref, zg_ref, wpb_ref, wab_ref, wo_ref, xo_ref, mix_ref):
        a = jnp.dot(p_ref[...], wpb_ref[...], preferred_element_type=F32)
        b = jnp.dot(o_ref[...], wab_ref[...], preferred_element_type=F32)
        mix = (_sigmoid(zg_ref[:, :d].astype(F32)) * a + _sigmoid(zg_ref[:, d:].astype(F32)) * b).astype(BF16)
        mix_ref[...] = mix
        xo_ref[...] = x_ref[...] + jnp.dot(mix, wo_ref[...], preferred_element_type=F32)

    row = lambda i: (i, 0)
    return pl.pallas_call(
        body, name=name, grid=(m // tm,),
        in_specs=[pl.BlockSpec((tm, d), row), pl.BlockSpec((tm, POOL_DIM), row), pl.BlockSpec((tm, ATTN_DIM), row),
                  pl.BlockSpec((tm, 2 * d), row), _resident((POOL_DIM, d), (0, 0)), _resident((ATTN_DIM, d), (0, 0)),
                  _resident((d, d), (0, 0))],
        out_specs=[pl.BlockSpec((tm, d), row)] * 2,
        out_shape=[_sds((m, d), F32), _sds((m, d), BF16)],
        compiler_params=_cp("parallel"),
    )(x, p, o, zg, w_pb, w_ab, w_out)


def _merge_bwd(dy, p, o, zg, w_out, w_pb, w_ab, name):
    m, d = dy.shape
    tm = _row_tile(m)

    def body(dy_ref, p_ref, o_ref, zg_ref, wo_ref, wpb_ref, wab_ref, dyb_ref, da_ref, db_ref, dp_ref, do_ref, dzg_ref):
        dyb = dy_ref[...].astype(BF16)
        dyb_ref[...] = dyb
        dmix = lax.dot_general(dyb, wo_ref[...], _NT, preferred_element_type=F32)
        gp = _sigmoid(zg_ref[:, :d].astype(F32))
        ga = _sigmoid(zg_ref[:, d:].astype(F32))
        da = (dmix * gp).astype(BF16)
        db = (dmix * ga).astype(BF16)
        da_ref[...] = da
        db_ref[...] = db
        a = jnp.dot(p_ref[...], wpb_ref[...], preferred_element_type=F32)
        b = jnp.dot(o_ref[...], wab_ref[...], preferred_element_type=F32)
        dzg_ref[:, :d] = (dmix * a * gp * (1.0 - gp)).astype(BF16)
        dzg_ref[:, d:] = (dmix * b * ga * (1.0 - ga)).astype(BF16)
        dp_ref[...] = lax.dot_general(da, wpb_ref[...], _NT, preferred_element_type=F32)
        do_ref[...] = lax.dot_general(db, wab_ref[...], _NT, preferred_element_type=F32).astype(BF16)

    row = lambda i: (i, 0)
    return pl.pallas_call(
        body, name=name, grid=(m // tm,),
        in_specs=[pl.BlockSpec((tm, d), row), pl.BlockSpec((tm, POOL_DIM), row), pl.BlockSpec((tm, ATTN_DIM), row),
                  pl.BlockSpec((tm, 2 * d), row), _resident((d, d), (0, 0)), _resident((POOL_DIM, d), (0, 0)),
                  _resident((ATTN_DIM, d), (0, 0))],
        out_specs=[pl.BlockSpec((tm, d), row), pl.BlockSpec((tm, d), row), pl.BlockSpec((tm, d), row),
                   pl.BlockSpec((tm, POOL_DIM), row), pl.BlockSpec((tm, ATTN_DIM), row), pl.BlockSpec((tm, 2 * d), row)],
        out_shape=[_sds((m, d), BF16), _sds((m, d), BF16), _sds((m, d), BF16), _sds((m, POOL_DIM), F32),
                   _sds((m, ATTN_DIM), BF16), _sds((m, 2 * d), BF16)],
        compiler_params=_cp("parallel"),
    )(dy, p, o, zg, w_out, w_pb, w_ab)


def _mix_bwd_x(dy, x, ln, dzp, dzqkv, dzg, w_in, name):
    m, d = dy.shape
    n_in = w_in.shape[0]
    tm = _row_tile(m)

    def body(dy_ref, x_ref, ln_ref, dzp_ref, dzq_ref, dzg_ref, w_ref, dx_ref, dln_ref):
        @pl.when(pl.program_id(0) == 0)
        def _():
            dln_ref[...] = jnp.zeros_like(dln_ref)

        dh = jnp.dot(dzp_ref[...], w_ref[:POOL_DIM, :], preferred_element_type=F32)
        dh += jnp.dot(dzq_ref[...], w_ref[POOL_DIM:QKV_END, :], preferred_element_type=F32)
        dh += jnp.dot(dzg_ref[...], w_ref[QKV_END:, :], preferred_element_type=F32)
        dx, dgain = _rms_bwd(dh, x_ref[...], ln_ref[...])
        dx_ref[...] = dy_ref[...] + dx
        dln_ref[...] += jnp.sum(dgain, axis=0, keepdims=True)

    row = lambda i: (i, 0)
    return pl.pallas_call(
        body, name=name, grid=(m // tm,),
        in_specs=[pl.BlockSpec((tm, d), row), pl.BlockSpec((tm, d), row), _resident((1, d), (0, 0)),
                  pl.BlockSpec((tm, POOL_DIM), row), pl.BlockSpec((tm, QKV_END - POOL_DIM), row),
                  pl.BlockSpec((tm, n_in - QKV_END), row), _resident((n_in, d), (0, 0))],
        out_specs=[pl.BlockSpec((tm, d), row), pl.BlockSpec((1, d), lambda i: (0, 0))],
        out_shape=[_sds((m, d), F32), _sds((1, d), F32)],
        compiler_params=_cp("arbitrary"),
    )(dy, x, ln, dzp, dzqkv, dzg, w_in)


def _loss_head(y, target, name):
    m, d = y.shape
    tm = _row_tile(m)

    def body(y_ref, t_ref, loss_ref, dy_ref):
        @pl.when(pl.program_id(0) == 0)
        def _():
            loss_ref[...] = jnp.zeros_like(loss_ref)

        diff = y_ref[...] - t_ref[...]
        dy_ref[...] = diff * (1.0 / d)
        loss_ref[...] += 0.5 * jnp.sum(jnp.mean(diff * diff, axis=-1, keepdims=True), axis=0, keepdims=True)

    row = lambda i: (i, 0)
    return pl.pallas_call(
        body, name=name, grid=(m // tm,),
        in_specs=[pl.BlockSpec((tm, d), row), pl.BlockSpec((tm, d), row)],
        out_specs=[pl.BlockSpec((1, 1), lambda i: (0, 0)), pl.BlockSpec((tm, d), row)],
        out_shape=[_sds((1, 1), F32), _sds((m, d), F32)],
        compiler_params=_cp("arbitrary"),
    )(y, target)


def _adamw_math(g, w, m, v):
    m2 = ADAM_B1 * m + (1.0 - ADAM_B1) * g
    v2 = ADAM_B2 * v + (1.0 - ADAM_B2) * (g * g)
    m_hat = m2 / (1.0 - ADAM_B1 ** ADAM_STEP)
    v_hat = v2 / (1.0 - ADAM_B2 ** ADAM_STEP)
    delta = -ADAM_LR * (m_hat / (jnp.sqrt(v_hat) + ADAM_EPS) + ADAM_WD * w)
    return delta, m2, v2


def _sum_parts(parts_ref):
    g = parts_ref[0].astype(F32)
    for s in range(1, N_DEV):
        g = g + parts_ref[s].astype(F32)
    return g


def _adamw_sharded(parts, w, m, v, name, after=None):
    n_layers, rows, cols = w.shape
    tr = max(t for t in range(16, rows + 1, 16) if rows % t == 0 and t * cols <= ADAMW_BLOCK_ELEMS)
    nr = rows // tr
    deps = [] if after is None else [after]

    def body(*refs):
        part_refs = refs[:n_layers]
        w_ref, m_ref, v_ref = refs[n_layers:n_layers + 3]
        g_out, d_out, m_out, v_out = refs[-4:]
        layer = pl.program_id(0)
        for l in range(n_layers):
            @pl.when(layer == l)
            def _(l=l):
                g = _sum_parts(part_refs[l])
                delta, m2, v2 = _adamw_math(g, w_ref[...], m_ref[...], v_ref[...])
                g_out[...] = g
                d_out[...] = delta
                m_out[...] = m2
                v_out[...] = v2

    def part_map(l):
        return lambda layer, r: (0, jnp.where(layer == l, r, jnp.where(layer < l, 0, nr - 1)), 0)

    wspec = pl.BlockSpec((None, tr, cols), lambda layer, r: (layer, r, 0))
    return pl.pallas_call(
        body, name=name, grid=(n_layers, nr),
        in_specs=([pl.BlockSpec((N_DEV, tr, cols), part_map(l)) for l in range(n_layers)] + [wspec] * 3
                  + [pl.BlockSpec(memory_space=pl.ANY)] * len(deps)),
        out_specs=[wspec] * 4,
        out_shape=[_sds(w.shape, F32)] * 4,
        compiler_params=_cp("arbitrary", "arbitrary"),
    )(*parts, w, m, v, *deps)


def _adamw_packed(parts, w, m, v, name):
    def body(p_ref, w_ref, m_ref, v_ref, g_out, d_out, m_out, v_out):
        g = _sum_parts(p_ref)
        delta, m2, v2 = _adamw_math(g, w_ref[...], m_ref[...], v_ref[...])
        g_out[...] = g
        d_out[...] = delta
        m_out[...] = m2
        v_out[...] = v2

    return pl.pallas_call(
        body, name=name, out_shape=[_sds(w.shape, F32)] * 4,
        compiler_params=pltpu.CompilerParams(vmem_limit_bytes=VMEM_LIMIT_BYTES),
    )(parts, w, m, v)


_SMALL = ("ln_ffn1", "ln_mix", "pool_w", "pool_scale", "q_norm", "k_norm", "sinks", "ln_ffn2")


def _pack_small(arrs):
    rows = []
    for a in arrs:
        flat = a.reshape(-1)
        pad = (-flat.shape[0]) % 1024
        rows.append(jnp.pad(flat, (0, pad)).reshape(-1, 128))
    return jnp.concatenate(rows, axis=0)


def _unpack_small(packed, like):
    out, r0 = [], 0
    for a in like:
        size = a.size
        nrows = (size + 1023) // 1024 * 8
        out.append(packed[r0:r0 + nrows].reshape(-1)[:size].reshape(a.shape))
        r0 += nrows
    return out


def _rope_tables(m):
    pos = jnp.arange(m, dtype=F32)
    inv_freq = ROPE_THETA ** (-jnp.arange(0, ROT_DIM, 2, dtype=F32) / ROT_DIM)
    ang = pos[:, None] * inv_freq[None, :]
    cos8, sin8 = jnp.cos(ang), jnp.sin(ang)
    rest = HEAD_DIM - ROT_DIM
    cos64 = jnp.concatenate([cos8, cos8, jnp.ones((m, rest), F32)], axis=1)
    sin64 = jnp.concatenate([-sin8, sin8, jnp.zeros((m, rest), F32)], axis=1)
    return jnp.tile(cos64, (1, 2)), jnp.tile(sin64, (1, 2))


def _to_shard_major_cols(w):
    k = w.shape[0]
    return w.reshape(k, N_DEV, -1).transpose(1, 0, 2)


def _from_shard_major_cols(w):
    return w.transpose(1, 0, 2).reshape(w.shape[1], -1)


def kernel(x, ln_ffn1, w_ffn1_gu, w_ffn1_down, ln_mix, w_in, pool_w, pool_scale, w_pool_branch, q_norm, k_norm, sinks, w_attn_branch, w_out, ln_ffn2, w_ffn2_gu, w_ffn2_down, loss_target, m_ln_ffn1, m_w_ffn1_gu, m_w_ffn1_down, m_ln_mix, m_w_in, m_pool_w, m_pool_scale, m_w_pool_branch, m_q_norm, m_k_norm, m_sinks, m_w_attn_branch, m_w_out, m_ln_ffn2, m_w_ffn2_gu, m_w_ffn2_down, v_ln_ffn1, v_w_ffn1_gu, v_w_ffn1_down, v_ln_mix, v_w_in, v_pool_w, v_pool_scale, v_w_pool_branch, v_q_norm, v_k_norm, v_sinks, v_w_attn_branch, v_w_out, v_ln_ffn2, v_w_ffn2_gu, v_w_ffn2_down):
    weights = dict(ln_ffn1=ln_ffn1, w_ffn1_gu=w_ffn1_gu, w_ffn1_down=w_ffn1_down, ln_mix=ln_mix, w_in=w_in, pool_w=pool_w,
                   pool_scale=pool_scale, w_pool_branch=w_pool_branch, q_norm=q_norm, k_norm=k_norm, sinks=sinks,
                   w_attn_branch=w_attn_branch, w_out=w_out, ln_ffn2=ln_ffn2, w_ffn2_gu=w_ffn2_gu, w_ffn2_down=w_ffn2_down)
    mom_m = dict(ln_ffn1=m_ln_ffn1, w_ffn1_gu=m_w_ffn1_gu, w_ffn1_down=m_w_ffn1_down, ln_mix=m_ln_mix, w_in=m_w_in,
                 pool_w=m_pool_w, pool_scale=m_pool_scale, w_pool_branch=m_w_pool_branch, q_norm=m_q_norm, k_norm=m_k_norm,
                 sinks=m_sinks, w_attn_branch=m_w_attn_branch, w_out=m_w_out, ln_ffn2=m_ln_ffn2, w_ffn2_gu=m_w_ffn2_gu,
                 w_ffn2_down=m_w_ffn2_down)
    mom_v = dict(ln_ffn1=v_ln_ffn1, w_ffn1_gu=v_w_ffn1_gu, w_ffn1_down=v_w_ffn1_down, ln_mix=v_ln_mix, w_in=v_w_in,
                 pool_w=v_pool_w, pool_scale=v_pool_scale, w_pool_branch=v_w_pool_branch, q_norm=v_q_norm, k_norm=v_k_norm,
                 sinks=v_sinks, w_attn_branch=v_w_attn_branch, w_out=v_w_out, ln_ffn2=v_ln_ffn2, w_ffn2_gu=v_w_ffn2_gu,
                 w_ffn2_down=v_w_ffn2_down)
    order = ("ln_ffn1", "w_ffn1_gu", "w_ffn1_down", "ln_mix", "w_in", "pool_w", "pool_scale", "w_pool_branch", "q_norm",
             "k_norm", "sinks", "w_attn_branch", "w_out", "ln_ffn2", "w_ffn2_gu", "w_ffn2_down")
    big = ("w_ffn1_gu", "w_ffn1_down", "w_in", "w_pool_branch", "w_attn_branch", "w_out", "w_ffn2_gu", "w_ffn2_down")

    transposed = ("w_ffn1_gu", "w_ffn2_gu", "w_in")
    for group in (weights, mom_m, mom_v):
        for k in transposed:
            group[k] = jnp.swapaxes(group[k], 1, 2)

    n_layers = ln_ffn1.shape[0]
    seq, d = x.shape[-2], x.shape[-1]
    xs = x.reshape(seq, d)
    target = loss_target.reshape(seq, d)
    cos, sin = _rope_tables(seq)

    first_keys = ("w_ffn1_gu", "w_ffn1_down")
    rest_keys = tuple(k for k in big if k not in first_keys)

    def layer_shards(l, keys=big):
        return [weights[k][l].astype(BF16) for k in keys]

    def first_weights(l, g):
        return dict(gu1=g["w_ffn1_gu"].reshape(2, -1, d), down1=g["w_ffn1_down"].reshape(-1, d), ln1=ln_ffn1[l][None])

    def rest_weights(l, g):
        return dict(
            gu2=g["w_ffn2_gu"].reshape(2, -1, d), down2=g["w_ffn2_down"].reshape(-1, d),
            w_in=g["w_in"].reshape(-1, d), w_pb=_from_shard_major_cols(g["w_pool_branch"]),
            w_ab=_from_shard_major_cols(g["w_attn_branch"]), w_out=g["w_out"].reshape(d, d),
            ln_mix=ln_mix[l][None], ln2=ln_ffn2[l][None], pool_w=pool_w[l],
            pool_scale=pool_scale[l][None], sinks=sinks[l],
            qgain=jnp.tile(q_norm[l], N_Q_HEADS)[None], kgain=jnp.tile(k_norm[l], N_KV_HEADS)[None])

    def layer_weights(l, full):
        g = dict(zip(big, full))
        return {**first_weights(l, g), **rest_weights(l, g)}

    got = _all_gather_many(layer_shards(0, first_keys), name="gather_first_l0")
    gathered = [first_weights(0, dict(zip(first_keys, got)))]
    rest_in_flight, token = _exchange_start(layer_shards(0, rest_keys), scatter=False, name="gather_start_l0",
                                            peers=_CHIP_PEERS)
    saved = []
    cur = xs
    for l in range(n_layers):
        lw = gathered[l]
        s = dict(x0=cur)
        in_flight = None
        if l + 1 < n_layers:
            in_flight, token = _exchange_start(layer_shards(l + 1), scatter=False, name=f"gather_start_l{l + 1}",
                                               after=token)
        s["h1"], s["gu1"], act1 = _ffn_up(cur, lw["ln1"], lw["gu1"], name=f"ffn1_up_l{l}", after=token)
        s["act1"] = act1
        x1 = _ffn_down(cur, act1, lw["down1"], name=f"ffn1_down_l{l}")
        s["x1"] = x1
        if l == 0:
            got = _forward_to_sibling(_exchange_wait(rest_in_flight, x1, name="gather_wait_l0"), name="gather_forward_l0")
            lw.update(rest_weights(0, dict(zip(rest_keys, got))))
        s["h2"], zq, zg = _mix_in(x1, lw["ln_mix"], lw["w_in"], name=f"mix_in_l{l}")
        s["zq"], s["zg"] = zq, zg
        s["d"], s["p"] = _pool_fwd(zq, lw["pool_w"], lw["pool_scale"], name=f"pool_fwd_l{l}")
        s["qr"], s["kr"], s["vb"] = _qk_prep(zq, lw["qgain"], lw["kgain"], cos, sin, name=f"qk_prep_l{l}")
        s["o"] = _attn_fwd(s["qr"], s["kr"], s["vb"], lw["sinks"], name=f"attn_fwd_l{l}")
        x2, s["mix"] = _merge_fwd(x1, s["p"], s["o"], zg, lw["w_pb"], lw["w_ab"], lw["w_out"],
                                                  name=f"merge_fwd_l{l}")
        s["x2"] = x2
        s["h3"], s["gu2"], act2 = _ffn_up(x2, lw["ln2"], lw["gu2"], name=f"ffn2_up_l{l}")
        s["act2"] = act2
        cur = _ffn_down(x2, act2, lw["down2"], name=f"ffn2_down_l{l}")
        saved.append(s)
        if in_flight is not None:
            gathered.append(layer_weights(l + 1, _exchange_wait(in_flight, cur, name=f"gather_wait_l{l + 1}")))

    loss_local, dy = _loss_head(cur, target, name="loss_head")
    loss = lax.psum(loss_local[0, 0], MESH_AXES)

    small_grads = {k: [None] * n_layers for k in _SMALL}
    received = {k: [None] * n_layers for k in big}
    big_late = ("w_ffn1_gu", "w_ffn1_down")
    big_early = tuple(k for k in big if k not in big_late)
    early_in_flight, late_in_flight, last_in_flight = [None] * n_layers, [None] * n_layers, [None] * n_layers
    token = None
    for l in reversed(range(n_layers)):
        lw, s = gathered[l], saved[l]
        d_ff = lw["down1"].shape[0]

        def dw_down_of(dyh, act, tag):
            return _matmul_tn(act[None], dyh[None], name=f"{tag}_dw_down_l{l}", a_chunk=d_ff // 2).reshape(N_DEV, -1, d)

        def dw_gu_of(dgu, h, tag, after=None):
            return _matmul_tn(dgu, h[None], name=f"{tag}_dw_gu_l{l}", a_chunk=d_ff // 2, after=after).reshape(N_DEV, -1, d)

        dyh, dgu, dx2, dln2 = _ffn_bwd(dy, s["x2"], lw["ln2"], s["gu2"], lw["down2"], lw["gu2"],
                                       name=f"ffn2_bwd_l{l}", after=token)
        dw_down2, dw_gu2 = dw_down_of(dyh, s["act2"], "ffn2"), dw_gu_of(dgu, s["h3"], "ffn2")

        dyb, da, db, dp, do, dzg = _merge_bwd(dx2, s["p"], s["o"], s["zg"], lw["w_out"], lw["w_pb"], lw["w_ab"],
                                              name=f"merge_bwd_l{l}")
        dw_out = _matmul_tn(s["mix"][None], dyb[None], name=f"dw_out_l{l}")[0]
        dw_pb = _matmul_tn(s["p"][None], da[None], name=f"dw_pb_l{l}")[0]
        dw_ab = _matmul_tn(s["o"][None], db[None], name=f"dw_ab_l{l}")[0]
        dzp, dpw, dsc = _pool_bwd(dp, s["d"], lw["pool_w"], lw["pool_scale"], name=f"pool_bwd_l{l}")
        dq, dk, dv, dsinks = _attn_bwd(do, s["qr"], s["kr"], s["vb"], lw["sinks"], name=f"attn_bwd_l{l}")
        dzqkv, dqg, dkg = _qk_bwd(dq, dk, dv, s["zq"], lw["qgain"], lw["kgain"], cos, sin, name=f"qk_bwd_l{l}")
        dw_in = jnp.concatenate([_matmul_tn(dzp[None], s["h2"][None], name=f"dw_in_pool_l{l}")[0],
                                 _matmul_tn(dzqkv[None], s["h2"][None], name=f"dw_in_qkv_l{l}")[0],
                                 _matmul_tn(dzg[None], s["h2"][None], name=f"dw_in_gate_l{l}")[0]], axis=0)
        dx1, dlnm = _mix_bwd_x(dx2, s["x1"], lw["ln_mix"], dzp, dzqkv, dzg, lw["w_in"], name=f"mix_bwd_x_l{l}")

        partial = dict(w_in=dw_in.reshape(N_DEV, -1, d), w_pool_branch=_to_shard_major_cols(dw_pb),
                       w_attn_branch=_to_shard_major_cols(dw_ab), w_out=dw_out.reshape(N_DEV, d // N_DEV, d),
                       w_ffn2_gu=dw_gu2, w_ffn2_down=dw_down2)
        early_in_flight[l], token = _exchange_start([partial[k] for k in big_early], scatter=True,
                                                    name=f"grads_early_start_l{l}")

        dyh, dgu, dy, dln1 = _ffn_bwd(dx1, s["x0"], lw["ln1"], s["gu1"], lw["down1"], lw["gu1"],
                                      name=f"ffn1_bwd_l{l}", after=token)
        late_in_flight[l], token = _exchange_start([dw_down_of(dyh, s["act1"], "ffn1")], scatter=True,
                                                   name=f"grads_late_start_l{l}")
        last_in_flight[l], token = _exchange_start([dw_gu_of(dgu, s["h1"], "ffn1", after=token)], scatter=True,
                                                   name=f"grads_last_start_l{l}")
        small_grads["ln_ffn1"][l] = dln1[0]
        small_grads["ln_mix"][l] = dlnm[0]
        small_grads["ln_ffn2"][l] = dln2[0]
        small_grads["pool_w"][l] = dpw
        small_grads["pool_scale"][l] = dsc[0]
        small_grads["q_norm"][l] = dqg[0]
        small_grads["k_norm"][l] = dkg[0]
        small_grads["sinks"][l] = dsinks[0, :N_Q_HEADS]

    grad_x = dy.reshape(x.shape)

    small_w = [weights[k] for k in _SMALL]
    packed_g = _pack_small([jnp.stack(small_grads[k]).reshape(weights[k].shape) for k in _SMALL])
    small_in_flight, after = _exchange_start([packed_g], scatter=False, name="small_grads_start")
    for l in reversed(range(n_layers)):
        got = _exchange_wait(early_in_flight[l], after, name=f"grads_early_wait_l{l}")
        after = got[0]
        for k, r in zip(big_early, got):
            received[k][l] = r

    grads, deltas, new_m, new_v = {}, {}, {}, {}

    def adamw(k, after):
        w = weights[k]
        shape2 = (n_layers, -1, w.shape[-1])
        parts = [r.reshape(N_DEV, -1, w.shape[-1]) for r in received[k]]
        outs = _adamw_sharded(parts, w.reshape(shape2), mom_m[k].reshape(shape2), mom_v[k].reshape(shape2),
                              name=f"adamw_{k}", after=after)
        grads[k], deltas[k], new_m[k], new_v[k] = (o.reshape(w.shape) for o in outs)
        return outs[0]

    after = None
    for k in big_early:
        after = adamw(k, after)
    for l in reversed(range(n_layers)):
        (received["w_ffn1_down"][l],) = _exchange_wait(late_in_flight[l], after, name=f"grads_late_wait_l{l}")
        (received["w_ffn1_gu"][l],) = _exchange_wait(last_in_flight[l], received["w_ffn1_down"][l],
                                                     name=f"grads_last_wait_l{l}")
        after = received["w_ffn1_gu"][l]
    after = None
    for k in big_late:
        after = adamw(k, after)

    (parts_small,) = _exchange_wait(small_in_flight, after, name="small_grads_wait")
    outs = _adamw_packed(parts_small, _pack_small(small_w), _pack_small([mom_m[k] for k in _SMALL]),
                         _pack_small([mom_v[k] for k in _SMALL]), name="adamw_small")
    for res, o in zip((grads, deltas, new_m, new_v), outs):
        for k, a in zip(_SMALL, _unpack_small(o, small_w)):
            res[k] = a

    for res in (grads, deltas, new_m, new_v):
        for k in transposed:
            res[k] = jnp.swapaxes(res[k], 1, 2)
    return (loss, grad_x, *[grads[k] for k in order], *[deltas[k] for k in order],
            *[new_m[k] for k in order], *[new_v[k] for k in order])
```

```python
import functools

import jax
import jax.numpy as jnp
from jax import lax
from jax.experimental import pallas as pl
from jax.experimental.pallas import tpu as pltpu

F32 = jnp.float32
BF16 = jnp.bfloat16

N_DEV = 8
MESH_AXES = ("x", "y", "c")
EPS = 1e-6

HEAD_DIM = 64
N_Q_HEADS = 8
N_KV_HEADS = 2
GQA_GROUP = N_Q_HEADS // N_KV_HEADS
ATTN_BLOCK = 128
ATTN_SCALE = HEAD_DIM ** -0.5
ROPE_THETA = 500000.0
ROT_DIM = 16
POOL_WINDOWS = (2, 4, 8, 16)
POOL_HALO = 16
GROUP_DIM = 128
POOL_DIM = 512
ATTN_DIM = 512
KV_DIM = 128
QKV_END = POOL_DIM + ATTN_DIM + 2 * KV_DIM

ADAM_LR = 0.001
ADAM_B1 = 0.9
ADAM_B2 = 0.999
ADAM_EPS = 1e-08
ADAM_WD = 0.01
ADAM_STEP = 10

ROW_TILE = 512
TN_ROW_TILE = 2048
FFN_CHUNK = 256
FFN_BWD_ROW_TILE = 256
VMEM_LIMIT_BYTES = 56 << 20
ADAMW_BLOCK_ELEMS = 192 * 1024
NEG_BIG = -1e30

_NT = (((1,), (1,)), ((), ()))
_TN = (((0,), (0,)), ((), ()))


def _cp(*sem):
    return pltpu.CompilerParams(dimension_semantics=sem, vmem_limit_bytes=VMEM_LIMIT_BYTES)


def _resident(block, index):
    return pl.BlockSpec(block, lambda *_: index, pipeline_mode=pl.Buffered(1))


def _row_tile(m):
    return min(ROW_TILE, m)


def _sds(shape, dtype):
    return jax.ShapeDtypeStruct(shape, dtype)


def _mesh_pos():
    return lax.axis_index("x"), lax.axis_index("y"), lax.axis_index("c")


def _all_gather_many(shards, name, after=None):
    n = len(shards)

    deps = [] if after is None else [after]

    def body(*refs):
        ins, outs = refs[:n], refs[n + len(deps):2 * n + len(deps)]
        send_sems, recv_sems, local_sems = refs[2 * n + len(deps):]
        x, y, c = _mesh_pos()
        me, sibling = (x, y, c), (x, y, 1 - c)
        chips = [(1 - x, y), (x, 1 - y), (1 - x, 1 - y)]

        def slot(a, pos):
            return outs[a].at[4 * pos[0] + 2 * pos[1] + pos[2]]

        def copy(a, k, block, to, src=None):
            return pltpu.make_async_remote_copy(
                src_ref=slot(a, block) if src is None else src, dst_ref=slot(a, block),
                send_sem=send_sems.at[a, k], recv_sem=recv_sems.at[a, k],
                device_id=to, device_id_type=pl.DeviceIdType.MESH)

        mine = [pltpu.make_async_copy(ins[a], slot(a, me), local_sems.at[a]) for a in range(n)]
        for cp in mine:
            cp.start()
        first = []
        for a in range(n):
            first.append(copy(a, 0, me, sibling, src=ins[a]))
            for j, chip in enumerate(chips):
                first.append(copy(a, 1 + j, me, (*chip, c), src=ins[a]))
        for cp in first:
            cp.start()
        passed = []
        for j, chip in enumerate(chips):
            for a in range(n):
                copy(a, 1 + j, (*chip, c), me).wait_recv()
                fwd = copy(a, 4 + j, (*chip, c), sibling)
                fwd.start()
                passed.append(fwd)
        for a in range(n):
            copy(a, 0, sibling, me).wait_recv()
        for j, chip in enumerate(chips):
            for a in range(n):
                copy(a, 4 + j, (*chip, 1 - c), me).wait_recv()
        for cp in first + passed:
            cp.wait_send()
        for cp in mine:
            cp.wait()

    any_spec = pl.BlockSpec(memory_space=pl.ANY)
    return pl.pallas_call(
        body, name=name,
        out_shape=[_sds((N_DEV,) + s.shape, s.dtype) for s in shards],
        in_specs=[any_spec] * (n + len(deps)), out_specs=[any_spec] * n,
        scratch_shapes=[pltpu.SemaphoreType.DMA((n, 7)), pltpu.SemaphoreType.DMA((n, 7)),
                        pltpu.SemaphoreType.DMA((n,))],
    )(*shards, *deps)


_ALL_PEERS = tuple(range(1, N_DEV))
_CHIP_PEERS = (1, 2, 4, 6)


def _direct_copies(src, land, send_sem, recv_sem, local_sem, scatter, peers=_ALL_PEERS):
    x, y, c = _mesh_pos()
    me = 4 * x + 2 * y + c
    local = pltpu.make_async_copy(src.at[me] if scatter else src, land.at[me], local_sem)
    remote = []
    for k in peers:
        px = 1 - x if k & 4 else x
        py = 1 - y if k & 2 else y
        pc = 1 - c if k & 1 else c
        remote.append(pltpu.make_async_remote_copy(
            src_ref=src.at[4 * px + 2 * py + pc] if scatter else src, dst_ref=land.at[me],
            send_sem=send_sem, recv_sem=recv_sem, device_id=(px, py, pc), device_id_type=pl.DeviceIdType.MESH))
    every = land.at[pl.ds(0, len(peers))]
    drain = pltpu.make_async_remote_copy(src_ref=every, dst_ref=every, send_sem=send_sem, recv_sem=recv_sem,
                                         device_id=(x, y, c), device_id_type=pl.DeviceIdType.MESH)
    return local, remote, drain


_HBM_SPEC = pl.BlockSpec(memory_space=pltpu.HBM)
_SEM_SPEC = pl.BlockSpec(memory_space=pltpu.SEMAPHORE)
_DATAFLOW = pltpu.SideEffectType.DATAFLOW_SIDE_EFFECTING
_SEMS_PER_ARRAY = 3


def _exchange_start(srcs, scatter, name, peers=_ALL_PEERS, after=None):
    n = len(srcs)
    deps = [] if after is None else [after]
    n_sems = _SEMS_PER_ARRAY * n
    land_shapes = [s.shape if scatter else (N_DEV,) + s.shape for s in srcs]

    def body(*refs):
        ins, lands = refs[:n], refs[n:2 * n]
        sems = refs[2 * n + len(deps):2 * n + len(deps) + n_sems]
        for a in range(n):
            local, remote, _ = _direct_copies(ins[a], lands[a], *sems[3 * a:3 * a + 3], scatter, peers)
            local.start()
            for cp in remote:
                cp.start()
        refs[-1][...] = jnp.zeros_like(refs[-1])

    outs = pl.pallas_call(
        body, name=name,
        out_shape=(*[pltpu.SemaphoreType.DMA(())] * n_sems,
                   *[pltpu.HBM(s.shape, s.dtype) for s in srcs],
                   *[pltpu.HBM(shape, s.dtype) for shape, s in zip(land_shapes, srcs)],
                   _sds((8, 128), F32)),
        in_specs=[_HBM_SPEC] * (2 * n) + [pl.BlockSpec(memory_space=pl.ANY)] * len(deps),
        out_specs=(*[_SEM_SPEC] * n_sems, *[_HBM_SPEC] * (2 * n), pl.BlockSpec(memory_space=pltpu.VMEM)),
        input_output_aliases={i: n_sems + i for i in range(2 * n)},
        compiler_params=pltpu.CompilerParams(has_side_effects=_DATAFLOW),
    )(*[pltpu.with_memory_space_constraint(s, pltpu.HBM) for s in srcs],
      *[pltpu.with_memory_space_constraint(lax.empty(shape, s.dtype), pltpu.HBM) for shape, s in zip(land_shapes, srcs)],
      *deps)
    return (outs[:n_sems], outs[n_sems:n_sems + n], outs[n_sems + n:n_sems + 2 * n], scatter, peers), outs[-1]


def _exchange_wait(state, after, name):
    sems, srcs, lands, scatter, peers = state
    n = len(srcs)
    n_sems = len(sems)

    def body(*refs):
        ins, zones, ss = refs[:n], refs[n:2 * n], refs[2 * n:2 * n + n_sems]
        for a in range(n):
            local, _, drain = _direct_copies(ins[a], zones[a], *ss[3 * a:3 * a + 3], scatter, peers)
            drain.wait_send()
            drain.wait_recv()
            local.wait()

    outs = pl.pallas_call(
        body, name=name,
        out_shape=(*[pltpu.HBM(s.shape, s.dtype) for s in srcs], *[pltpu.HBM(z.shape, z.dtype) for z in lands]),
        in_specs=[_HBM_SPEC] * (2 * n) + [_SEM_SPEC] * n_sems + [pl.BlockSpec(memory_space=pl.ANY)],
        out_specs=[_HBM_SPEC] * (2 * n),
        input_output_aliases={i: i for i in range(2 * n)},
        compiler_params=pltpu.CompilerParams(has_side_effects=_DATAFLOW),
    )(*srcs, *lands, *sems, after)
    return outs[n:]


def _forward_to_sibling(lands, name):
    n = len(lands)

    def body(*refs):
        zones = refs[n:2 * n]
        send_sems, recv_sems = refs[2 * n:]
        x, y, c = _mesh_pos()
        chips = [(1 - x, y), (x, 1 - y), (1 - x, 1 - y)]
        sends, recvs = [], []
        for a in range(n):
            for j, (px, py) in enumerate(chips):
                mine = zones[a].at[4 * px + 2 * py + c]
                theirs = zones[a].at[4 * px + 2 * py + 1 - c]
                sends.append(pltpu.make_async_remote_copy(
                    src_ref=mine, dst_ref=mine, send_sem=send_sems.at[a, j], recv_sem=recv_sems.at[a, j],
                    device_id=(x, y, 1 - c), device_id_type=pl.DeviceIdType.MESH))
                recvs.append(pltpu.make_async_remote_copy(
                    src_ref=theirs, dst_ref=theirs, send_sem=send_sems.at[a, j], recv_sem=recv_sems.at[a, j],
                    device_id=(x, y, 1 - c), device_id_type=pl.DeviceIdType.MESH))
        for cp in sends:
            cp.start()
        for cp in recvs:
            cp.wait_recv()
        for cp in sends:
            cp.wait_send()

    any_spec = pl.BlockSpec(memory_space=pl.ANY)
    return pl.pallas_call(
        body, name=name,
        out_shape=[_sds(z.shape, z.dtype) for z in lands],
        in_specs=[any_spec] * n, out_specs=[any_spec] * n,
        input_output_aliases={a: a for a in range(n)},
        scratch_shapes=[pltpu.SemaphoreType.DMA((n, 3)), pltpu.SemaphoreType.DMA((n, 3))],
    )(*lands)


def _rms_fwd(xv, gain):
    r = lax.rsqrt(jnp.mean(xv * xv, axis=-1, keepdims=True) + EPS)
    return xv * r * gain


def _rms_bwd(dh, xv, gain):
    r = lax.rsqrt(jnp.mean(xv * xv, axis=-1, keepdims=True) + EPS)
    xn = xv * r
    dxn = dh * gain
    dx = r * (dxn - xn * jnp.mean(dxn * xn, axis=-1, keepdims=True))
    return dx, dh * xn


def _sigmoid(v):
    return 0.5 * jnp.tanh(0.5 * v) + 0.5


def _silu_parts(g):
    s = _sigmoid(g)
    return g * s, s * (1.0 + g * (1.0 - s))


def _segment_mean(v, width):
    r = lax.broadcasted_iota(jnp.int32, (width, width), 0) >> 6
    c = lax.broadcasted_iota(jnp.int32, (width, width), 1) >> 6
    bd = (r == c).astype(BF16)
    hi = v.astype(BF16)
    lo = (v - hi.astype(F32)).astype(BF16)
    total = jnp.dot(hi, bd, preferred_element_type=F32) + jnp.dot(lo, bd, preferred_element_type=F32)
    return total * (1.0 / HEAD_DIM)


def _rope_partner(v):
    width = v.shape[1]
    half = ROT_DIM // 2
    lane = lax.broadcasted_iota(jnp.int32, v.shape, 1) & (HEAD_DIM - 1)
    up = jnp.where(lane < ROT_DIM, pltpu.roll(v, half, 1), 0.0)
    return jnp.where(lane < half, pltpu.roll(v, width - half, 1), up)


def _tile_lanes(t, width):
    return t if width == t.shape[1] else jnp.tile(t, (1, width // t.shape[1]))


def _ffn_chunks(f):
    return [slice(j * FFN_CHUNK, (j + 1) * FFN_CHUNK) for j in range(f // FFN_CHUNK)]


def _ffn_up(x, ln, wgu, name, after=None):
    m, d = x.shape
    f = wgu.shape[1]
    tm = _row_tile(m)
    deps = [] if after is None else [after]

    def body(*refs):
        x_ref, ln_ref, w_ref = refs[:3]
        h_ref, gu_ref, a_ref = refs[-3:]
        h = _rms_fwd(x_ref[...], ln_ref[...]).astype(BF16)
        h_ref[...] = h
        for cols in _ffn_chunks(f):
            g = lax.dot_general(h, w_ref[0, cols, :], _NT, preferred_element_type=F32)
            u = lax.dot_general(h, w_ref[1, cols, :], _NT, preferred_element_type=F32)
            gu_ref[0, :, cols] = g.astype(BF16)
            gu_ref[1, :, cols] = u.astype(BF16)
            a_ref[:, cols] = (g * _sigmoid(g) * u).astype(BF16)

    return pl.pallas_call(
        body, name=name, grid=(m // tm,),
        in_specs=[pl.BlockSpec((tm, d), lambda i: (i, 0)), _resident((1, d), (0, 0)),
                  _resident((2, f, d), (0, 0, 0))] + [pl.BlockSpec(memory_space=pl.ANY)] * len(deps),
        out_specs=[pl.BlockSpec((tm, d), lambda i: (i, 0)), pl.BlockSpec((2, tm, f), lambda i: (0, i, 0)),
                   pl.BlockSpec((tm, f), lambda i: (i, 0))],
        out_shape=[_sds((m, d), BF16), _sds((2, m, f), BF16), _sds((m, f), BF16)],
        compiler_params=_cp("parallel"),
    )(x, ln, wgu, *deps)


def _ffn_down(x, act, wd, name):
    m, d = x.shape
    f = act.shape[-1]
    tm = _row_tile(m)

    def body(x_ref, a_ref, w_ref, o_ref):
        o_ref[...] = x_ref[...] + 0.5 * jnp.dot(a_ref[...], w_ref[...], preferred_element_type=F32)

    return pl.pallas_call(
        body, name=name, grid=(m // tm,),
        in_specs=[pl.BlockSpec((tm, d), lambda i: (i, 0)), pl.BlockSpec((tm, f), lambda i: (i, 0)),
                  _resident((f, d), (0, 0))],
        out_specs=pl.BlockSpec((tm, d), lambda i: (i, 0)),
        out_shape=_sds((m, d), F32),
        compiler_params=_cp("parallel"),
    )(x, act, wd)


def _ffn_bwd(dy, x, ln, gu, wd, wgu, name, after=None):
    m, d = dy.shape
    f = gu.shape[-1]
    tm = min(FFN_BWD_ROW_TILE, m)
    deps = [] if after is None else [after]

    def body(*refs):
        dy_ref, x_ref, ln_ref, gu_ref, wd_ref, wgu_ref = refs[:6]
        dyh_ref, dgu_ref, dx_ref, dln_ref = refs[-4:]

        @pl.when(pl.program_id(0) == 0)
        def _():
            dln_ref[...] = jnp.zeros_like(dln_ref)

        dyh = (0.5 * dy_ref[...]).astype(BF16)
        dyh_ref[...] = dyh
        dgs, dus = [], []
        for cols in _ffn_chunks(f):
            da = lax.dot_general(dyh, wd_ref[cols, :], _NT, preferred_element_type=F32)
            g = gu_ref[0, :, cols].astype(F32)
            u = gu_ref[1, :, cols].astype(F32)
            silu, dsilu = _silu_parts(g)
            dgs.append((da * u * dsilu).astype(BF16))
            dus.append((da * silu).astype(BF16))
            dgu_ref[0, :, cols] = dgs[-1]
            dgu_ref[1, :, cols] = dus[-1]
        dh = jnp.dot(jnp.concatenate(dgs, axis=1), wgu_ref[0], preferred_element_type=F32)
        dh += jnp.dot(jnp.concatenate(dus, axis=1), wgu_ref[1], preferred_element_type=F32)
        dx, dgain = _rms_bwd(dh, x_ref[...], ln_ref[...])
        dx_ref[...] = dy_ref[...] + dx
        dln_ref[...] += jnp.sum(dgain, axis=0, keepdims=True)

    row = lambda i: (i, 0)
    return pl.pallas_call(
        body, name=name, grid=(m // tm,),
        in_specs=[pl.BlockSpec((tm, d), row), pl.BlockSpec((tm, d), row), _resident((1, d), (0, 0)),
                  pl.BlockSpec((2, tm, f), lambda i: (0, i, 0)), _resident((f, d), (0, 0)),
                  _resident((2, f, d), (0, 0, 0))] + [pl.BlockSpec(memory_space=pl.ANY)] * len(deps),
        out_specs=[pl.BlockSpec((tm, d), row), pl.BlockSpec((2, tm, f), lambda i: (0, i, 0)),
                   pl.BlockSpec((tm, d), row), pl.BlockSpec((1, d), lambda i: (0, 0))],
        out_shape=[_sds((m, d), BF16), _sds((2, m, f), BF16), _sds((m, d), F32), _sds((1, d), F32)],
        compiler_params=_cp("arbitrary"),
    )(dy, x, ln, gu, wd, wgu, *deps)


def _matmul_tn(a, b, name, a_chunk=None, out_dtype=BF16, after=None):
    ja, m, k = a.shape
    jb, _, n = b.shape
    nj = max(ja, jb)
    kc = k if a_chunk is None else a_chunk
    tm = min(TN_ROW_TILE, m)
    nm = m // tm
    deps = [] if after is None else [after]

    def body(*refs):
        a_ref, b_ref = refs[:2]
        o_ref, acc = refs[-2:]
        step = pl.program_id(2)

        @pl.when(step == 0)
        def _():
            acc[...] = jnp.zeros_like(acc)

        acc[...] += lax.dot_general(a_ref[...], b_ref[...], _TN, preferred_element_type=F32)

        @pl.when(step == nm - 1)
        def _():
            o_ref[...] = acc[...].astype(o_ref.dtype)

    return pl.pallas_call(
        body, name=name, grid=(nj, k // kc, nm),
        in_specs=[pl.BlockSpec((None, tm, kc), (lambda j, c, s: (j, s, c)) if ja > 1 else (lambda j, c, s: (0, s, c))),
                  pl.BlockSpec((None, tm, n), (lambda j, c, s: (j, s, 0)) if jb > 1 else (lambda j, c, s: (0, s, 0)))]
        + [pl.BlockSpec(memory_space=pl.ANY)] * len(deps),
        out_specs=pl.BlockSpec((None, kc, n), lambda j, c, s: (j, c, 0)),
        out_shape=_sds((nj, k, n), out_dtype),
        scratch_shapes=[pltpu.VMEM((kc, n), F32)],
        compiler_params=_cp("parallel", "parallel", "arbitrary"),
    )(a, b, *deps)


def _mix_in(x, ln, w_in, name):
    m, d = x.shape
    n_in = w_in.shape[0]
    tm = _row_tile(m)

    def body(x_ref, ln_ref, w_ref, h_ref, zq_ref, zg_ref):
        h = _rms_fwd(x_ref[...], ln_ref[...]).astype(BF16)
        h_ref[...] = h
        zq_ref[...] = lax.dot_general(h, w_ref[:QKV_END, :], _NT, preferred_element_type=F32)
        zg_ref[...] = lax.dot_general(h, w_ref[QKV_END:, :], _NT, preferred_element_type=F32).astype(BF16)

    return pl.pallas_call(
        body, name=name, grid=(m // tm,),
        in_specs=[pl.BlockSpec((tm, d), lambda i: (i, 0)), _resident((1, d), (0, 0)), _resident((n_in, d), (0, 0))],
        out_specs=[pl.BlockSpec((tm, d), lambda i: (i, 0)), pl.BlockSpec((tm, QKV_END), lambda i: (i, 0)),
                   pl.BlockSpec((tm, n_in - QKV_END), lambda i: (i, 0))],
        out_shape=[_sds((m, d), BF16), _sds((m, QKV_END), F32), _sds((m, n_in - QKV_END), BF16)],
        compiler_params=_cp("parallel"),
    )(x, ln, w_in)


def _pool_fwd(zq, pool_w, scale, name):
    m = zq.shape[0]
    tm = _row_tile(m)
    halo_blocks = tm // POOL_HALO

    def body(zc_ref, zh_ref, pw_ref, sc_ref, d_ref, p_ref):
        i = pl.program_id(0)
        halo = jnp.where(i > 0, zh_ref[...], 0.0)
        ext = jnp.concatenate([halo, zc_ref[...]], axis=0)
        t = i * tm + lax.broadcasted_iota(jnp.int32, (tm, 1), 0)
        for g, w in enumerate(POOL_WINDOWS):
            lanes = slice(g * GROUP_DIM, (g + 1) * GROUP_DIM)
            e = ext[:, lanes]
            s, k = e, 1
            while k < w:
                s = s + pltpu.roll(s, k, 0)
                k *= 2
            cnt = jnp.minimum(t + 1, w).astype(F32)
            dg = (s[POOL_HALO:, :] / cnt - e[POOL_HALO:, :]).astype(BF16)
            y = jnp.dot(dg, pw_ref[g].astype(BF16), preferred_element_type=F32)
            d_ref[:, lanes] = dg
            p_ref[:, lanes] = (y * sc_ref[:, lanes]).astype(BF16)

    return pl.pallas_call(
        body, name=name, grid=(m // tm,),
        in_specs=[pl.BlockSpec((tm, POOL_DIM), lambda i: (i, 0)),
                  pl.BlockSpec((POOL_HALO, POOL_DIM), lambda i: (jnp.maximum(i * halo_blocks - 1, 0), 0)),
                  _resident((4, GROUP_DIM, GROUP_DIM), (0, 0, 0)), _resident((1, POOL_DIM), (0, 0))],
        out_specs=[pl.BlockSpec((tm, POOL_DIM), lambda i: (i, 0)), pl.BlockSpec((tm, POOL_DIM), lambda i: (i, 0))],
        out_shape=[_sds((m, POOL_DIM), BF16), _sds((m, POOL_DIM), BF16)],
        compiler_params=_cp("parallel"),
    )(zq, zq, pool_w, scale)


def _pool_bwd(dp, d, pool_w, scale, name):
    m = dp.shape[0]
    tm = _row_tile(m)
    nb = m // tm
    halo_blocks = tm // POOL_HALO
    rows = tm + POOL_HALO

    def body(dpc_ref, dph_ref, d_ref, pw_ref, sc_ref, du_ref, dpw_ref, dsc_ref):
        i = pl.program_id(0)

        @pl.when(i == 0)
        def _():
            dpw_ref[...] = jnp.zeros_like(dpw_ref)
            dsc_ref[...] = jnp.zeros_like(dsc_ref)

        halo = jnp.where(i < nb - 1, dph_ref[...], 0.0)
        dpc = dpc_ref[...]
        ext = jnp.concatenate([dpc, halo], axis=0)
        t = i * tm + lax.broadcasted_iota(jnp.int32, (rows, 1), 0)
        for g, w in enumerate(POOL_WINDOWS):
            lanes = slice(g * GROUP_DIM, (g + 1) * GROUP_DIM)
            pwb = pw_ref[g].astype(BF16)
            dyb = (ext[:, lanes] * sc_ref[:, lanes]).astype(BF16)
            dd = lax.dot_general(dyb, pwb, _NT, preferred_element_type=F32)
            cnt = jnp.minimum(t + 1, w).astype(F32)
            s, k = dd / cnt, 1
            while k < w:
                s = s + pltpu.roll(s, rows - k, 0)
                k *= 2
            du_ref[:, lanes] = (s[:tm, :] - dd[:tm, :]).astype(BF16)
            dcur = d_ref[:, lanes]
            y = jnp.dot(dcur, pwb, preferred_element_type=F32)
            dsc_ref[:, lanes] += jnp.sum(dpc[:, lanes] * y, axis=0, keepdims=True)
            dpw_ref[g] += lax.dot_general(dcur, dyb[:tm, :], _TN, preferred_element_type=F32)

    return pl.pallas_call(
        body, name=name, grid=(nb,),
        in_specs=[pl.BlockSpec((tm, POOL_DIM), lambda i: (i, 0)),
                  pl.BlockSpec((POOL_HALO, POOL_DIM), lambda i: (jnp.minimum((i + 1) * halo_blocks, nb * halo_blocks - 1), 0)),
                  pl.BlockSpec((tm, POOL_DIM), lambda i: (i, 0)),
                  _resident((4, GROUP_DIM, GROUP_DIM), (0, 0, 0)), _resident((1, POOL_DIM), (0, 0))],
        out_specs=[pl.BlockSpec((tm, POOL_DIM), lambda i: (i, 0)),
                   pl.BlockSpec((4, GROUP_DIM, GROUP_DIM), lambda i: (0, 0, 0)),
                   pl.BlockSpec((1, POOL_DIM), lambda i: (0, 0))],
        out_shape=[_sds((m, POOL_DIM), BF16), _sds((4, GROUP_DIM, GROUP_DIM), F32), _sds((1, POOL_DIM), F32)],
        compiler_params=_cp("arbitrary"),
    )(dp, dp, d, pool_w, scale)


def _qk_norm_rope(xv, gain, cos, sin):
    width = xv.shape[1]
    r = lax.rsqrt(_segment_mean(xv * xv, width) + EPS)
    y = xv * r * gain
    return y * _tile_lanes(cos, width) + _rope_partner(y) * _tile_lanes(sin, width)


def _qk_prep(zq, qgain, kgain, cos, sin, name):
    m = zq.shape[0]
    tm = _row_tile(m)

    def body(q_ref, kv_ref, qg_ref, kg_ref, cos_ref, sin_ref, qr_ref, kr_ref, v_ref):
        cos_v, sin_v = cos_ref[...], sin_ref[...]
        qr_ref[...] = (_qk_norm_rope(q_ref[...], qg_ref[...], cos_v, sin_v) * ATTN_SCALE).astype(BF16)
        kv = kv_ref[...]
        kr_ref[...] = _qk_norm_rope(kv[:, :KV_DIM], kg_ref[...], cos_v, sin_v).astype(BF16)
        v_ref[...] = kv[:, KV_DIM:].astype(BF16)

    return pl.pallas_call(
        body, name=name, grid=(m // tm,),
        in_specs=[pl.BlockSpec((tm, ATTN_DIM), lambda i: (i, 1)), pl.BlockSpec((tm, 2 * KV_DIM), lambda i: (i, 4)),
                  _resident((1, ATTN_DIM), (0, 0)), _resident((1, KV_DIM), (0, 0)),
                  pl.BlockSpec((tm, 128), lambda i: (i, 0)), pl.BlockSpec((tm, 128), lambda i: (i, 0))],
        out_specs=[pl.BlockSpec((tm, ATTN_DIM), lambda i: (i, 0)), pl.BlockSpec((tm, KV_DIM), lambda i: (i, 0)),
                   pl.BlockSpec((tm, KV_DIM), lambda i: (i, 0))],
        out_shape=[_sds((m, ATTN_DIM), BF16), _sds((m, KV_DIM), BF16), _sds((m, KV_DIM), BF16)],
        compiler_params=_cp("parallel"),
    )(zq, zq, qgain, kgain, cos, sin)


STACK_ROWS = N_Q_HEADS * ATTN_BLOCK


def _band_bias():
    qi = jnp.arange(STACK_ROWS)[:, None] % ATTN_BLOCK
    ki = jnp.arange(2 * ATTN_BLOCK)[None, :]
    diff = qi + ATTN_BLOCK - ki
    band = (diff >= 0) & (diff < ATTN_BLOCK)
    first = band & (ki >= ATTN_BLOCK)
    return jnp.where(jnp.stack([first, band]), 0.0, NEG_BIG).astype(F32)


def _band_tables():
    qi = jnp.arange(STACK_ROWS)[:, None] % ATTN_BLOCK
    use_prev = jnp.arange(ATTN_BLOCK)[None, :] > qi
    return jnp.stack([use_prev.astype(F32), jnp.where(use_prev, NEG_BIG, 0.0).astype(F32),
                      jnp.zeros((STACK_ROWS, ATTN_BLOCK), F32)])


def _band_merge(use_prev, from_prev, from_cur):
    return jnp.where(use_prev, from_prev, from_cur)


def _band_split(use_prev, merged):
    return jnp.where(use_prev, merged, 0.0).astype(BF16), jnp.where(use_prev, 0.0, merged).astype(BF16)


def _stack_heads(v):
    zeros = jnp.zeros((ATTN_BLOCK, HEAD_DIM), v.dtype)
    rows = []
    for h in range(N_Q_HEADS):
        qh = v[:, h * HEAD_DIM:(h + 1) * HEAD_DIM]
        rows.append(jnp.concatenate([qh, zeros] if h < GQA_GROUP else [zeros, qh], axis=1))
    return jnp.concatenate(rows, axis=0)


def _unstack_heads(stacked):
    parts = []
    for h in range(N_Q_HEADS):
        lanes = slice(0, HEAD_DIM) if h < GQA_GROUP else slice(HEAD_DIM, 2 * HEAD_DIM)
        parts.append(stacked[h * ATTN_BLOCK:(h + 1) * ATTN_BLOCK, lanes])
    return jnp.concatenate(parts, axis=1)


def _stacked_sinks(sk_ref):
    row_head = lax.broadcasted_iota(jnp.int32, (STACK_ROWS, 1), 0) >> 7
    col = jnp.full((STACK_ROWS, 1), sk_ref[0], F32)
    for h in range(1, N_Q_HEADS):
        col = jnp.where(row_head == h, sk_ref[h], col)
    return col


def _softmax_with_sink(s, sink):
    mx = jnp.maximum(jnp.max(s, axis=-1, keepdims=True), sink)
    p = jnp.exp(s - mx)
    es = jnp.exp(sink - mx)
    inv = 1.0 / (jnp.sum(p, axis=-1, keepdims=True) + es)
    return p * inv, es * inv


def _merged_logits(qh, kp, kc, use_prev, bias):
    return _band_merge(use_prev, lax.dot_general(qh, kp, _NT, preferred_element_type=F32),
                       lax.dot_general(qh, kc, _NT, preferred_element_type=F32)) + bias


def _attn_fwd(qr, kr, vb, sinks, name):
    m = qr.shape[0]
    nb = m // ATTN_BLOCK

    def body(q_ref, kp_ref, kc_ref, vp_ref, vc_ref, sk_ref, bias_ref, o_ref):
        kk = jnp.concatenate([kp_ref[...], kc_ref[...]], axis=0)
        vv = jnp.concatenate([vp_ref[...], vc_ref[...]], axis=0)
        s = lax.dot_general(_stack_heads(q_ref[...]), kk, _NT, preferred_element_type=F32)
        p, _ = _softmax_with_sink(s + bias_ref[jnp.minimum(pl.program_id(0), 1)], _stacked_sinks(sk_ref))
        o_ref[...] = _unstack_heads(jnp.dot(p.astype(BF16), vv, preferred_element_type=F32)).astype(BF16)

    prev = lambda n: (jnp.maximum(n - 1, 0), 0)
    cur = lambda n: (n, 0)
    return pl.pallas_call(
        body, name=name, grid=(nb,),
        in_specs=[pl.BlockSpec((ATTN_BLOCK, ATTN_DIM), cur),
                  pl.BlockSpec((ATTN_BLOCK, KV_DIM), prev), pl.BlockSpec((ATTN_BLOCK, KV_DIM), cur),
                  pl.BlockSpec((ATTN_BLOCK, KV_DIM), prev), pl.BlockSpec((ATTN_BLOCK, KV_DIM), cur),
                  pl.BlockSpec(memory_space=pltpu.SMEM), _resident((2, STACK_ROWS, 2 * ATTN_BLOCK), (0, 0, 0))],
        out_specs=pl.BlockSpec((ATTN_BLOCK, ATTN_DIM), cur),
        out_shape=_sds((m, ATTN_DIM), BF16),
        compiler_params=_cp("parallel"),
    )(qr, kr, kr, vb, vb, sinks, _band_bias())


def _attn_bwd(do, qr, kr, vb, sinks, name):
    m = qr.shape[0]
    nb = m // ATTN_BLOCK

    def body(do_ref, q_ref, kp_ref, kc_ref, vp_ref, vc_ref, sk_ref, tab_ref, dq_ref, dk_ref, dv_ref, ds_ref,
             carry_k, carry_v):
        n = pl.program_id(0)

        @pl.when(n == 0)
        def _():
            carry_k[...] = jnp.zeros_like(carry_k)
            carry_v[...] = jnp.zeros_like(carry_v)
            ds_ref[...] = jnp.zeros_like(ds_ref)

        @pl.when(n < nb)
        def _():
            use_prev = tab_ref[0] > 0.5
            bias = tab_ref[1 + jnp.minimum(n, 1)]
            kp, kc, vp, vc = kp_ref[...], kc_ref[...], vp_ref[...], vc_ref[...]
            lane = lax.broadcasted_iota(jnp.int32, (1, 128), 1)
            qs = _stack_heads(q_ref[...])
            dos = _stack_heads(do_ref[...].astype(BF16))
            p, ps = _softmax_with_sink(_merged_logits(qs, kp, kc, use_prev, bias), _stacked_sinks(sk_ref))
            dpr = _band_merge(use_prev, lax.dot_general(dos, vp, _NT, preferred_element_type=F32),
                              lax.dot_general(dos, vc, _NT, preferred_element_type=F32))
            delta = jnp.sum(p * dpr, axis=-1, keepdims=True)
            ds_prev, ds_cur = _band_split(use_prev, p * (dpr - delta))
            p_prev, p_cur = _band_split(use_prev, p)
            sink_term = ps * delta
            dsink = jnp.zeros((1, 128), F32)
            for h in range(N_Q_HEADS):
                rows = slice(h * ATTN_BLOCK, (h + 1) * ATTN_BLOCK)
                dsink = dsink + jnp.where(lane == h, -jnp.sum(sink_term[rows, :]), 0.0)
            dq = jnp.dot(ds_prev, kp, preferred_element_type=F32) + jnp.dot(ds_cur, kc, preferred_element_type=F32)
            dq_ref[...] = _unstack_heads(dq * ATTN_SCALE)
            dk_ref[...] = carry_k[...] + lax.dot_general(ds_prev, qs, _TN, preferred_element_type=F32)
            dv_ref[...] = carry_v[...] + lax.dot_general(p_prev, dos, _TN, preferred_element_type=F32)
            carry_k[...] = lax.dot_general(ds_cur, qs, _TN, preferred_element_type=F32)
            carry_v[...] = lax.dot_general(p_cur, dos, _TN, preferred_element_type=F32)
            ds_ref[...] += dsink

        @pl.when(n == nb)
        def _():
            dk_ref[...] = carry_k[...]
            dv_ref[...] = carry_v[...]

    cur = lambda n: (jnp.minimum(n, nb - 1), 0)
    prev = lambda n: (jnp.clip(n - 1, 0, nb - 1), 0)
    return pl.pallas_call(
        body, name=name, grid=(nb + 1,),
        in_specs=[pl.BlockSpec((ATTN_BLOCK, ATTN_DIM), cur), pl.BlockSpec((ATTN_BLOCK, ATTN_DIM), cur),
                  pl.BlockSpec((ATTN_BLOCK, KV_DIM), prev), pl.BlockSpec((ATTN_BLOCK, KV_DIM), cur),
                  pl.BlockSpec((ATTN_BLOCK, KV_DIM), prev), pl.BlockSpec((ATTN_BLOCK, KV_DIM), cur),
                  pl.BlockSpec(memory_space=pltpu.SMEM), _resident((3, STACK_ROWS, ATTN_BLOCK), (0, 0, 0))],
        out_specs=[pl.BlockSpec((ATTN_BLOCK, ATTN_DIM), cur), pl.BlockSpec((ATTN_BLOCK, KV_DIM), prev),
                   pl.BlockSpec((ATTN_BLOCK, KV_DIM), prev), pl.BlockSpec((1, 128), lambda n: (0, 0))],
        out_shape=[_sds((m, ATTN_DIM), F32), _sds((m, KV_DIM), F32), _sds((m, KV_DIM), F32), _sds((1, 128), F32)],
        scratch_shapes=[pltpu.VMEM((ATTN_BLOCK, KV_DIM), F32), pltpu.VMEM((ATTN_BLOCK, KV_DIM), F32)],
        compiler_params=_cp("arbitrary"),
    )(do, qr, kr, kr, vb, vb, sinks, _band_tables())


def _qk_norm_rope_bwd(dout, xv, gain, cos, sin):
    width = xv.shape[1]
    r = lax.rsqrt(_segment_mean(xv * xv, width) + EPS)
    xn = xv * r
    dy = dout * _tile_lanes(cos, width) + _rope_partner(dout * _tile_lanes(sin, width))
    dxn = dy * gain
    dx = r * (dxn - xn * _segment_mean(dxn * xn, width))
    return dx, jnp.sum(dy * xn, axis=0, keepdims=True)


def _fold_heads(v):
    out = v[:, :HEAD_DIM]
    for h in range(1, v.shape[1] // HEAD_DIM):
        out = out + v[:, h * HEAD_DIM:(h + 1) * HEAD_DIM]
    return out


def _qk_bwd(dq, dk, dv, zq, qgain, kgain, cos, sin, name):
    m = zq.shape[0]
    tm = _row_tile(m)

    def body(dq_ref, dk_ref, dv_ref, q_ref, kv_ref, qg_ref, kg_ref, cos_ref, sin_ref, dz_ref, dqg_ref, dkg_ref):
        @pl.when(pl.program_id(0) == 0)
        def _():
            dqg_ref[...] = jnp.zeros_like(dqg_ref)
            dkg_ref[...] = jnp.zeros_like(dkg_ref)

        cos_v, sin_v = cos_ref[...], sin_ref[...]
        dxq, dgq = _qk_norm_rope_bwd(dq_ref[...], q_ref[...], qg_ref[...], cos_v, sin_v)
        dxk, dgk = _qk_norm_rope_bwd(dk_ref[...], kv_ref[:, :KV_DIM], kg_ref[...], cos_v, sin_v)
        dz_ref[:, :ATTN_DIM] = dxq.astype(BF16)
        dz_ref[:, ATTN_DIM:ATTN_DIM + KV_DIM] = dxk.astype(BF16)
        dz_ref[:, ATTN_DIM + KV_DIM:] = dv_ref[...].astype(BF16)
        dqg_ref[...] += _fold_heads(dgq)
        dkg_ref[...] += _fold_heads(dgk)

    row = lambda i: (i, 0)
    return pl.pallas_call(
        body, name=name, grid=(m // tm,),
        in_specs=[pl.BlockSpec((tm, ATTN_DIM), row), pl.BlockSpec((tm, KV_DIM), row), pl.BlockSpec((tm, KV_DIM), row),
                  pl.BlockSpec((tm, ATTN_DIM), lambda i: (i, 1)), pl.BlockSpec((tm, 2 * KV_DIM), lambda i: (i, 4)),
                  _resident((1, ATTN_DIM), (0, 0)), _resident((1, KV_DIM), (0, 0)),
                  pl.BlockSpec((tm, 128), row), pl.BlockSpec((tm, 128), row)],
        out_specs=[pl.BlockSpec((tm, ATTN_DIM + 2 * KV_DIM), row), pl.BlockSpec((1, HEAD_DIM), lambda i: (0, 0)),
                   pl.BlockSpec((1, HEAD_DIM), lambda i: (0, 0))],
        out_shape=[_sds((m, ATTN_DIM + 2 * KV_DIM), BF16), _sds((1, HEAD_DIM), F32), _sds((1, HEAD_DIM), F32)],
        compiler_params=_cp("arbitrary"),
    )(dq, dk, dv, zq, zq, qgain, kgain, cos, sin)


def _merge_fwd(x, p, o, zg, w_pb, w_ab, w_out, name):
    m, d = x.shape
    tm = _row_tile(m)

    def body(x_ref, p_ref, o_ref, zg_ref, wpb_ref, wab_ref, wo_ref, xo_ref, mix_ref):
        a = jnp.dot(p_ref[...], wpb_ref[...], preferred_element_type=F32)
        b = jnp.dot(o_ref[...], wab_ref[...], preferred_element_type=F32)
        mix = (_sigmoid(zg_ref[:, :d].astype(F32)) * a + _sigmoid(zg_ref[:, d:].astype(F32)) * b).astype(BF16)
        mix_ref[...] = mix
        xo_ref[...] = x_ref[...] + jnp.dot(mix, wo_ref[...], preferred_element_type=F32)

    row = lambda i: (i, 0)
    return pl.pallas_call(
        body, name=name, grid=(m // tm,),
        in_specs=[pl.BlockSpec((tm, d), row), pl.BlockSpec((tm, POOL_DIM), row), pl.BlockSpec((tm, ATTN_DIM), row),
                  pl.BlockSpec((tm, 2 * d), row), _resident((POOL_DIM, d), (0, 0)), _resident((ATTN_DIM, d), (0, 0)),
                  _resident((d, d), (0, 0))],
        out_specs=[pl.BlockSpec((tm, d), row)] * 2,
        out_shape=[_sds((m, d), F32), _sds((m, d), BF16)],
        compiler_params=_cp("parallel"),
    )(x, p, o, zg, w_pb, w_ab, w_out)


def _merge_bwd(dy, p, o, zg, w_out, w_pb, w_ab, name):
    m, d = dy.shape
    tm = _row_tile(m)

    def body(dy_ref, p_ref, o_ref, zg_ref, wo_ref, wpb_ref, wab_ref, dyb_ref, da_ref, db_ref, dp_ref, do_ref, dzg_ref):
        dyb = dy_ref[...].astype(BF16)
        dyb_ref[...] = dyb
        dmix = lax.dot_general(dyb, wo_ref[...], _NT, preferred_element_type=F32)
        gp = _sigmoid(zg_ref[:, :d].astype(F32))
        ga = _sigmoid(zg_ref[:, d:].astype(F32))
        da = (dmix * gp).astype(BF16)
        db = (dmix * ga).astype(BF16)
        da_ref[...] = da
        db_ref[...] = db
        a = jnp.dot(p_ref[...], wpb_ref[...], preferred_element_type=F32)
        b = jnp.dot(o_ref[...], wab_ref[...], preferred_element_type=F32)
        dzg_ref[:, :d] = (dmix * a * gp * (1.0 - gp)).astype(BF16)
        dzg_ref[:, d:] = (dmix * b * ga * (1.0 - ga)).astype(BF16)
        dp_ref[...] = lax.dot_general(da, wpb_ref[...], _NT, preferred_element_type=F32)
        do_ref[...] = lax.dot_general(db, wab_ref[...], _NT, preferred_element_type=F32).astype(BF16)

    row = lambda i: (i, 0)
    return pl.pallas_call(
        body, name=name, grid=(m // tm,),
        in_specs=[pl.BlockSpec((tm, d), row), pl.BlockSpec((tm, POOL_DIM), row), pl.BlockSpec((tm, ATTN_DIM), row),
                  pl.BlockSpec((tm, 2 * d), row), _resident((d, d), (0, 0)), _resident((POOL_DIM, d), (0, 0)),
                  _resident((ATTN_DIM, d), (0, 0))],
        out_specs=[pl.BlockSpec((tm, d), row), pl.BlockSpec((tm, d), row), pl.BlockSpec((tm, d), row),
                   pl.BlockSpec((tm, POOL_DIM), row), pl.BlockSpec((tm, ATTN_DIM), row), pl.BlockSpec((tm, 2 * d), row)],
        out_shape=[_sds((m, d), BF16), _sds((m, d), BF16), _sds((m, d), BF16), _sds((m, POOL_DIM), F32),
                   _sds((m, ATTN_DIM), BF16), _sds((m, 2 * d), BF16)],
        compiler_params=_cp("parallel"),
    )(dy, p, o, zg, w_out, w_pb, w_ab)


def _mix_bwd_x(dy, x, ln, dzp, dzqkv, dzg, w_in, name):
    m, d = dy.shape
    n_in = w_in.shape[0]
    tm = _row_tile(m)

    def body(dy_ref, x_ref, ln_ref, dzp_ref, dzq_ref, dzg_ref, w_ref, dx_ref, dln_ref):
        @pl.when(pl.program_id(0) == 0)
        def _():
            dln_ref[...] = jnp.zeros_like(dln_ref)

        dh = jnp.dot(dzp_ref[...], w_ref[:POOL_DIM, :], preferred_element_type=F32)
        dh += jnp.dot(dzq_ref[...], w_ref[POOL_DIM:QKV_END, :], preferred_element_type=F32)
        dh += jnp.dot(dzg_ref[...], w_ref[QKV_END:, :], preferred_element_type=F32)
        dx, dgain = _rms_bwd(dh, x_ref[...], ln_ref[...])
        dx_ref[...] = dy_ref[...] + dx
        dln_ref[...] += jnp.sum(dgain, axis=0, keepdims=True)

    row = lambda i: (i, 0)
    return pl.pallas_call(
        body, name=name, grid=(m // tm,),
        in_specs=[pl.BlockSpec((tm, d), row), pl.BlockSpec((tm, d), row), _resident((1, d), (0, 0)),
                  pl.BlockSpec((tm, POOL_DIM), row), pl.BlockSpec((tm, QKV_END - POOL_DIM), row),
                  pl.BlockSpec((tm, n_in - QKV_END), row), _resident((n_in, d), (0, 0))],
        out_specs=[pl.BlockSpec((tm, d), row), pl.BlockSpec((1, d), lambda i: (0, 0))],
        out_shape=[_sds((m, d), F32), _sds((1, d), F32)],
        compiler_params=_cp("arbitrary"),
    )(dy, x, ln, dzp, dzqkv, dzg, w_in)


def _loss_head(y, target, name):
    m, d = y.shape
    tm = _row_tile(m)

    def body(y_ref, t_ref, loss_ref, dy_ref):
        @pl.when(pl.program_id(0) == 0)
        def _():
            loss_ref[...] = jnp.zeros_like(loss_ref)

        diff = y_ref[...] - t_ref[...]
        dy_ref[...] = diff * (1.0 / d)
        loss_ref[...] += 0.5 * jnp.sum(jnp.mean(diff * diff, axis=-1, keepdims=True), axis=0, keepdims=True)

    row = lambda i: (i, 0)
    return pl.pallas_call(
        body, name=name, grid=(m // tm,),
        in_specs=[pl.BlockSpec((tm, d), row), pl.BlockSpec((tm, d), row)],
        out_specs=[pl.BlockSpec((1, 1), lambda i: (0, 0)), pl.BlockSpec((tm, d), row)],
        out_shape=[_sds((1, 1), F32), _sds((m, d), F32)],
        compiler_params=_cp("arbitrary"),
    )(y, target)


def _adamw_math(g, w, m, v):
    m2 = ADAM_B1 * m + (1.0 - ADAM_B1) * g
    v2 = ADAM_B2 * v + (1.0 - ADAM_B2) * (g * g)
    m_hat = m2 / (1.0 - ADAM_B1 ** ADAM_STEP)
    v_hat = v2 / (1.0 - ADAM_B2 ** ADAM_STEP)
    delta = -ADAM_LR * (m_hat / (jnp.sqrt(v_hat) + ADAM_EPS) + ADAM_WD * w)
    return delta, m2, v2


def _sum_parts(parts_ref):
    g = parts_ref[0].astype(F32)
    for s in range(1, N_DEV):
        g = g + parts_ref[s].astype(F32)
    return g


def _adamw_sharded(parts, w, m, v, name, after=None):
    n_layers, rows, cols = w.shape
    tr = max(t for t in range(16, rows + 1, 16) if rows % t == 0 and t * cols <= ADAMW_BLOCK_ELEMS)
    nr = rows // tr
    deps = [] if after is None else [after]

    def body(*refs):
        part_refs = refs[:n_layers]
        w_ref, m_ref, v_ref = refs[n_layers:n_layers + 3]
        g_out, d_out, m_out, v_out = refs[-4:]
        layer = pl.program_id(0)
        for l in range(n_layers):
            @pl.when(layer == l)
            def _(l=l):
                g = _sum_parts(part_refs[l])
                delta, m2, v2 = _adamw_math(g, w_ref[...], m_ref[...], v_ref[...])
                g_out[...] = g
                d_out[...] = delta
                m_out[...] = m2
                v_out[...] = v2

    def part_map(l):
        return lambda layer, r: (0, jnp.where(layer == l, r, jnp.where(layer < l, 0, nr - 1)), 0)

    wspec = pl.BlockSpec((None, tr, cols), lambda layer, r: (layer, r, 0))
    return pl.pallas_call(
        body, name=name, grid=(n_layers, nr),
        in_specs=([pl.BlockSpec((N_DEV, tr, cols), part_map(l)) for l in range(n_layers)] + [wspec] * 3
                  + [pl.BlockSpec(memory_space=pl.ANY)] * len(deps)),
        out_specs=[wspec] * 4,
        out_shape=[_sds(w.shape, F32)] * 4,
        compiler_params=_cp("arbitrary", "arbitrary"),
    )(*parts, w, m, v, *deps)


def _adamw_packed(parts, w, m, v, name):
    def body(p_ref, w_ref, m_ref, v_ref, g_out, d_out, m_out, v_out):
        g = _sum_parts(p_ref)
        delta, m2, v2 = _adamw_math(g, w_ref[...], m_ref[...], v_ref[...])
        g_out[...] = g
        d_out[...] = delta
        m_out[...] = m2
        v_out[...] = v2

    return pl.pallas_call(
        body, name=name, out_shape=[_sds(w.shape, F32)] * 4,
        compiler_params=pltpu.CompilerParams(vmem_limit_bytes=VMEM_LIMIT_BYTES),
    )(parts, w, m, v)


_SMALL = ("ln_ffn1", "ln_mix", "pool_w", "pool_scale", "q_norm", "k_norm", "sinks", "ln_ffn2")


def _pack_small(arrs):
    rows = []
    for a in arrs:
        flat = a.reshape(-1)
        pad = (-flat.shape[0]) % 1024
        rows.append(jnp.pad(flat, (0, pad)).reshape(-1, 128))
    return jnp.concatenate(rows, axis=0)


def _unpack_small(packed, like):
    out, r0 = [], 0
    for a in like:
        size = a.size
        nrows = (size + 1023) // 1024 * 8
        out.append(packed[r0:r0 + nrows].reshape(-1)[:size].reshape(a.shape))
        r0 += nrows
    return out


def _rope_tables(m):
    pos = jnp.arange(m, dtype=F32)
    inv_freq = ROPE_THETA ** (-jnp.arange(0, ROT_DIM, 2, dtype=F32) / ROT_DIM)
    ang = pos[:, None] * inv_freq[None, :]
    cos8, sin8 = jnp.cos(ang), jnp.sin(ang)
    rest = HEAD_DIM - ROT_DIM
    cos64 = jnp.concatenate([cos8, cos8, jnp.ones((m, rest), F32)], axis=1)
    sin64 = jnp.concatenate([-sin8, sin8, jnp.zeros((m, rest), F32)], axis=1)
    return jnp.tile(cos64, (1, 2)), jnp.tile(sin64, (1, 2))


def _to_shard_major_cols(w):
    k = w.shape[0]
    return w.reshape(k, N_DEV, -1).transpose(1, 0, 2)


def _from_shard_major_cols(w):
    return w.transpose(1, 0, 2).reshape(w.shape[1], -1)


def kernel(x, ln_ffn1, w_ffn1_gu, w_ffn1_down, ln_mix, w_in, pool_w, pool_scale, w_pool_branch, q_norm, k_norm, sinks, w_attn_branch, w_out, ln_ffn2, w_ffn2_gu, w_ffn2_down, loss_target, m_ln_ffn1, m_w_ffn1_gu, m_w_ffn1_down, m_ln_mix, m_w_in, m_pool_w, m_pool_scale, m_w_pool_branch, m_q_norm, m_k_norm, m_sinks, m_w_attn_branch, m_w_out, m_ln_ffn2, m_w_ffn2_gu, m_w_ffn2_down, v_ln_ffn1, v_w_ffn1_gu, v_w_ffn1_down, v_ln_mix, v_w_in, v_pool_w, v_pool_scale, v_w_pool_branch, v_q_norm, v_k_norm, v_sinks, v_w_attn_branch, v_w_out, v_ln_ffn2, v_w_ffn2_gu, v_w_ffn2_down):
    weights = dict(ln_ffn1=ln_ffn1, w_ffn1_gu=w_ffn1_gu, w_ffn1_down=w_ffn1_down, ln_mix=ln_mix, w_in=w_in, pool_w=pool_w,
                   pool_scale=pool_scale, w_pool_branch=w_pool_branch, q_norm=q_norm, k_norm=k_norm, sinks=sinks,
                   w_attn_branch=w_attn_branch, w_out=w_out, ln_ffn2=ln_ffn2, w_ffn2_gu=w_ffn2_gu, w_ffn2_down=w_ffn2_down)
    mom_m = dict(ln_ffn1=m_ln_ffn1, w_ffn1_gu=m_w_ffn1_gu, w_ffn1_down=m_w_ffn1_down, ln_mix=m_ln_mix, w_in=m_w_in,
                 pool_w=m_pool_w, pool_scale=m_pool_scale, w_pool_branch=m_w_pool_branch, q_norm=m_q_norm, k_norm=m_k_norm,
                 sinks=m_sinks, w_attn_branch=m_w_attn_branch, w_out=m_w_out, ln_ffn2=m_ln_ffn2, w_ffn2_gu=m_w_ffn2_gu,
                 w_ffn2_down=m_w_ffn2_down)
    mom_v = dict(ln_ffn1=v_ln_ffn1, w_ffn1_gu=v_w_ffn1_gu, w_ffn1_down=v_w_ffn1_down, ln_mix=v_ln_mix, w_in=v_w_in,
                 pool_w=v_pool_w, pool_scale=v_pool_scale, w_pool_branch=v_w_pool_branch, q_norm=v_q_norm, k_norm=v_k_norm,
                 sinks=v_sinks, w_attn_branch=v_w_attn_branch, w_out=v_w_out, ln_ffn2=v_ln_ffn2, w_ffn2_gu=v_w_ffn2_gu,
                 w_ffn2_down=v_w_ffn2_down)
    order = ("ln_ffn1", "w_ffn1_gu", "w_ffn1_down", "ln_mix", "w_in", "pool_w", "pool_scale", "w_pool_branch", "q_norm",
             "k_norm", "sinks", "w_attn_branch", "w_out", "ln_ffn2", "w_ffn2_gu", "w_ffn2_down")
    big = ("w_ffn1_gu", "w_ffn1_down", "w_in", "w_pool_branch", "w_attn_branch", "w_out", "w_ffn2_gu", "w_ffn2_down")

    transposed = ("w_ffn1_gu", "w_ffn2_gu", "w_in")
    for group in (weights, mom_m, mom_v):
        for k in transposed:
            group[k] = jnp.swapaxes(group[k], 1, 2)

    n_layers = ln_ffn1.shape[0]
    seq, d = x.shape[-2], x.shape[-1]
    xs = x.reshape(seq, d)
    target = loss_target.reshape(seq, d)
    cos, sin = _rope_tables(seq)

    first_keys = ("w_ffn1_gu", "w_ffn1_down")
    rest_keys = tuple(k for k in big if k not in first_keys)

    def layer_shards(l, keys=big):
        return [weights[k][l].astype(BF16) for k in keys]

    def first_weights(l, g):
        return dict(gu1=g["w_ffn1_gu"].reshape(2, -1, d), down1=g["w_ffn1_down"].reshape(-1, d), ln1=ln_ffn1[l][None])

    def rest_weights(l, g):
        return dict(
            gu2=g["w_ffn2_gu"].reshape(2, -1, d), down2=g["w_ffn2_down"].reshape(-1, d),
            w_in=g["w_in"].reshape(-1, d), w_pb=_from_shard_major_cols(g["w_pool_branch"]),
            w_ab=_from_shard_major_cols(g["w_attn_branch"]), w_out=g["w_out"].reshape(d, d),
            ln_mix=ln_mix[l][None], ln2=ln_ffn2[l][None], pool_w=pool_w[l],
            pool_scale=pool_scale[l][None], sinks=sinks[l],
            qgain=jnp.tile(q_norm[l], N_Q_HEADS)[None], kgain=jnp.tile(k_norm[l], N_KV_HEADS)[None])

    def layer_weights(l, full):
        g = dict(zip(big, full))
        return {**first_weights(l, g), **rest_weights(l, g)}

    got = _all_gather_many(layer_shards(0, first_keys), name="gather_first_l0")
    gathered = [first_weights(0, dict(zip(first_keys, got)))]
    rest_in_flight, token = _exchange_start(layer_shards(0, rest_keys), scatter=False, name="gather_start_l0",
                                            peers=_CHIP_PEERS)
    saved = []
    cur = xs
    for l in range(n_layers):
        lw = gathered[l]
        s = dict(x0=cur)
        in_flight = None
        if l + 1 < n_layers:
            in_flight, token = _exchange_start(layer_shards(l + 1), scatter=False, name=f"gather_start_l{l + 1}",
                                               after=token)
        s["h1"], s["gu1"], act1 = _ffn_up(cur, lw["ln1"], lw["gu1"], name=f"ffn1_up_l{l}", after=token)
        s["act1"] = act1
        x1 = _ffn_down(cur, act1, lw["down1"], name=f"ffn1_down_l{l}")
        s["x1"] = x1
        if l == 0:
            got = _forward_to_sibling(_exchange_wait(rest_in_flight, x1, name="gather_wait_l0"), name="gather_forward_l0")
            lw.update(rest_weights(0, dict(zip(rest_keys, got))))
        s["h2"], zq, zg = _mix_in(x1, lw["ln_mix"], lw["w_in"], name=f"mix_in_l{l}")
        s["zq"], s["zg"] = zq, zg
        s["d"], s["p"] = _pool_fwd(zq, lw["pool_w"], lw["pool_scale"], name=f"pool_fwd_l{l}")
        s["qr"], s["kr"], s["vb"] = _qk_prep(zq, lw["qgain"], lw["kgain"], cos, sin, name=f"qk_prep_l{l}")
        s["o"] = _attn_fwd(s["qr"], s["kr"], s["vb"], lw["sinks"], name=f"attn_fwd_l{l}")
        x2, s["mix"] = _merge_fwd(x1, s["p"], s["o"], zg, lw["w_pb"], lw["w_ab"], lw["w_out"],
                                                  name=f"merge_fwd_l{l}")
        s["x2"] = x2
        s["h3"], s["gu2"], act2 = _ffn_up(x2, lw["ln2"], lw["gu2"], name=f"ffn2_up_l{l}")
        s["act2"] = act2
        cur = _ffn_down(x2, act2, lw["down2"], name=f"ffn2_down_l{l}")
        saved.append(s)
        if in_flight is not None:
            gathered.append(layer_weights(l + 1, _exchange_wait(in_flight, cur, name=f"gather_wait_l{l + 1}")))

    loss_local, dy = _loss_head(cur, target, name="loss_head")
    loss = lax.psum(loss_local[0, 0], MESH_AXES)

    small_grads = {k: [None] * n_layers for k in _SMALL}
    received = {k: [None] * n_layers for k in big}
    big_late = ("w_ffn1_gu", "w_ffn1_down")
    big_early = tuple(k for k in big if k not in big_late)
    early_in_flight, late_in_flight, last_in_flight = [None] * n_layers, [None] * n_layers, [None] * n_layers
    token = None
    for l in reversed(range(n_layers)):
        lw, s = gathered[l], saved[l]
        d_ff = lw["down1"].shape[0]

        def dw_down_of(dyh, act, tag):
            return _matmul_tn(act[None], dyh[None], name=f"{tag}_dw_down_l{l}", a_chunk=d_ff // 2).reshape(N_DEV, -1, d)

        def dw_gu_of(dgu, h, tag, after=None):
            return _matmul_tn(dgu, h[None], name=f"{tag}_dw_gu_l{l}", a_chunk=d_ff // 2, after=after).reshape(N_DEV, -1, d)

        dyh, dgu, dx2, dln2 = _ffn_bwd(dy, s["x2"], lw["ln2"], s["gu2"], lw["down2"], lw["gu2"],
                                       name=f"ffn2_bwd_l{l}", after=token)
        dw_down2, dw_gu2 = dw_down_of(dyh, s["act2"], "ffn2"), dw_gu_of(dgu, s["h3"], "ffn2")

        dyb, da, db, dp, do, dzg = _merge_bwd(dx2, s["p"], s["o"], s["zg"], lw["w_out"], lw["w_pb"], lw["w_ab"],
                                              name=f"merge_bwd_l{l}")
        dw_out = _matmul_tn(s["mix"][None], dyb[None], name=f"dw_out_l{l}")[0]
        dw_pb = _matmul_tn(s["p"][None], da[None], name=f"dw_pb_l{l}")[0]
        dw_ab = _matmul_tn(s["o"][None], db[None], name=f"dw_ab_l{l}")[0]
        dzp, dpw, dsc = _pool_bwd(dp, s["d"], lw["pool_w"], lw["pool_scale"], name=f"pool_bwd_l{l}")
        dq, dk, dv, dsinks = _attn_bwd(do, s["qr"], s["kr"], s["vb"], lw["sinks"], name=f"attn_bwd_l{l}")
        dzqkv, dqg, dkg = _qk_bwd(dq, dk, dv, s["zq"], lw["qgain"], lw["kgain"], cos, sin, name=f"qk_bwd_l{l}")
        dw_in = jnp.concatenate([_matmul_tn(dzp[None], s["h2"][None], name=f"dw_in_pool_l{l}")[0],
                                 _matmul_tn(dzqkv[None], s["h2"][None], name=f"dw_in_qkv_l{l}")[0],
                                 _matmul_tn(dzg[None], s["h2"][None], name=f"dw_in_gate_l{l}")[0]], axis=0)
        dx1, dlnm = _mix_bwd_x(dx2, s["x1"], lw["ln_mix"], dzp, dzqkv, dzg, lw["w_in"], name=f"mix_bwd_x_l{l}")

        partial = dict(w_in=dw_in.reshape(N_DEV, -1, d), w_pool_branch=_to_shard_major_cols(dw_pb),
                       w_attn_branch=_to_shard_major_cols(dw_ab), w_out=dw_out.reshape(N_DEV, d // N_DEV, d),
                       w_ffn2_gu=dw_gu2, w_ffn2_down=dw_down2)
        early_in_flight[l], token = _exchange_start([partial[k] for k in big_early], scatter=True,
                                                    name=f"grads_early_start_l{l}")

        dyh, dgu, dy, dln1 = _ffn_bwd(dx1, s["x0"], lw["ln1"], s["gu1"], lw["down1"], lw["gu1"],
                                      name=f"ffn1_bwd_l{l}", after=token)
        late_in_flight[l], token = _exchange_start([dw_down_of(dyh, s["act1"], "ffn1")], scatter=True,
                                                   name=f"grads_late_start_l{l}")
        last_in_flight[l], token = _exchange_start([dw_gu_of(dgu, s["h1"], "ffn1", after=token)], scatter=True,
                                                   name=f"grads_last_start_l{l}")
        small_grads["ln_ffn1"][l] = dln1[0]
        small_grads["ln_mix"][l] = dlnm[0]
        small_grads["ln_ffn2"][l] = dln2[0]
        small_grads["pool_w"][l] = dpw
        small_grads["pool_scale"][l] = dsc[0]
        small_grads["q_norm"][l] = dqg[0]
        small_grads["k_norm"][l] = dkg[0]
        small_grads["sinks"][l] = dsinks[0, :N_Q_HEADS]

    grad_x = dy.reshape(x.shape)

    small_w = [weights[k] for k in _SMALL]
    packed_g = _pack_small([jnp.stack(small_grads[k]).reshape(weights[k].shape) for k in _SMALL])
    small_in_flight, after = _exchange_start([packed_g], scatter=False, name="small_grads_start", after=token)
    for l in reversed(range(n_layers)):
        got = _exchange_wait(early_in_flight[l], after, name=f"grads_early_wait_l{l}")
        after = got[0]
        for k, r in zip(big_early, got):
            received[k][l] = r

    grads, deltas, new_m, new_v = {}, {}, {}, {}

    def adamw(k, after):
        w = weights[k]
        shape2 = (n_layers, -1, w.shape[-1])
        parts = [r.reshape(N_DEV, -1, w.shape[-1]) for r in received[k]]
        outs = _adamw_sharded(parts, w.reshape(shape2), mom_m[k].reshape(shape2), mom_v[k].reshape(shape2),
                              name=f"adamw_{k}", after=after)
        grads[k], deltas[k], new_m[k], new_v[k] = (o.reshape(w.shape) for o in outs)
        return outs[0]

    after = None
    for k in big_early:
        after = adamw(k, after)
    for l in reversed(range(n_layers)):
        (received["w_ffn1_down"][l],) = _exchange_wait(late_in_flight[l], after, name=f"grads_late_wait_l{l}")
        (received["w_ffn1_gu"][l],) = _exchange_wait(last_in_flight[l], received["w_ffn1_down"][l],
                                                     name=f"grads_last_wait_l{l}")
        after = received["w_ffn1_gu"][l]
    after = None
    for k in big_late:
        after = adamw(k, after)

    (parts_small,) = _exchange_wait(small_in_flight, after, name="small_grads_wait")
    outs = _adamw_packed(parts_small, _pack_small(small_w), _pack_small([mom_m[k] for k in _SMALL]),
                         _pack_small([mom_v[k] for k in _SMALL]), name="adamw_small")
    for res, o in zip((grads, deltas, new_m, new_v), outs):
        for k, a in zip(_SMALL, _unpack_small(o, small_w)):
            res[k] = a

    for res in (grads, deltas, new_m, new_v):
        for k in transposed:
            res[k] = jnp.swapaxes(res[k], 1, 2)
    return (loss, grad_x, *[grads[k] for k in order], *[deltas[k] for k in order],
            *[new_m[k] for k in order], *[new_v[k] for k in order])
```

```python
import jax
import jax.numpy as jnp
from jax import lax
from jax.experimental import pallas as pl
from jax.experimental.pallas import tpu as pltpu

F32 = jnp.float32
BF16 = jnp.bfloat16

N_DEV = 8
MESH_AXES = ("x", "y", "c")
EPS = 1e-6

HEAD_DIM = 64
N_Q_HEADS = 8
N_KV_HEADS = 2
GQA_GROUP = N_Q_HEADS // N_KV_HEADS
ATTN_BLOCK = 128
ATTN_SCALE = HEAD_DIM ** -0.5
ROPE_THETA = 500000.0
ROT_DIM = 16
POOL_WINDOWS = (2, 4, 8, 16)
POOL_HALO = 16
GROUP_DIM = 128
POOL_DIM = 512
ATTN_DIM = 512
KV_DIM = 128
QKV_END = POOL_DIM + ATTN_DIM + 2 * KV_DIM

ADAM_LR = 0.001
ADAM_B1 = 0.9
ADAM_B2 = 0.999
ADAM_EPS = 1e-08
ADAM_WD = 0.01
ADAM_STEP = 10

ROW_TILE = 512
TN_ROW_TILE = 2048
TN_NARROW_ROWS = 768
FFN_CHUNK = 256
FFN_BWD_ROW_TILE = 256
VMEM_LIMIT_BYTES = 56 << 20
ADAMW_BLOCK_ELEMS = 192 * 1024
NEG_BIG = -1e30

_NT = (((1,), (1,)), ((), ()))
_TN = (((0,), (0,)), ((), ()))


def _cp(*sem):
    return pltpu.CompilerParams(dimension_semantics=sem, vmem_limit_bytes=VMEM_LIMIT_BYTES)


def _resident(block, index):
    return pl.BlockSpec(block, lambda *_: index, pipeline_mode=pl.Buffered(1))


def _row_tile(m):
    return min(ROW_TILE, m)


def _sds(shape, dtype):
    return jax.ShapeDtypeStruct(shape, dtype)


def _mesh_pos():
    return lax.axis_index("x"), lax.axis_index("y"), lax.axis_index("c")


def _all_gather_many(shards, name):
    n = len(shards)

    def body(*refs):
        ins, outs = refs[:n], refs[n:2 * n]
        send_sems, recv_sems, local_sems = refs[2 * n:]
        x, y, c = _mesh_pos()
        me, sibling = (x, y, c), (x, y, 1 - c)
        chips = [(1 - x, y), (x, 1 - y), (1 - x, 1 - y)]

        def slot(a, pos):
            return outs[a].at[4 * pos[0] + 2 * pos[1] + pos[2]]

        def copy(a, k, block, to, src=None):
            return pltpu.make_async_remote_copy(
                src_ref=slot(a, block) if src is None else src, dst_ref=slot(a, block),
                send_sem=send_sems.at[a, k], recv_sem=recv_sems.at[a, k],
                device_id=to, device_id_type=pl.DeviceIdType.MESH)

        mine = [pltpu.make_async_copy(ins[a], slot(a, me), local_sems.at[a]) for a in range(n)]
        for cp in mine:
            cp.start()
        first = []
        for a in range(n):
            first.append(copy(a, 0, me, sibling, src=ins[a]))
            for j, chip in enumerate(chips):
                first.append(copy(a, 1 + j, me, (*chip, c), src=ins[a]))
        for cp in first:
            cp.start()
        passed = []
        for j, chip in enumerate(chips):
            for a in range(n):
                copy(a, 1 + j, (*chip, c), me).wait_recv()
                fwd = copy(a, 4 + j, (*chip, c), sibling)
                fwd.start()
                passed.append(fwd)
        for a in range(n):
            copy(a, 0, sibling, me).wait_recv()
        for j, chip in enumerate(chips):
            for a in range(n):
                copy(a, 4 + j, (*chip, 1 - c), me).wait_recv()
        for cp in first + passed:
            cp.wait_send()
        for cp in mine:
            cp.wait()

    any_spec = pl.BlockSpec(memory_space=pl.ANY)
    return pl.pallas_call(
        body, name=name,
        out_shape=[_sds((N_DEV,) + s.shape, s.dtype) for s in shards],
        in_specs=[any_spec] * n, out_specs=[any_spec] * n,
        scratch_shapes=[pltpu.SemaphoreType.DMA((n, 7)), pltpu.SemaphoreType.DMA((n, 7)),
                        pltpu.SemaphoreType.DMA((n,))],
    )(*shards)


_ALL_PEERS = tuple(range(1, N_DEV))
_CHIP_PEERS = (1, 2, 4, 6)


def _direct_copies(src, land, send_sem, recv_sem, local_sem, scatter, peers=_ALL_PEERS):
    x, y, c = _mesh_pos()
    me = 4 * x + 2 * y + c
    local = pltpu.make_async_copy(src.at[me] if scatter else src, land.at[me], local_sem)
    remote = []
    for k in peers:
        px = 1 - x if k & 4 else x
        py = 1 - y if k & 2 else y
        pc = 1 - c if k & 1 else c
        remote.append(pltpu.make_async_remote_copy(
            src_ref=src.at[4 * px + 2 * py + pc] if scatter else src, dst_ref=land.at[me],
            send_sem=send_sem, recv_sem=recv_sem, device_id=(px, py, pc), device_id_type=pl.DeviceIdType.MESH))
    every = land.at[pl.ds(0, len(peers))]
    drain = pltpu.make_async_remote_copy(src_ref=every, dst_ref=every, send_sem=send_sem, recv_sem=recv_sem,
                                         device_id=(x, y, c), device_id_type=pl.DeviceIdType.MESH)
    return local, remote, drain


_HBM_SPEC = pl.BlockSpec(memory_space=pltpu.HBM)
_SEM_SPEC = pl.BlockSpec(memory_space=pltpu.SEMAPHORE)
_DATAFLOW = pltpu.SideEffectType.DATAFLOW_SIDE_EFFECTING
_SEMS_PER_ARRAY = 3


def _exchange_start(srcs, scatter, name, peers=_ALL_PEERS, after=None):
    n = len(srcs)
    deps = [] if after is None else [after]
    n_sems = _SEMS_PER_ARRAY * n
    land_shapes = [s.shape if scatter else (N_DEV,) + s.shape for s in srcs]

    def body(*refs):
        ins, lands = refs[:n], refs[n:2 * n]
        sems = refs[2 * n + len(deps):2 * n + len(deps) + n_sems]
        for a in range(n):
            local, remote, _ = _direct_copies(ins[a], lands[a], *sems[3 * a:3 * a + 3], scatter, peers)
            local.start()
            for cp in remote:
                cp.start()
        refs[-1][...] = jnp.zeros_like(refs[-1])

    outs = pl.pallas_call(
        body, name=name,
        out_shape=(*[pltpu.SemaphoreType.DMA(())] * n_sems,
                   *[pltpu.HBM(s.shape, s.dtype) for s in srcs],
                   *[pltpu.HBM(shape, s.dtype) for shape, s in zip(land_shapes, srcs)],
                   _sds((8, 128), F32)),
        in_specs=[_HBM_SPEC] * (2 * n) + [pl.BlockSpec(memory_space=pl.ANY)] * len(deps),
        out_specs=(*[_SEM_SPEC] * n_sems, *[_HBM_SPEC] * (2 * n), pl.BlockSpec(memory_space=pltpu.VMEM)),
        input_output_aliases={i: n_sems + i for i in range(2 * n)},
        compiler_params=pltpu.CompilerParams(has_side_effects=_DATAFLOW),
    )(*[pltpu.with_memory_space_constraint(s, pltpu.HBM) for s in srcs],
      *[pltpu.with_memory_space_constraint(lax.empty(shape, s.dtype), pltpu.HBM) for shape, s in zip(land_shapes, srcs)],
      *deps)
    return (outs[:n_sems], outs[n_sems:n_sems + n], outs[n_sems + n:n_sems + 2 * n], scatter, peers), outs[-1]


def _exchange_wait(state, after, name):
    sems, srcs, lands, scatter, peers = state
    n = len(srcs)
    n_sems = len(sems)

    def body(*refs):
        ins, zones, ss = refs[:n], refs[n:2 * n], refs[2 * n:2 * n + n_sems]
        for a in range(n):
            local, _, drain = _direct_copies(ins[a], zones[a], *ss[3 * a:3 * a + 3], scatter, peers)
            drain.wait_send()
            drain.wait_recv()
            local.wait()

    outs = pl.pallas_call(
        body, name=name,
        out_shape=(*[pltpu.HBM(s.shape, s.dtype) for s in srcs], *[pltpu.HBM(z.shape, z.dtype) for z in lands]),
        in_specs=[_HBM_SPEC] * (2 * n) + [_SEM_SPEC] * n_sems + [pl.BlockSpec(memory_space=pl.ANY)],
        out_specs=[_HBM_SPEC] * (2 * n),
        input_output_aliases={i: i for i in range(2 * n)},
        compiler_params=pltpu.CompilerParams(has_side_effects=_DATAFLOW),
    )(*srcs, *lands, *sems, after)
    return outs[n:]


def _forward_to_sibling(lands, name):
    n = len(lands)

    def body(*refs):
        zones = refs[n:2 * n]
        send_sems, recv_sems = refs[2 * n:]
        x, y, c = _mesh_pos()
        chips = [(1 - x, y), (x, 1 - y), (1 - x, 1 - y)]
        sends, recvs = [], []
        for a in range(n):
            for j, (px, py) in enumerate(chips):
                mine = zones[a].at[4 * px + 2 * py + c]
                theirs = zones[a].at[4 * px + 2 * py + 1 - c]
                sends.append(pltpu.make_async_remote_copy(
                    src_ref=mine, dst_ref=mine, send_sem=send_sems.at[a, j], recv_sem=recv_sems.at[a, j],
                    device_id=(x, y, 1 - c), device_id_type=pl.DeviceIdType.MESH))
                recvs.append(pltpu.make_async_remote_copy(
                    src_ref=theirs, dst_ref=theirs, send_sem=send_sems.at[a, j], recv_sem=recv_sems.at[a, j],
                    device_id=(x, y, 1 - c), device_id_type=pl.DeviceIdType.MESH))
        for cp in sends:
            cp.start()
        for cp in recvs:
            cp.wait_recv()
        for cp in sends:
            cp.wait_send()

    any_spec = pl.BlockSpec(memory_space=pl.ANY)
    return pl.pallas_call(
        body, name=name,
        out_shape=[_sds(z.shape, z.dtype) for z in lands],
        in_specs=[any_spec] * n, out_specs=[any_spec] * n,
        input_output_aliases={a: a for a in range(n)},
        scratch_shapes=[pltpu.SemaphoreType.DMA((n, 3)), pltpu.SemaphoreType.DMA((n, 3))],
    )(*lands)


def _rms_fwd(xv, gain):
    r = lax.rsqrt(jnp.mean(xv * xv, axis=-1, keepdims=True) + EPS)
    return xv * r * gain


def _rms_bwd(dh, xv, gain):
    r = lax.rsqrt(jnp.mean(xv * xv, axis=-1, keepdims=True) + EPS)
    xn = xv * r
    dxn = dh * gain
    dx = r * (dxn - xn * jnp.mean(dxn * xn, axis=-1, keepdims=True))
    return dx, dh * xn


def _sigmoid(v):
    return 0.5 * jnp.tanh(0.5 * v) + 0.5


def _silu_parts(g):
    s = _sigmoid(g)
    return g * s, s * (1.0 + g * (1.0 - s))


def _segment_mean(v, width):
    r = lax.broadcasted_iota(jnp.int32, (width, width), 0) >> 6
    c = lax.broadcasted_iota(jnp.int32, (width, width), 1) >> 6
    bd = (r == c).astype(BF16)
    hi = v.astype(BF16)
    lo = (v - hi.astype(F32)).astype(BF16)
    total = jnp.dot(hi, bd, preferred_element_type=F32) + jnp.dot(lo, bd, preferred_element_type=F32)
    return total * (1.0 / HEAD_DIM)


def _rope_partner(v):
    width = v.shape[1]
    half = ROT_DIM // 2
    lane = lax.broadcasted_iota(jnp.int32, v.shape, 1) & (HEAD_DIM - 1)
    up = jnp.where(lane < ROT_DIM, pltpu.roll(v, half, 1), 0.0)
    return jnp.where(lane < half, pltpu.roll(v, width - half, 1), up)


def _tile_lanes(t, width):
    return t if width == t.shape[1] else jnp.tile(t, (1, width // t.shape[1]))


def _ffn_chunks(f):
    return [slice(j * FFN_CHUNK, (j + 1) * FFN_CHUNK) for j in range(f // FFN_CHUNK)]


def _ffn_up(x, ln, wgu, name, after=None):
    m, d = x.shape
    f = wgu.shape[1]
    tm = _row_tile(m)
    deps = [] if after is None else [after]

    def body(*refs):
        x_ref, ln_ref, w_ref = refs[:3]
        h_ref, gu_ref, a_ref = refs[-3:]
        h = _rms_fwd(x_ref[...], ln_ref[...]).astype(BF16)
        h_ref[...] = h
        for cols in _ffn_chunks(f):
            g = lax.dot_general(h, w_ref[0, cols, :], _NT, preferred_element_type=F32)
            u = lax.dot_general(h, w_ref[1, cols, :], _NT, preferred_element_type=F32)
            gu_ref[0, :, cols] = g.astype(BF16)
            gu_ref[1, :, cols] = u.astype(BF16)
            a_ref[:, cols] = (g * _sigmoid(g) * u).astype(BF16)

    return pl.pallas_call(
        body, name=name, grid=(m // tm,),
        in_specs=[pl.BlockSpec((tm, d), lambda i: (i, 0)), _resident((1, d), (0, 0)),
                  _resident((2, f, d), (0, 0, 0))] + [pl.BlockSpec(memory_space=pl.ANY)] * len(deps),
        out_specs=[pl.BlockSpec((tm, d), lambda i: (i, 0)), pl.BlockSpec((2, tm, f), lambda i: (0, i, 0)),
                   pl.BlockSpec((tm, f), lambda i: (i, 0))],
        out_shape=[_sds((m, d), BF16), _sds((2, m, f), BF16), _sds((m, f), BF16)],
        compiler_params=_cp("parallel"),
    )(x, ln, wgu, *deps)


def _ffn_down(x, act, wd, name):
    m, d = x.shape
    f = act.shape[-1]
    tm = _row_tile(m)

    def body(x_ref, a_ref, w_ref, o_ref):
        o_ref[...] = x_ref[...] + 0.5 * jnp.dot(a_ref[...], w_ref[...], preferred_element_type=F32)

    return pl.pallas_call(
        body, name=name, grid=(m // tm,),
        in_specs=[pl.BlockSpec((tm, d), lambda i: (i, 0)), pl.BlockSpec((tm, f), lambda i: (i, 0)),
                  _resident((f, d), (0, 0))],
        out_specs=pl.BlockSpec((tm, d), lambda i: (i, 0)),
        out_shape=_sds((m, d), F32),
        compiler_params=_cp("parallel"),
    )(x, act, wd)


def _ffn_bwd(dy, x, ln, gu, wd, wgu, name, after=None):
    m, d = dy.shape
    f = gu.shape[-1]
    tm = min(FFN_BWD_ROW_TILE, m)
    deps = [] if after is None else [after]

    def body(*refs):
        dy_ref, x_ref, ln_ref, gu_ref, wd_ref, wgu_ref = refs[:6]
        dyh_ref, dgu_ref, dx_ref, dln_ref = refs[-4:]

        @pl.when(pl.program_id(0) == 0)
        def _():
            dln_ref[...] = jnp.zeros_like(dln_ref)

        dyh = (0.5 * dy_ref[...]).astype(BF16)
        dyh_ref[...] = dyh
        dgs, dus = [], []
        for cols in _ffn_chunks(f):
            da = lax.dot_general(dyh, wd_ref[cols, :], _NT, preferred_element_type=F32)
            g = gu_ref[0, :, cols].astype(F32)
            u = gu_ref[1, :, cols].astype(F32)
            silu, dsilu = _silu_parts(g)
            dgs.append((da * u * dsilu).astype(BF16))
            dus.append((da * silu).astype(BF16))
            dgu_ref[0, :, cols] = dgs[-1]
            dgu_ref[1, :, cols] = dus[-1]
        dh = jnp.dot(jnp.concatenate(dgs, axis=1), wgu_ref[0], preferred_element_type=F32)
        dh += jnp.dot(jnp.concatenate(dus, axis=1), wgu_ref[1], preferred_element_type=F32)
        dx, dgain = _rms_bwd(dh, x_ref[...], ln_ref[...])
        dx_ref[...] = dy_ref[...] + dx
        dln_ref[...] += jnp.sum(dgain, axis=0, keepdims=True)

    row = lambda i: (i, 0)
    return pl.pallas_call(
        body, name=name, grid=(m // tm,),
        in_specs=[pl.BlockSpec((tm, d), row), pl.BlockSpec((tm, d), row), _resident((1, d), (0, 0)),
                  pl.BlockSpec((2, tm, f), lambda i: (0, i, 0)), _resident((f, d), (0, 0)),
                  _resident((2, f, d), (0, 0, 0))] + [pl.BlockSpec(memory_space=pl.ANY)] * len(deps),
        out_specs=[pl.BlockSpec((tm, d), row), pl.BlockSpec((2, tm, f), lambda i: (0, i, 0)),
                   pl.BlockSpec((tm, d), row), pl.BlockSpec((1, d), lambda i: (0, 0))],
        out_shape=[_sds((m, d), BF16), _sds((2, m, f), BF16), _sds((m, d), F32), _sds((1, d), F32)],
        compiler_params=_cp("arbitrary"),
    )(dy, x, ln, gu, wd, wgu, *deps)


def _matmul_tn(a, b, name, a_chunk=None, out_dtype=BF16, after=None):
    ja, m, k = a.shape
    jb, _, n = b.shape
    nj = max(ja, jb)
    kc = k if a_chunk is None else a_chunk
    tm = min(TN_ROW_TILE * (2 if kc <= TN_NARROW_ROWS else 1), m)
    nm = m // tm
    deps = [] if after is None else [after]

    def body(*refs):
        a_ref, b_ref = refs[:2]
        o_ref, acc = refs[-2:]
        step = pl.program_id(2)

        @pl.when(step == 0)
        def _():
            acc[...] = jnp.zeros_like(acc)

        acc[...] += lax.dot_general(a_ref[...], b_ref[...], _TN, preferred_element_type=F32)

        @pl.when(step == nm - 1)
        def _():
            o_ref[...] = acc[...].astype(o_ref.dtype)

    return pl.pallas_call(
        body, name=name, grid=(nj, k // kc, nm),
        in_specs=[pl.BlockSpec((None, tm, kc), (lambda j, c, s: (j, s, c)) if ja > 1 else (lambda j, c, s: (0, s, c))),
                  pl.BlockSpec((None, tm, n), (lambda j, c, s: (j, s, 0)) if jb > 1 else (lambda j, c, s: (0, s, 0)))]
        + [pl.BlockSpec(memory_space=pl.ANY)] * len(deps),
        out_specs=pl.BlockSpec((None, kc, n), lambda j, c, s: (j, c, 0)),
        out_shape=_sds((nj, k, n), out_dtype),
        scratch_shapes=[pltpu.VMEM((kc, n), F32)],
        compiler_params=_cp("parallel", "parallel", "arbitrary"),
    )(a, b, *deps)


def _mix_in(x, ln, w_in, name):
    m, d = x.shape
    n_in = w_in.shape[0]
    tm = _row_tile(m)

    def body(x_ref, ln_ref, w_ref, h_ref, zq_ref, zg_ref):
        h = _rms_fwd(x_ref[...], ln_ref[...]).astype(BF16)
        h_ref[...] = h
        zq_ref[...] = lax.dot_general(h, w_ref[:QKV_END, :], _NT, preferred_element_type=F32)
        zg_ref[...] = lax.dot_general(h, w_ref[QKV_END:, :], _NT, preferred_element_type=F32).astype(BF16)

    return pl.pallas_call(
        body, name=name, grid=(m // tm,),
        in_specs=[pl.BlockSpec((tm, d), lambda i: (i, 0)), _resident((1, d), (0, 0)), _resident((n_in, d), (0, 0))],
        out_specs=[pl.BlockSpec((tm, d), lambda i: (i, 0)), pl.BlockSpec((tm, QKV_END), lambda i: (i, 0)),
                   pl.BlockSpec((tm, n_in - QKV_END), lambda i: (i, 0))],
        out_shape=[_sds((m, d), BF16), _sds((m, QKV_END), F32), _sds((m, n_in - QKV_END), BF16)],
        compiler_params=_cp("parallel"),
    )(x, ln, w_in)


def _pool_fwd(zq, pool_w, scale, name):
    m = zq.shape[0]
    tm = _row_tile(m)
    halo_blocks = tm // POOL_HALO

    def body(zc_ref, zh_ref, pw_ref, sc_ref, d_ref, p_ref):
        i = pl.program_id(0)
        halo = jnp.where(i > 0, zh_ref[...], 0.0)
        ext = jnp.concatenate([halo, zc_ref[...]], axis=0)
        t = i * tm + lax.broadcasted_iota(jnp.int32, (tm, 1), 0)
        for g, w in enumerate(POOL_WINDOWS):
            lanes = slice(g * GROUP_DIM, (g + 1) * GROUP_DIM)
            e = ext[:, lanes]
            s, k = e, 1
            while k < w:
                s = s + pltpu.roll(s, k, 0)
                k *= 2
            cnt = jnp.minimum(t + 1, w).astype(F32)
            dg = (s[POOL_HALO:, :] / cnt - e[POOL_HALO:, :]).astype(BF16)
            y = jnp.dot(dg, pw_ref[g].astype(BF16), preferred_element_type=F32)
            d_ref[:, lanes] = dg
            p_ref[:, lanes] = (y * sc_ref[:, lanes]).astype(BF16)

    return pl.pallas_call(
        body, name=name, grid=(m // tm,),
        in_specs=[pl.BlockSpec((tm, POOL_DIM), lambda i: (i, 0)),
                  pl.BlockSpec((POOL_HALO, POOL_DIM), lambda i: (jnp.maximum(i * halo_blocks - 1, 0), 0)),
                  _resident((4, GROUP_DIM, GROUP_DIM), (0, 0, 0)), _resident((1, POOL_DIM), (0, 0))],
        out_specs=[pl.BlockSpec((tm, POOL_DIM), lambda i: (i, 0)), pl.BlockSpec((tm, POOL_DIM), lambda i: (i, 0))],
        out_shape=[_sds((m, POOL_DIM), BF16), _sds((m, POOL_DIM), BF16)],
        compiler_params=_cp("parallel"),
    )(zq, zq, pool_w, scale)


def _pool_bwd(dp, d, pool_w, scale, name):
    m = dp.shape[0]
    tm = _row_tile(m)
    nb = m // tm
    halo_blocks = tm // POOL_HALO
    rows = tm + POOL_HALO

    def body(dpc_ref, dph_ref, d_ref, pw_ref, sc_ref, du_ref, dpw_ref, dsc_ref):
        i = pl.program_id(0)

        @pl.when(i == 0)
        def _():
            dpw_ref[...] = jnp.zeros_like(dpw_ref)
            dsc_ref[...] = jnp.zeros_like(dsc_ref)

        halo = jnp.where(i < nb - 1, dph_ref[...], 0.0)
        dpc = dpc_ref[...]
        ext = jnp.concatenate([dpc, halo], axis=0)
        t = i * tm + lax.broadcasted_iota(jnp.int32, (rows, 1), 0)
        for g, w in enumerate(POOL_WINDOWS):
            lanes = slice(g * GROUP_DIM, (g + 1) * GROUP_DIM)
            pwb = pw_ref[g].astype(BF16)
            dyb = (ext[:, lanes] * sc_ref[:, lanes]).astype(BF16)
            dd = lax.dot_general(dyb, pwb, _NT, preferred_element_type=F32)
            cnt = jnp.minimum(t + 1, w).astype(F32)
            s, k = dd / cnt, 1
            while k < w:
                s = s + pltpu.roll(s, rows - k, 0)
                k *= 2
            du_ref[:, lanes] = (s[:tm, :] - dd[:tm, :]).astype(BF16)
            dcur = d_ref[:, lanes]
            y = jnp.dot(dcur, pwb, preferred_element_type=F32)
            dsc_ref[:, lanes] += jnp.sum(dpc[:, lanes] * y, axis=0, keepdims=True)
            dpw_ref[g] += lax.dot_general(dcur, dyb[:tm, :], _TN, preferred_element_type=F32)

    return pl.pallas_call(
        body, name=name, grid=(nb,),
        in_specs=[pl.BlockSpec((tm, POOL_DIM), lambda i: (i, 0)),
                  pl.BlockSpec((POOL_HALO, POOL_DIM), lambda i: (jnp.minimum((i + 1) * halo_blocks, nb * halo_blocks - 1), 0)),
                  pl.BlockSpec((tm, POOL_DIM), lambda i: (i, 0)),
                  _resident((4, GROUP_DIM, GROUP_DIM), (0, 0, 0)), _resident((1, POOL_DIM), (0, 0))],
        out_specs=[pl.BlockSpec((tm, POOL_DIM), lambda i: (i, 0)),
                   pl.BlockSpec((4, GROUP_DIM, GROUP_DIM), lambda i: (0, 0, 0)),
                   pl.BlockSpec((1, POOL_DIM), lambda i: (0, 0))],
        out_shape=[_sds((m, POOL_DIM), BF16), _sds((4, GROUP_DIM, GROUP_DIM), F32), _sds((1, POOL_DIM), F32)],
        compiler_params=_cp("arbitrary"),
    )(dp, dp, d, pool_w, scale)


def _qk_norm_rope(xv, gain, cos, sin):
    width = xv.shape[1]
    r = lax.rsqrt(_segment_mean(xv * xv, width) + EPS)
    y = xv * r * gain
    return y * _tile_lanes(cos, width) + _rope_partner(y) * _tile_lanes(sin, width)


def _qk_prep(zq, qgain, kgain, cos, sin, name):
    m = zq.shape[0]
    tm = _row_tile(m)

    def body(q_ref, kv_ref, qg_ref, kg_ref, cos_ref, sin_ref, qr_ref, kr_ref, v_ref):
        cos_v, sin_v = cos_ref[...], sin_ref[...]
        qr_ref[...] = (_qk_norm_rope(q_ref[...], qg_ref[...], cos_v, sin_v) * ATTN_SCALE).astype(BF16)
        kv = kv_ref[...]
        kr_ref[...] = _qk_norm_rope(kv[:, :KV_DIM], kg_ref[...], cos_v, sin_v).astype(BF16)
        v_ref[...] = kv[:, KV_DIM:].astype(BF16)

    return pl.pallas_call(
        body, name=name, grid=(m // tm,),
        in_specs=[pl.BlockSpec((tm, ATTN_DIM), lambda i: (i, 1)), pl.BlockSpec((tm, 2 * KV_DIM), lambda i: (i, 4)),
                  _resident((1, ATTN_DIM), (0, 0)), _resident((1, KV_DIM), (0, 0)),
                  pl.BlockSpec((tm, 128), lambda i: (i, 0)), pl.BlockSpec((tm, 128), lambda i: (i, 0))],
        out_specs=[pl.BlockSpec((tm, ATTN_DIM), lambda i: (i, 0)), pl.BlockSpec((tm, KV_DIM), lambda i: (i, 0)),
                   pl.BlockSpec((tm, KV_DIM), lambda i: (i, 0))],
        out_shape=[_sds((m, ATTN_DIM), BF16), _sds((m, KV_DIM), BF16), _sds((m, KV_DIM), BF16)],
        compiler_params=_cp("parallel"),
    )(zq, zq, qgain, kgain, cos, sin)


STACK_ROWS = N_Q_HEADS * ATTN_BLOCK


def _band_bias():
    qi = jnp.arange(STACK_ROWS)[:, None] % ATTN_BLOCK
    ki = jnp.arange(2 * ATTN_BLOCK)[None, :]
    diff = qi + ATTN_BLOCK - ki
    band = (diff >= 0) & (diff < ATTN_BLOCK)
    first = band & (ki >= ATTN_BLOCK)
    return jnp.where(jnp.stack([first, band]), 0.0, NEG_BIG).astype(F32)


def _band_tables():
    qi = jnp.arange(STACK_ROWS)[:, None] % ATTN_BLOCK
    use_prev = jnp.arange(ATTN_BLOCK)[None, :] > qi
    return jnp.stack([use_prev.astype(F32), jnp.where(use_prev, NEG_BIG, 0.0).astype(F32),
                      jnp.zeros((STACK_ROWS, ATTN_BLOCK), F32)])


def _band_merge(use_prev, from_prev, from_cur):
    return jnp.where(use_prev, from_prev, from_cur)


def _band_split(use_prev, merged):
    return jnp.where(use_prev, merged, 0.0).astype(BF16), jnp.where(use_prev, 0.0, merged).astype(BF16)


def _stack_heads(v):
    zeros = jnp.zeros((ATTN_BLOCK, HEAD_DIM), v.dtype)
    rows = []
    for h in range(N_Q_HEADS):
        qh = v[:, h * HEAD_DIM:(h + 1) * HEAD_DIM]
        rows.append(jnp.concatenate([qh, zeros] if h < GQA_GROUP else [zeros, qh], axis=1))
    return jnp.concatenate(rows, axis=0)


def _unstack_heads(stacked):
    parts = []
    for h in range(N_Q_HEADS):
        lanes = slice(0, HEAD_DIM) if h < GQA_GROUP else slice(HEAD_DIM, 2 * HEAD_DIM)
        parts.append(stacked[h * ATTN_BLOCK:(h + 1) * ATTN_BLOCK, lanes])
    return jnp.concatenate(parts, axis=1)


def _stacked_sinks(sk_ref):
    row_head = lax.broadcasted_iota(jnp.int32, (STACK_ROWS, 1), 0) >> 7
    col = jnp.full((STACK_ROWS, 1), sk_ref[0], F32)
    for h in range(1, N_Q_HEADS):
        col = jnp.where(row_head == h, sk_ref[h], col)
    return col


def _softmax_with_sink(s, sink):
    mx = jnp.maximum(jnp.max(s, axis=-1, keepdims=True), sink)
    p = jnp.exp(s - mx)
    es = jnp.exp(sink - mx)
    inv = 1.0 / (jnp.sum(p, axis=-1, keepdims=True) + es)
    return p * inv, es * inv


def _merged_logits(qh, kp, kc, use_prev, bias):
    return _band_merge(use_prev, lax.dot_general(qh, kp, _NT, preferred_element_type=F32),
                       lax.dot_general(qh, kc, _NT, preferred_element_type=F32)) + bias


def _attn_fwd(qr, kr, vb, sinks, name):
    m = qr.shape[0]
    nb = m // ATTN_BLOCK

    def body(q_ref, kp_ref, kc_ref, vp_ref, vc_ref, sk_ref, bias_ref, o_ref):
        kk = jnp.concatenate([kp_ref[...], kc_ref[...]], axis=0)
        vv = jnp.concatenate([vp_ref[...], vc_ref[...]], axis=0)
        s = lax.dot_general(_stack_heads(q_ref[...]), kk, _NT, preferred_element_type=F32)
        p, _ = _softmax_with_sink(s + bias_ref[jnp.minimum(pl.program_id(0), 1)], _stacked_sinks(sk_ref))
        o_ref[...] = _unstack_heads(jnp.dot(p.astype(BF16), vv, preferred_element_type=F32)).astype(BF16)

    prev = lambda n: (jnp.maximum(n - 1, 0), 0)
    cur = lambda n: (n, 0)
    return pl.pallas_call(
        body, name=name, grid=(nb,),
        in_specs=[pl.BlockSpec((ATTN_BLOCK, ATTN_DIM), cur),
                  pl.BlockSpec((ATTN_BLOCK, KV_DIM), prev), pl.BlockSpec((ATTN_BLOCK, KV_DIM), cur),
                  pl.BlockSpec((ATTN_BLOCK, KV_DIM), prev), pl.BlockSpec((ATTN_BLOCK, KV_DIM), cur),
                  pl.BlockSpec(memory_space=pltpu.SMEM), _resident((2, STACK_ROWS, 2 * ATTN_BLOCK), (0, 0, 0))],
        out_specs=pl.BlockSpec((ATTN_BLOCK, ATTN_DIM), cur),
        out_shape=_sds((m, ATTN_DIM), BF16),
        compiler_params=_cp("parallel"),
    )(qr, kr, kr, vb, vb, sinks, _band_bias())


def _attn_bwd(do, qr, kr, vb, sinks, name):
    m = qr.shape[0]
    nb = m // ATTN_BLOCK

    def body(do_ref, q_ref, kp_ref, kc_ref, vp_ref, vc_ref, sk_ref, tab_ref, dq_ref, dk_ref, dv_ref, ds_ref,
             carry_k, carry_v):
        n = pl.program_id(0)

        @pl.when(n == 0)
        def _():
            carry_k[...] = jnp.zeros_like(carry_k)
            carry_v[...] = jnp.zeros_like(carry_v)
            ds_ref[...] = jnp.zeros_like(ds_ref)

        @pl.when(n < nb)
        def _():
            use_prev = tab_ref[0] > 0.5
            bias = tab_ref[1 + jnp.minimum(n, 1)]
            kp, kc, vp, vc = kp_ref[...], kc_ref[...], vp_ref[...], vc_ref[...]
            lane = lax.broadcasted_iota(jnp.int32, (1, 128), 1)
            qs = _stack_heads(q_ref[...])
            dos = _stack_heads(do_ref[...].astype(BF16))
            p, ps = _softmax_with_sink(_merged_logits(qs, kp, kc, use_prev, bias), _stacked_sinks(sk_ref))
            dpr = _band_merge(use_prev, lax.dot_general(dos, vp, _NT, preferred_element_type=F32),
                              lax.dot_general(dos, vc, _NT, preferred_element_type=F32))
            delta = jnp.sum(p * dpr, axis=-1, keepdims=True)
            ds_prev, ds_cur = _band_split(use_prev, p * (dpr - delta))
            p_prev, p_cur = _band_split(use_prev, p)
            sink_term = ps * delta
            dsink = jnp.zeros((1, 128), F32)
            for h in range(N_Q_HEADS):
                rows = slice(h * ATTN_BLOCK, (h + 1) * ATTN_BLOCK)
                dsink = dsink + jnp.where(lane == h, -jnp.sum(sink_term[rows, :]), 0.0)
            dq = jnp.dot(ds_prev, kp, preferred_element_type=F32) + jnp.dot(ds_cur, kc, preferred_element_type=F32)
            dq_ref[...] = _unstack_heads(dq * ATTN_SCALE)
            dk_ref[...] = carry_k[...] + lax.dot_general(ds_prev, qs, _TN, preferred_element_type=F32)
            dv_ref[...] = carry_v[...] + lax.dot_general(p_prev, dos, _TN, preferred_element_type=F32)
            carry_k[...] = lax.dot_general(ds_cur, qs, _TN, preferred_element_type=F32)
            carry_v[...] = lax.dot_general(p_cur, dos, _TN, preferred_element_type=F32)
            ds_ref[...] += dsink

        @pl.when(n == nb)
        def _():
            dk_ref[...] = carry_k[...]
            dv_ref[...] = carry_v[...]

    cur = lambda n: (jnp.minimum(n, nb - 1), 0)
    prev = lambda n: (jnp.clip(n - 1, 0, nb - 1), 0)
    return pl.pallas_call(
        body, name=name, grid=(nb + 1,),
        in_specs=[pl.BlockSpec((ATTN_BLOCK, ATTN_DIM), cur), pl.BlockSpec((ATTN_BLOCK, ATTN_DIM), cur),
                  pl.BlockSpec((ATTN_BLOCK, KV_DIM), prev), pl.BlockSpec((ATTN_BLOCK, KV_DIM), cur),
                  pl.BlockSpec((ATTN_BLOCK, KV_DIM), prev), pl.BlockSpec((ATTN_BLOCK, KV_DIM), cur),
                  pl.BlockSpec(memory_space=pltpu.SMEM), _resident((3, STACK_ROWS, ATTN_BLOCK), (0, 0, 0))],
        out_specs=[pl.BlockSpec((ATTN_BLOCK, ATTN_DIM), cur), pl.BlockSpec((ATTN_BLOCK, KV_DIM), prev),
                   pl.BlockSpec((ATTN_BLOCK, KV_DIM), prev), pl.BlockSpec((1, 128), lambda n: (0, 0))],
        out_shape=[_sds((m, ATTN_DIM), F32), _sds((m, KV_DIM), F32), _sds((m, KV_DIM), F32), _sds((1, 128), F32)],
        scratch_shapes=[pltpu.VMEM((ATTN_BLOCK, KV_DIM), F32), pltpu.VMEM((ATTN_BLOCK, KV_DIM), F32)],
        compiler_params=_cp("arbitrary"),
    )(do, qr, kr, kr, vb, vb, sinks, _band_tables())


def _qk_norm_rope_bwd(dout, xv, gain, cos, sin):
    width = xv.shape[1]
    r = lax.rsqrt(_segment_mean(xv * xv, width) + EPS)
    xn = xv * r
    dy = dout * _tile_lanes(cos, width) + _rope_partner(dout * _tile_lanes(sin, width))
    dxn = dy * gain
    dx = r * (dxn - xn * _segment_mean(dxn * xn, width))
    return dx, jnp.sum(dy * xn, axis=0, keepdims=True)


def _fold_heads(v):
    out = v[:, :HEAD_DIM]
    for h in range(1, v.shape[1] // HEAD_DIM):
        out = out + v[:, h * HEAD_DIM:(h + 1) * HEAD_DIM]
    return out


def _qk_bwd(dq, dk, dv, zq, qgain, kgain, cos, sin, name):
    m = zq.shape[0]
    tm = _row_tile(m)

    def body(dq_ref, dk_ref, dv_ref, q_ref, kv_ref, qg_ref, kg_ref, cos_ref, sin_ref, dz_ref, dqg_ref, dkg_ref):
        @pl.when(pl.program_id(0) == 0)
        def _():
            dqg_ref[...] = jnp.zeros_like(dqg_ref)
            dkg_ref[...] = jnp.zeros_like(dkg_ref)

        cos_v, sin_v = cos_ref[...], sin_ref[...]
        dxq, dgq = _qk_norm_rope_bwd(dq_ref[...], q_ref[...], qg_ref[...], cos_v, sin_v)
        dxk, dgk = _qk_norm_rope_bwd(dk_ref[...], kv_ref[:, :KV_DIM], kg_ref[...], cos_v, sin_v)
        dz_ref[:, :ATTN_DIM] = dxq.astype(BF16)
        dz_ref[:, ATTN_DIM:ATTN_DIM + KV_DIM] = dxk.astype(BF16)
        dz_ref[:, ATTN_DIM + KV_DIM:] = dv_ref[...].astype(BF16)
        dqg_ref[...] += _fold_heads(dgq)
        dkg_ref[...] += _fold_heads(dgk)

    row = lambda i: (i, 0)
    return pl.pallas_call(
        body, name=name, grid=(m // tm,),
        in_specs=[pl.BlockSpec((tm, ATTN_DIM), row), pl.BlockSpec((tm, KV_DIM), row), pl.BlockSpec((tm, KV_DIM), row),
                  pl.BlockSpec((tm, ATTN_DIM), lambda i: (i, 1)), pl.BlockSpec((tm, 2 * KV_DIM), lambda i: (i, 4)),
                  _resident((1, ATTN_DIM), (0, 0)), _resident((1, KV_DIM), (0, 0)),
                  pl.BlockSpec((tm, 128), row), pl.BlockSpec((tm, 128), row)],
        out_specs=[pl.BlockSpec((tm, ATTN_DIM + 2 * KV_DIM), row), pl.BlockSpec((1, HEAD_DIM), lambda i: (0, 0)),
                   pl.BlockSpec((1, HEAD_DIM), lambda i: (0, 0))],
        out_shape=[_sds((m, ATTN_DIM + 2 * KV_DIM), BF16), _sds((1, HEAD_DIM), F32), _sds((1, HEAD_DIM), F32)],
        compiler_params=_cp("arbitrary"),
    )(dq, dk, dv, zq, zq, qgain, kgain, cos, sin)


def _merge_fwd(x, p, o, zg, w_pb, w_ab, w_out, name):
    m, d = x.shape
    tm = _row_tile(m)

    def body(x_ref, p_ref, o_ref, zg_ref, wpb_ref, wab_ref, wo_ref, xo_ref, mix_ref):
        a = jnp.dot(p_ref[...], wpb_ref[...], preferred_element_type=F32)
        b = jnp.dot(o_ref[...], wab_ref[...], preferred_element_type=F32)
        mix = (_sigmoid(zg_ref[:, :d].astype(F32)) * a + _sigmoid(zg_ref[:, d:].astype(F32)) * b).astype(BF16)
        mix_ref[...] = mix
        xo_ref[...] = x_ref[...] + jnp.dot(mix, wo_ref[...], preferred_element_type=F32)

    row = lambda i: (i, 0)
    return pl.pallas_call(
        body, name=name, grid=(m // tm,),
        in_specs=[pl.BlockSpec((tm, d), row), pl.BlockSpec((tm, POOL_DIM), row), pl.BlockSpec((tm, ATTN_DIM), row),
                  pl.BlockSpec((tm, 2 * d), row), _resident((POOL_DIM, d), (0, 0)), _resident((ATTN_DIM, d), (0, 0)),
                  _resident((d, d), (0, 0))],
        out_specs=[pl.BlockSpec((tm, d), row)] * 2,
        out_shape=[_sds((m, d), F32), _sds((m, d), BF16)],
        compiler_params=_cp("parallel"),
    )(x, p, o, zg, w_pb, w_ab, w_out)


def _merge_bwd(dy, p, o, zg, w_out, w_pb, w_ab, name):
    m, d = dy.shape
    tm = _row_tile(m)

    def body(dy_ref, p_ref, o_ref, zg_ref, wo_ref, wpb_ref, wab_ref, dyb_ref, da_ref, db_ref, dp_ref, do_ref, dzg_ref):
        dyb = dy_ref[...].astype(BF16)
        dyb_ref[...] = dyb
        dmix = lax.dot_general(dyb, wo_ref[...], _NT, preferred_element_type=F32)
        gp = _sigmoid(zg_ref[:, :d].astype(F32))
        ga = _sigmoid(zg_ref[:, d:].astype(F32))
        da = (dmix * gp).astype(BF16)
        db = (dmix * ga).astype(BF16)
        da_ref[...] = da
        db_ref[...] = db
        a = jnp.dot(p_ref[...], wpb_ref[...], preferred_element_type=F32)
        b = jnp.dot(o_ref[...], wab_ref[...], preferred_element_type=F32)
        dzg_ref[:, :d] = (dmix * a * gp * (1.0 - gp)).astype(BF16)
        dzg_ref[:, d:] = (dmix * b * ga * (1.0 - ga)).astype(BF16)
        dp_ref[...] = lax.dot_general(da, wpb_ref[...], _NT, preferred_element_type=F32)
        do_ref[...] = lax.dot_general(db, wab_ref[...], _NT, preferred_element_type=F32).astype(BF16)

    row = lambda i: (i, 0)
    return pl.pallas_call(
        body, name=name, grid=(m // tm,),
        in_specs=[pl.BlockSpec((tm, d), row), pl.BlockSpec((tm, POOL_DIM), row), pl.BlockSpec((tm, ATTN_DIM), row),
                  pl.BlockSpec((tm, 2 * d), row), _resident((d, d), (0, 0)), _resident((POOL_DIM, d), (0, 0)),
                  _resident((ATTN_DIM, d), (0, 0))],
        out_specs=[pl.BlockSpec((tm, d), row), pl.BlockSpec((tm, d), row), pl.BlockSpec((tm, d), row),
                   pl.BlockSpec((tm, POOL_DIM), row), pl.BlockSpec((tm, ATTN_DIM), row), pl.BlockSpec((tm, 2 * d), row)],
        out_shape=[_sds((m, d), BF16), _sds((m, d), BF16), _sds((m, d), BF16), _sds((m, POOL_DIM), F32),
                   _sds((m, ATTN_DIM), BF16), _sds((m, 2 * d), BF16)],
        compiler_params=_cp("parallel"),
    )(dy, p, o, zg, w_out, w_pb, w_ab)


def _mix_bwd_x(dy, x, ln, dzp, dzqkv, dzg, w_in, name):
    m, d = dy.shape
    n_in = w_in.shape[0]
    tm = _row_tile(m)

    def body(dy_ref, x_ref, ln_ref, dzp_ref, dzq_ref, dzg_ref, w_ref, dx_ref, dln_ref):
        @pl.when(pl.program_id(0) == 0)
        def _():
            dln_ref[...] = jnp.zeros_like(dln_ref)

        dh = jnp.dot(dzp_ref[...], w_ref[:POOL_DIM, :], preferred_element_type=F32)
        dh += jnp.dot(dzq_ref[...], w_ref[POOL_DIM:QKV_END, :], preferred_element_type=F32)
        dh += jnp.dot(dzg_ref[...], w_ref[QKV_END:, :], preferred_element_type=F32)
        dx, dgain = _rms_bwd(dh, x_ref[...], ln_ref[...])
        dx_ref[...] = dy_ref[...] + dx
        dln_ref[...] += jnp.sum(dgain, axis=0, keepdims=True)

    row = lambda i: (i, 0)
    return pl.pallas_call(
        body, name=name, grid=(m // tm,),
        in_specs=[pl.BlockSpec((tm, d), row), pl.BlockSpec((tm, d), row), _resident((1, d), (0, 0)),
                  pl.BlockSpec((tm, POOL_DIM), row), pl.BlockSpec((tm, QKV_END - POOL_DIM), row),
                  pl.BlockSpec((tm, n_in - QKV_END), row), _resident((n_in, d), (0, 0))],
        out_specs=[pl.BlockSpec((tm, d), row), pl.BlockSpec((1, d), lambda i: (0, 0))],
        out_shape=[_sds((m, d), F32), _sds((1, d), F32)],
        compiler_params=_cp("arbitrary"),
    )(dy, x, ln, dzp, dzqkv, dzg, w_in)


def _loss_head(y, target, name):
    m, d = y.shape
    tm = _row_tile(m)

    def body(y_ref, t_ref, loss_ref, dy_ref):
        @pl.when(pl.program_id(0) == 0)
        def _():
            loss_ref[...] = jnp.zeros_like(loss_ref)

        diff = y_ref[...] - t_ref[...]
        dy_ref[...] = diff * (1.0 / d)
        loss_ref[...] += 0.5 * jnp.sum(jnp.mean(diff * diff, axis=-1, keepdims=True), axis=0, keepdims=True)

    row = lambda i: (i, 0)
    return pl.pallas_call(
        body, name=name, grid=(m // tm,),
        in_specs=[pl.BlockSpec((tm, d), row), pl.BlockSpec((tm, d), row)],
        out_specs=[pl.BlockSpec((1, 1), lambda i: (0, 0)), pl.BlockSpec((tm, d), row)],
        out_shape=[_sds((1, 1), F32), _sds((m, d), F32)],
        compiler_params=_cp("arbitrary"),
    )(y, target)


def _adamw_math(g, w, m, v):
    m2 = ADAM_B1 * m + (1.0 - ADAM_B1) * g
    v2 = ADAM_B2 * v + (1.0 - ADAM_B2) * (g * g)
    m_hat = m2 / (1.0 - ADAM_B1 ** ADAM_STEP)
    v_hat = v2 / (1.0 - ADAM_B2 ** ADAM_STEP)
    delta = -ADAM_LR * (m_hat / (jnp.sqrt(v_hat) + ADAM_EPS) + ADAM_WD * w)
    return delta, m2, v2


def _sum_parts(parts_ref):
    g = parts_ref[0].astype(F32)
    for s in range(1, N_DEV):
        g = g + parts_ref[s].astype(F32)
    return g


def _adamw_sharded(parts, w, m, v, name, after=None):
    n_layers, rows, cols = w.shape
    tr = max(t for t in range(16, rows + 1, 16) if rows % t == 0 and t * cols <= ADAMW_BLOCK_ELEMS)
    nr = rows // tr
    deps = [] if after is None else [after]

    def body(*refs):
        part_refs = refs[:n_layers]
        w_ref, m_ref, v_ref = refs[n_layers:n_layers + 3]
        g_out, d_out, m_out, v_out = refs[-4:]
        layer = pl.program_id(0)
        for l in range(n_layers):
            @pl.when(layer == l)
            def _(l=l):
                g = _sum_parts(part_refs[l])
                delta, m2, v2 = _adamw_math(g, w_ref[...], m_ref[...], v_ref[...])
                g_out[...] = g
                d_out[...] = delta
                m_out[...] = m2
                v_out[...] = v2

    def part_map(l):
        return lambda layer, r: (0, jnp.where(layer == l, r, jnp.where(layer < l, 0, nr - 1)), 0)

    wspec = pl.BlockSpec((None, tr, cols), lambda layer, r: (layer, r, 0))
    return pl.pallas_call(
        body, name=name, grid=(n_layers, nr),
        in_specs=([pl.BlockSpec((N_DEV, tr, cols), part_map(l)) for l in range(n_layers)] + [wspec] * 3
                  + [pl.BlockSpec(memory_space=pl.ANY)] * len(deps)),
        out_specs=[wspec] * 4,
        out_shape=[_sds(w.shape, F32)] * 4,
        compiler_params=_cp("arbitrary", "arbitrary"),
    )(*parts, w, m, v, *deps)


def _adamw_packed(parts, w, m, v, name):
    def body(p_ref, w_ref, m_ref, v_ref, g_out, d_out, m_out, v_out):
        g = _sum_parts(p_ref)
        delta, m2, v2 = _adamw_math(g, w_ref[...], m_ref[...], v_ref[...])
        g_out[...] = g
        d_out[...] = delta
        m_out[...] = m2
        v_out[...] = v2

    return pl.pallas_call(
        body, name=name, out_shape=[_sds(w.shape, F32)] * 4,
        compiler_params=pltpu.CompilerParams(vmem_limit_bytes=VMEM_LIMIT_BYTES),
    )(parts, w, m, v)


_SMALL = ("ln_ffn1", "ln_mix", "pool_w", "pool_scale", "q_norm", "k_norm", "sinks", "ln_ffn2")


def _pack_small(arrs):
    rows = []
    for a in arrs:
        flat = a.reshape(-1)
        pad = (-flat.shape[0]) % 1024
        rows.append(jnp.pad(flat, (0, pad)).reshape(-1, 128))
    return jnp.concatenate(rows, axis=0)


def _unpack_small(packed, like):
    out, r0 = [], 0
    for a in like:
        size = a.size
        nrows = (size + 1023) // 1024 * 8
        out.append(packed[r0:r0 + nrows].reshape(-1)[:size].reshape(a.shape))
        r0 += nrows
    return out


def _rope_tables(m):
    pos = jnp.arange(m, dtype=F32)
    inv_freq = ROPE_THETA ** (-jnp.arange(0, ROT_DIM, 2, dtype=F32) / ROT_DIM)
    ang = pos[:, None] * inv_freq[None, :]
    cos8, sin8 = jnp.cos(ang), jnp.sin(ang)
    rest = HEAD_DIM - ROT_DIM
    cos64 = jnp.concatenate([cos8, cos8, jnp.ones((m, rest), F32)], axis=1)
    sin64 = jnp.concatenate([-sin8, sin8, jnp.zeros((m, rest), F32)], axis=1)
    return jnp.tile(cos64, (1, 2)), jnp.tile(sin64, (1, 2))


def _to_shard_major_cols(w):
    k = w.shape[0]
    return w.reshape(k, N_DEV, -1).transpose(1, 0, 2)


def _from_shard_major_cols(w):
    return w.transpose(1, 0, 2).reshape(w.shape[1], -1)


def kernel(x, ln_ffn1, w_ffn1_gu, w_ffn1_down, ln_mix, w_in, pool_w, pool_scale, w_pool_branch, q_norm, k_norm, sinks, w_attn_branch, w_out, ln_ffn2, w_ffn2_gu, w_ffn2_down, loss_target, m_ln_ffn1, m_w_ffn1_gu, m_w_ffn1_down, m_ln_mix, m_w_in, m_pool_w, m_pool_scale, m_w_pool_branch, m_q_norm, m_k_norm, m_sinks, m_w_attn_branch, m_w_out, m_ln_ffn2, m_w_ffn2_gu, m_w_ffn2_down, v_ln_ffn1, v_w_ffn1_gu, v_w_ffn1_down, v_ln_mix, v_w_in, v_pool_w, v_pool_scale, v_w_pool_branch, v_q_norm, v_k_norm, v_sinks, v_w_attn_branch, v_w_out, v_ln_ffn2, v_w_ffn2_gu, v_w_ffn2_down):
    weights = dict(ln_ffn1=ln_ffn1, w_ffn1_gu=w_ffn1_gu, w_ffn1_down=w_ffn1_down, ln_mix=ln_mix, w_in=w_in, pool_w=pool_w,
                   pool_scale=pool_scale, w_pool_branch=w_pool_branch, q_norm=q_norm, k_norm=k_norm, sinks=sinks,
                   w_attn_branch=w_attn_branch, w_out=w_out, ln_ffn2=ln_ffn2, w_ffn2_gu=w_ffn2_gu, w_ffn2_down=w_ffn2_down)
    mom_m = dict(ln_ffn1=m_ln_ffn1, w_ffn1_gu=m_w_ffn1_gu, w_ffn1_down=m_w_ffn1_down, ln_mix=m_ln_mix, w_in=m_w_in,
                 pool_w=m_pool_w, pool_scale=m_pool_scale, w_pool_branch=m_w_pool_branch, q_norm=m_q_norm, k_norm=m_k_norm,
                 sinks=m_sinks, w_attn_branch=m_w_attn_branch, w_out=m_w_out, ln_ffn2=m_ln_ffn2, w_ffn2_gu=m_w_ffn2_gu,
                 w_ffn2_down=m_w_ffn2_down)
    mom_v = dict(ln_ffn1=v_ln_ffn1, w_ffn1_gu=v_w_ffn1_gu, w_ffn1_down=v_w_ffn1_down, ln_mix=v_ln_mix, w_in=v_w_in,
                 pool_w=v_pool_w, pool_scale=v_pool_scale, w_pool_branch=v_w_pool_branch, q_norm=v_q_norm, k_norm=v_k_norm,
                 sinks=v_sinks, w_attn_branch=v_w_attn_branch, w_out=v_w_out, ln_ffn2=v_ln_ffn2, w_ffn2_gu=v_w_ffn2_gu,
                 w_ffn2_down=v_w_ffn2_down)
    order = ("ln_ffn1", "w_ffn1_gu", "w_ffn1_down", "ln_mix", "w_in", "pool_w", "pool_scale", "w_pool_branch", "q_norm",
             "k_norm", "sinks", "w_attn_branch", "w_out", "ln_ffn2", "w_ffn2_gu", "w_ffn2_down")
    big = ("w_ffn1_gu", "w_ffn1_down", "w_in", "w_pool_branch", "w_attn_branch", "w_out", "w_ffn2_gu", "w_ffn2_down")

    transposed = ("w_ffn1_gu", "w_ffn2_gu", "w_in")
    for group in (weights, mom_m, mom_v):
        for k in transposed:
            group[k] = jnp.swapaxes(group[k], 1, 2)

    n_layers = ln_ffn1.shape[0]
    seq, d = x.shape[-2], x.shape[-1]
    xs = x.reshape(seq, d)
    target = loss_target.reshape(seq, d)
    cos, sin = _rope_tables(seq)

    first_keys = ("w_ffn1_gu", "w_ffn1_down")
    rest_keys = tuple(k for k in big if k not in first_keys)

    def layer_shards(l, keys=big):
        return [weights[k][l].astype(BF16) for k in keys]

    def first_weights(l, g):
        return dict(gu1=g["w_ffn1_gu"].reshape(2, -1, d), down1=g["w_ffn1_down"].reshape(-1, d), ln1=ln_ffn1[l][None])

    def rest_weights(l, g):
        return dict(
            gu2=g["w_ffn2_gu"].reshape(2, -1, d), down2=g["w_ffn2_down"].reshape(-1, d),
            w_in=g["w_in"].reshape(-1, d), w_pb=_from_shard_major_cols(g["w_pool_branch"]),
            w_ab=_from_shard_major_cols(g["w_attn_branch"]), w_out=g["w_out"].reshape(d, d),
            ln_mix=ln_mix[l][None], ln2=ln_ffn2[l][None], pool_w=pool_w[l],
            pool_scale=pool_scale[l][None], sinks=sinks[l],
            qgain=jnp.tile(q_norm[l], N_Q_HEADS)[None], kgain=jnp.tile(k_norm[l], N_KV_HEADS)[None])

    def layer_weights(l, full):
        g = dict(zip(big, full))
        return {**first_weights(l, g), **rest_weights(l, g)}

    got = _all_gather_many(layer_shards(0, first_keys), name="gather_first_l0")
    gathered = [first_weights(0, dict(zip(first_keys, got)))]
    rest_in_flight, token = _exchange_start(layer_shards(0, rest_keys), scatter=False, name="gather_start_l0",
                                            peers=_CHIP_PEERS)
    saved = []
    cur = xs
    for l in range(n_layers):
        lw = gathered[l]
        s = dict(x0=cur)
        in_flight = None
        if l + 1 < n_layers:
            in_flight, token = _exchange_start(layer_shards(l + 1), scatter=False, name=f"gather_start_l{l + 1}",
                                               after=token)
        s["h1"], s["gu1"], act1 = _ffn_up(cur, lw["ln1"], lw["gu1"], name=f"ffn1_up_l{l}", after=token)
        s["act1"] = act1
        x1 = _ffn_down(cur, act1, lw["down1"], name=f"ffn1_down_l{l}")
        s["x1"] = x1
        if l == 0:
            got = _forward_to_sibling(_exchange_wait(rest_in_flight, x1, name="gather_wait_l0"), name="gather_forward_l0")
            lw.update(rest_weights(0, dict(zip(rest_keys, got))))
        s["h2"], zq, zg = _mix_in(x1, lw["ln_mix"], lw["w_in"], name=f"mix_in_l{l}")
        s["zq"], s["zg"] = zq, zg
        s["d"], s["p"] = _pool_fwd(zq, lw["pool_w"], lw["pool_scale"], name=f"pool_fwd_l{l}")
        s["qr"], s["kr"], s["vb"] = _qk_prep(zq, lw["qgain"], lw["kgain"], cos, sin, name=f"qk_prep_l{l}")
        s["o"] = _attn_fwd(s["qr"], s["kr"], s["vb"], lw["sinks"], name=f"attn_fwd_l{l}")
        x2, s["mix"] = _merge_fwd(x1, s["p"], s["o"], zg, lw["w_pb"], lw["w_ab"], lw["w_out"],
                                                  name=f"merge_fwd_l{l}")
        s["x2"] = x2
        s["h3"], s["gu2"], act2 = _ffn_up(x2, lw["ln2"], lw["gu2"], name=f"ffn2_up_l{l}")
        s["act2"] = act2
        cur = _ffn_down(x2, act2, lw["down2"], name=f"ffn2_down_l{l}")
        saved.append(s)
        if in_flight is not None:
            gathered.append(layer_weights(l + 1, _exchange_wait(in_flight, cur, name=f"gather_wait_l{l + 1}")))

    loss_local, dy = _loss_head(cur, target, name="loss_head")
    loss = lax.psum(loss_local[0, 0], MESH_AXES)

    small_grads = {k: [None] * n_layers for k in _SMALL}
    received = {k: [None] * n_layers for k in big}
    big_late = ("w_ffn1_gu", "w_ffn1_down")
    big_early = tuple(k for k in big if k not in big_late)
    early_in_flight, late_in_flight, last_in_flight = [None] * n_layers, [None] * n_layers, [None] * n_layers
    token = None
    for l in reversed(range(n_layers)):
        lw, s = gathered[l], saved[l]
        d_ff = lw["down1"].shape[0]

        def dw_down_of(dyh, act, tag):
            return _matmul_tn(act[None], dyh[None], name=f"{tag}_dw_down_l{l}", a_chunk=d_ff // 2).reshape(N_DEV, -1, d)

        def dw_gu_of(dgu, h, tag, after=None):
            return _matmul_tn(dgu, h[None], name=f"{tag}_dw_gu_l{l}", a_chunk=d_ff // 2, after=after).reshape(N_DEV, -1, d)

        dyh, dgu, dx2, dln2 = _ffn_bwd(dy, s["x2"], lw["ln2"], s["gu2"], lw["down2"], lw["gu2"],
                                       name=f"ffn2_bwd_l{l}", after=token)
        dw_down2, dw_gu2 = dw_down_of(dyh, s["act2"], "ffn2"), dw_gu_of(dgu, s["h3"], "ffn2")

        dyb, da, db, dp, do, dzg = _merge_bwd(dx2, s["p"], s["o"], s["zg"], lw["w_out"], lw["w_pb"], lw["w_ab"],
                                              name=f"merge_bwd_l{l}")
        dw_out = _matmul_tn(s["mix"][None], dyb[None], name=f"dw_out_l{l}")[0]
        dw_pb = _matmul_tn(s["p"][None], da[None], name=f"dw_pb_l{l}")[0]
        dw_ab = _matmul_tn(s["o"][None], db[None], name=f"dw_ab_l{l}")[0]
        dzp, dpw, dsc = _pool_bwd(dp, s["d"], lw["pool_w"], lw["pool_scale"], name=f"pool_bwd_l{l}")
        dq, dk, dv, dsinks = _attn_bwd(do, s["qr"], s["kr"], s["vb"], lw["sinks"], name=f"attn_bwd_l{l}")
        dzqkv, dqg, dkg = _qk_bwd(dq, dk, dv, s["zq"], lw["qgain"], lw["kgain"], cos, sin, name=f"qk_bwd_l{l}")
        dw_in = jnp.concatenate([_matmul_tn(dzp[None], s["h2"][None], name=f"dw_in_pool_l{l}")[0],
                                 _matmul_tn(dzqkv[None], s["h2"][None], name=f"dw_in_qkv_l{l}")[0],
                                 _matmul_tn(dzg[None], s["h2"][None], name=f"dw_in_gate_l{l}")[0]], axis=0)
        dx1, dlnm = _mix_bwd_x(dx2, s["x1"], lw["ln_mix"], dzp, dzqkv, dzg, lw["w_in"], name=f"mix_bwd_x_l{l}")

        partial = dict(w_in=dw_in.reshape(N_DEV, -1, d), w_pool_branch=_to_shard_major_cols(dw_pb),
                       w_attn_branch=_to_shard_major_cols(dw_ab), w_out=dw_out.reshape(N_DEV, d // N_DEV, d),
                       w_ffn2_gu=dw_gu2, w_ffn2_down=dw_down2)
        early_in_flight[l], token = _exchange_start([partial[k] for k in big_early], scatter=True,
                                                    name=f"grads_early_start_l{l}")

        dyh, dgu, dy, dln1 = _ffn_bwd(dx1, s["x0"], lw["ln1"], s["gu1"], lw["down1"], lw["gu1"],
                                      name=f"ffn1_bwd_l{l}", after=token)
        late_in_flight[l], token = _exchange_start([dw_down_of(dyh, s["act1"], "ffn1")], scatter=True,
                                                   name=f"grads_late_start_l{l}")
        last_in_flight[l], token = _exchange_start([dw_gu_of(dgu, s["h1"], "ffn1", after=token)], scatter=True,
                                                   name=f"grads_last_start_l{l}")
        small_grads["ln_ffn1"][l] = dln1[0]
        small_grads["ln_mix"][l] = dlnm[0]
        small_grads["ln_ffn2"][l] = dln2[0]
        small_grads["pool_w"][l] = dpw
        small_grads["pool_scale"][l] = dsc[0]
        small_grads["q_norm"][l] = dqg[0]
        small_grads["k_norm"][l] = dkg[0]
        small_grads["sinks"][l] = dsinks[0, :N_Q_HEADS]

    grad_x = dy.reshape(x.shape)

    small_w = [weights[k] for k in _SMALL]
    packed_g = _pack_small([jnp.stack(small_grads[k]).reshape(weights[k].shape) for k in _SMALL])
    small_in_flight, after = _exchange_start([packed_g], scatter=False, name="small_grads_start", after=token)
    for l in reversed(range(n_layers)):
        got = _exchange_wait(early_in_flight[l], after, name=f"grads_early_wait_l{l}")
        after = got[0]
        for k, r in zip(big_early, got):
            received[k][l] = r

    grads, deltas, new_m, new_v = {}, {}, {}, {}

    def adamw(k, after):
        w = weights[k]
        shape2 = (n_layers, -1, w.shape[-1])
        parts = [r.reshape(N_DEV, -1, w.shape[-1]) for r in received[k]]
        outs = _adamw_sharded(parts, w.reshape(shape2), mom_m[k].reshape(shape2), mom_v[k].reshape(shape2),
                              name=f"adamw_{k}", after=after)
        grads[k], deltas[k], new_m[k], new_v[k] = (o.reshape(w.shape) for o in outs)
        return outs[0]

    after = None
    for k in big_early:
        after = adamw(k, after)
    for l in reversed(range(n_layers)):
        (received["w_ffn1_down"][l],) = _exchange_wait(late_in_flight[l], after, name=f"grads_late_wait_l{l}")
        (received["w_ffn1_gu"][l],) = _exchange_wait(last_in_flight[l], received["w_ffn1_down"][l],
                                                     name=f"grads_last_wait_l{l}")
        after = received["w_ffn1_gu"][l]
    after = None
    for k in big_late:
        after = adamw(k, after)

    (parts_small,) = _exchange_wait(small_in_flight, after, name="small_grads_wait")
    outs = _adamw_packed(parts_small, _pack_small(small_w), _pack_small([mom_m[k] for k in _SMALL]),
                         _pack_small([mom_v[k] for k in _SMALL]), name="adamw_small")
    for res, o in zip((grads, deltas, new_m, new_v), outs):
        for k, a in zip(_SMALL, _unpack_small(o, small_w)):
            res[k] = a

    for res in (grads, deltas, new_m, new_v):
        for k in transposed:
            res[k] = jnp.swapaxes(res[k], 1, 2)
    return (loss, grad_x, *[grads[k] for k in order], *[deltas[k] for k in order],
            *[new_m[k] for k in order], *[new_v[k] for k in order])
```

```python
import jax
import jax.numpy as jnp
from jax import lax
from jax.experimental import pallas as pl
from jax.experimental.pallas import tpu as pltpu

F32 = jnp.float32
BF16 = jnp.bfloat16

N_DEV = 8
MESH_AXES = ("x", "y", "c")
EPS = 1e-6

HEAD_DIM = 64
N_Q_HEADS = 8
N_KV_HEADS = 2
GQA_GROUP = N_Q_HEADS // N_KV_HEADS
ATTN_BLOCK = 128
ATTN_SCALE = HEAD_DIM ** -0.5
ROPE_THETA = 500000.0
ROT_DIM = 16
POOL_WINDOWS = (2, 4, 8, 16)
POOL_HALO = 16
GROUP_DIM = 128
POOL_DIM = 512
ATTN_DIM = 512
KV_DIM = 128
QKV_END = POOL_DIM + ATTN_DIM + 2 * KV_DIM

ADAM_LR = 0.001
ADAM_B1 = 0.9
ADAM_B2 = 0.999
ADAM_EPS = 1e-08
ADAM_WD = 0.01
ADAM_STEP = 10

ROW_TILE = 512
TN_ROW_TILE = 2048
FFN_CHUNK = 256
FFN_BWD_ROW_TILE = 256
VMEM_LIMIT_BYTES = 56 << 20
ADAMW_BLOCK_ELEMS = 192 * 1024
NEG_BIG = -1e30

_NT = (((1,), (1,)), ((), ()))
_TN = (((0,), (0,)), ((), ()))


def _cp(*sem):
    return pltpu.CompilerParams(dimension_semantics=sem, vmem_limit_bytes=VMEM_LIMIT_BYTES)


def _resident(block, index):
    return pl.BlockSpec(block, lambda *_: index, pipeline_mode=pl.Buffered(1))


def _row_tile(m):
    return min(ROW_TILE, m)


def _sds(shape, dtype):
    return jax.ShapeDtypeStruct(shape, dtype)


def _mesh_pos():
    return lax.axis_index("x"), lax.axis_index("y"), lax.axis_index("c")


def _all_gather_many(shards, name):
    n = len(shards)

    def body(*refs):
        ins, outs = refs[:n], refs[n:2 * n]
        send_sems, recv_sems, local_sems = refs[2 * n:]
        x, y, c = _mesh_pos()
        me, sibling = (x, y, c), (x, y, 1 - c)
        chips = [(1 - x, y), (x, 1 - y), (1 - x, 1 - y)]

        def slot(a, pos):
            return outs[a].at[4 * pos[0] + 2 * pos[1] + pos[2]]

        def copy(a, k, block, to, src=None):
            return pltpu.make_async_remote_copy(
                src_ref=slot(a, block) if src is None else src, dst_ref=slot(a, block),
                send_sem=send_sems.at[a, k], recv_sem=recv_sems.at[a, k],
                device_id=to, device_id_type=pl.DeviceIdType.MESH)

        mine = [pltpu.make_async_copy(ins[a], slot(a, me), local_sems.at[a]) for a in range(n)]
        for cp in mine:
            cp.start()
        first = []
        for a in range(n):
            first.append(copy(a, 0, me, sibling, src=ins[a]))
            for j, chip in enumerate(chips):
                first.append(copy(a, 1 + j, me, (*chip, c), src=ins[a]))
        for cp in first:
            cp.start()
        passed = []
        for j, chip in enumerate(chips):
            for a in range(n):
                copy(a, 1 + j, (*chip, c), me).wait_recv()
                fwd = copy(a, 4 + j, (*chip, c), sibling)
                fwd.start()
                passed.append(fwd)
        for a in range(n):
            copy(a, 0, sibling, me).wait_recv()
        for j, chip in enumerate(chips):
            for a in range(n):
                copy(a, 4 + j, (*chip, 1 - c), me).wait_recv()
        for cp in first + passed:
            cp.wait_send()
        for cp in mine:
            cp.wait()

    any_spec = pl.BlockSpec(memory_space=pl.ANY)
    return pl.pallas_call(
        body, name=name,
        out_shape=[_sds((N_DEV,) + s.shape, s.dtype) for s in shards],
        in_specs=[any_spec] * n, out_specs=[any_spec] * n,
        scratch_shapes=[pltpu.SemaphoreType.DMA((n, 7)), pltpu.SemaphoreType.DMA((n, 7)),
                        pltpu.SemaphoreType.DMA((n,))],
    )(*shards)


_ALL_PEERS = tuple(range(1, N_DEV))
_CHIP_PEERS = (1, 2, 4, 6)


def _direct_copies(src, land, send_sem, recv_sem, local_sem, scatter, peers=_ALL_PEERS):
    x, y, c = _mesh_pos()
    me = 4 * x + 2 * y + c
    local = pltpu.make_async_copy(src.at[me] if scatter else src, land.at[me], local_sem)
    remote = []
    for k in peers:
        px = 1 - x if k & 4 else x
        py = 1 - y if k & 2 else y
        pc = 1 - c if k & 1 else c
        remote.append(pltpu.make_async_remote_copy(
            src_ref=src.at[4 * px + 2 * py + pc] if scatter else src, dst_ref=land.at[me],
            send_sem=send_sem, recv_sem=recv_sem, device_id=(px, py, pc), device_id_type=pl.DeviceIdType.MESH))
    every = land.at[pl.ds(0, len(peers))]
    drain = pltpu.make_async_remote_copy(src_ref=every, dst_ref=every, send_sem=send_sem, recv_sem=recv_sem,
                                         device_id=(x, y, c), device_id_type=pl.DeviceIdType.MESH)
    return local, remote, drain


_HBM_SPEC = pl.BlockSpec(memory_space=pltpu.HBM)
_SEM_SPEC = pl.BlockSpec(memory_space=pltpu.SEMAPHORE)
_DATAFLOW = pltpu.SideEffectType.DATAFLOW_SIDE_EFFECTING
_SEMS_PER_ARRAY = 3


def _exchange_start(srcs, scatter, name, peers=_ALL_PEERS, after=None):
    n = len(srcs)
    deps = [] if after is None else [after]
    n_sems = _SEMS_PER_ARRAY * n
    land_shapes = [s.shape if scatter else (N_DEV,) + s.shape for s in srcs]

    def body(*refs):
        ins, lands = refs[:n], refs[n:2 * n]
        sems = refs[2 * n + len(deps):2 * n + len(deps) + n_sems]
        for a in range(n):
            local, remote, _ = _direct_copies(ins[a], lands[a], *sems[3 * a:3 * a + 3], scatter, peers)
            local.start()
            for cp in remote:
                cp.start()
        refs[-1][...] = jnp.zeros_like(refs[-1])

    outs = pl.pallas_call(
        body, name=name,
        out_shape=(*[pltpu.SemaphoreType.DMA(())] * n_sems,
                   *[pltpu.HBM(s.shape, s.dtype) for s in srcs],
                   *[pltpu.HBM(shape, s.dtype) for shape, s in zip(land_shapes, srcs)],
                   _sds((8, 128), F32)),
        in_specs=[_HBM_SPEC] * (2 * n) + [pl.BlockSpec(memory_space=pl.ANY)] * len(deps),
        out_specs=(*[_SEM_SPEC] * n_sems, *[_HBM_SPEC] * (2 * n), pl.BlockSpec(memory_space=pltpu.VMEM)),
        input_output_aliases={i: n_sems + i for i in range(2 * n)},
        compiler_params=pltpu.CompilerParams(has_side_effects=_DATAFLOW),
    )(*[pltpu.with_memory_space_constraint(s, pltpu.HBM) for s in srcs],
      *[pltpu.with_memory_space_constraint(lax.empty(shape, s.dtype), pltpu.HBM) for shape, s in zip(land_shapes, srcs)],
      *deps)
    return (outs[:n_sems], outs[n_sems:n_sems + n], outs[n_sems + n:n_sems + 2 * n], scatter, peers), outs[-1]


def _exchange_wait(state, after, name):
    sems, srcs, lands, scatter, peers = state
    n = len(srcs)
    n_sems = len(sems)

    def body(*refs):
        ins, zones, ss = refs[:n], refs[n:2 * n], refs[2 * n:2 * n + n_sems]
        for a in range(n):
            local, _, drain = _direct_copies(ins[a], zones[a], *ss[3 * a:3 * a + 3], scatter, peers)
            drain.wait_send()
            drain.wait_recv()
            local.wait()

    outs = pl.pallas_call(
        body, name=name,
        out_shape=(*[pltpu.HBM(s.shape, s.dtype) for s in srcs], *[pltpu.HBM(z.shape, z.dtype) for z in lands]),
        in_specs=[_HBM_SPEC] * (2 * n) + [_SEM_SPEC] * n_sems + [pl.BlockSpec(memory_space=pl.ANY)],
        out_specs=[_HBM_SPEC] * (2 * n),
        input_output_aliases={i: i for i in range(2 * n)},
        compiler_params=pltpu.CompilerParams(has_side_effects=_DATAFLOW),
    )(*srcs, *lands, *sems, after)
    return outs[n:]


def _forward_to_sibling(lands, name):
    n = len(lands)

    def body(*refs):
        zones = refs[n:2 * n]
        send_sems, recv_sems = refs[2 * n:]
        x, y, c = _mesh_pos()
        chips = [(1 - x, y), (x, 1 - y), (1 - x, 1 - y)]
        sends, recvs = [], []
        for a in range(n):
            for j, (px, py) in enumerate(chips):
                mine = zones[a].at[4 * px + 2 * py + c]
                theirs = zones[a].at[4 * px + 2 * py + 1 - c]
                sends.append(pltpu.make_async_remote_copy(
                    src_ref=mine, dst_ref=mine, send_sem=send_sems.at[a, j], recv_sem=recv_sems.at[a, j],
                    device_id=(x, y, 1 - c), device_id_type=pl.DeviceIdType.MESH))
                recvs.append(pltpu.make_async_remote_copy(
                    src_ref=theirs, dst_ref=theirs, send_sem=send_sems.at[a, j], recv_sem=recv_sems.at[a, j],
                    device_id=(x, y, 1 - c), device_id_type=pl.DeviceIdType.MESH))
        for cp in sends:
            cp.start()
        for cp in recvs:
            cp.wait_recv()
        for cp in sends:
            cp.wait_send()

    any_spec = pl.BlockSpec(memory_space=pl.ANY)
    return pl.pallas_call(
        body, name=name,
        out_shape=[_sds(z.shape, z.dtype) for z in lands],
        in_specs=[any_spec] * n, out_specs=[any_spec] * n,
        input_output_aliases={a: a for a in range(n)},
        scratch_shapes=[pltpu.SemaphoreType.DMA((n, 3)), pltpu.SemaphoreType.DMA((n, 3))],
    )(*lands)


def _rms_fwd(xv, gain):
    r = lax.rsqrt(jnp.mean(xv * xv, axis=-1, keepdims=True) + EPS)
    return xv * r * gain


def _rms_bwd(dh, xv, gain):
    r = lax.rsqrt(jnp.mean(xv * xv, axis=-1, keepdims=True) + EPS)
    xn = xv * r
    dxn = dh * gain
    dx = r * (dxn - xn * jnp.mean(dxn * xn, axis=-1, keepdims=True))
    return dx, dh * xn


def _sigmoid(v):
    return 0.5 * jnp.tanh(0.5 * v) + 0.5


def _silu_parts(g):
    s = _sigmoid(g)
    return g * s, s * (1.0 + g * (1.0 - s))


def _segment_mean(v, width):
    r = lax.broadcasted_iota(jnp.int32, (width, width), 0) >> 6
    c = lax.broadcasted_iota(jnp.int32, (width, width), 1) >> 6
    bd = (r == c).astype(BF16)
    hi = v.astype(BF16)
    lo = (v - hi.astype(F32)).astype(BF16)
    total = jnp.dot(hi, bd, preferred_element_type=F32) + jnp.dot(lo, bd, preferred_element_type=F32)
    return total * (1.0 / HEAD_DIM)


def _rope_partner(v):
    width = v.shape[1]
    half = ROT_DIM // 2
    lane = lax.broadcasted_iota(jnp.int32, v.shape, 1) & (HEAD_DIM - 1)
    up = jnp.where(lane < ROT_DIM, pltpu.roll(v, half, 1), 0.0)
    return jnp.where(lane < half, pltpu.roll(v, width - half, 1), up)


def _tile_lanes(t, width):
    return t if width == t.shape[1] else jnp.tile(t, (1, width // t.shape[1]))


def _ffn_chunks(f):
    return [slice(j * FFN_CHUNK, (j + 1) * FFN_CHUNK) for j in range(f // FFN_CHUNK)]


def _ffn_fwd(x, ln, wgu, wd, name, after=None):
    m, d = x.shape
    f = wgu.shape[1]
    tm = min(FFN_BWD_ROW_TILE, m)
    deps = [] if after is None else [after]

    def body(*refs):
        x_ref, ln_ref, w_ref, wd_ref = refs[:4]
        h_ref, gu_ref, a_ref, o_ref = refs[-4:]
        xv = x_ref[...]
        h = _rms_fwd(xv, ln_ref[...]).astype(BF16)
        h_ref[...] = h
        acts = []
        for cols in _ffn_chunks(f):
            g = lax.dot_general(h, w_ref[0, cols, :], _NT, preferred_element_type=F32)
            u = lax.dot_general(h, w_ref[1, cols, :], _NT, preferred_element_type=F32)
            gu_ref[0, :, cols] = g.astype(BF16)
            gu_ref[1, :, cols] = u.astype(BF16)
            acts.append((g * _sigmoid(g) * u).astype(BF16))
            a_ref[:, cols] = acts[-1]
        o_ref[...] = xv + 0.5 * jnp.dot(jnp.concatenate(acts, axis=1), wd_ref[...], preferred_element_type=F32)

    row = lambda i: (i, 0)
    return pl.pallas_call(
        body, name=name, grid=(m // tm,),
        in_specs=[pl.BlockSpec((tm, d), row), _resident((1, d), (0, 0)), _resident((2, f, d), (0, 0, 0)),
                  _resident((f, d), (0, 0))] + [pl.BlockSpec(memory_space=pl.ANY)] * len(deps),
        out_specs=[pl.BlockSpec((tm, d), row), pl.BlockSpec((2, tm, f), lambda i: (0, i, 0)),
                   pl.BlockSpec((tm, f), row), pl.BlockSpec((tm, d), row)],
        out_shape=[_sds((m, d), BF16), _sds((2, m, f), BF16), _sds((m, f), BF16), _sds((m, d), F32)],
        compiler_params=_cp("parallel"),
    )(x, ln, wgu, wd, *deps)


def _ffn_bwd(dy, x, ln, gu, wd, wgu, name, after=None):
    m, d = dy.shape
    f = gu.shape[-1]
    tm = min(FFN_BWD_ROW_TILE, m)
    deps = [] if after is None else [after]

    def body(*refs):
        dy_ref, x_ref, ln_ref, gu_ref, wd_ref, wgu_ref = refs[:6]
        dyh_ref, dgu_ref, dx_ref, dln_ref = refs[-4:]

        @pl.when(pl.program_id(0) == 0)
        def _():
            dln_ref[...] = jnp.zeros_like(dln_ref)

        dyh = (0.5 * dy_ref[...]).astype(BF16)
        dyh_ref[...] = dyh
        dgs, dus = [], []
        for cols in _ffn_chunks(f):
            da = lax.dot_general(dyh, wd_ref[cols, :], _NT, preferred_element_type=F32)
            g = gu_ref[0, :, cols].astype(F32)
            u = gu_ref[1, :, cols].astype(F32)
            silu, dsilu = _silu_parts(g)
            dgs.append((da * u * dsilu).astype(BF16))
            dus.append((da * silu).astype(BF16))
            dgu_ref[0, :, cols] = dgs[-1]
            dgu_ref[1, :, cols] = dus[-1]
        dh = jnp.dot(jnp.concatenate(dgs, axis=1), wgu_ref[0], preferred_element_type=F32)
        dh += jnp.dot(jnp.concatenate(dus, axis=1), wgu_ref[1], preferred_element_type=F32)
        dx, dgain = _rms_bwd(dh, x_ref[...], ln_ref[...])
        dx_ref[...] = dy_ref[...] + dx
        dln_ref[...] += jnp.sum(dgain, axis=0, keepdims=True)

    row = lambda i: (i, 0)
    return pl.pallas_call(
        body, name=name, grid=(m // tm,),
        in_specs=[pl.BlockSpec((tm, d), row), pl.BlockSpec((tm, d), row), _resident((1, d), (0, 0)),
                  pl.BlockSpec((2, tm, f), lambda i: (0, i, 0)), _resident((f, d), (0, 0)),
                  _resident((2, f, d), (0, 0, 0))] + [pl.BlockSpec(memory_space=pl.ANY)] * len(deps),
        out_specs=[pl.BlockSpec((tm, d), row), pl.BlockSpec((2, tm, f), lambda i: (0, i, 0)),
                   pl.BlockSpec((tm, d), row), pl.BlockSpec((1, d), lambda i: (0, 0))],
        out_shape=[_sds((m, d), BF16), _sds((2, m, f), BF16), _sds((m, d), F32), _sds((1, d), F32)],
        compiler_params=_cp("arbitrary"),
    )(dy, x, ln, gu, wd, wgu, *deps)


def _matmul_tn(a, b, name, a_chunk=None, out_dtype=BF16, after=None):
    ja, m, k = a.shape
    jb, _, n = b.shape
    nj = max(ja, jb)
    kc = k if a_chunk is None else a_chunk
    tm = min(TN_ROW_TILE, m)
    nm = m // tm
    deps = [] if after is None else [after]

    def body(*refs):
        a_ref, b_ref = refs[:2]
        o_ref, acc = refs[-2:]
        step = pl.program_id(2)

        @pl.when(step == 0)
        def _():
            acc[...] = jnp.zeros_like(acc)

        acc[...] += lax.dot_general(a_ref[...], b_ref[...], _TN, preferred_element_type=F32)

        @pl.when(step == nm - 1)
        def _():
            o_ref[...] = acc[...].astype(o_ref.dtype)

    return pl.pallas_call(
        body, name=name, grid=(nj, k // kc, nm),
        in_specs=[pl.BlockSpec((None, tm, kc), (lambda j, c, s: (j, s, c)) if ja > 1 else (lambda j, c, s: (0, s, c))),
                  pl.BlockSpec((None, tm, n), (lambda j, c, s: (j, s, 0)) if jb > 1 else (lambda j, c, s: (0, s, 0)))]
        + [pl.BlockSpec(memory_space=pl.ANY)] * len(deps),
        out_specs=pl.BlockSpec((None, kc, n), lambda j, c, s: (j, c, 0)),
        out_shape=_sds((nj, k, n), out_dtype),
        scratch_shapes=[pltpu.VMEM((kc, n), F32)],
        compiler_params=_cp("parallel", "parallel", "arbitrary"),
    )(a, b, *deps)


def _mix_in(x, ln, w_in, name):
    m, d = x.shape
    n_in = w_in.shape[0]
    tm = _row_tile(m)

    def body(x_ref, ln_ref, w_ref, h_ref, zq_ref, zg_ref):
        h = _rms_fwd(x_ref[...], ln_ref[...]).astype(BF16)
        h_ref[...] = h
        zq_ref[...] = lax.dot_general(h, w_ref[:QKV_END, :], _NT, preferred_element_type=F32)
        zg_ref[...] = lax.dot_general(h, w_ref[QKV_END:, :], _NT, preferred_element_type=F32).astype(BF16)

    return pl.pallas_call(
        body, name=name, grid=(m // tm,),
        in_specs=[pl.BlockSpec((tm, d), lambda i: (i, 0)), _resident((1, d), (0, 0)), _resident((n_in, d), (0, 0))],
        out_specs=[pl.BlockSpec((tm, d), lambda i: (i, 0)), pl.BlockSpec((tm, QKV_END), lambda i: (i, 0)),
                   pl.BlockSpec((tm, n_in - QKV_END), lambda i: (i, 0))],
        out_shape=[_sds((m, d), BF16), _sds((m, QKV_END), F32), _sds((m, n_in - QKV_END), BF16)],
        compiler_params=_cp("parallel"),
    )(x, ln, w_in)


def _pool_fwd(zq, pool_w, scale, name):
    m = zq.shape[0]
    tm = _row_tile(m)
    halo_blocks = tm // POOL_HALO

    def body(zc_ref, zh_ref, pw_ref, sc_ref, d_ref, p_ref):
        i = pl.program_id(0)
        halo = jnp.where(i > 0, zh_ref[...], 0.0)
        ext = jnp.concatenate([halo, zc_ref[...]], axis=0)
        t = i * tm + lax.broadcasted_iota(jnp.int32, (tm, 1), 0)
        for g, w in enumerate(POOL_WINDOWS):
            lanes = slice(g * GROUP_DIM, (g + 1) * GROUP_DIM)
            e = ext[:, lanes]
            s, k = e, 1
            while k < w:
                s = s + pltpu.roll(s, k, 0)
                k *= 2
            cnt = jnp.minimum(t + 1, w).astype(F32)
            dg = (s[POOL_HALO:, :] / cnt - e[POOL_HALO:, :]).astype(BF16)
            y = jnp.dot(dg, pw_ref[g].astype(BF16), preferred_element_type=F32)
            d_ref[:, lanes] = dg
            p_ref[:, lanes] = (y * sc_ref[:, lanes]).astype(BF16)

    return pl.pallas_call(
        body, name=name, grid=(m // tm,),
        in_specs=[pl.BlockSpec((tm, POOL_DIM), lambda i: (i, 0)),
                  pl.BlockSpec((POOL_HALO, POOL_DIM), lambda i: (jnp.maximum(i * halo_blocks - 1, 0), 0)),
                  _resident((4, GROUP_DIM, GROUP_DIM), (0, 0, 0)), _resident((1, POOL_DIM), (0, 0))],
        out_specs=[pl.BlockSpec((tm, POOL_DIM), lambda i: (i, 0)), pl.BlockSpec((tm, POOL_DIM), lambda i: (i, 0))],
        out_shape=[_sds((m, POOL_DIM), BF16), _sds((m, POOL_DIM), BF16)],
        compiler_params=_cp("parallel"),
    )(zq, zq, pool_w, scale)


def _pool_bwd(dp, d, pool_w, scale, name):
    m = dp.shape[0]
    tm = _row_tile(m)
    nb = m // tm
    halo_blocks = tm // POOL_HALO
    rows = tm + POOL_HALO

    def body(dpc_ref, dph_ref, d_ref, pw_ref, sc_ref, du_ref, dpw_ref, dsc_ref):
        i = pl.program_id(0)

        @pl.when(i == 0)
        def _():
            dpw_ref[...] = jnp.zeros_like(dpw_ref)
            dsc_ref[...] = jnp.zeros_like(dsc_ref)

        halo = jnp.where(i < nb - 1, dph_ref[...], 0.0)
        dpc = dpc_ref[...]
        ext = jnp.concatenate([dpc, halo], axis=0)
        t = i * tm + lax.broadcasted_iota(jnp.int32, (rows, 1), 0)
        for g, w in enumerate(POOL_WINDOWS):
            lanes = slice(g * GROUP_DIM, (g + 1) * GROUP_DIM)
            pwb = pw_ref[g].astype(BF16)
            dyb = (ext[:, lanes] * sc_ref[:, lanes]).astype(BF16)
            dd = lax.dot_general(dyb, pwb, _NT, preferred_element_type=F32)
            cnt = jnp.minimum(t + 1, w).astype(F32)
            s, k = dd / cnt, 1
            while k < w:
                s = s + pltpu.roll(s, rows - k, 0)
                k *= 2
            du_ref[:, lanes] = (s[:tm, :] - dd[:tm, :]).astype(BF16)
            dcur = d_ref[:, lanes]
            y = jnp.dot(dcur, pwb, preferred_element_type=F32)
            dsc_ref[:, lanes] += jnp.sum(dpc[:, lanes] * y, axis=0, keepdims=True)
            dpw_ref[g] += lax.dot_general(dcur, dyb[:tm, :], _TN, preferred_element_type=F32)

    return pl.pallas_call(
        body, name=name, grid=(nb,),
        in_specs=[pl.BlockSpec((tm, POOL_DIM), lambda i: (i, 0)),
                  pl.BlockSpec((POOL_HALO, POOL_DIM), lambda i: (jnp.minimum((i + 1) * halo_blocks, nb * halo_blocks - 1), 0)),
                  pl.BlockSpec((tm, POOL_DIM), lambda i: (i, 0)),
                  _resident((4, GROUP_DIM, GROUP_DIM), (0, 0, 0)), _resident((1, POOL_DIM), (0, 0))],
        out_specs=[pl.BlockSpec((tm, POOL_DIM), lambda i: (i, 0)),
                   pl.BlockSpec((4, GROUP_DIM, GROUP_DIM), lambda i: (0, 0, 0)),
                   pl.BlockSpec((1, POOL_DIM), lambda i: (0, 0))],
        out_shape=[_sds((m, POOL_DIM), BF16), _sds((4, GROUP_DIM, GROUP_DIM), F32), _sds((1, POOL_DIM), F32)],
        compiler_params=_cp("arbitrary"),
    )(dp, dp, d, pool_w, scale)


def _qk_norm_rope(xv, gain, cos, sin):
    width = xv.shape[1]
    r = lax.rsqrt(_segment_mean(xv * xv, width) + EPS)
    y = xv * r * gain
    return y * _tile_lanes(cos, width) + _rope_partner(y) * _tile_lanes(sin, width)


def _qk_prep(zq, qgain, kgain, cos, sin, name):
    m = zq.shape[0]
    tm = _row_tile(m)

    def body(q_ref, kv_ref, qg_ref, kg_ref, cos_ref, sin_ref, qr_ref, kr_ref, v_ref):
        cos_v, sin_v = cos_ref[...], sin_ref[...]
        qr_ref[...] = (_qk_norm_rope(q_ref[...], qg_ref[...], cos_v, sin_v) * ATTN_SCALE).astype(BF16)
        kv = kv_ref[...]
        kr_ref[...] = _qk_norm_rope(kv[:, :KV_DIM], kg_ref[...], cos_v, sin_v).astype(BF16)
        v_ref[...] = kv[:, KV_DIM:].astype(BF16)

    return pl.pallas_call(
        body, name=name, grid=(m // tm,),
        in_specs=[pl.BlockSpec((tm, ATTN_DIM), lambda i: (i, 1)), pl.BlockSpec((tm, 2 * KV_DIM), lambda i: (i, 4)),
                  _resident((1, ATTN_DIM), (0, 0)), _resident((1, KV_DIM), (0, 0)),
                  pl.BlockSpec((tm, 128), lambda i: (i, 0)), pl.BlockSpec((tm, 128), lambda i: (i, 0))],
        out_specs=[pl.BlockSpec((tm, ATTN_DIM), lambda i: (i, 0)), pl.BlockSpec((tm, KV_DIM), lambda i: (i, 0)),
                   pl.BlockSpec((tm, KV_DIM), lambda i: (i, 0))],
        out_shape=[_sds((m, ATTN_DIM), BF16), _sds((m, KV_DIM), BF16), _sds((m, KV_DIM), BF16)],
        compiler_params=_cp("parallel"),
    )(zq, zq, qgain, kgain, cos, sin)


STACK_ROWS = N_Q_HEADS * ATTN_BLOCK


def _band_bias():
    qi = jnp.arange(STACK_ROWS)[:, None] % ATTN_BLOCK
    ki = jnp.arange(2 * ATTN_BLOCK)[None, :]
    diff = qi + ATTN_BLOCK - ki
    band = (diff >= 0) & (diff < ATTN_BLOCK)
    first = band & (ki >= ATTN_BLOCK)
    return jnp.where(jnp.stack([first, band]), 0.0, NEG_BIG).astype(F32)


def _band_tables():
    qi = jnp.arange(STACK_ROWS)[:, None] % ATTN_BLOCK
    use_prev = jnp.arange(ATTN_BLOCK)[None, :] > qi
    return jnp.stack([use_prev.astype(F32), jnp.where(use_prev, NEG_BIG, 0.0).astype(F32),
                      jnp.zeros((STACK_ROWS, ATTN_BLOCK), F32)])


def _band_merge(use_prev, from_prev, from_cur):
    return jnp.where(use_prev, from_prev, from_cur)


def _band_split(use_prev, merged):
    return jnp.where(use_prev, merged, 0.0).astype(BF16), jnp.where(use_prev, 0.0, merged).astype(BF16)


def _stack_heads(v):
    zeros = jnp.zeros((ATTN_BLOCK, HEAD_DIM), v.dtype)
    rows = []
    for h in range(N_Q_HEADS):
        qh = v[:, h * HEAD_DIM:(h + 1) * HEAD_DIM]
        rows.append(jnp.concatenate([qh, zeros] if h < GQA_GROUP else [zeros, qh], axis=1))
    return jnp.concatenate(rows, axis=0)


def _unstack_heads(stacked):
    parts = []
    for h in range(N_Q_HEADS):
        lanes = slice(0, HEAD_DIM) if h < GQA_GROUP else slice(HEAD_DIM, 2 * HEAD_DIM)
        parts.append(stacked[h * ATTN_BLOCK:(h + 1) * ATTN_BLOCK, lanes])
    return jnp.concatenate(parts, axis=1)


def _stacked_sinks(sk_ref):
    row_head = lax.broadcasted_iota(jnp.int32, (STACK_ROWS, 1), 0) >> 7
    col = jnp.full((STACK_ROWS, 1), sk_ref[0], F32)
    for h in range(1, N_Q_HEADS):
        col = jnp.where(row_head == h, sk_ref[h], col)
    return col


def _softmax_with_sink(s, sink):
    mx = jnp.maximum(jnp.max(s, axis=-1, keepdims=True), sink)
    p = jnp.exp(s - mx)
    es = jnp.exp(sink - mx)
    inv = 1.0 / (jnp.sum(p, axis=-1, keepdims=True) + es)
    return p * inv, es * inv


def _merged_logits(qh, kp, kc, use_prev, bias):
    return _band_merge(use_prev, lax.dot_general(qh, kp, _NT, preferred_element_type=F32),
                       lax.dot_general(qh, kc, _NT, preferred_element_type=F32)) + bias


def _attn_fwd(qr, kr, vb, sinks, name):
    m = qr.shape[0]
    nb = m // ATTN_BLOCK

    def body(q_ref, kp_ref, kc_ref, vp_ref, vc_ref, sk_ref, bias_ref, o_ref):
        kk = jnp.concatenate([kp_ref[...], kc_ref[...]], axis=0)
        vv = jnp.concatenate([vp_ref[...], vc_ref[...]], axis=0)
        s = lax.dot_general(_stack_heads(q_ref[...]), kk, _NT, preferred_element_type=F32)
        p, _ = _softmax_with_sink(s + bias_ref[jnp.minimum(pl.program_id(0), 1)], _stacked_sinks(sk_ref))
        o_ref[...] = _unstack_heads(jnp.dot(p.astype(BF16), vv, preferred_element_type=F32)).astype(BF16)

    prev = lambda n: (jnp.maximum(n - 1, 0), 0)
    cur = lambda n: (n, 0)
    return pl.pallas_call(
        body, name=name, grid=(nb,),
        in_specs=[pl.BlockSpec((ATTN_BLOCK, ATTN_DIM), cur),
                  pl.BlockSpec((ATTN_BLOCK, KV_DIM), prev), pl.BlockSpec((ATTN_BLOCK, KV_DIM), cur),
                  pl.BlockSpec((ATTN_BLOCK, KV_DIM), prev), pl.BlockSpec((ATTN_BLOCK, KV_DIM), cur),
                  pl.BlockSpec(memory_space=pltpu.SMEM), _resident((2, STACK_ROWS, 2 * ATTN_BLOCK), (0, 0, 0))],
        out_specs=pl.BlockSpec((ATTN_BLOCK, ATTN_DIM), cur),
        out_shape=_sds((m, ATTN_DIM), BF16),
        compiler_params=_cp("parallel"),
    )(qr, kr, kr, vb, vb, sinks, _band_bias())


def _attn_bwd(do, qr, kr, vb, sinks, name):
    m = qr.shape[0]
    nb = m // ATTN_BLOCK

    def body(do_ref, q_ref, kp_ref, kc_ref, vp_ref, vc_ref, sk_ref, tab_ref, dq_ref, dk_ref, dv_ref, ds_ref,
             carry_k, carry_v):
        n = pl.program_id(0)

        @pl.when(n == 0)
        def _():
            carry_k[...] = jnp.zeros_like(carry_k)
            carry_v[...] = jnp.zeros_like(carry_v)
            ds_ref[...] = jnp.zeros_like(ds_ref)

        @pl.when(n < nb)
        def _():
            use_prev = tab_ref[0] > 0.5
            bias = tab_ref[1 + jnp.minimum(n, 1)]
            kp, kc, vp, vc = kp_ref[...], kc_ref[...], vp_ref[...], vc_ref[...]
            lane = lax.broadcasted_iota(jnp.int32, (1, 128), 1)
            qs = _stack_heads(q_ref[...])
            dos = _stack_heads(do_ref[...].astype(BF16))
            p, ps = _softmax_with_sink(_merged_logits(qs, kp, kc, use_prev, bias), _stacked_sinks(sk_ref))
            dpr = _band_merge(use_prev, lax.dot_general(dos, vp, _NT, preferred_element_type=F32),
                              lax.dot_general(dos, vc, _NT, preferred_element_type=F32))
            delta = jnp.sum(p * dpr, axis=-1, keepdims=True)
            ds_prev, ds_cur = _band_split(use_prev, p * (dpr - delta))
            p_prev, p_cur = _band_split(use_prev, p)
            sink_term = ps * delta
            dsink = jnp.zeros((1, 128), F32)
            for h in range(N_Q_HEADS):
                rows = slice(h * ATTN_BLOCK, (h + 1) * ATTN_BLOCK)
                dsink = dsink + jnp.where(lane == h, -jnp.sum(sink_term[rows, :]), 0.0)
            dq = jnp.dot(ds_prev, kp, preferred_element_type=F32) + jnp.dot(ds_cur, kc, preferred_element_type=F32)
            dq_ref[...] = _unstack_heads(dq * ATTN_SCALE)
            dk_ref[...] = carry_k[...] + lax.dot_general(ds_prev, qs, _TN, preferred_element_type=F32)
            dv_ref[...] = carry_v[...] + lax.dot_general(p_prev, dos, _TN, preferred_element_type=F32)
            carry_k[...] = lax.dot_general(ds_cur, qs, _TN, preferred_element_type=F32)
            carry_v[...] = lax.dot_general(p_cur, dos, _TN, preferred_element_type=F32)
            ds_ref[...] += dsink

        @pl.when(n == nb)
        def _():
            dk_ref[...] = carry_k[...]
            dv_ref[...] = carry_v[...]

    cur = lambda n: (jnp.minimum(n, nb - 1), 0)
    prev = lambda n: (jnp.clip(n - 1, 0, nb - 1), 0)
    return pl.pallas_call(
        body, name=name, grid=(nb + 1,),
        in_specs=[pl.BlockSpec((ATTN_BLOCK, ATTN_DIM), cur), pl.BlockSpec((ATTN_BLOCK, ATTN_DIM), cur),
                  pl.BlockSpec((ATTN_BLOCK, KV_DIM), prev), pl.BlockSpec((ATTN_BLOCK, KV_DIM), cur),
                  pl.BlockSpec((ATTN_BLOCK, KV_DIM), prev), pl.BlockSpec((ATTN_BLOCK, KV_DIM), cur),
                  pl.BlockSpec(memory_space=pltpu.SMEM), _resident((3, STACK_ROWS, ATTN_BLOCK), (0, 0, 0))],
        out_specs=[pl.BlockSpec((ATTN_BLOCK, ATTN_DIM), cur), pl.BlockSpec((ATTN_BLOCK, KV_DIM), prev),
                   pl.BlockSpec((ATTN_BLOCK, KV_DIM), prev), pl.BlockSpec((1, 128), lambda n: (0, 0))],
        out_shape=[_sds((m, ATTN_DIM), F32), _sds((m, KV_DIM), F32), _sds((m, KV_DIM), F32), _sds((1, 128), F32)],
        scratch_shapes=[pltpu.VMEM((ATTN_BLOCK, KV_DIM), F32), pltpu.VMEM((ATTN_BLOCK, KV_DIM), F32)],
        compiler_params=_cp("arbitrary"),
    )(do, qr, kr, kr, vb, vb, sinks, _band_tables())


def _qk_norm_rope_bwd(dout, xv, gain, cos, sin):
    width = xv.shape[1]
    r = lax.rsqrt(_segment_mean(xv * xv, width) + EPS)
    xn = xv * r
    dy = dout * _tile_lanes(cos, width) + _rope_partner(dout * _tile_lanes(sin, width))
    dxn = dy * gain
    dx = r * (dxn - xn * _segment_mean(dxn * xn, width))
    return dx, jnp.sum(dy * xn, axis=0, keepdims=True)


def _fold_heads(v):
    out = v[:, :HEAD_DIM]
    for h in range(1, v.shape[1] // HEAD_DIM):
        out = out + v[:, h * HEAD_DIM:(h + 1) * HEAD_DIM]
    return out


def _qk_bwd(dq, dk, dv, zq, qgain, kgain, cos, sin, name):
    m = zq.shape[0]
    tm = _row_tile(m)

    def body(dq_ref, dk_ref, dv_ref, q_ref, kv_ref, qg_ref, kg_ref, cos_ref, sin_ref, dz_ref, dqg_ref, dkg_ref):
        @pl.when(pl.program_id(0) == 0)
        def _():
            dqg_ref[...] = jnp.zeros_like(dqg_ref)
            dkg_ref[...] = jnp.zeros_like(dkg_ref)

        cos_v, sin_v = cos_ref[...], sin_ref[...]
        dxq, dgq = _qk_norm_rope_bwd(dq_ref[...], q_ref[...], qg_ref[...], cos_v, sin_v)
        dxk, dgk = _qk_norm_rope_bwd(dk_ref[...], kv_ref[:, :KV_DIM], kg_ref[...], cos_v, sin_v)
        dz_ref[:, :ATTN_DIM] = dxq.astype(BF16)
        dz_ref[:, ATTN_DIM:ATTN_DIM + KV_DIM] = dxk.astype(BF16)
        dz_ref[:, ATTN_DIM + KV_DIM:] = dv_ref[...].astype(BF16)
        dqg_ref[...] += _fold_heads(dgq)
        dkg_ref[...] += _fold_heads(dgk)

    row = lambda i: (i, 0)
    return pl.pallas_call(
        body, name=name, grid=(m // tm,),
        in_specs=[pl.BlockSpec((tm, ATTN_DIM), row), pl.BlockSpec((tm, KV_DIM), row), pl.BlockSpec((tm, KV_DIM), row),
                  pl.BlockSpec((tm, ATTN_DIM), lambda i: (i, 1)), pl.BlockSpec((tm, 2 * KV_DIM), lambda i: (i, 4)),
                  _resident((1, ATTN_DIM), (0, 0)), _resident((1, KV_DIM), (0, 0)),
                  pl.BlockSpec((tm, 128), row), pl.BlockSpec((tm, 128), row)],
        out_specs=[pl.BlockSpec((tm, ATTN_DIM + 2 * KV_DIM), row), pl.BlockSpec((1, HEAD_DIM), lambda i: (0, 0)),
                   pl.BlockSpec((1, HEAD_DIM), lambda i: (0, 0))],
        out_shape=[_sds((m, ATTN_DIM + 2 * KV_DIM), BF16), _sds((1, HEAD_DIM), F32), _sds((1, HEAD_DIM), F32)],
        compiler_params=_cp("arbitrary"),
    )(dq, dk, dv, zq, zq, qgain, kgain, cos, sin)


def _merge_fwd(x, p, o, zg, w_pb, w_ab, w_out, name):
    m, d = x.shape
    tm = _row_tile(m)

    def body(x_ref, p_ref, o_ref, zg_ref, wpb_ref, wab_ref, wo_ref, xo_ref, mix_ref):
        a = jnp.dot(p_ref[...], wpb_ref[...], preferred_element_type=F32)
        b = jnp.dot(o_ref[...], wab_ref[...], preferred_element_type=F32)
        mix = (_sigmoid(zg_ref[:, :d].astype(F32)) * a + _sigmoid(zg_ref[:, d:].astype(F32)) * b).astype(BF16)
        mix_ref[...] = mix
        xo_ref[...] = x_ref[...] + jnp.dot(mix, wo_ref[...], preferred_element_type=F32)

    row = lambda i: (i, 0)
    return pl.pallas_call(
        body, name=name, grid=(m // tm,),
        in_specs=[pl.BlockSpec((tm, d), row), pl.BlockSpec((tm, POOL_DIM), row), pl.BlockSpec((tm, ATTN_DIM), row),
                  pl.BlockSpec((tm, 2 * d), row), _resident((POOL_DIM, d), (0, 0)), _resident((ATTN_DIM, d), (0, 0)),
                  _resident((d, d), (0, 0))],
        out_specs=[pl.BlockSpec((tm, d), row)] * 2,
        out_shape=[_sds((m, d), F32), _sds((m, d), BF16)],
        compiler_params=_cp("parallel"),
    )(x, p, o, zg, w_pb, w_ab, w_out)


def _merge_bwd(dy, p, o, zg, w_out, w_pb, w_ab, name):
    m, d = dy.shape
    tm = _row_tile(m)

    def body(dy_ref, p_ref, o_ref, zg_ref, wo_ref, wpb_ref, wab_ref, dyb_ref, da_ref, db_ref, dp_ref, do_ref, dzg_ref):
        dyb = dy_ref[...].astype(BF16)
        dyb_ref[...] = dyb
        dmix = lax.dot_general(dyb, wo_ref[...], _NT, preferred_element_type=F32)
        gp = _sigmoid(zg_ref[:, :d].astype(F32))
        ga = _sigmoid(zg_ref[:, d:].astype(F32))
        da = (dmix * gp).astype(BF16)
        db = (dmix * ga).astype(BF16)
        da_ref[...] = da
        db_ref[...] = db
        a = jnp.dot(p_ref[...], wpb_ref[...], preferred_element_type=F32)
        b = jnp.dot(o_ref[...], wab_ref[...], preferred_element_type=F32)
        dzg_ref[:, :d] = (dmix * a * gp * (1.0 - gp)).astype(BF16)
        dzg_ref[:, d:] = (dmix * b * ga * (1.0 - ga)).astype(BF16)
        dp_ref[...] = lax.dot_general(da, wpb_ref[...], _NT, preferred_element_type=F32)
        do_ref[...] = lax.dot_general(db, wab_ref[...], _NT, preferred_element_type=F32).astype(BF16)

    row = lambda i: (i, 0)
    return pl.pallas_call(
        body, name=name, grid=(m // tm,),
        in_specs=[pl.BlockSpec((tm, d), row), pl.BlockSpec((tm, POOL_DIM), row), pl.BlockSpec((tm, ATTN_DIM), row),
                  pl.BlockSpec((tm, 2 * d), row), _resident((d, d), (0, 0)), _resident((POOL_DIM, d), (0, 0)),
                  _resident((ATTN_DIM, d), (0, 0))],
        out_specs=[pl.BlockSpec((tm, d), row), pl.BlockSpec((tm, d), row), pl.BlockSpec((tm, d), row),
                   pl.BlockSpec((tm, POOL_DIM), row), pl.BlockSpec((tm, ATTN_DIM), row), pl.BlockSpec((tm, 2 * d), row)],
        out_shape=[_sds((m, d), BF16), _sds((m, d), BF16), _sds((m, d), BF16), _sds((m, POOL_DIM), F32),
                   _sds((m, ATTN_DIM), BF16), _sds((m, 2 * d), BF16)],
        compiler_params=_cp("parallel"),
    )(dy, p, o, zg, w_out, w_pb, w_ab)


def _mix_bwd_x(dy, x, ln, dzp, dzqkv, dzg, w_in, name):
    m, d = dy.shape
    n_in = w_in.shape[0]
    tm = _row_tile(m)

    def body(dy_ref, x_ref, ln_ref, dzp_ref, dzq_ref, dzg_ref, w_ref, dx_ref, dln_ref):
        @pl.when(pl.program_id(0) == 0)
        def _():
            dln_ref[...] = jnp.zeros_like(dln_ref)

        dh = jnp.dot(dzp_ref[...], w_ref[:POOL_DIM, :], preferred_element_type=F32)
        dh += jnp.dot(dzq_ref[...], w_ref[POOL_DIM:QKV_END, :], preferred_element_type=F32)
        dh += jnp.dot(dzg_ref[...], w_ref[QKV_END:, :], preferred_element_type=F32)
        dx, dgain = _rms_bwd(dh, x_ref[...], ln_ref[...])
        dx_ref[...] = dy_ref[...] + dx
        dln_ref[...] += jnp.sum(dgain, axis=0, keepdims=True)

    row = lambda i: (i, 0)
    return pl.pallas_call(
        body, name=name, grid=(m // tm,),
        in_specs=[pl.BlockSpec((tm, d), row), pl.BlockSpec((tm, d), row), _resident((1, d), (0, 0)),
                  pl.BlockSpec((tm, POOL_DIM), row), pl.BlockSpec((tm, QKV_END - POOL_DIM), row),
                  pl.BlockSpec((tm, n_in - QKV_END), row), _resident((n_in, d), (0, 0))],
        out_specs=[pl.BlockSpec((tm, d), row), pl.BlockSpec((1, d), lambda i: (0, 0))],
        out_shape=[_sds((m, d), F32), _sds((1, d), F32)],
        compiler_params=_cp("arbitrary"),
    )(dy, x, ln, dzp, dzqkv, dzg, w_in)


def _loss_head(y, target, name):
    m, d = y.shape
    tm = _row_tile(m)

    def body(y_ref, t_ref, loss_ref, dy_ref):
        @pl.when(pl.program_id(0) == 0)
        def _():
            loss_ref[...] = jnp.zeros_like(loss_ref)

        diff = y_ref[...] - t_ref[...]
        dy_ref[...] = diff * (1.0 / d)
        loss_ref[...] += 0.5 * jnp.sum(jnp.mean(diff * diff, axis=-1, keepdims=True), axis=0, keepdims=True)

    row = lambda i: (i, 0)
    return pl.pallas_call(
        body, name=name, grid=(m // tm,),
        in_specs=[pl.BlockSpec((tm, d), row), pl.BlockSpec((tm, d), row)],
        out_specs=[pl.BlockSpec((1, 1), lambda i: (0, 0)), pl.BlockSpec((tm, d), row)],
        out_shape=[_sds((1, 1), F32), _sds((m, d), F32)],
        compiler_params=_cp("arbitrary"),
    )(y, target)


def _adamw_math(g, w, m, v):
    m2 = ADAM_B1 * m + (1.0 - ADAM_B1) * g
    v2 = ADAM_B2 * v + (1.0 - ADAM_B2) * (g * g)
    m_hat = m2 / (1.0 - ADAM_B1 ** ADAM_STEP)
    v_hat = v2 / (1.0 - ADAM_B2 ** ADAM_STEP)
    delta = -ADAM_LR * (m_hat / (jnp.sqrt(v_hat) + ADAM_EPS) + ADAM_WD * w)
    return delta, m2, v2


def _sum_parts(parts_ref):
    g = parts_ref[0].astype(F32)
    for s in range(1, N_DEV):
        g = g + parts_ref[s].astype(F32)
    return g


def _adamw_sharded(parts, w, m, v, name, after=None):
    n_layers, rows, cols = w.shape
    tr = max(t for t in range(16, rows + 1, 16) if rows % t == 0 and t * cols <= ADAMW_BLOCK_ELEMS)
    nr = rows // tr
    deps = [] if after is None else [after]

    def body(*refs):
        part_refs = refs[:n_layers]
        w_ref, m_ref, v_ref = refs[n_layers:n_layers + 3]
        g_out, d_out, m_out, v_out = refs[-4:]
        layer = pl.program_id(0)
        for l in range(n_layers):
            @pl.when(layer == l)
            def _(l=l):
                g = _sum_parts(part_refs[l])
                delta, m2, v2 = _adamw_math(g, w_ref[...], m_ref[...], v_ref[...])
                g_out[...] = g
                d_out[...] = delta
                m_out[...] = m2
                v_out[...] = v2

    def part_map(l):
        return lambda layer, r: (0, jnp.where(layer == l, r, jnp.where(layer < l, 0, nr - 1)), 0)

    wspec = pl.BlockSpec((None, tr, cols), lambda layer, r: (layer, r, 0))
    return pl.pallas_call(
        body, name=name, grid=(n_layers, nr),
        in_specs=([pl.BlockSpec((N_DEV, tr, cols), part_map(l)) for l in range(n_layers)] + [wspec] * 3
                  + [pl.BlockSpec(memory_space=pl.ANY)] * len(deps)),
        out_specs=[wspec] * 4,
        out_shape=[_sds(w.shape, F32)] * 4,
        compiler_params=_cp("arbitrary", "arbitrary"),
    )(*parts, w, m, v, *deps)


def _adamw_packed(parts, w, m, v, name):
    def body(p_ref, w_ref, m_ref, v_ref, g_out, d_out, m_out, v_out):
        g = _sum_parts(p_ref)
        delta, m2, v2 = _adamw_math(g, w_ref[...], m_ref[...], v_ref[...])
        g_out[...] = g
        d_out[...] = delta
        m_out[...] = m2
        v_out[...] = v2

    return pl.pallas_call(
        body, name=name, out_shape=[_sds(w.shape, F32)] * 4,
        compiler_params=pltpu.CompilerParams(vmem_limit_bytes=VMEM_LIMIT_BYTES),
    )(parts, w, m, v)


_SMALL = ("ln_ffn1", "ln_mix", "pool_w", "pool_scale", "q_norm", "k_norm", "sinks", "ln_ffn2")


def _pack_small(arrs):
    rows = []
    for a in arrs:
        flat = a.reshape(-1)
        pad = (-flat.shape[0]) % 1024
        rows.append(jnp.pad(flat, (0, pad)).reshape(-1, 128))
    return jnp.concatenate(rows, axis=0)


def _unpack_small(packed, like):
    out, r0 = [], 0
    for a in like:
        size = a.size
        nrows = (size + 1023) // 1024 * 8
        out.append(packed[r0:r0 + nrows].reshape(-1)[:size].reshape(a.shape))
        r0 += nrows
    return out


def _rope_tables(m):
    pos = jnp.arange(m, dtype=F32)
    inv_freq = ROPE_THETA ** (-jnp.arange(0, ROT_DIM, 2, dtype=F32) / ROT_DIM)
    ang = pos[:, None] * inv_freq[None, :]
    cos8, sin8 = jnp.cos(ang), jnp.sin(ang)
    rest = HEAD_DIM - ROT_DIM
    cos64 = jnp.concatenate([cos8, cos8, jnp.ones((m, rest), F32)], axis=1)
    sin64 = jnp.concatenate([-sin8, sin8, jnp.zeros((m, rest), F32)], axis=1)
    return jnp.tile(cos64, (1, 2)), jnp.tile(sin64, (1, 2))


def _to_shard_major_cols(w):
    k = w.shape[0]
    return w.reshape(k, N_DEV, -1).transpose(1, 0, 2)


def _from_shard_major_cols(w):
    return w.transpose(1, 0, 2).reshape(w.shape[1], -1)


def kernel(x, ln_ffn1, w_ffn1_gu, w_ffn1_down, ln_mix, w_in, pool_w, pool_scale, w_pool_branch, q_norm, k_norm, sinks, w_attn_branch, w_out, ln_ffn2, w_ffn2_gu, w_ffn2_down, loss_target, m_ln_ffn1, m_w_ffn1_gu, m_w_ffn1_down, m_ln_mix, m_w_in, m_pool_w, m_pool_scale, m_w_pool_branch, m_q_norm, m_k_norm, m_sinks, m_w_attn_branch, m_w_out, m_ln_ffn2, m_w_ffn2_gu, m_w_ffn2_down, v_ln_ffn1, v_w_ffn1_gu, v_w_ffn1_down, v_ln_mix, v_w_in, v_pool_w, v_pool_scale, v_w_pool_branch, v_q_norm, v_k_norm, v_sinks, v_w_attn_branch, v_w_out, v_ln_ffn2, v_w_ffn2_gu, v_w_ffn2_down):
    weights = dict(ln_ffn1=ln_ffn1, w_ffn1_gu=w_ffn1_gu, w_ffn1_down=w_ffn1_down, ln_mix=ln_mix, w_in=w_in, pool_w=pool_w,
                   pool_scale=pool_scale, w_pool_branch=w_pool_branch, q_norm=q_norm, k_norm=k_norm, sinks=sinks,
                   w_attn_branch=w_attn_branch, w_out=w_out, ln_ffn2=ln_ffn2, w_ffn2_gu=w_ffn2_gu, w_ffn2_down=w_ffn2_down)
    mom_m = dict(ln_ffn1=m_ln_ffn1, w_ffn1_gu=m_w_ffn1_gu, w_ffn1_down=m_w_ffn1_down, ln_mix=m_ln_mix, w_in=m_w_in,
                 pool_w=m_pool_w, pool_scale=m_pool_scale, w_pool_branch=m_w_pool_branch, q_norm=m_q_norm, k_norm=m_k_norm,
                 sinks=m_sinks, w_attn_branch=m_w_attn_branch, w_out=m_w_out, ln_ffn2=m_ln_ffn2, w_ffn2_gu=m_w_ffn2_gu,
                 w_ffn2_down=m_w_ffn2_down)
    mom_v = dict(ln_ffn1=v_ln_ffn1, w_ffn1_gu=v_w_ffn1_gu, w_ffn1_down=v_w_ffn1_down, ln_mix=v_ln_mix, w_in=v_w_in,
                 pool_w=v_pool_w, pool_scale=v_pool_scale, w_pool_branch=v_w_pool_branch, q_norm=v_q_norm, k_norm=v_k_norm,
                 sinks=v_sinks, w_attn_branch=v_w_attn_branch, w_out=v_w_out, ln_ffn2=v_ln_ffn2, w_ffn2_gu=v_w_ffn2_gu,
                 w_ffn2_down=v_w_ffn2_down)
    order = ("ln_ffn1", "w_ffn1_gu", "w_ffn1_down", "ln_mix", "w_in", "pool_w", "pool_scale", "w_pool_branch", "q_norm",
             "k_norm", "sinks", "w_attn_branch", "w_out", "ln_ffn2", "w_ffn2_gu", "w_ffn2_down")
    big = ("w_ffn1_gu", "w_ffn1_down", "w_in", "w_pool_branch", "w_attn_branch", "w_out", "w_ffn2_gu", "w_ffn2_down")

    transposed = ("w_ffn1_gu", "w_ffn2_gu", "w_in")
    for group in (weights, mom_m, mom_v):
        for k in transposed:
            group[k] = jnp.swapaxes(group[k], 1, 2)

    n_layers = ln_ffn1.shape[0]
    seq, d = x.shape[-2], x.shape[-1]
    xs = x.reshape(seq, d)
    target = loss_target.reshape(seq, d)
    cos, sin = _rope_tables(seq)

    first_keys = ("w_ffn1_gu", "w_ffn1_down")
    rest_keys = tuple(k for k in big if k not in first_keys)

    def layer_shards(l, keys=big):
        return [weights[k][l].astype(BF16) for k in keys]

    def first_weights(l, g):
        return dict(gu1=g["w_ffn1_gu"].reshape(2, -1, d), down1=g["w_ffn1_down"].reshape(-1, d), ln1=ln_ffn1[l][None])

    def rest_weights(l, g):
        return dict(
            gu2=g["w_ffn2_gu"].reshape(2, -1, d), down2=g["w_ffn2_down"].reshape(-1, d),
            w_in=g["w_in"].reshape(-1, d), w_pb=_from_shard_major_cols(g["w_pool_branch"]),
            w_ab=_from_shard_major_cols(g["w_attn_branch"]), w_out=g["w_out"].reshape(d, d),
            ln_mix=ln_mix[l][None], ln2=ln_ffn2[l][None], pool_w=pool_w[l],
            pool_scale=pool_scale[l][None], sinks=sinks[l],
            qgain=jnp.tile(q_norm[l], N_Q_HEADS)[None], kgain=jnp.tile(k_norm[l], N_KV_HEADS)[None])

    def layer_weights(l, full):
        g = dict(zip(big, full))
        return {**first_weights(l, g), **rest_weights(l, g)}

    got = _all_gather_many(layer_shards(0, first_keys), name="gather_first_l0")
    gathered = [first_weights(0, dict(zip(first_keys, got)))]
    rest_in_flight, token = _exchange_start(layer_shards(0, rest_keys), scatter=False, name="gather_start_l0",
                                            peers=_CHIP_PEERS)
    saved = []
    cur = xs
    for l in range(n_layers):
        lw = gathered[l]
        s = dict(x0=cur)
        in_flight = None
        if l + 1 < n_layers:
            in_flight, token = _exchange_start(layer_shards(l + 1), scatter=False, name=f"gather_start_l{l + 1}",
                                               after=token)
        s["h1"], s["gu1"], act1, x1 = _ffn_fwd(cur, lw["ln1"], lw["gu1"], lw["down1"], name=f"ffn1_fwd_l{l}", after=token)
        s["act1"] = act1
        s["x1"] = x1
        if l == 0:
            got = _forward_to_sibling(_exchange_wait(rest_in_flight, x1, name="gather_wait_l0"), name="gather_forward_l0")
            lw.update(rest_weights(0, dict(zip(rest_keys, got))))
        s["h2"], zq, zg = _mix_in(x1, lw["ln_mix"], lw["w_in"], name=f"mix_in_l{l}")
        s["zq"], s["zg"] = zq, zg
        s["d"], s["p"] = _pool_fwd(zq, lw["pool_w"], lw["pool_scale"], name=f"pool_fwd_l{l}")
        s["qr"], s["kr"], s["vb"] = _qk_prep(zq, lw["qgain"], lw["kgain"], cos, sin, name=f"qk_prep_l{l}")
        s["o"] = _attn_fwd(s["qr"], s["kr"], s["vb"], lw["sinks"], name=f"attn_fwd_l{l}")
        x2, s["mix"] = _merge_fwd(x1, s["p"], s["o"], zg, lw["w_pb"], lw["w_ab"], lw["w_out"],
                                                  name=f"merge_fwd_l{l}")
        s["x2"] = x2
        s["h3"], s["gu2"], act2, cur = _ffn_fwd(x2, lw["ln2"], lw["gu2"], lw["down2"], name=f"ffn2_fwd_l{l}")
        s["act2"] = act2
        saved.append(s)
        if in_flight is not None:
            gathered.append(layer_weights(l + 1, _exchange_wait(in_flight, cur, name=f"gather_wait_l{l + 1}")))

    loss_local, dy = _loss_head(cur, target, name="loss_head")
    loss = lax.psum(loss_local[0, 0], MESH_AXES)

    small_grads = {k: [None] * n_layers for k in _SMALL}
    received = {k: [None] * n_layers for k in big}
    big_late = ("w_ffn1_gu", "w_ffn1_down")
    big_early = tuple(k for k in big if k not in big_late)
    early_in_flight, late_in_flight, last_in_flight = [None] * n_layers, [None] * n_layers, [None] * n_layers
    token = None
    for l in reversed(range(n_layers)):
        lw, s = gathered[l], saved[l]
        d_ff = lw["down1"].shape[0]

        def dw_down_of(dyh, act, tag):
            return _matmul_tn(act[None], dyh[None], name=f"{tag}_dw_down_l{l}", a_chunk=d_ff // 2).reshape(N_DEV, -1, d)

        def dw_gu_of(dgu, h, tag, after=None):
            return _matmul_tn(dgu, h[None], name=f"{tag}_dw_gu_l{l}", a_chunk=d_ff // 2, after=after).reshape(N_DEV, -1, d)

        dyh, dgu, dx2, dln2 = _ffn_bwd(dy, s["x2"], lw["ln2"], s["gu2"], lw["down2"], lw["gu2"],
                                       name=f"ffn2_bwd_l{l}", after=token)
        dw_down2, dw_gu2 = dw_down_of(dyh, s["act2"], "ffn2"), dw_gu_of(dgu, s["h3"], "ffn2")

        dyb, da, db, dp, do, dzg = _merge_bwd(dx2, s["p"], s["o"], s["zg"], lw["w_out"], lw["w_pb"], lw["w_ab"],
                                              name=f"merge_bwd_l{l}")
        dw_out = _matmul_tn(s["mix"][None], dyb[None], name=f"dw_out_l{l}")[0]
        dw_pb = _matmul_tn(s["p"][None], da[None], name=f"dw_pb_l{l}")[0]
        dw_ab = _matmul_tn(s["o"][None], db[None], name=f"dw_ab_l{l}")[0]
        dzp, dpw, dsc = _pool_bwd(dp, s["d"], lw["pool_w"], lw["pool_scale"], name=f"pool_bwd_l{l}")
        dq, dk, dv, dsinks = _attn_bwd(do, s["qr"], s["kr"], s["vb"], lw["sinks"], name=f"attn_bwd_l{l}")
        dzqkv, dqg, dkg = _qk_bwd(dq, dk, dv, s["zq"], lw["qgain"], lw["kgain"], cos, sin, name=f"qk_bwd_l{l}")
        dw_in = jnp.concatenate([_matmul_tn(dzp[None], s["h2"][None], name=f"dw_in_pool_l{l}")[0],
                                 _matmul_tn(dzqkv[None], s["h2"][None], name=f"dw_in_qkv_l{l}")[0],
                                 _matmul_tn(dzg[None], s["h2"][None], name=f"dw_in_gate_l{l}")[0]], axis=0)
        dx1, dlnm = _mix_bwd_x(dx2, s["x1"], lw["ln_mix"], dzp, dzqkv, dzg, lw["w_in"], name=f"mix_bwd_x_l{l}")

        partial = dict(w_in=dw_in.reshape(N_DEV, -1, d), w_pool_branch=_to_shard_major_cols(dw_pb),
                       w_attn_branch=_to_shard_major_cols(dw_ab), w_out=dw_out.reshape(N_DEV, d // N_DEV, d),
                       w_ffn2_gu=dw_gu2, w_ffn2_down=dw_down2)
        early_in_flight[l], token = _exchange_start([partial[k] for k in big_early], scatter=True,
                                                    name=f"grads_early_start_l{l}")

        dyh, dgu, dy, dln1 = _ffn_bwd(dx1, s["x0"], lw["ln1"], s["gu1"], lw["down1"], lw["gu1"],
                                      name=f"ffn1_bwd_l{l}", after=token)
        late_in_flight[l], token = _exchange_start([dw_down_of(dyh, s["act1"], "ffn1")], scatter=True,
                                                   name=f"grads_late_start_l{l}")
        last_in_flight[l], token = _exchange_start([dw_gu_of(dgu, s["h1"], "ffn1", after=token)], scatter=True,
                                                   name=f"grads_last_start_l{l}")
        small_grads["ln_ffn1"][l] = dln1[0]
        small_grads["ln_mix"][l] = dlnm[0]
        small_grads["ln_ffn2"][l] = dln2[0]
        small_grads["pool_w"][l] = dpw
        small_grads["pool_scale"][l] = dsc[0]
        small_grads["q_norm"][l] = dqg[0]
        small_grads["k_norm"][l] = dkg[0]
        small_grads["sinks"][l] = dsinks[0, :N_Q_HEADS]

    grad_x = dy.reshape(x.shape)

    small_w = [weights[k] for k in _SMALL]
    packed_g = _pack_small([jnp.stack(small_grads[k]).reshape(weights[k].shape) for k in _SMALL])
    small_in_flight, after = _exchange_start([packed_g], scatter=False, name="small_grads_start", after=token)
    for l in reversed(range(n_layers)):
        got = _exchange_wait(early_in_flight[l], after, name=f"grads_early_wait_l{l}")
        after = got[0]
        for k, r in zip(big_early, got):
            received[k][l] = r

    grads, deltas, new_m, new_v = {}, {}, {}, {}

    def adamw(k, after):
        w = weights[k]
        shape2 = (n_layers, -1, w.shape[-1])
        parts = [r.reshape(N_DEV, -1, w.shape[-1]) for r in received[k]]
        outs = _adamw_sharded(parts, w.reshape(shape2), mom_m[k].reshape(shape2), mom_v[k].reshape(shape2),
                              name=f"adamw_{k}", after=after)
        grads[k], deltas[k], new_m[k], new_v[k] = (o.reshape(w.shape) for o in outs)
        return outs[0]

    after = None
    for k in big_early:
        after = adamw(k, after)
    for l in reversed(range(n_layers)):
        (received["w_ffn1_down"][l],) = _exchange_wait(late_in_flight[l], after, name=f"grads_late_wait_l{l}")
        (received["w_ffn1_gu"][l],) = _exchange_wait(last_in_flight[l], received["w_ffn1_down"][l],
                                                     name=f"grads_last_wait_l{l}")
        after = received["w_ffn1_gu"][l]
    after = None
    for k in big_late:
        after = adamw(k, after)

    (parts_small,) = _exchange_wait(small_in_flight, after, name="small_grads_wait")
    outs = _adamw_packed(parts_small, _pack_small(small_w), _pack_small([mom_m[k] for k in _SMALL]),
                         _pack_small([mom_v[k] for k in _SMALL]), name="adamw_small")
    for res, o in zip((grads, deltas, new_m, new_v), outs):
        for k, a in zip(_SMALL, _unpack_small(o, small_w)):
            res[k] = a

    for res in (grads, deltas, new_m, new_v):
        for k in transposed:
            res[k] = jnp.swapaxes(res[k], 1, 2)
    return (loss, grad_x, *[grads[k] for k in order], *[deltas[k] for k in order],
            *[new_m[k] for k in order], *[new_v[k] for k in order])
```
